```python
import math
import jax, jax.numpy as jnp
from jax import lax
import numpy as np

D_MODEL = 1024
BATCH = 8
SEQ = 8192
DEPTH = 2

PLE_DIM = 256
D_FF = 2816
DA_QK_DIM = 64
DA_V_DIM = 2 * DA_QK_DIM
DA_HEADS = (D_MODEL // 2) // DA_V_DIM
DA_WIDTH = DA_HEADS * DA_V_DIM
SSM_GROUP = 16
SSM_WIDTH = D_MODEL // 4
SSM_GROUPS = SSM_WIDTH // SSM_GROUP
SSM_STATE = 64
SA_HEAD_DIM = 64
SA_WIDTH = D_MODEL // 4
SA_HEADS = SA_WIDTH // SA_HEAD_DIM
IDX_HEADS = 8
IDX_DIM = 32
TOPK_MAX = 256
REL_BUCKETS = 32
REL_MAX_DIST = 128
N_ATT_HEADS = DA_HEADS + SA_HEADS
Q_BLOCK = 128
D_MIX = DA_WIDTH + SSM_WIDTH + SA_WIDTH
ALPHA = (2 * DEPTH) ** 0.25
BETA = (8 * DEPTH) ** -0.25
LN_EPS = 1e-5
IN_SIZES = (DA_HEADS * 2 * DA_QK_DIM, DA_HEADS * 2 * DA_QK_DIM, DA_HEADS * DA_V_DIM,
            SSM_WIDTH,
            SA_HEADS * SA_HEAD_DIM, SA_HEAD_DIM, SA_HEAD_DIM,
            IDX_HEADS * IDX_DIM, IDX_DIM, IDX_HEADS)
IN_COLS = sum(IN_SIZES)

kernel_name = "hybrid_diffattn_s5_dsa_macaron_deepnorm"


def layer_norm(x, g, b):
    xf = x.astype(jnp.float32)
    mu = xf.mean(-1, keepdims=True)
    var = jnp.square(xf - mu).mean(-1, keepdims=True)
    return ((xf - mu) * lax.rsqrt(var + LN_EPS) * g + b).astype(x.dtype)


def rms_norm(x, g):
    xf = x.astype(jnp.float32)
    return (xf * lax.rsqrt(jnp.mean(xf * xf, -1, keepdims=True) + LN_EPS) * g).astype(x.dtype)


def swiglu(x, w_gate, w_up, w_down):
    return (jax.nn.silu(x @ w_gate) * (x @ w_up)) @ w_down


def t5_bucket(rel):
    n = jnp.maximum(rel, 0)
    max_exact = REL_BUCKETS // 2
    nf = jnp.maximum(n, 1).astype(jnp.float32)
    large = max_exact + (jnp.log(nf / max_exact) / math.log(REL_MAX_DIST / max_exact)
                         * (REL_BUCKETS - max_exact)).astype(jnp.int32)
    large = jnp.minimum(large, REL_BUCKETS - 1)
    return jnp.where(n < max_exact, n, large)


def diff_attention(q, k, v, lam, bias_table):
    bsz, L, H = q.shape[0], q.shape[1], q.shape[2]
    nb = L // Q_BLOCK
    scale = DA_QK_DIM ** -0.5
    key_pos = jnp.arange(L)
    q_blocks = q.reshape(bsz, nb, Q_BLOCK, H, 2, DA_QK_DIM).swapaxes(0, 1)

    def block(args):
        qb, start = args
        qpos = start + jnp.arange(Q_BLOCK)
        rel = qpos[:, None] - key_pos[None, :]
        bias = bias_table[t5_bucket(rel)].transpose(2, 0, 1).astype(jnp.float32)
        s = jnp.einsum('bqhcd,bshcd->bhcqs', qb, k).astype(jnp.float32) * scale + bias[None, :, None]
        s = jnp.where(rel >= 0, s, -jnp.inf)
        pr = jax.nn.softmax(s, axis=-1)
        a = (pr[:, :, 0] - lam * pr[:, :, 1]).astype(v.dtype)
        return jnp.einsum('bhqs,bshe->bqhe', a, v)

    o = lax.map(block, (q_blocks, jnp.arange(nb) * Q_BLOCK))
    return o.swapaxes(0, 1).reshape(bsz, L, H, DA_V_DIM)


def _ssm_combine(e1, e2):
    a1r, a1i, b1r, b1i = e1
    a2r, a2i, b2r, b2i = e2
    return (a2r * a1r - a2i * a1i, a2r * a1i + a2i * a1r,
            a2r * b1r - a2i * b1i + b2r, a2r * b1i + a2i * b1r + b2i)


def s5_ssm(u, lam_re, lam_im, log_dt, b_re, b_im, c_re, c_im, d_skip, w_glu):
    f32 = jnp.float32
    bsz, L, _ = u.shape
    ug = u.reshape(bsz, L, SSM_GROUPS, SSM_GROUP).astype(f32)
    dt = jnp.exp(log_dt.astype(f32))[:, None]
    lr, li = lam_re.astype(f32), lam_im.astype(f32)
    mag = jnp.exp(lr * dt)
    ab_re, ab_im = mag * jnp.cos(li * dt), mag * jnp.sin(li * dt)
    den = lr * lr + li * li
    nr, ni = ab_re - 1.0, ab_im
    f_re, f_im = (nr * lr + ni * li) / den, (ni * lr - nr * li) / den
    br, bi = b_re.astype(f32), b_im.astype(f32)
    bb_re = f_re[..., None] * br - f_im[..., None] * bi
    bb_im = f_re[..., None] * bi + f_im[..., None] * br
    bu_re = jnp.einsum('gpc,blgc->blgp', bb_re, ug)
    bu_im = jnp.einsum('gpc,blgc->blgp', bb_im, ug)
    a_re = jnp.broadcast_to(ab_re, bu_re.shape)
    a_im = jnp.broadcast_to(ab_im, bu_re.shape)
    _, _, x_re, x_im = lax.associative_scan(_ssm_combine, (a_re, a_im, bu_re, bu_im), axis=1)
    y = (jnp.einsum('gcp,blgp->blgc', c_re.astype(f32), x_re)
         - jnp.einsum('gcp,blgp->blgc', c_im.astype(f32), x_im))
    y = y + d_skip.astype(f32).reshape(SSM_GROUPS, SSM_GROUP) * ug
    y = jax.nn.gelu(y.reshape(bsz, L, SSM_WIDTH))
    y = y * jax.nn.sigmoid(y @ w_glu.astype(f32))
    return y.astype(u.dtype)


def dsa_attention(q, k, v, qi, ki, wi, bias_table):
    bsz, L = q.shape[0], q.shape[1]
    topk = min(TOPK_MAX, L // 4)
    nb = L // Q_BLOCK
    key_pos = jnp.arange(L)
    scale = SA_HEAD_DIM ** -0.5
    w_scale = IDX_HEADS ** -0.5 * IDX_DIM ** -0.5

    def to_blocks(a):
        return a.reshape(bsz, nb, Q_BLOCK, *a.shape[2:]).swapaxes(0, 1)

    def block(args):
        qb, qib, wib, start = args
        qpos = start + jnp.arange(Q_BLOCK)
        iscore = jax.nn.relu(jnp.einsum('bqhd,bsd->bqhs', qib, ki).astype(jnp.float32))
        iscore = jnp.einsum('bqhs,bqh->bqs', iscore, wib.astype(jnp.float32) * w_scale)
        iscore = jnp.where(key_pos[None, :] <= qpos[:, None], iscore, -jnp.inf)
        _, idx = lax.top_k(iscore, topk)
        k_sel = jax.vmap(lambda kk, ii: kk[ii])(k, idx)
        v_sel = jax.vmap(lambda vv, ii: vv[ii])(v, idx)
        rel = qpos[None, :, None] - idx
        bias = bias_table[t5_bucket(rel)].transpose(0, 3, 1, 2).astype(jnp.float32)
        s = jnp.einsum('bqhd,bqkd->bhqk', qb, k_sel).astype(jnp.float32) * scale + bias
        s = jnp.where((rel >= 0)[:, None], s, -jnp.inf)
        pr = jax.nn.softmax(s, axis=-1).astype(v.dtype)
        return jnp.einsum('bhqk,bqkd->bqhd', pr, v_sel)

    o = lax.map(block, (to_blocks(q), to_blocks(qi), to_blocks(wi), jnp.arange(nb) * Q_BLOCK))
    return o.swapaxes(0, 1).reshape(bsz, L, SA_WIDTH)


def hybrid_mixer(x, w_in, w_o, lam_q1, lam_k1, lam_q2, lam_k2, lam_init, subln_g,
                 ssm_lam_re, ssm_lam_im, ssm_log_dt, ssm_b_re, ssm_b_im, ssm_c_re, ssm_c_im,
                 ssm_d, ssm_w_glu, rel_bias):
    bsz, L, _ = x.shape
    h = x @ w_in
    offsets = [int(o) for o in np.cumsum(IN_SIZES)[:-1]]
    da_q, da_k, da_v, ssm_u, sa_q, sa_k, sa_v, ix_q, ix_k, ix_w = jnp.split(h, offsets, axis=-1)
    lam = (jnp.exp(jnp.sum(lam_q1.astype(jnp.float32) * lam_k1))
           - jnp.exp(jnp.sum(lam_q2.astype(jnp.float32) * lam_k2)) + lam_init)
    o_da = diff_attention(da_q.reshape(bsz, L, DA_HEADS, 2, DA_QK_DIM),
                          da_k.reshape(bsz, L, DA_HEADS, 2, DA_QK_DIM),
                          da_v.reshape(bsz, L, DA_HEADS, DA_V_DIM),
                          lam, rel_bias[:, :DA_HEADS])
    o_da = (rms_norm(o_da, subln_g) * (1.0 - lam_init)).reshape(bsz, L, DA_WIDTH)
    o_ssm = s5_ssm(ssm_u, ssm_lam_re, ssm_lam_im, ssm_log_dt, ssm_b_re, ssm_b_im,
                   ssm_c_re, ssm_c_im, ssm_d, ssm_w_glu)
    o_sa = dsa_attention(sa_q.reshape(bsz, L, SA_HEADS, SA_HEAD_DIM), sa_k, sa_v,
                         ix_q.reshape(bsz, L, IDX_HEADS, IDX_DIM), ix_k, ix_w,
                         rel_bias[:, DA_HEADS:])
    return jnp.concatenate([o_da, o_ssm, o_sa], axis=-1) @ w_o


def setup_inputs(seed: int = 0) -> dict:
    key = jax.random.key(seed)
    ks = iter(jax.random.split(key, 40))
    f32 = jnp.float32

    def nrm(shape, scale):
        return jax.random.normal(next(ks), shape, f32) * scale

    def gain(shape):
        return 1.0 + nrm(shape, 0.02)

    N = DEPTH
    n_idx = jnp.arange(SSM_STATE, dtype=f32)
    log_dt = jax.random.uniform(next(ks), (N, SSM_GROUPS), f32, math.log(1e-3), math.log(1e-1))
    return {
        'x': nrm((BATCH, SEQ, D_MODEL), 1.0),
        'p': nrm((DEPTH, BATCH, SEQ, PLE_DIM), 1.0),
        'rel_bias': nrm((REL_BUCKETS, N_ATT_HEADS), 0.5),
        'ffn1_w_gate': nrm((N, D_MODEL, D_FF), D_MODEL ** -0.5),
        'ffn1_w_up': nrm((N, D_MODEL, D_FF), D_MODEL ** -0.5),
        'ffn1_w_down': nrm((N, D_FF, D_MODEL), D_FF ** -0.5 * BETA),
        'ln1_g': gain((N, D_MODEL)),
        'ln1_b': nrm((N, D_MODEL), 0.02),
        'w_in': nrm((N, D_MODEL, IN_COLS), D_MODEL ** -0.5),
        'w_o': nrm((N, D_MIX, D_MODEL), D_MIX ** -0.5 * BETA),
        'da_lam_q1': nrm((N, DA_QK_DIM), 0.1),
        'da_lam_k1': nrm((N, DA_QK_DIM), 0.1),
        'da_lam_q2': nrm((N, DA_QK_DIM), 0.1),
        'da_lam_k2': nrm((N, DA_QK_DIM), 0.1),
        'da_subln_g': gain((N, DA_V_DIM)),
        'ssm_lam_re': -0.5 + nrm((N, SSM_GROUPS, SSM_STATE), 0.01),
        'ssm_lam_im': jnp.pi * n_idx + nrm((N, SSM_GROUPS, SSM_STATE), 0.01),
        'ssm_log_dt': log_dt,
        'ssm_b_re': nrm((N, SSM_GROUPS, SSM_STATE, SSM_GROUP), (2.0 * SSM_GROUP) ** -0.5),
        'ssm_b_im': nrm((N, SSM_GROUPS, SSM_STATE, SSM_GROUP), (2.0 * SSM_GROUP) ** -0.5),
        'ssm_c_re': nrm((N, SSM_GROUPS, SSM_GROUP, SSM_STATE), SSM_STATE ** -0.5),
        'ssm_c_im': nrm((N, SSM_GROUPS, SSM_GROUP, SSM_STATE), SSM_STATE ** -0.5),
        'ssm_d': nrm((N, SSM_WIDTH), 1.0),
        'ssm_w_glu': nrm((N, SSM_WIDTH, SSM_WIDTH), SSM_WIDTH ** -0.5),
        'ln2_g': gain((N, D_MODEL)),
        'ln2_b': nrm((N, D_MODEL), 0.02),
        'ffn2_w_gate': nrm((N, D_MODEL, D_FF), D_MODEL ** -0.5),
        'ffn2_w_up': nrm((N, D_MODEL, D_FF), D_MODEL ** -0.5),
        'ffn2_w_down': nrm((N, D_FF, D_MODEL), D_FF ** -0.5 * BETA),
        'ple_w_proj': nrm((N, PLE_DIM, D_MODEL), PLE_DIM ** -0.5 * BETA),
        'ple_w_gate': nrm((N, D_MODEL, D_MODEL), D_MODEL ** -0.5),
        'ln3_g': gain((N, D_MODEL)),
        'ln3_b': nrm((N, D_MODEL), 0.02),
    }


def reference(x, p, rel_bias, ffn1_w_gate, ffn1_w_up, ffn1_w_down, ln1_g, ln1_b,
              w_in, w_o, da_lam_q1, da_lam_k1, da_lam_q2, da_lam_k2, da_subln_g,
              ssm_lam_re, ssm_lam_im, ssm_log_dt, ssm_b_re, ssm_b_im, ssm_c_re, ssm_c_im,
              ssm_d, ssm_w_glu, ln2_g, ln2_b, ffn2_w_gate, ffn2_w_up, ffn2_w_down,
              ple_w_proj, ple_w_gate, ln3_g, ln3_b):
    for i in range(DEPTH):
        lam_init = 0.8 - 0.6 * math.exp(-0.3 * i)
        x = layer_norm(ALPHA * x + 0.5 * swiglu(x, ffn1_w_gate[i], ffn1_w_up[i], ffn1_w_down[i]),
                       ln1_g[i], ln1_b[i])
        mix = hybrid_mixer(x, w_in[i], w_o[i], da_lam_q1[i], da_lam_k1[i], da_lam_q2[i], da_lam_k2[i],
                           lam_init, da_subln_g[i], ssm_lam_re[i], ssm_lam_im[i], ssm_log_dt[i],
                           ssm_b_re[i], ssm_b_im[i], ssm_c_re[i], ssm_c_im[i], ssm_d[i], ssm_w_glu[i],
                           rel_bias)
        x = layer_norm(ALPHA * x + mix, ln2_g[i], ln2_b[i])
        ple = (p[i] @ ple_w_proj[i]) * jax.nn.sigmoid(x @ ple_w_gate[i])
        x = layer_norm(ALPHA * x + 0.5 * swiglu(x, ffn2_w_gate[i], ffn2_w_up[i], ffn2_w_down[i]) + ple,
                       ln3_g[i], ln3_b[i])
    return x
```

```python
import functools
import math

import jax
import jax.numpy as jnp
from jax import lax
from jax.experimental import pallas as pl
from jax.experimental.pallas import tpu as pltpu

F32 = jnp.float32
BF16 = jnp.bfloat16
I32 = jnp.int32

DEPTH = 2
DA_QK_DIM = 64
DA_V_DIM = 2 * DA_QK_DIM
DA_HEADS = 4
SSM_GROUP = 16
SSM_GROUPS = 16
SSM_STATE = 64
SSM_WIDTH = SSM_GROUP * SSM_GROUPS
SA_HEAD_DIM = 64
SA_HEADS = 4
SA_WIDTH = SA_HEADS * SA_HEAD_DIM
IDX_HEADS = 8
IDX_DIM = 32
TOPK_MAX = 256
REL_BUCKETS = 32
REL_MAX_DIST = 128
ALPHA = (2 * DEPTH) ** 0.25
LN_EPS = 1e-5
IN_SIZES = (DA_HEADS * 2 * DA_QK_DIM, DA_HEADS * 2 * DA_QK_DIM, DA_HEADS * DA_V_DIM,
            SSM_WIDTH, SA_WIDTH, SA_HEAD_DIM, SA_HEAD_DIM,
            IDX_HEADS * IDX_DIM, IDX_DIM, IDX_HEADS)

LANES = 128
NEG = -1e30
INT_MIN = -2 ** 31
VMEM_LIMIT = 56 * 1024 * 1024
SSM_CHUNK = 64


def _params(*sem):
    return pltpu.CompilerParams(dimension_semantics=sem, vmem_limit_bytes=VMEM_LIMIT)


def _resident(shape):
    return pl.BlockSpec(shape, lambda *_: (0,) * len(shape), pipeline_mode=pl.Buffered(1))


def _dot(a, b):
    return jnp.dot(a, b, preferred_element_type=F32)


def _dot_nt(a, b):
    return lax.dot_general(a, b, (((1,), (1,)), ((), ())), preferred_element_type=F32)


def _lane_tile(x, n):
    reps = n // LANES
    return x if reps == 1 else pltpu.repeat(x, reps, axis=1)


def _layer_norm(y, g, b):
    mu = jnp.mean(y, axis=-1, keepdims=True)
    d = y - mu
    var = jnp.mean(d * d, axis=-1, keepdims=True)
    return d * lax.rsqrt(var + LN_EPS) * g + b


def _ffn_kernel(*refs, has_ple, ff_chunk):
    if has_ple:
        x_ref, p_ref, wg, wu, wd, wpp, wpg, g_ref, b_ref, o_ref = refs
    else:
        x_ref, wg, wu, wd, g_ref, b_ref, o_ref = refs
    x = x_ref[...]
    xb = x.astype(BF16)
    d_ff = wg.shape[1]
    acc = None
    for c0 in range(0, d_ff, ff_chunk):
        c1 = min(c0 + ff_chunk, d_ff)
        gate = _dot(xb, wg[:, c0:c1])
        up = _dot(xb, wu[:, c0:c1])
        hid = (gate * jax.nn.sigmoid(gate) * up).astype(BF16)
        part = _dot(hid, wd[c0:c1, :])
        acc = part if acc is None else acc + part
    y = ALPHA * x + 0.5 * acc
    if has_ple:
        y = y + _dot(p_ref[...].astype(BF16), wpp[...]) * jax.nn.sigmoid(_dot(xb, wpg[...]))
    o_ref[...] = _layer_norm(y, g_ref[...], b_ref[...])


def _ffn(x, wg, wu, wd, g, b, ple=None, *, tm):
    t_rows, d = x.shape
    d_ff = wg.shape[1]
    row = lambda w: pl.BlockSpec((tm, w), lambda i: (i, 0))
    args = [x]
    specs = [row(d)]
    if ple is not None:
        p, wpp, wpg = ple
        args.append(p)
        specs.append(row(p.shape[1]))
    args += [wg, wu, wd]
    specs += [_resident(wg.shape), _resident(wu.shape), _resident(wd.shape)]
    if ple is not None:
        args += [wpp, wpg]
        specs += [_resident(wpp.shape), _resident(wpg.shape)]
    args += [g, b]
    specs += [_resident(g.shape), _resident(b.shape)]
    return pl.pallas_call(
        functools.partial(_ffn_kernel, has_ple=ple is not None, ff_chunk=min(512, d_ff)),
        grid=(t_rows // tm,),
        in_specs=specs,
        out_specs=row(d),
        out_shape=jax.ShapeDtypeStruct((t_rows, d), F32),
        compiler_params=_params("parallel"),
        name="ffn_ple_ln" if ple is not None else "ffn_ln",
    )(*args)


def _inproj_kernel(x_ref, *refs):
    n = len(refs) // 2
    xb = x_ref[...].astype(BF16)
    for w_ref, o_ref in zip(refs[:n], refs[n:]):
        o_ref[...] = _dot(xb, w_ref[...]).astype(o_ref.dtype)


def _inproj(x, weights, dtypes, *, tm):
    t_rows, d = x.shape
    return pl.pallas_call(
        _inproj_kernel,
        grid=(t_rows // tm,),
        in_specs=[pl.BlockSpec((tm, d), lambda i: (i, 0))] + [_resident(w.shape) for w in weights],
        out_specs=[pl.BlockSpec((tm, w.shape[1]), lambda i: (i, 0)) for w in weights],
        out_shape=[jax.ShapeDtypeStruct((t_rows, w.shape[1]), dt) for w, dt in zip(weights, dtypes)],
        compiler_params=_params("parallel"),
        name="in_proj",
    )(x, *weights)


def _softmax_step(s, v, m_ref, l_ref, acc_ref, shift):
    m_prev = m_ref[...]
    m_cur = jnp.max(s, axis=1, keepdims=True)
    if shift is not None:
        m_cur = m_cur + shift
    m_next = jnp.maximum(m_prev, m_cur)
    sub = m_next if shift is None else m_next - shift
    p = jnp.exp(s - _lane_tile(sub, s.shape[1]))
    alpha = jnp.exp(m_prev - m_next)
    l_ref[...] = alpha * l_ref[...] + jnp.sum(p, axis=1, keepdims=True)
    acc_ref[...] = _lane_tile(alpha, acc_ref.shape[1]) * acc_ref[...] + _dot(p.astype(BF16), v)
    m_ref[...] = m_next


def _da_kernel(far_ref, lam_ref, q_ref, k_ref, v_ref, bn_ref, g_ref, o_ref,
               qq_ref, m_ref, l_ref, acc_ref, *, t, post_scale):
    h = pl.program_id(1)
    qi = pl.program_id(2)
    q = q_ref[...].astype(F32)
    lane = lax.broadcasted_iota(I32, q.shape, 1)
    qq_ref[0:t, :] = jnp.where(lane < DA_QK_DIM, q, 0.0).astype(BF16)
    qq_ref[t:2 * t, :] = jnp.where(lane >= DA_QK_DIM, q, 0.0).astype(BF16)
    m_ref[...] = jnp.full(m_ref.shape, NEG, F32)
    l_ref[...] = jnp.zeros(l_ref.shape, F32)
    acc_ref[...] = jnp.zeros(acc_ref.shape, F32)
    far = jnp.full((2 * t, LANES), far_ref[h], F32)

    def block(kb, near):
        off = pl.multiple_of(kb * t, t)
        s = _dot_nt(qq_ref[...], k_ref[pl.ds(off, t), :])
        if near is None:
            shift = far
        else:
            bias = bn_ref[near]
            s = s + jnp.concatenate([bias, bias], axis=0)
            shift = None
        _softmax_step(s, v_ref[pl.ds(off, t), :], m_ref, l_ref, acc_ref, shift)

    def far_body(kb, carry):
        block(kb, None)
        return carry

    lax.fori_loop(0, jnp.maximum(qi - 1, 0), far_body, 0)

    @pl.when(qi >= 1)
    def _():
        block(qi - 1, 1)

    block(qi, 0)

    o = acc_ref[...] / l_ref[...]
    o = o[0:t] - lam_ref[0] * o[t:2 * t]
    ms = jnp.mean(o * o, axis=-1, keepdims=True)
    o_ref[...] = (o * lax.rsqrt(ms + LN_EPS) * g_ref[...] * post_scale).astype(o_ref.dtype)


def _diff_attention(q, k, v, bias_near, bias_far, lam, subln_g, *, t, post_scale):
    bsz, seq, width = q.shape
    heads = width // DA_V_DIM
    smem = pl.BlockSpec(memory_space=pltpu.SMEM)
    return pl.pallas_call(
        functools.partial(_da_kernel, t=t, post_scale=post_scale),
        grid=(bsz, heads, seq // t),
        in_specs=[
            smem, smem,
            pl.BlockSpec((None, t, DA_V_DIM), lambda b, h, i: (b, i, h)),
            pl.BlockSpec((None, seq, DA_V_DIM), lambda b, h, i: (b, 0, h)),
            pl.BlockSpec((None, seq, DA_V_DIM), lambda b, h, i: (b, 0, h)),
            pl.BlockSpec((None, 2, t, t), lambda b, h, i: (h, 0, 0, 0)),
            pl.BlockSpec((1, DA_V_DIM), lambda b, h, i: (0, 0)),
        ],
        out_specs=pl.BlockSpec((None, t, DA_V_DIM), lambda b, h, i: (b, i, h)),
        out_shape=jax.ShapeDtypeStruct((bsz, seq, width), BF16),
        scratch_shapes=[
            pltpu.VMEM((2 * t, DA_V_DIM), BF16),
            pltpu.VMEM((2 * t, LANES), F32),
            pltpu.VMEM((2 * t, LANES), F32),
            pltpu.VMEM((2 * t, DA_V_DIM), F32),
        ],
        compiler_params=_params("parallel", "parallel", "arbitrary"),
        name="diff_attention",
    )(bias_far, lam, q, k, v, bias_near, subln_g)


def _dsa_kernel(far_ref, ixq_ref, ixw_ref, ixk_ref, saq_ref, sak_ref, sav_ref, bn_ref, o_ref,
                qs8_ref, wb_ref, keys_ref, qs4_ref, shift_ref, m_ref, l_ref, acc_ref,
                *, t, topk, idx_bits):
    qi = pl.program_id(1)
    nsub = t // LANES
    kf = float(topk)

    ixq = ixq_ref[...].astype(F32)
    lane_q = lax.broadcasted_iota(I32, ixq.shape, 1)
    for h in range(IDX_HEADS):
        qs8_ref[h * t:(h + 1) * t, :] = jnp.where((lane_q >> 5) == h, ixq, 0.0).astype(BF16)
    w = ixw_ref[...]
    lane_w = lax.broadcasted_iota(I32, w.shape, 1)
    for h in range(IDX_HEADS):
        col = jnp.sum(jnp.where(lane_w == h, w, 0.0), axis=1, keepdims=True)
        wb_ref[h] = jnp.broadcast_to(col, (t, LANES))

    def index_keys(kb):
        off = pl.multiple_of(kb * t, t)
        r = _dot_nt(qs8_ref[...], ixk_ref[pl.ds(off, t), :])
        sc = None
        for h in range(IDX_HEADS):
            term = _lane_tile(wb_ref[h], t) * jnp.maximum(r[h * t:(h + 1) * t], 0.0)
            sc = term if sc is None else sc + term
        sc = jnp.where(sc == 0.0, 0.0, sc)
        bits = lax.bitcast_convert_type(sc, I32)
        return bits ^ ((bits >> 31) & 0x7FFFFFFF)

    def fill(kb, carry):
        keys_ref[kb] = index_keys(kb)
        return carry

    lax.fori_loop(0, qi, fill, 0)
    row = lax.broadcasted_iota(I32, (t, t), 0)
    colm = lax.broadcasted_iota(I32, (t, t), 1)
    keys_ref[qi] = jnp.where(colm <= row, index_keys(qi), INT_MIN)

    def count(pred):
        def body(kb, cnt):
            for c in range(nsub):
                cnt = cnt + pred(kb, c, keys_ref[kb, :, c * LANES:(c + 1) * LANES]).astype(I32)
            return cnt
        cnt = lax.fori_loop(0, qi + 1, body, jnp.zeros((t, LANES), I32))
        tot = jnp.sum(cnt.astype(F32), axis=1, keepdims=True)
        return jnp.broadcast_to(tot, (t, LANES))

    def tau_bit(i, tu):
        cand = tu | jnp.left_shift(jnp.int32(1), 31 - i)
        cand_s = cand ^ INT_MIN
        c = count(lambda kb, cc, blk: blk >= cand_s)
        return jnp.where(c >= kf, cand, tu)

    tau = lax.fori_loop(0, 32, tau_bit, jnp.zeros((t, LANES), I32)) ^ INT_MIN
    c_ge = count(lambda kb, cc, blk: blk >= tau)

    @pl.when(jnp.max(c_ge) > kf)
    def _():
        need = kf - count(lambda kb, cc, blk: blk > tau)
        lane = lax.broadcasted_iota(I32, (t, LANES), 1)

        def j_bit(i, ju):
            cand = ju | jnp.left_shift(jnp.int32(1), idx_bits - 1 - i)
            c = count(lambda kb, cc, blk: jnp.where(
                blk == tau, jnp.where((lane + (kb * t + cc * LANES)) < cand, 1, 0), 0))
            return jnp.where(c < need, cand, ju)

        ju = lax.fori_loop(0, idx_bits, j_bit, jnp.zeros((t, LANES), I32))

        def retire(kb, carry):
            for c in range(nsub):
                blk = keys_ref[kb, :, c * LANES:(c + 1) * LANES]
                late = jnp.where((lane + (kb * t + c * LANES)) > ju, INT_MIN, blk)
                keys_ref[kb, :, c * LANES:(c + 1) * LANES] = jnp.where(blk == tau, late, blk)
            return carry

        lax.fori_loop(0, qi + 1, retire, 0)

    saq = saq_ref[...].astype(F32)
    lane_a = lax.broadcasted_iota(I32, saq.shape, 1)
    for h in range(SA_HEADS):
        qs4_ref[h * t:(h + 1) * t, :] = jnp.where((lane_a >> 6) == h, saq, 0.0).astype(BF16)
        shift_ref[h * t:(h + 1) * t, :] = jnp.full((t, LANES), far_ref[h], F32)
    m_ref[...] = jnp.full(m_ref.shape, NEG, F32)
    l_ref[...] = jnp.zeros(l_ref.shape, F32)
    acc_ref[...] = jnp.zeros(acc_ref.shape, F32)
    tau_t = _lane_tile(tau, t)

    def block(kb, near):
        off = pl.multiple_of(kb * t, t)
        s = _dot_nt(qs4_ref[...], sak_ref[pl.ds(off, t), :])
        sel = keys_ref[kb] >= tau_t
        parts = []
        for h in range(SA_HEADS):
            sh = s[h * t:(h + 1) * t]
            if near is not None:
                sh = sh + bn_ref[h, near]
            parts.append(jnp.where(sel, sh, NEG))
        s = jnp.concatenate(parts, axis=0)
        _softmax_step(s, sav_ref[pl.ds(off, t), :], m_ref, l_ref, acc_ref,
                      shift_ref[...] if near is None else None)

    def far_body(kb, carry):
        block(kb, None)
        return carry

    lax.fori_loop(0, jnp.maximum(qi - 1, 0), far_body, 0)

    @pl.when(qi >= 1)
    def _():
        block(qi - 1, 1)

    block(qi, 0)

    o_all = acc_ref[...] / _lane_tile(l_ref[...], SA_WIDTH)
    o = None
    for h in range(SA_HEADS):
        part = jnp.where((lane_a >> 6) == h, o_all[h * t:(h + 1) * t], 0.0)
        o = part if o is None else o + part
    o_ref[...] = o.astype(o_ref.dtype)


def _sparse_attention(ixq, ixw, ixk, saq, sak, sav, bias_near, bias_far, *, t, topk):
    bsz, seq, _ = saq.shape
    smem = pl.BlockSpec(memory_space=pltpu.SMEM)
    qblk = lambda w: pl.BlockSpec((None, t, w), lambda b, i: (b, i, 0))
    full = lambda w: pl.BlockSpec((None, seq, w), lambda b, i: (b, 0, 0))
    return pl.pallas_call(
        functools.partial(_dsa_kernel, t=t, topk=topk, idx_bits=max(1, (seq - 1).bit_length())),
        grid=(bsz, seq // t),
        in_specs=[
            smem,
            qblk(IDX_HEADS * IDX_DIM), qblk(LANES), full(IDX_HEADS * IDX_DIM),
            qblk(SA_WIDTH), full(SA_WIDTH), full(SA_WIDTH),
            pl.BlockSpec((SA_HEADS, 2, t, t), lambda b, i: (0, 0, 0, 0)),
        ],
        out_specs=qblk(SA_WIDTH),
        out_shape=jax.ShapeDtypeStruct((bsz, seq, SA_WIDTH), BF16),
        scratch_shapes=[
            pltpu.VMEM((IDX_HEADS * t, IDX_HEADS * IDX_DIM), BF16),
            pltpu.VMEM((IDX_HEADS, t, LANES), F32),
            pltpu.VMEM((seq // t, t, t), I32),
            pltpu.VMEM((SA_HEADS * t, SA_WIDTH), BF16),
            pltpu.VMEM((SA_HEADS * t, LANES), F32),
            pltpu.VMEM((SA_HEADS * t, LANES), F32),
            pltpu.VMEM((SA_HEADS * t, LANES), F32),
            pltpu.VMEM((SA_HEADS * t, SA_WIDTH), F32),
        ],
        compiler_params=_params("parallel", "arbitrary"),
        name="sparse_attention",
    )(bias_far, ixq, ixw, ixk, saq, sak, sav, bias_near)


def _ssm_state_kernel(u_ref, w_ref, o_ref):
    o_ref[...] = _dot(u_ref[...].astype(BF16), w_ref[...])


def _ssm_scan_kernel(loc_ref, a1_ref, a2_ref, o_ref):
    a1 = a1_ref[...]
    a2 = a2_ref[...]

    def body(n, s):
        o_ref[n] = s
        return a1 * s + a2 * pltpu.roll(s, SSM_STATE, axis=1) + loc_ref[n]

    lax.fori_loop(0, loc_ref.shape[0], body, jnp.zeros(a1.shape, F32))


def _ssm_out_kernel(u_ref, s_ref, toep_ref, wout_ref, o_ref):
    o_ref[...] = (_dot(u_ref[...].astype(BF16), toep_ref[...])
                  + _dot(s_ref[...].astype(BF16), wout_ref[...]))


def _ssm_gate_kernel(y_ref, u_ref, d_ref, w_ref, o_ref):
    y = jax.nn.gelu(y_ref[...] + d_ref[...] * u_ref[...])
    o_ref[...] = (y * jax.nn.sigmoid(_dot(y.astype(BF16), w_ref[...]))).astype(o_ref.dtype)


def _ssm_tables(lam_re, lam_im, log_dt, b_re, b_im, c_re, c_im, tc):
    hp = lax.Precision.HIGHEST
    dt = jnp.exp(log_dt)[:, None]
    n = jnp.arange(tc + 1, dtype=F32)[:, None, None]
    mag = jnp.exp(lam_re * dt * n)
    pw_re = mag * jnp.cos(lam_im * dt * n)
    pw_im = mag * jnp.sin(lam_im * dt * n)
    den = lam_re * lam_re + lam_im * lam_im
    nr, ni = pw_re[1] - 1.0, pw_im[1]
    f_re = (nr * lam_re + ni * lam_im) / den
    f_im = (ni * lam_re - nr * lam_im) / den
    bb_re = f_re[..., None] * b_re - f_im[..., None] * b_im
    bb_im = f_re[..., None] * b_im + f_im[..., None] * b_re
    ca_re = c_re[None] * pw_re[:, :, None, :] - c_im[None] * pw_im[:, :, None, :]
    ca_im = c_re[None] * pw_im[:, :, None, :] + c_im[None] * pw_re[:, :, None, :]
    kern = (jnp.einsum('tgcp,gpd->tgcd', ca_re[:tc], bb_re, precision=hp)
            - jnp.einsum('tgcp,gpd->tgcd', ca_im[:tc], bb_im, precision=hp))
    s_idx = jnp.arange(tc)[:, None]
    t_idx = jnp.arange(tc)[None, :]
    tau = t_idx - s_idx
    toep = jnp.where((tau >= 0)[:, :, None, None, None], kern[jnp.maximum(tau, 0)], 0.0)
    groups = lam_re.shape[0]
    toep = toep.transpose(2, 0, 4, 1, 3).reshape(groups, tc * SSM_GROUP, tc * SSM_GROUP)
    rev_re, rev_im = pw_re[tc - 1::-1][:tc], pw_im[tc - 1::-1][:tc]
    ws_re = rev_re[..., None] * bb_re[None] - rev_im[..., None] * bb_im[None]
    ws_im = rev_re[..., None] * bb_im[None] + rev_im[..., None] * bb_re[None]
    wstate = jnp.concatenate([ws_re, ws_im], axis=2)
    wstate = wstate.transpose(1, 0, 3, 2).reshape(groups, tc * SSM_GROUP, 2 * SSM_STATE)
    wout = jnp.concatenate([ca_re[1:], -ca_im[1:]], axis=3)
    wout = wout.transpose(1, 3, 0, 2).reshape(groups, 2 * SSM_STATE, tc * SSM_GROUP)
    dec_re, dec_im = pw_re[tc], pw_im[tc]
    a1 = jnp.concatenate([dec_re, dec_re], axis=1)
    a2 = jnp.concatenate([-dec_im, dec_im], axis=1)
    return toep.astype(BF16), wstate.astype(BF16), wout.astype(BF16), a1, a2


def _ssm(u, tables, d_skip, w_glu, *, bsz, seq, tm):
    toep, wstate, wout, a1, a2 = tables
    groups = toep.shape[0]
    tc = SSM_CHUNK
    nc = seq // tc
    rows = bsz * nc
    cw = tc * SSM_GROUP
    ug = u.reshape(bsz, nc, tc, groups, SSM_GROUP).transpose(3, 0, 1, 2, 4).reshape(groups, rows, cw)
    gspec = lambda r, c: pl.BlockSpec((None, r, c), lambda g: (g, 0, 0))
    loc = pl.pallas_call(
        _ssm_state_kernel,
        grid=(groups,),
        in_specs=[gspec(rows, cw), gspec(cw, 2 * SSM_STATE)],
        out_specs=gspec(rows, 2 * SSM_STATE),
        out_shape=jax.ShapeDtypeStruct((groups, rows, 2 * SSM_STATE), F32),
        compiler_params=_params("parallel"),
        name="ssm_chunk_state",
    )(ug, wstate)
    gb = groups * bsz
    loc_t = loc.reshape(groups, bsz, nc, 2 * SSM_STATE).transpose(2, 0, 1, 3).reshape(nc, gb, 2 * SSM_STATE)
    a1r = jnp.repeat(a1, bsz, axis=0)
    a2r = jnp.repeat(a2, bsz, axis=0)
    rb = 8 if gb % 8 == 0 else gb
    prev = pl.pallas_call(
        _ssm_scan_kernel,
        grid=(gb // rb,),
        in_specs=[pl.BlockSpec((nc, rb, 2 * SSM_STATE), lambda i: (0, i, 0)),
                  pl.BlockSpec((rb, 2 * SSM_STATE), lambda i: (i, 0)),
                  pl.BlockSpec((rb, 2 * SSM_STATE), lambda i: (i, 0))],
        out_specs=pl.BlockSpec((nc, rb, 2 * SSM_STATE), lambda i: (0, i, 0)),
        out_shape=jax.ShapeDtypeStruct((nc, gb, 2 * SSM_STATE), F32),
        compiler_params=_params("parallel"),
        name="ssm_chunk_scan",
    )(loc_t, a1r, a2r)
    prev_g = prev.reshape(nc, groups, bsz, 2 * SSM_STATE).transpose(1, 2, 0, 3).reshape(groups, rows, 2 * SSM_STATE)
    y = pl.pallas_call(
        _ssm_out_kernel,
        grid=(groups,),
        in_specs=[gspec(rows, cw), gspec(rows, 2 * SSM_STATE), gspec(cw, cw), gspec(2 * SSM_STATE, cw)],
        out_specs=gspec(rows, cw),
        out_shape=jax.ShapeDtypeStruct((groups, rows, cw), F32),
        compiler_params=_params("parallel"),
        name="ssm_chunk_out",
    )(ug, prev_g, toep, wout)
    y = y.reshape(groups, bsz, nc, tc, SSM_GROUP).transpose(1, 2, 3, 0, 4).reshape(bsz * seq, groups * SSM_GROUP)
    width = groups * SSM_GROUP
    return pl.pallas_call(
        _ssm_gate_kernel,
        grid=(bsz * seq // tm,),
        in_specs=[pl.BlockSpec((tm, width), lambda i: (i, 0)),
                  pl.BlockSpec((tm, width), lambda i: (i, 0)),
                  _resident((1, width)), _resident((width, width))],
        out_specs=pl.BlockSpec((tm, width), lambda i: (i, 0)),
        out_shape=jax.ShapeDtypeStruct((bsz * seq, width), BF16),
        compiler_params=_params("parallel"),
        name="ssm_gate",
    )(y, u, d_skip, w_glu)


def _outproj_kernel(x_ref, da_ref, ssm_ref, sa_ref, w1, w2, w3, g_ref, b_ref, o_ref):
    mix = _dot(da_ref[...], w1[...]) + _dot(ssm_ref[...], w2[...]) + _dot(sa_ref[...], w3[...])
    o_ref[...] = _layer_norm(ALPHA * x_ref[...] + mix, g_ref[...], b_ref[...])


def _outproj(x, o_da, o_ssm, o_sa, w1, w2, w3, g, b, *, tm):
    t_rows, d = x.shape
    row = lambda w: pl.BlockSpec((tm, w), lambda i: (i, 0))
    return pl.pallas_call(
        _outproj_kernel,
        grid=(t_rows // tm,),
        in_specs=[row(d), row(o_da.shape[1]), row(o_ssm.shape[1]), row(o_sa.shape[1]),
                  _resident(w1.shape), _resident(w2.shape), _resident(w3.shape),
                  _resident(g.shape), _resident(b.shape)],
        out_specs=row(d),
        out_shape=jax.ShapeDtypeStruct((t_rows, d), F32),
        compiler_params=_params("parallel"),
        name="out_proj_ln",
    )(x, o_da, o_ssm, o_sa, w1, w2, w3, g, b)


def _t5_bucket(n):
    max_exact = REL_BUCKETS // 2
    nf = jnp.maximum(n, 1).astype(F32)
    large = max_exact + (jnp.log(nf / max_exact) / math.log(REL_MAX_DIST / max_exact)
                         * (REL_BUCKETS - max_exact)).astype(I32)
    large = jnp.minimum(large, REL_BUCKETS - 1)
    return jnp.where(n < max_exact, n, large)


def _bias_tiles(table, t):
    assert t >= REL_MAX_DIST
    r = jnp.arange(t)[:, None]
    c = jnp.arange(t)[None, :]
    dist = jnp.stack([r - c, t + r - c])
    tiles = table[_t5_bucket(jnp.maximum(dist, 0))]
    tiles = jnp.where((dist >= 0)[..., None], tiles, NEG)
    return tiles.transpose(3, 0, 1, 2).astype(F32), table[REL_BUCKETS - 1].astype(F32)


def _split_w_in(w_in):
    offs = [0]
    for s in IN_SIZES:
        offs.append(offs[-1] + s)
    da_q, da_k, da_v, ssm_u, sa_q, sa_k, sa_v, ix_q, ix_k, ix_w = (
        w_in[:, offs[j]:offs[j + 1]] for j in range(len(IN_SIZES)))
    w_scale = IDX_HEADS ** -0.5 * IDX_DIM ** -0.5
    ix_w = jnp.pad(ix_w * w_scale, ((0, 0), (0, LANES - IDX_HEADS)))
    weights = [da_q * DA_QK_DIM ** -0.5, da_k, da_v, ssm_u,
               sa_q * SA_HEAD_DIM ** -0.5, jnp.tile(sa_k, (1, SA_HEADS)), jnp.tile(sa_v, (1, SA_HEADS)),
               ix_q, jnp.tile(ix_k, (1, IDX_HEADS)), ix_w]
    dtypes = [BF16, BF16, BF16, F32, BF16, BF16, BF16, BF16, BF16, F32]
    return [w.astype(BF16) for w in weights], dtypes


def _plan(bsz, seq):
    rows = bsz * seq
    tm = 512 if rows % 512 == 0 else rows
    t_da = 512 if seq % 512 == 0 and seq >= 2048 else 256 if seq % 256 == 0 else 128
    return dict(tm=tm, t_da=t_da, t_sa=128)


def kernel(x, p, rel_bias, ffn1_w_gate, ffn1_w_up, ffn1_w_down, ln1_g, ln1_b, w_in, w_o, da_lam_q1, da_lam_k1, da_lam_q2, da_lam_k2, da_subln_g, ssm_lam_re, ssm_lam_im, ssm_log_dt, ssm_b_re, ssm_b_im, ssm_c_re, ssm_c_im, ssm_d, ssm_w_glu, ln2_g, ln2_b, ffn2_w_gate, ffn2_w_up, ffn2_w_down, ple_w_proj, ple_w_gate, ln3_g, ln3_b):
    bsz, seq, d = x.shape
    rows = bsz * seq
    plan = _plan(bsz, seq)
    tm, t_da, t_sa = plan["tm"], plan["t_da"], plan["t_sa"]
    topk = min(TOPK_MAX, seq // 4)
    da_near, da_far = _bias_tiles(rel_bias[:, :DA_HEADS], t_da)
    sa_near, sa_far = _bias_tiles(rel_bias[:, DA_HEADS:], t_sa)
    vec = lambda a: a.reshape(1, -1).astype(F32)
    da_w = DA_HEADS * DA_V_DIM

    h = x.reshape(rows, d)
    for i in range(DEPTH):
        lam_init = 0.8 - 0.6 * math.exp(-0.3 * i)
        h = _ffn(h, ffn1_w_gate[i].astype(BF16), ffn1_w_up[i].astype(BF16), ffn1_w_down[i].astype(BF16),
                 vec(ln1_g[i]), vec(ln1_b[i]), tm=tm)

        weights, dtypes = _split_w_in(w_in[i])
        (da_q, da_k, da_v, ssm_u, sa_q, sa_k, sa_v, ix_q, ix_k, ix_w) = _inproj(h, weights, dtypes, tm=tm)
        b3 = lambda a: a.reshape(bsz, seq, a.shape[-1])

        lam = (jnp.exp(jnp.sum(da_lam_q1[i].astype(F32) * da_lam_k1[i]))
               - jnp.exp(jnp.sum(da_lam_q2[i].astype(F32) * da_lam_k2[i])) + lam_init)
        o_da = _diff_attention(b3(da_q), b3(da_k), b3(da_v), da_near, da_far, lam.reshape(1).astype(F32),
                               vec(da_subln_g[i]), t=t_da, post_scale=1.0 - lam_init)

        tables = _ssm_tables(ssm_lam_re[i].astype(F32), ssm_lam_im[i].astype(F32), ssm_log_dt[i].astype(F32),
                             ssm_b_re[i].astype(F32), ssm_b_im[i].astype(F32),
                             ssm_c_re[i].astype(F32), ssm_c_im[i].astype(F32), SSM_CHUNK)
        o_ssm = _ssm(ssm_u, tables, vec(ssm_d[i]), ssm_w_glu[i].astype(BF16), bsz=bsz, seq=seq, tm=tm)

        o_sa = _sparse_attention(b3(ix_q), b3(ix_w), b3(ix_k), b3(sa_q), b3(sa_k), b3(sa_v),
                                 sa_near, sa_far, t=t_sa, topk=topk)

        wo = w_o[i].astype(BF16)
        h = _outproj(h, o_da.reshape(rows, da_w), o_ssm, o_sa.reshape(rows, SA_WIDTH),
                     wo[:da_w], wo[da_w:da_w + SSM_WIDTH], wo[da_w + SSM_WIDTH:],
                     vec(ln2_g[i]), vec(ln2_b[i]), tm=tm)

        h = _ffn(h, ffn2_w_gate[i].astype(BF16), ffn2_w_up[i].astype(BF16), ffn2_w_down[i].astype(BF16),
                 vec(ln3_g[i]), vec(ln3_b[i]),
                 ple=(p[i].reshape(rows, -1), ple_w_proj[i].astype(BF16), ple_w_gate[i].astype(BF16)), tm=tm)
    return h.reshape(bsz, seq, d)
```

```python
import functools
import math

import jax
import jax.numpy as jnp
from jax import lax
from jax.experimental import pallas as pl
from jax.experimental.pallas import tpu as pltpu

F32 = jnp.float32
BF16 = jnp.bfloat16
I32 = jnp.int32

DEPTH = 2
DA_QK_DIM = 64
DA_V_DIM = 2 * DA_QK_DIM
DA_HEADS = 4
SSM_GROUP = 16
SSM_GROUPS = 16
SSM_STATE = 64
SSM_WIDTH = SSM_GROUP * SSM_GROUPS
SA_HEAD_DIM = 64
SA_HEADS = 4
SA_WIDTH = SA_HEADS * SA_HEAD_DIM
IDX_HEADS = 8
IDX_DIM = 32
TOPK_MAX = 256
REL_BUCKETS = 32
REL_MAX_DIST = 128
ALPHA = (2 * DEPTH) ** 0.25
LN_EPS = 1e-5
IN_SIZES = (DA_HEADS * 2 * DA_QK_DIM, DA_HEADS * 2 * DA_QK_DIM, DA_HEADS * DA_V_DIM,
            SSM_WIDTH, SA_WIDTH, SA_HEAD_DIM, SA_HEAD_DIM,
            IDX_HEADS * IDX_DIM, IDX_DIM, IDX_HEADS)

LANES = 128
NEG = -1e30
INT_MIN = -2 ** 31
VMEM_LIMIT = 56 * 1024 * 1024
SSM_CHUNK = 64
SA_WIDE = 4


def _params(*sem):
    return pltpu.CompilerParams(dimension_semantics=sem, vmem_limit_bytes=VMEM_LIMIT)


def _resident(shape):
    return pl.BlockSpec(shape, lambda *_: (0,) * len(shape), pipeline_mode=pl.Buffered(1))


def _dot(a, b):
    return jnp.dot(a, b, preferred_element_type=F32)


def _dot_nt(a, b):
    return lax.dot_general(a, b, (((1,), (1,)), ((), ())), preferred_element_type=F32)


def _lane_tile(x, n):
    reps = n // LANES
    return x if reps == 1 else pltpu.repeat(x, reps, axis=1)


def _layer_norm(y, g, b):
    mu = jnp.mean(y, axis=-1, keepdims=True)
    d = y - mu
    var = jnp.mean(d * d, axis=-1, keepdims=True)
    return d * lax.rsqrt(var + LN_EPS) * g + b


def _ffn_kernel(*refs, has_ple, ff_chunk):
    if has_ple:
        x_ref, p_ref, wg, wu, wd, wpp, wpg, g_ref, b_ref, o_ref = refs
    else:
        x_ref, wg, wu, wd, g_ref, b_ref, o_ref = refs
    x = x_ref[...]
    xb = x.astype(BF16)
    d_ff = wg.shape[1]
    acc = None
    for c0 in range(0, d_ff, ff_chunk):
        c1 = min(c0 + ff_chunk, d_ff)
        gate = _dot(xb, wg[:, c0:c1])
        up = _dot(xb, wu[:, c0:c1])
        hid = (gate * jax.nn.sigmoid(gate) * up).astype(BF16)
        part = _dot(hid, wd[c0:c1, :])
        acc = part if acc is None else acc + part
    y = ALPHA * x + 0.5 * acc
    if has_ple:
        y = y + _dot(p_ref[...].astype(BF16), wpp[...]) * jax.nn.sigmoid(_dot(xb, wpg[...]))
    o_ref[...] = _layer_norm(y, g_ref[...], b_ref[...])


def _ffn(x, wg, wu, wd, g, b, ple=None, *, tm):
    t_rows, d = x.shape
    d_ff = wg.shape[1]
    row = lambda w: pl.BlockSpec((tm, w), lambda i: (i, 0))
    args = [x]
    specs = [row(d)]
    if ple is not None:
        p, wpp, wpg = ple
        args.append(p)
        specs.append(row(p.shape[1]))
    args += [wg, wu, wd]
    specs += [_resident(wg.shape), _resident(wu.shape), _resident(wd.shape)]
    if ple is not None:
        args += [wpp, wpg]
        specs += [_resident(wpp.shape), _resident(wpg.shape)]
    args += [g, b]
    specs += [_resident(g.shape), _resident(b.shape)]
    return pl.pallas_call(
        functools.partial(_ffn_kernel, has_ple=ple is not None, ff_chunk=min(512, d_ff)),
        grid=(t_rows // tm,),
        in_specs=specs,
        out_specs=row(d),
        out_shape=jax.ShapeDtypeStruct((t_rows, d), F32),
        compiler_params=_params("parallel"),
        name="ffn_ple_ln" if ple is not None else "ffn_ln",
    )(*args)


def _inproj_kernel(x_ref, *refs):
    n = len(refs) // 2
    xb = x_ref[...].astype(BF16)
    for w_ref, o_ref in zip(refs[:n], refs[n:]):
        o_ref[...] = _dot(xb, w_ref[...]).astype(o_ref.dtype)


def _inproj(x, weights, dtypes, *, tm):
    t_rows, d = x.shape
    return pl.pallas_call(
        _inproj_kernel,
        grid=(t_rows // tm,),
        in_specs=[pl.BlockSpec((tm, d), lambda i: (i, 0))] + [_resident(w.shape) for w in weights],
        out_specs=[pl.BlockSpec((tm, w.shape[1]), lambda i: (i, 0)) for w in weights],
        out_shape=[jax.ShapeDtypeStruct((t_rows, w.shape[1]), dt) for w, dt in zip(weights, dtypes)],
        compiler_params=_params("parallel"),
        name="in_proj",
    )(x, *weights)


def _softmax_step(s, v, m_ref, l_ref, acc_ref, shift):
    m_prev = m_ref[...]
    m_cur = jnp.max(s, axis=1, keepdims=True)
    if shift is not None:
        m_cur = m_cur + shift
    m_next = jnp.maximum(m_prev, m_cur)
    sub = m_next if shift is None else m_next - shift
    p = jnp.exp(s - _lane_tile(sub, s.shape[1]))
    alpha = jnp.exp(m_prev - m_next)
    l_ref[...] = alpha * l_ref[...] + jnp.sum(p, axis=1, keepdims=True)
    acc_ref[...] = _lane_tile(alpha, acc_ref.shape[1]) * acc_ref[...] + _dot(p.astype(BF16), v)
    m_ref[...] = m_next


def _da_kernel(far_ref, lam_ref, q_ref, k_ref, v_ref, bn_ref, g_ref, o_ref,
               qq_ref, m_ref, l_ref, acc_ref, *, t, post_scale):
    h = pl.program_id(1)
    qi = pl.program_id(2)
    q = q_ref[...].astype(F32)
    lane = lax.broadcasted_iota(I32, q.shape, 1)
    qq_ref[0:t, :] = jnp.where(lane < DA_QK_DIM, q, 0.0).astype(BF16)
    qq_ref[t:2 * t, :] = jnp.where(lane >= DA_QK_DIM, q, 0.0).astype(BF16)
    m_ref[...] = jnp.full(m_ref.shape, NEG, F32)
    l_ref[...] = jnp.zeros(l_ref.shape, F32)
    acc_ref[...] = jnp.zeros(acc_ref.shape, F32)
    far = jnp.full((2 * t, LANES), far_ref[h], F32)

    def block(kb, near):
        off = pl.multiple_of(kb * t, t)
        s = _dot_nt(qq_ref[...], k_ref[pl.ds(off, t), :])
        if near is None:
            shift = far
        else:
            bias = bn_ref[near]
            s = s + jnp.concatenate([bias, bias], axis=0)
            shift = None
        _softmax_step(s, v_ref[pl.ds(off, t), :], m_ref, l_ref, acc_ref, shift)

    def far_body(kb, carry):
        block(kb, None)
        return carry

    lax.fori_loop(0, jnp.maximum(qi - 1, 0), far_body, 0)

    @pl.when(qi >= 1)
    def _():
        block(qi - 1, 1)

    block(qi, 0)

    o = acc_ref[...] / l_ref[...]
    o = o[0:t] - lam_ref[0] * o[t:2 * t]
    ms = jnp.mean(o * o, axis=-1, keepdims=True)
    o_ref[...] = (o * lax.rsqrt(ms + LN_EPS) * g_ref[...] * post_scale).astype(o_ref.dtype)


def _diff_attention(q, k, v, bias_near, bias_far, lam, subln_g, *, t, post_scale):
    bsz, seq, width = q.shape
    heads = width // DA_V_DIM
    smem = pl.BlockSpec(memory_space=pltpu.SMEM)
    return pl.pallas_call(
        functools.partial(_da_kernel, t=t, post_scale=post_scale),
        grid=(bsz, heads, seq // t),
        in_specs=[
            smem, smem,
            pl.BlockSpec((None, t, DA_V_DIM), lambda b, h, i: (b, i, h)),
            pl.BlockSpec((None, seq, DA_V_DIM), lambda b, h, i: (b, 0, h)),
            pl.BlockSpec((None, seq, DA_V_DIM), lambda b, h, i: (b, 0, h)),
            pl.BlockSpec((None, 2, t, t), lambda b, h, i: (h, 0, 0, 0)),
            pl.BlockSpec((1, DA_V_DIM), lambda b, h, i: (0, 0)),
        ],
        out_specs=pl.BlockSpec((None, t, DA_V_DIM), lambda b, h, i: (b, i, h)),
        out_shape=jax.ShapeDtypeStruct((bsz, seq, width), BF16),
        scratch_shapes=[
            pltpu.VMEM((2 * t, DA_V_DIM), BF16),
            pltpu.VMEM((2 * t, LANES), F32),
            pltpu.VMEM((2 * t, LANES), F32),
            pltpu.VMEM((2 * t, DA_V_DIM), F32),
        ],
        compiler_params=_params("parallel", "parallel", "arbitrary"),
        name="diff_attention",
    )(bias_far, lam, q, k, v, bias_near, subln_g)


def _dsa_kernel(far_ref, ixq_ref, ixw_ref, ixk_ref, saq_ref, sak_ref, sav_ref, bn_ref, o_ref,
                qs8_ref, wb_ref, keys_ref, qs4_ref, shift_ref, m_ref, l_ref, acc_ref,
                *, topk, idx_bits):
    t = LANES
    wide = SA_WIDE * LANES
    qi = pl.program_id(1)
    kf = float(topk)

    ixq = ixq_ref[...].astype(F32)
    lane_q = lax.broadcasted_iota(I32, ixq.shape, 1)
    for h in range(IDX_HEADS):
        qs8_ref[h * t:(h + 1) * t, :] = jnp.where((lane_q >> 5) == h, ixq, 0.0).astype(BF16)
    w = ixw_ref[...]
    lane_w = lax.broadcasted_iota(I32, w.shape, 1)
    for h in range(IDX_HEADS):
        col = jnp.sum(jnp.where(lane_w == h, w, 0.0), axis=1, keepdims=True)
        wb_ref[h] = jnp.broadcast_to(col, (t, LANES))

    def index_keys(off, width):
        r = _dot_nt(qs8_ref[...], ixk_ref[pl.ds(off, width), :])
        sc = None
        for h in range(IDX_HEADS):
            term = _lane_tile(wb_ref[h], width) * jnp.maximum(r[h * t:(h + 1) * t], 0.0)
            sc = term if sc is None else sc + term
        sc = jnp.where(sc == 0.0, 0.0, sc)
        bits = lax.bitcast_convert_type(sc, I32)
        return bits ^ ((bits >> 31) & 0x7FFFFFFF)

    def fill_wide(c, carry):
        keys = index_keys(pl.multiple_of(c * wide, wide), wide)
        for j in range(SA_WIDE):
            keys_ref[c * SA_WIDE + j] = keys[:, j * LANES:(j + 1) * LANES]
        return carry

    def fill_one(kb, carry):
        keys_ref[kb] = index_keys(pl.multiple_of(kb * LANES, LANES), LANES)
        return carry

    n_wide = qi // SA_WIDE
    lax.fori_loop(0, n_wide, fill_wide, 0)
    lax.fori_loop(n_wide * SA_WIDE, qi, fill_one, 0)
    row = lax.broadcasted_iota(I32, (t, LANES), 0)
    lane = lax.broadcasted_iota(I32, (t, LANES), 1)
    keys_ref[qi] = jnp.where(lane <= row, index_keys(pl.multiple_of(qi * LANES, LANES), LANES), INT_MIN)
    for j in range(1, SA_WIDE):
        @pl.when(qi + j < keys_ref.shape[0])
        def _():
            keys_ref[qi + j] = jnp.full((t, LANES), INT_MIN, I32)

    def lane_total(cnt):
        tot = jnp.sum(cnt.astype(F32), axis=1, keepdims=True)
        return jnp.broadcast_to(tot, (t, LANES))

    def count_ge(cand):
        def body(g, cnt):
            for j in range(SA_WIDE):
                cnt = cnt + jnp.where(keys_ref[g * SA_WIDE + j] >= cand, 1, 0)
            return cnt
        return lane_total(lax.fori_loop(0, n_wide + 1, body, jnp.zeros((t, LANES), I32)))

    def count_slabs(fn):
        def body(kb, cnt):
            return cnt + fn(kb, keys_ref[kb])
        return lane_total(lax.fori_loop(0, qi + 1, body, jnp.zeros((t, LANES), I32)))

    def search_cond(st):
        i, _, _, n_open = st
        return jnp.logical_and(i < 32, n_open > 0.0)

    def search_body(st):
        i, tau, open_, _ = st
        cand = tau + jnp.left_shift(jnp.int32(1), 31 - i)
        c = count_ge(cand)
        tau = jnp.where(open_ > 0.0, jnp.where(c >= kf, cand, tau), tau)
        open_ = jnp.where(c == kf, 0.0, open_)
        return i + 1, tau, open_, jnp.max(open_)

    _, tau, _, _ = lax.while_loop(
        search_cond, search_body,
        (jnp.int32(0), jnp.full((t, LANES), INT_MIN, I32), jnp.ones((t, LANES), F32), jnp.float32(1.0)))
    c_ge = count_slabs(lambda kb, slab: jnp.where(slab >= tau, 1, 0))

    @pl.when(jnp.max(c_ge) > kf)
    def _():
        need = kf - count_slabs(lambda kb, slab: jnp.where(slab > tau, 1, 0))

        def j_bit(i, ju):
            cand = ju | jnp.left_shift(jnp.int32(1), idx_bits - 1 - i)
            c = count_slabs(lambda kb, slab: jnp.where(
                slab == tau, jnp.where((lane + kb * LANES) < cand, 1, 0), 0))
            return jnp.where(c < need, cand, ju)

        ju = lax.fori_loop(0, idx_bits, j_bit, jnp.zeros((t, LANES), I32))

        def retire(kb, carry):
            slab = keys_ref[kb]
            late = jnp.where((lane + kb * LANES) > ju, INT_MIN, slab)
            keys_ref[kb] = jnp.where(slab == tau, late, slab)
            return carry

        lax.fori_loop(0, qi + 1, retire, 0)

    saq = saq_ref[...].astype(F32)
    lane_a = lax.broadcasted_iota(I32, saq.shape, 1)
    for h in range(SA_HEADS):
        qs4_ref[h * t:(h + 1) * t, :] = jnp.where((lane_a >> 6) == h, saq, 0.0).astype(BF16)
        shift_ref[h * t:(h + 1) * t, :] = jnp.full((t, LANES), far_ref[h], F32)
    m_ref[...] = jnp.full(m_ref.shape, NEG, F32)
    l_ref[...] = jnp.zeros(l_ref.shape, F32)
    acc_ref[...] = jnp.zeros(acc_ref.shape, F32)

    def attend(slab0, width, near):
        off = pl.multiple_of(slab0 * LANES, width)
        s = _dot_nt(qs4_ref[...], sak_ref[pl.ds(off, width), :])
        n = width // LANES
        keys = keys_ref[slab0] if n == 1 else jnp.concatenate([keys_ref[slab0 + j] for j in range(n)], axis=1)
        sel = keys >= _lane_tile(tau, width)
        parts = []
        for h in range(SA_HEADS):
            sh = s[h * t:(h + 1) * t]
            if near is not None:
                sh = sh + bn_ref[h, near]
            parts.append(jnp.where(sel, sh, NEG))
        _softmax_step(jnp.concatenate(parts, axis=0), sav_ref[pl.ds(off, width), :], m_ref, l_ref, acc_ref,
                      shift_ref[...] if near is None else None)

    def far_wide(c, carry):
        attend(c * SA_WIDE, wide, None)
        return carry

    def far_one(kb, carry):
        attend(kb, LANES, None)
        return carry

    n_far = jnp.maximum(qi - 1, 0)
    lax.fori_loop(0, n_far // SA_WIDE, far_wide, 0)
    lax.fori_loop((n_far // SA_WIDE) * SA_WIDE, n_far, far_one, 0)

    @pl.when(qi >= 1)
    def _():
        attend(qi - 1, LANES, 1)

    attend(qi, LANES, 0)

    o_all = acc_ref[...] / _lane_tile(l_ref[...], SA_WIDTH)
    o = None
    for h in range(SA_HEADS):
        part = jnp.where((lane_a >> 6) == h, o_all[h * t:(h + 1) * t], 0.0)
        o = part if o is None else o + part
    o_ref[...] = o.astype(o_ref.dtype)


def _sparse_attention(ixq, ixw, ixk, saq, sak, sav, bias_near, bias_far, *, t, topk):
    bsz, seq, _ = saq.shape
    assert t == LANES
    smem = pl.BlockSpec(memory_space=pltpu.SMEM)
    qblk =lambda w: pl.BlockSpec((None, t, w), lambda b, i: (b, i, 0))
    full = lambda w: pl.BlockSpec((None, seq, w), lambda b, i: (b, 0, 0))
    return pl.pallas_call(
        functools.partial(_dsa_kernel, topk=topk, idx_bits=max(1, (seq - 1).bit_length())),
        grid=(bsz, seq // t),
        in_specs=[
            smem,
            qblk(IDX_HEADS * IDX_DIM), qblk(LANES), full(IDX_HEADS * IDX_DIM),
            qblk(SA_WIDTH), full(SA_WIDTH), full(SA_WIDTH),
            pl.BlockSpec((SA_HEADS, 2, t, t), lambda b, i: (0, 0, 0, 0)),
        ],
        out_specs=qblk(SA_WIDTH),
        out_shape=jax.ShapeDtypeStruct((bsz, seq, SA_WIDTH), BF16),
        scratch_shapes=[
            pltpu.VMEM((IDX_HEADS * t, IDX_HEADS * IDX_DIM), BF16),
            pltpu.VMEM((IDX_HEADS, t, LANES), F32),
            pltpu.VMEM((pl.cdiv(seq // t, SA_WIDE) * SA_WIDE, t, LANES), I32),
            pltpu.VMEM((SA_HEADS * t, SA_WIDTH), BF16),
            pltpu.VMEM((SA_HEADS * t, LANES), F32),
            pltpu.VMEM((SA_HEADS * t, LANES), F32),
            pltpu.VMEM((SA_HEADS * t, LANES), F32),
            pltpu.VMEM((SA_HEADS * t, SA_WIDTH), F32),
        ],
        compiler_params=_params("parallel", "arbitrary"),
        name="sparse_attention",
    )(bias_far, ixq, ixw, ixk, saq, sak, sav, bias_near)


def _ssm_state_kernel(u_ref, w_ref, o_ref):
    o_ref[...] = _dot(u_ref[...].astype(BF16), w_ref[...])


def _ssm_scan_kernel(loc_ref, a1_ref, a2_ref, o_ref):
    a1 = a1_ref[...]
    a2 = a2_ref[...]

    def body(n, s):
        o_ref[n] = s
        return a1 * s + a2 * pltpu.roll(s, SSM_STATE, axis=1) + loc_ref[n]

    lax.fori_loop(0, loc_ref.shape[0], body, jnp.zeros(a1.shape, F32))


def _ssm_out_kernel(u_ref, s_ref, toep_ref, wout_ref, o_ref):
    o_ref[...] = (_dot(u_ref[...].astype(BF16), toep_ref[...])
                  + _dot(s_ref[...].astype(BF16), wout_ref[...]))


def _ssm_gate_kernel(y_ref, u_ref, d_ref, w_ref, o_ref):
    y = jax.nn.gelu(y_ref[...] + d_ref[...] * u_ref[...])
    o_ref[...] = (y * jax.nn.sigmoid(_dot(y.astype(BF16), w_ref[...]))).astype(o_ref.dtype)


def _ssm_tables(lam_re, lam_im, log_dt, b_re, b_im, c_re, c_im, tc):
    hp = lax.Precision.HIGHEST
    dt = jnp.exp(log_dt)[:, None]
    n = jnp.arange(tc + 1, dtype=F32)[:, None, None]
    mag = jnp.exp(lam_re * dt * n)
    pw_re = mag * jnp.cos(lam_im * dt * n)
    pw_im = mag * jnp.sin(lam_im * dt * n)
    den = lam_re * lam_re + lam_im * lam_im
    nr, ni = pw_re[1] - 1.0, pw_im[1]
    f_re = (nr * lam_re + ni * lam_im) / den
    f_im = (ni * lam_re - nr * lam_im) / den
    bb_re = f_re[..., None] * b_re - f_im[..., None] * b_im
    bb_im = f_re[..., None] * b_im + f_im[..., None] * b_re
    ca_re = c_re[None] * pw_re[:, :, None, :] - c_im[None] * pw_im[:, :, None, :]
    ca_im = c_re[None] * pw_im[:, :, None, :] + c_im[None] * pw_re[:, :, None, :]
    kern = (jnp.einsum('tgcp,gpd->tgcd', ca_re[:tc], bb_re, precision=hp)
            - jnp.einsum('tgcp,gpd->tgcd', ca_im[:tc], bb_im, precision=hp))
    s_idx = jnp.arange(tc)[:, None]
    t_idx = jnp.arange(tc)[None, :]
    tau = t_idx - s_idx
    toep = jnp.where((tau >= 0)[:, :, None, None, None], kern[jnp.maximum(tau, 0)], 0.0)
    groups = lam_re.shape[0]
    toep = toep.transpose(2, 0, 4, 1, 3).reshape(groups, tc * SSM_GROUP, tc * SSM_GROUP)
    rev_re, rev_im = pw_re[tc - 1::-1][:tc], pw_im[tc - 1::-1][:tc]
    ws_re = rev_re[..., None] * bb_re[None] - rev_im[..., None] * bb_im[None]
    ws_im = rev_re[..., None] * bb_im[None] + rev_im[..., None] * bb_re[None]
    wstate = jnp.concatenate([ws_re, ws_im], axis=2)
    wstate = wstate.transpose(1, 0, 3, 2).reshape(groups, tc * SSM_GROUP, 2 * SSM_STATE)
    wout = jnp.concatenate([ca_re[1:], -ca_im[1:]], axis=3)
    wout = wout.transpose(1, 3, 0, 2).reshape(groups, 2 * SSM_STATE, tc * SSM_GROUP)
    dec_re, dec_im = pw_re[tc], pw_im[tc]
    a1 = jnp.concatenate([dec_re, dec_re], axis=1)
    a2 = jnp.concatenate([-dec_im, dec_im], axis=1)
    return toep.astype(BF16), wstate.astype(BF16), wout.astype(BF16), a1, a2


def _ssm(u, tables, d_skip, w_glu, *, bsz, seq, tm):
    toep, wstate, wout, a1, a2 = tables
    groups = toep.shape[0]
    tc = SSM_CHUNK
    nc = seq // tc
    rows = bsz * nc
    cw = tc * SSM_GROUP
    ug = u.reshape(bsz, nc, tc, groups, SSM_GROUP).transpose(3, 0, 1, 2, 4).reshape(groups, rows, cw)
    gspec = lambda r, c: pl.BlockSpec((None, r, c), lambda g: (g, 0, 0))
    loc = pl.pallas_call(
        _ssm_state_kernel,
        grid=(groups,),
        in_specs=[gspec(rows, cw), gspec(cw, 2 * SSM_STATE)],
        out_specs=gspec(rows, 2 * SSM_STATE),
        out_shape=jax.ShapeDtypeStruct((groups, rows, 2 * SSM_STATE), F32),
        compiler_params=_params("parallel"),
        name="ssm_chunk_state",
    )(ug, wstate)
    gb = groups * bsz
    loc_t = loc.reshape(groups, bsz, nc, 2 * SSM_STATE).transpose(2, 0, 1, 3).reshape(nc, gb, 2 * SSM_STATE)
    a1r = jnp.repeat(a1, bsz, axis=0)
    a2r = jnp.repeat(a2, bsz, axis=0)
    rb = 8 if gb % 8 == 0 else gb
    prev = pl.pallas_call(
        _ssm_scan_kernel,
        grid=(gb // rb,),
        in_specs=[pl.BlockSpec((nc, rb, 2 * SSM_STATE), lambda i: (0, i, 0)),
                  pl.BlockSpec((rb, 2 * SSM_STATE), lambda i: (i, 0)),
                  pl.BlockSpec((rb, 2 * SSM_STATE), lambda i: (i, 0))],
        out_specs=pl.BlockSpec((nc, rb, 2 * SSM_STATE), lambda i: (0, i, 0)),
        out_shape=jax.ShapeDtypeStruct((nc, gb, 2 * SSM_STATE), F32),
        compiler_params=_params("parallel"),
        name="ssm_chunk_scan",
    )(loc_t, a1r, a2r)
    prev_g = prev.reshape(nc, groups, bsz, 2 * SSM_STATE).transpose(1, 2, 0, 3).reshape(groups, rows, 2 * SSM_STATE)
    y = pl.pallas_call(
        _ssm_out_kernel,
        grid=(groups,),
        in_specs=[gspec(rows, cw), gspec(rows, 2 * SSM_STATE), gspec(cw, cw), gspec(2 * SSM_STATE, cw)],
        out_specs=gspec(rows, cw),
        out_shape=jax.ShapeDtypeStruct((groups, rows, cw), F32),
        compiler_params=_params("parallel"),
        name="ssm_chunk_out",
    )(ug, prev_g, toep, wout)
    y = y.reshape(groups, bsz, nc, tc, SSM_GROUP).transpose(1, 2, 3, 0, 4).reshape(bsz * seq, groups * SSM_GROUP)
    width = groups * SSM_GROUP
    return pl.pallas_call(
        _ssm_gate_kernel,
        grid=(bsz * seq // tm,),
        in_specs=[pl.BlockSpec((tm, width), lambda i: (i, 0)),
                  pl.BlockSpec((tm, width), lambda i: (i, 0)),
                  _resident((1, width)), _resident((width, width))],
        out_specs=pl.BlockSpec((tm, width), lambda i: (i, 0)),
        out_shape=jax.ShapeDtypeStruct((bsz * seq, width), BF16),
        compiler_params=_params("parallel"),
        name="ssm_gate",
    )(y, u, d_skip, w_glu)


def _outproj_kernel(x_ref, da_ref, ssm_ref, sa_ref, w1, w2, w3, g_ref, b_ref, o_ref):
    mix = _dot(da_ref[...], w1[...]) + _dot(ssm_ref[...], w2[...]) + _dot(sa_ref[...], w3[...])
    o_ref[...] = _layer_norm(ALPHA * x_ref[...] + mix, g_ref[...], b_ref[...])


def _outproj(x, o_da, o_ssm, o_sa, w1, w2, w3, g, b, *, tm):
    t_rows, d = x.shape
    row = lambda w: pl.BlockSpec((tm, w), lambda i: (i, 0))
    return pl.pallas_call(
        _outproj_kernel,
        grid=(t_rows // tm,),
        in_specs=[row(d), row(o_da.shape[1]), row(o_ssm.shape[1]), row(o_sa.shape[1]),
                  _resident(w1.shape), _resident(w2.shape), _resident(w3.shape),
                  _resident(g.shape), _resident(b.shape)],
        out_specs=row(d),
        out_shape=jax.ShapeDtypeStruct((t_rows, d), F32),
        compiler_params=_params("parallel"),
        name="out_proj_ln",
    )(x, o_da, o_ssm, o_sa, w1, w2, w3, g, b)


def _t5_bucket(n):
    max_exact = REL_BUCKETS // 2
    nf = jnp.maximum(n, 1).astype(F32)
    large = max_exact + (jnp.log(nf / max_exact) / math.log(REL_MAX_DIST / max_exact)
                         * (REL_BUCKETS - max_exact)).astype(I32)
    large = jnp.minimum(large, REL_BUCKETS - 1)
    return jnp.where(n < max_exact, n, large)


def _bias_tiles(table, t):
    assert t >= REL_MAX_DIST
    r = jnp.arange(t)[:, None]
    c = jnp.arange(t)[None, :]
    dist = jnp.stack([r - c, t + r - c])
    tiles = table[_t5_bucket(jnp.maximum(dist, 0))]
    tiles = jnp.where((dist >= 0)[..., None], tiles, NEG)
    return tiles.transpose(3, 0, 1, 2).astype(F32), table[REL_BUCKETS - 1].astype(F32)


def _split_w_in(w_in):
    offs = [0]
    for s in IN_SIZES:
        offs.append(offs[-1] + s)
    da_q, da_k, da_v, ssm_u, sa_q, sa_k, sa_v, ix_q, ix_k, ix_w = (
        w_in[:, offs[j]:offs[j + 1]] for j in range(len(IN_SIZES)))
    w_scale = IDX_HEADS ** -0.5 * IDX_DIM ** -0.5
    ix_w = jnp.pad(ix_w * w_scale, ((0, 0), (0, LANES - IDX_HEADS)))
    weights = [da_q * DA_QK_DIM ** -0.5, da_k, da_v, ssm_u,
               sa_q * SA_HEAD_DIM ** -0.5, jnp.tile(sa_k, (1, SA_HEADS)), jnp.tile(sa_v, (1, SA_HEADS)),
               ix_q, jnp.tile(ix_k, (1, IDX_HEADS)), ix_w]
    dtypes = [BF16, BF16, BF16, F32, BF16, BF16, BF16, BF16, BF16, F32]
    return [w.astype(BF16) for w in weights], dtypes


def _plan(bsz, seq):
    rows = bsz * seq
    tm = 512 if rows % 512 == 0 else rows
    t_da = 512 if seq % 512 == 0 and seq >= 2048 else 256 if seq % 256 == 0 else 128
    return dict(tm=tm, t_da=t_da, t_sa=128)


def kernel(x, p, rel_bias, ffn1_w_gate, ffn1_w_up, ffn1_w_down, ln1_g, ln1_b, w_in, w_o, da_lam_q1, da_lam_k1, da_lam_q2, da_lam_k2, da_subln_g, ssm_lam_re, ssm_lam_im, ssm_log_dt, ssm_b_re, ssm_b_im, ssm_c_re, ssm_c_im, ssm_d, ssm_w_glu, ln2_g, ln2_b, ffn2_w_gate, ffn2_w_up, ffn2_w_down, ple_w_proj, ple_w_gate, ln3_g, ln3_b):
    bsz, seq, d = x.shape
    rows = bsz * seq
    plan = _plan(bsz, seq)
    tm, t_da, t_sa = plan["tm"], plan["t_da"], plan["t_sa"]
    topk = min(TOPK_MAX, seq // 4)
    da_near, da_far = _bias_tiles(rel_bias[:, :DA_HEADS], t_da)
    sa_near, sa_far = _bias_tiles(rel_bias[:, DA_HEADS:], t_sa)
    vec = lambda a: a.reshape(1, -1).astype(F32)
    da_w = DA_HEADS * DA_V_DIM

    h = x.reshape(rows, d)
    for i in range(DEPTH):
        lam_init = 0.8 - 0.6 * math.exp(-0.3 * i)
        h = _ffn(h, ffn1_w_gate[i].astype(BF16), ffn1_w_up[i].astype(BF16), ffn1_w_down[i].astype(BF16),
                 vec(ln1_g[i]), vec(ln1_b[i]), tm=tm)

        weights, dtypes = _split_w_in(w_in[i])
        (da_q, da_k, da_v, ssm_u, sa_q, sa_k, sa_v, ix_q, ix_k, ix_w) = _inproj(h, weights, dtypes, tm=tm)
        b3 = lambda a: a.reshape(bsz, seq, a.shape[-1])

        lam = (jnp.exp(jnp.sum(da_lam_q1[i].astype(F32) * da_lam_k1[i]))
               - jnp.exp(jnp.sum(da_lam_q2[i].astype(F32) * da_lam_k2[i])) + lam_init)
        o_da = _diff_attention(b3(da_q), b3(da_k), b3(da_v), da_near, da_far, lam.reshape(1).astype(F32),
                               vec(da_subln_g[i]), t=t_da, post_scale=1.0 - lam_init)

        tables = _ssm_tables(ssm_lam_re[i].astype(F32), ssm_lam_im[i].astype(F32), ssm_log_dt[i].astype(F32),
                             ssm_b_re[i].astype(F32), ssm_b_im[i].astype(F32),
                             ssm_c_re[i].astype(F32), ssm_c_im[i].astype(F32), SSM_CHUNK)
        o_ssm = _ssm(ssm_u, tables, vec(ssm_d[i]), ssm_w_glu[i].astype(BF16), bsz=bsz, seq=seq, tm=tm)

        o_sa = _sparse_attention(b3(ix_q), b3(ix_w), b3(ix_k), b3(sa_q), b3(sa_k), b3(sa_v),
                                 sa_near, sa_far, t=t_sa, topk=topk)

        wo = w_o[i].astype(BF16)
        h = _outproj(h, o_da.reshape(rows, da_w), o_ssm, o_sa.reshape(rows, SA_WIDTH),
                     wo[:da_w], wo[da_w:da_w + SSM_WIDTH], wo[da_w + SSM_WIDTH:],
                     vec(ln2_g[i]), vec(ln2_b[i]), tm=tm)

        h = _ffn(h, ffn2_w_gate[i].astype(BF16), ffn2_w_up[i].astype(BF16), ffn2_w_down[i].astype(BF16),
                 vec(ln3_g[i]), vec(ln3_b[i]),
                 ple=(p[i].reshape(rows, -1), ple_w_proj[i].astype(BF16), ple_w_gate[i].astype(BF16)), tm=tm)
    return h.reshape(bsz, seq, d)
```

```python
import functools
import math

import jax
import jax.numpy as jnp
from jax import lax
from jax.experimental import pallas as pl
from jax.experimental.pallas import tpu as pltpu

F32 = jnp.float32
BF16 = jnp.bfloat16
I32 = jnp.int32

DEPTH = 2
DA_QK_DIM = 64
DA_V_DIM = 2 * DA_QK_DIM
DA_HEADS = 4
SSM_GROUP = 16
SSM_GROUPS = 16
SSM_STATE = 64
SSM_WIDTH = SSM_GROUP * SSM_GROUPS
SA_HEAD_DIM = 64
SA_HEADS = 4
SA_WIDTH = SA_HEADS * SA_HEAD_DIM
IDX_HEADS = 8
IDX_DIM = 32
TOPK_MAX = 256
REL_BUCKETS = 32
REL_MAX_DIST = 128
ALPHA = (2 * DEPTH) ** 0.25
LN_EPS = 1e-5
IN_SIZES = (DA_HEADS * 2 * DA_QK_DIM, DA_HEADS * 2 * DA_QK_DIM, DA_HEADS * DA_V_DIM,
            SSM_WIDTH, SA_WIDTH, SA_HEAD_DIM, SA_HEAD_DIM,
            IDX_HEADS * IDX_DIM, IDX_DIM, IDX_HEADS)

LANES = 128
SUBLANES = 8
BF16_ROWS = 16
NEG = -1e30
INT_MIN = -2 ** 31
LOG2E = math.log2(math.e)
VMEM_LIMIT = 56 * 1024 * 1024
SSM_CHUNK = 64
DA_COLS = 1024
SA_GROUP = 4


def _params(*sem):
    return pltpu.CompilerParams(dimension_semantics=sem, vmem_limit_bytes=VMEM_LIMIT)


def _resident(shape):
    return pl.BlockSpec(shape, lambda *_: (0,) * len(shape), pipeline_mode=pl.Buffered(1))


def _dot(a, b):
    return jnp.dot(a, b, preferred_element_type=F32)


def _dot_nt(a, b):
    return lax.dot_general(a, b, (((1,), (1,)), ((), ())), preferred_element_type=F32)


def _layer_norm(y, g, b):
    mu = jnp.mean(y, axis=-1, keepdims=True)
    d = y - mu
    var = jnp.mean(d * d, axis=-1, keepdims=True)
    return d * lax.rsqrt(var + LN_EPS) * g + b


def _fold_rows(x):
    while x.shape[0] > SUBLANES:
        half = x.shape[0] // 2
        x = x[:half] + x[half:]
    return x


def _ffn_kernel(*refs, has_ple, ff_chunk):
    if has_ple:
        x_ref, p_ref, wg, wu, wd, wpp, wpg, g_ref, b_ref, o_ref = refs
    else:
        x_ref, wg, wu, wd, g_ref, b_ref, o_ref = refs
    x = x_ref[...]
    xb = x.astype(BF16)
    d_ff = wg.shape[1]
    acc = None
    for c0 in range(0, d_ff, ff_chunk):
        c1 = min(c0 + ff_chunk, d_ff)
        gate = _dot(xb, wg[:, c0:c1])
        up = _dot(xb, wu[:, c0:c1])
        hid = (gate * jax.nn.sigmoid(gate) * up).astype(BF16)
        part = _dot(hid, wd[c0:c1, :])
        acc = part if acc is None else acc + part
    y = ALPHA * x + 0.5 * acc
    if has_ple:
        y = y + _dot(p_ref[...].astype(BF16), wpp[...]) * jax.nn.sigmoid(_dot(xb, wpg[...]))
    o_ref[...] = _layer_norm(y, g_ref[...], b_ref[...])


def _ffn(x, wg, wu, wd, g, b, ple=None, *, tm):
    t_rows, d = x.shape
    d_ff = wg.shape[1]
    row = lambda w: pl.BlockSpec((tm, w), lambda i: (i, 0))
    args = [x]
    specs = [row(d)]
    if ple is not None:
        p, wpp, wpg = ple
        args.append(p)
        specs.append(row(p.shape[1]))
    args += [wg, wu, wd]
    specs += [_resident(wg.shape), _resident(wu.shape), _resident(wd.shape)]
    if ple is not None:
        args += [wpp, wpg]
        specs += [_resident(wpp.shape), _resident(wpg.shape)]
    args += [g, b]
    specs += [_resident(g.shape), _resident(b.shape)]
    return pl.pallas_call(
        functools.partial(_ffn_kernel, has_ple=ple is not None, ff_chunk=min(512, d_ff)),
        grid=(t_rows // tm,),
        in_specs=specs,
        out_specs=row(d),
        out_shape=jax.ShapeDtypeStruct((t_rows, d), F32),
        compiler_params=_params("parallel"),
        name="ffn_ple_ln" if ple is not None else "ffn_ln",
    )(*args)


def _inproj_kernel(x_ref, *refs, transposed):
    n = len(refs) // 2
    xb = x_ref[...].astype(BF16)
    for w_ref, o_ref, tr in zip(refs[:n], refs[n:], transposed):
        out = _dot_nt(w_ref[...], xb) if tr else _dot(xb, w_ref[...])
        o_ref[...] = out.astype(o_ref.dtype)


def _inproj(x, weights, dtypes, transposed, *, tm):
    t_rows, d = x.shape
    out_specs, out_shape = [], []
    for w, dt, tr in zip(weights, dtypes, transposed):
        if tr:
            out_specs.append(pl.BlockSpec((w.shape[0], tm), lambda i: (0, i)))
            out_shape.append(jax.ShapeDtypeStruct((w.shape[0], t_rows), dt))
        else:
            out_specs.append(pl.BlockSpec((tm, w.shape[1]), lambda i: (i, 0)))
            out_shape.append(jax.ShapeDtypeStruct((t_rows, w.shape[1]), dt))
    return pl.pallas_call(
        functools.partial(_inproj_kernel, transposed=tuple(transposed)),
        grid=(t_rows // tm,),
        in_specs=[pl.BlockSpec((tm, d), lambda i: (i, 0))] + [_resident(w.shape) for w in weights],
        out_specs=out_specs,
        out_shape=out_shape,
        compiler_params=_params("parallel"),
        name="in_proj",
    )(x, *weights)


def _softmax_step(s, v_aug, m_ref, acc_ref, cols, shift):
    m_prev = m_ref[:, cols]
    m_cur = jnp.max(s, axis=0, keepdims=True)
    if shift is not None:
        m_cur = m_cur + shift
    m_next = jnp.maximum(m_prev, m_cur)
    p = jnp.exp2(s - (m_next if shift is None else m_next - shift))
    acc_ref[:, cols] = jnp.exp2(m_prev - m_next) * acc_ref[:, cols] + _dot(v_aug, p.astype(BF16))
    m_ref[:, cols] = m_next


def _da_kernel(far_ref, lam_ref, q_ref, k_ref, v_ref, bn_ref, g_ref, o_ref,
               qd_ref, m_ref, acc_ref, *, t, post_scale):
    h = pl.program_id(1)
    qi = pl.program_id(2)
    q = q_ref[...].astype(F32)
    rowq = lax.broadcasted_iota(I32, q.shape, 0)
    qd_ref[:, 0:t] = jnp.where(rowq < DA_QK_DIM, q, 0.0).astype(BF16)
    qd_ref[:, t:2 * t] = jnp.where(rowq >= DA_QK_DIM, q, 0.0).astype(BF16)
    m_ref[...] = jnp.full(m_ref.shape, NEG, F32)
    acc_ref[...] = jnp.zeros(acc_ref.shape, F32)
    far = far_ref[h]
    cw = min(2 * t, DA_COLS)

    def block(kb, near):
        kblk = k_ref[pl.ds(pl.multiple_of(kb * t, t), t), :]
        v_aug = v_ref[kb]
        for c0 in range(0, 2 * t, cw):
            cols = slice(c0, c0 + cw)
            s = _dot(kblk, qd_ref[:, cols])
            if near is not None:
                if cw > t:
                    s = s + jnp.concatenate([bn_ref[near], bn_ref[near]], axis=1)
                else:
                    s = s + bn_ref[near, :, c0 % t:c0 % t + cw]
            _softmax_step(s, v_aug, m_ref, acc_ref, cols, far if near is None else None)

    def far_body(kb, carry):
        block(kb, None)
        return carry

    lax.fori_loop(0, jnp.maximum(qi - 1, 0), far_body, 0)

    @pl.when(qi >= 1)
    def _():
        block(qi - 1, 1)

    block(qi, 0)

    acc = acc_ref[...]
    o = acc[0:DA_V_DIM] / acc[DA_V_DIM:DA_V_DIM + 1]
    o = o[:, 0:t] - lam_ref[0] * o[:, t:2 * t]
    ms = jnp.mean(o * o, axis=0, keepdims=True)
    o = o * lax.rsqrt(ms + LN_EPS) * g_ref[...] * post_scale
    o_ref[...] = o.T.astype(o_ref.dtype)


def _diff_attention(q_t, k, v_aug, bias_near, bias_far, lam, subln_g, *, bsz, seq, t, post_scale):
    heads = q_t.shape[0] // DA_V_DIM
    nq = seq // t
    smem = pl.BlockSpec(memory_space=pltpu.SMEM)
    va = DA_V_DIM + BF16_ROWS
    return pl.pallas_call(
        functools.partial(_da_kernel, t=t, post_scale=post_scale),
        grid=(bsz, heads, nq),
        in_specs=[
            smem, smem,
            pl.BlockSpec((DA_V_DIM, t), lambda b, h, i: (h, b * nq + i)),
            pl.BlockSpec((None, seq, DA_V_DIM), lambda b, h, i: (b, 0, h)),
            pl.BlockSpec((None, None, nq, va, t), lambda b, h, i: (b, h, 0, 0, 0)),
            pl.BlockSpec((None, 2, t, t), lambda b, h, i: (h, 0, 0, 0)),
            pl.BlockSpec((DA_V_DIM, t), lambda b, h, i: (0, 0)),
        ],
        out_specs=pl.BlockSpec((None, t, DA_V_DIM), lambda b, h, i: (b, i, h)),
        out_shape=jax.ShapeDtypeStruct((bsz, seq, heads * DA_V_DIM), BF16),
        scratch_shapes=[
            pltpu.VMEM((DA_V_DIM, 2 * t), BF16),
            pltpu.VMEM((1, 2 * t), F32),
            pltpu.VMEM((va, 2 * t), F32),
        ],
        compiler_params=_params("parallel", "parallel", "arbitrary"),
        name="diff_attention",
    )(bias_far, lam, q_t, k, v_aug, bias_near, subln_g)


def _dsa_kernel(far_ref, ixq_ref, ixw_ref, ixk_ref, saq_ref, sak_ref, sav_ref, bn_ref, o_ref,
                qi8_ref, keys_ref, qs4_ref, m_ref, acc_ref, *, t, topk, idx_bits):
    qi = pl.program_id(1)
    kf = float(topk)

    qi8_ref[...] = jnp.zeros(qi8_ref.shape, BF16)
    for h in range(IDX_HEADS):
        qi8_ref[0:IDX_DIM, h * t:(h + 1) * t] = ixq_ref[h * IDX_DIM:(h + 1) * IDX_DIM, :]

    def index_keys(kb):
        kblk = ixk_ref[pl.ds(pl.multiple_of(kb * t, t), t), :]
        sc = None
        for h in range(IDX_HEADS):
            r = _dot(kblk, qi8_ref[:, h * t:(h + 1) * t])
            term = ixw_ref[h:h + 1, :] * jnp.maximum(r, 0.0)
            sc = term if sc is None else sc + term
        sc = jnp.where(sc == 0.0, 0.0, sc)
        bits = lax.bitcast_convert_type(sc, I32)
        return bits ^ ((bits >> 31) & 0x7FFFFFFF)

    def chunk(kb):
        return keys_ref[pl.ds(pl.multiple_of(kb * t, t), t), :]

    def fill(kb, carry):
        keys_ref[pl.ds(pl.multiple_of(kb * t, t), t), :] = index_keys(kb)
        return carry

    lax.fori_loop(0, qi, fill, 0)
    krow = lax.broadcasted_iota(I32, (t, t), 0)
    qcol = lax.broadcasted_iota(I32, (t, t), 1)
    keys_ref[pl.ds(pl.multiple_of(qi * t, t), t), :] = jnp.where(krow <= qcol, index_keys(qi), INT_MIN)

    def count(fn):
        def body(kb, cnt):
            return cnt + _fold_rows(fn(kb, chunk(kb)))
        cnt = lax.fori_loop(0, qi + 1, body, jnp.zeros((SUBLANES, t), I32))
        return jnp.sum(cnt.astype(F32), axis=0, keepdims=True)

    def search_cond(st):
        i, _, _, n_open = st
        return jnp.logical_and(i < 32, n_open > 0.0)

    def search_body(st):
        i, tau, open_, _ = st
        cand = tau + jnp.left_shift(jnp.int32(1), 31 - i)
        c = count(lambda kb, blk: jnp.where(blk >= cand, 1, 0))
        tau = jnp.where(open_ > 0.0, jnp.where(c >= kf, cand, tau), tau)
        open_ = jnp.where(c == kf, 0.0, open_)
        return i + 1, tau, open_, jnp.max(open_)

    _, tau, _, _ = lax.while_loop(
        search_cond, search_body,
        (jnp.int32(0), jnp.full((1, t), INT_MIN, I32), jnp.ones((1, t), F32), jnp.float32(1.0)))
    c_ge = count(lambda kb, blk: jnp.where(blk >= tau, 1, 0))

    @pl.when(jnp.max(c_ge) > kf)
    def _():
        need = kf - count(lambda kb, blk: jnp.where(blk > tau, 1, 0))

        def j_bit(i, ju):
            cand = ju | jnp.left_shift(jnp.int32(1), idx_bits - 1 - i)
            c = count(lambda kb, blk: jnp.where(blk == tau, jnp.where((krow + kb * t) < cand, 1, 0), 0))
            return jnp.where(c < need, cand, ju)

        ju = lax.fori_loop(0, idx_bits, j_bit, jnp.zeros((1, t), I32))

        def retire(kb, carry):
            blk = chunk(kb)
            late = jnp.where((krow + kb * t) > ju, INT_MIN, blk)
            keys_ref[pl.ds(pl.multiple_of(kb * t, t), t), :] = jnp.where(blk == tau, late, blk)
            return carry

        lax.fori_loop(0, qi + 1, retire, 0)

    qs4_ref[...] = jnp.zeros(qs4_ref.shape, BF16)
    for h in range(SA_HEADS):
        qs4_ref[0:SA_HEAD_DIM, h * t:(h + 1) * t] = saq_ref[h * SA_HEAD_DIM:(h + 1) * SA_HEAD_DIM, :]
    m_ref[...] = jnp.full(m_ref.shape, NEG, F32)
    acc_ref[...] = jnp.zeros(acc_ref.shape, F32)

    def attend(kb, near):
        kblk = sak_ref[pl.ds(pl.multiple_of(kb * t, t), t), :]
        v_aug = sav_ref[kb]
        sel = chunk(kb) >= tau
        for h0 in range(0, SA_HEADS, SA_GROUP):
            heads = range(h0, h0 + SA_GROUP)
            cols = slice(h0 * t, (h0 + SA_GROUP) * t)
            s = _dot(kblk, qs4_ref[:, cols])
            parts = []
            for j, h in enumerate(heads):
                sh = s[:, j * t:(j + 1) * t]
                if near is not None:
                    sh = sh + bn_ref[h, near]
                parts.append(jnp.where(sel, sh, NEG))
            shift = None
            if near is None:
                shift = jnp.concatenate([jnp.full((1, t), far_ref[h], F32) for h in heads], axis=1)
            _softmax_step(parts[0] if SA_GROUP == 1 else jnp.concatenate(parts, axis=1),
                          v_aug, m_ref, acc_ref, cols, shift)

    def far_body(kb, carry):
        attend(kb, None)
        return carry

    lax.fori_loop(0, jnp.maximum(qi - 1, 0), far_body, 0)

    @pl.when(qi >= 1)
    def _():
        attend(qi - 1, 1)

    attend(qi, 0)

    acc = acc_ref[...]
    o = acc[0:SA_HEAD_DIM] / acc[SA_HEAD_DIM:SA_HEAD_DIM + 1]
    o = jnp.concatenate([o[:, h * t:(h + 1) * t] for h in range(SA_HEADS)], axis=0)
    o_ref[...] = o.T.astype(o_ref.dtype)


def _sparse_attention(ixq_t, ixw_t, ixk, saq_t, sak, sav_aug, bias_near, bias_far, *, bsz, seq, t, topk):
    nq = seq // t
    smem = pl.BlockSpec(memory_space=pltpu.SMEM)
    qcols = lambda r: pl.BlockSpec((r, t), lambda b, i: (0, b * nq + i))
    full = lambda w: pl.BlockSpec((None, seq, w), lambda b, i: (b, 0, 0))
    va = SA_HEAD_DIM + BF16_ROWS
    return pl.pallas_call(
        functools.partial(_dsa_kernel, t=t, topk=topk, idx_bits=max(1, (seq - 1).bit_length())),
        grid=(bsz, nq),
        in_specs=[
            smem,
            qcols(IDX_HEADS * IDX_DIM), qcols(IDX_HEADS), full(LANES),
            qcols(SA_WIDTH), full(LANES),
            pl.BlockSpec((None, nq, va, t), lambda b, i: (b, 0, 0, 0)),
            _resident((SA_HEADS, 2, t, t)),
        ],
        out_specs=pl.BlockSpec((None, t, SA_WIDTH), lambda b, i: (b, i, 0)),
        out_shape=jax.ShapeDtypeStruct((bsz, seq, SA_WIDTH), BF16),
        scratch_shapes=[
            pltpu.VMEM((LANES, IDX_HEADS * t), BF16),
            pltpu.VMEM((seq, t), I32),
            pltpu.VMEM((LANES, SA_HEADS * t), BF16),
            pltpu.VMEM((1, SA_HEADS * t), F32),
            pltpu.VMEM((va, SA_HEADS * t), F32),
        ],
        compiler_params=_params("parallel", "arbitrary"),
        name="sparse_attention",
    )(bias_far, ixq_t, ixw_t, ixk, saq_t, sak, sav_aug, bias_near)


def _ssm_state_kernel(u_ref, w_ref, o_ref):
    o_ref[...] = _dot(u_ref[...].astype(BF16), w_ref[...])


def _ssm_scan_kernel(loc_ref, a1_ref, a2_ref, o_ref):
    a1 = a1_ref[...]
    a2 = a2_ref[...]

    def body(n, s):
        o_ref[n] = s
        return a1 * s + a2 * pltpu.roll(s, SSM_STATE, axis=1) + loc_ref[n]

    lax.fori_loop(0, loc_ref.shape[0], body, jnp.zeros(a1.shape, F32))


def _ssm_out_kernel(u_ref, s_ref, toep_ref, wout_ref, o_ref):
    o_ref[...] = (_dot(u_ref[...].astype(BF16), toep_ref[...])
                  + _dot(s_ref[...].astype(BF16), wout_ref[...]))


def _ssm_gate_kernel(y_ref, u_ref, d_ref, w_ref, o_ref):
    y = jax.nn.gelu(y_ref[...] + d_ref[...] * u_ref[...])
    o_ref[...] = (y * jax.nn.sigmoid(_dot(y.astype(BF16), w_ref[...]))).astype(o_ref.dtype)


def _ssm_tables(lam_re, lam_im, log_dt, b_re, b_im, c_re, c_im, tc):
    hp = lax.Precision.HIGHEST
    dt = jnp.exp(log_dt)[:, None]
    n = jnp.arange(tc + 1, dtype=F32)[:, None, None]
    mag = jnp.exp(lam_re * dt * n)
    pw_re = mag * jnp.cos(lam_im * dt * n)
    pw_im = mag * jnp.sin(lam_im * dt * n)
    den = lam_re * lam_re + lam_im * lam_im
    nr, ni = pw_re[1] - 1.0, pw_im[1]
    f_re = (nr * lam_re + ni * lam_im) / den
    f_im = (ni * lam_re - nr * lam_im) / den
    bb_re = f_re[..., None] * b_re - f_im[..., None] * b_im
    bb_im = f_re[..., None] * b_im + f_im[..., None] * b_re
    ca_re = c_re[None] * pw_re[:, :, None, :] - c_im[None] * pw_im[:, :, None, :]
    ca_im = c_re[None] * pw_im[:, :, None, :] + c_im[None] * pw_re[:, :, None, :]
    kern = (jnp.einsum('tgcp,gpd->tgcd', ca_re[:tc], bb_re, precision=hp)
            - jnp.einsum('tgcp,gpd->tgcd', ca_im[:tc], bb_im, precision=hp))
    groups = lam_re.shape[0]
    toep = _toeplitz(jnp.concatenate([kern, jnp.zeros_like(kern)], axis=0), tc)
    toep = toep.transpose(2, 0, 4, 1, 3).reshape(groups, tc * SSM_GROUP, tc * SSM_GROUP)
    rev_re, rev_im = pw_re[tc - 1::-1][:tc], pw_im[tc - 1::-1][:tc]
    ws_re = rev_re[..., None] * bb_re[None] - rev_im[..., None] * bb_im[None]
    ws_im = rev_re[..., None] * bb_im[None] + rev_im[..., None] * bb_re[None]
    wstate = jnp.concatenate([ws_re, ws_im], axis=2)
    wstate = wstate.transpose(1, 0, 3, 2).reshape(groups, tc * SSM_GROUP, 2 * SSM_STATE)
    wout = jnp.concatenate([ca_re[1:], -ca_im[1:]], axis=3)
    wout = wout.transpose(1, 3, 0, 2).reshape(groups, 2 * SSM_STATE, tc * SSM_GROUP)
    dec_re, dec_im = pw_re[tc], pw_im[tc]
    a1 = jnp.concatenate([dec_re, dec_re], axis=1)
    a2 = jnp.concatenate([-dec_im, dec_im], axis=1)
    return toep.astype(BF16), wstate.astype(BF16), wout.astype(BF16), a1, a2


def _ssm(u, tables, d_skip, w_glu, *, bsz, seq, tm):
    toep, wstate, wout, a1, a2 = tables
    groups = toep.shape[0]
    tc = SSM_CHUNK
    nc = seq // tc
    rows = bsz * nc
    cw = tc * SSM_GROUP
    ug = u.reshape(bsz, nc, tc, groups, SSM_GROUP).transpose(3, 0, 1, 2, 4).reshape(groups, rows, cw)
    gspec = lambda r, c: pl.BlockSpec((None, r, c), lambda g: (g, 0, 0))
    loc = pl.pallas_call(
        _ssm_state_kernel,
        grid=(groups,),
        in_specs=[gspec(rows, cw), gspec(cw, 2 * SSM_STATE)],
        out_specs=gspec(rows, 2 * SSM_STATE),
        out_shape=jax.ShapeDtypeStruct((groups, rows, 2 * SSM_STATE), F32),
        compiler_params=_params("parallel"),
        name="ssm_chunk_state",
    )(ug, wstate)
    gb = groups * bsz
    loc_t = loc.reshape(groups, bsz, nc, 2 * SSM_STATE).transpose(2, 0, 1, 3).reshape(nc, gb, 2 * SSM_STATE)
    a1r = jnp.repeat(a1, bsz, axis=0)
    a2r = jnp.repeat(a2, bsz, axis=0)
    rb = 8 if gb % 8 == 0 else gb
    prev = pl.pallas_call(
        _ssm_scan_kernel,
        grid=(gb // rb,),
        in_specs=[pl.BlockSpec((nc, rb, 2 * SSM_STATE), lambda i: (0, i, 0)),
                  pl.BlockSpec((rb, 2 * SSM_STATE), lambda i: (i, 0)),
                  pl.BlockSpec((rb, 2 * SSM_STATE), lambda i: (i, 0))],
        out_specs=pl.BlockSpec((nc, rb, 2 * SSM_STATE), lambda i: (0, i, 0)),
        out_shape=jax.ShapeDtypeStruct((nc, gb, 2 * SSM_STATE), F32),
        compiler_params=_params("parallel"),
        name="ssm_chunk_scan",
    )(loc_t, a1r, a2r)
    prev_g = prev.reshape(nc, groups, bsz, 2 * SSM_STATE).transpose(1, 2, 0, 3).reshape(groups, rows, 2 * SSM_STATE)
    y = pl.pallas_call(
        _ssm_out_kernel,
        grid=(groups,),
        in_specs=[gspec(rows, cw), gspec(rows, 2 * SSM_STATE), gspec(cw, cw), gspec(2 * SSM_STATE, cw)],
        out_specs=gspec(rows, cw),
        out_shape=jax.ShapeDtypeStruct((groups, rows, cw), F32),
        compiler_params=_params("parallel"),
        name="ssm_chunk_out",
    )(ug, prev_g, toep, wout)
    y = y.reshape(groups, bsz, nc, tc, SSM_GROUP).transpose(1, 2, 3, 0, 4).reshape(bsz * seq, groups * SSM_GROUP)
    width = groups * SSM_GROUP
    return pl.pallas_call(
        _ssm_gate_kernel,
        grid=(bsz * seq // tm,),
        in_specs=[pl.BlockSpec((tm, width), lambda i: (i, 0)),
                  pl.BlockSpec((tm, width), lambda i: (i, 0)),
                  _resident((1, width)), _resident((width, width))],
        out_specs=pl.BlockSpec((tm, width), lambda i: (i, 0)),
        out_shape=jax.ShapeDtypeStruct((bsz * seq, width), BF16),
        compiler_params=_params("parallel"),
        name="ssm_gate",
    )(y, u, d_skip, w_glu)


def _outproj_kernel(x_ref, da_ref, ssm_ref, sa_ref, w1, w2, w3, g_ref, b_ref, o_ref):
    mix = _dot(da_ref[...], w1[...]) + _dot(ssm_ref[...], w2[...]) + _dot(sa_ref[...], w3[...])
    o_ref[...] = _layer_norm(ALPHA * x_ref[...] + mix, g_ref[...], b_ref[...])


def _outproj(x, o_da, o_ssm, o_sa, w1, w2, w3, g, b, *, tm):
    t_rows, d = x.shape
    row = lambda w: pl.BlockSpec((tm, w), lambda i: (i, 0))
    return pl.pallas_call(
        _outproj_kernel,
        grid=(t_rows // tm,),
        in_specs=[row(d), row(o_da.shape[1]), row(o_ssm.shape[1]), row(o_sa.shape[1]),
                  _resident(w1.shape), _resident(w2.shape), _resident(w3.shape),
                  _resident(g.shape), _resident(b.shape)],
        out_specs=row(d),
        out_shape=jax.ShapeDtypeStruct((t_rows, d), F32),
        compiler_params=_params("parallel"),
        name="out_proj_ln",
    )(x, o_da, o_ssm, o_sa, w1, w2, w3, g, b)


def _toeplitz(w, n):
    m = w.shape[0]
    flat = jnp.tile(w, (n,) + (1,) * (w.ndim - 1))[:n * (m - 1)]
    return flat.reshape((n, m - 1) + w.shape[1:])[:, :n]


def _t5_bucket(n):
    max_exact = REL_BUCKETS // 2
    nf = jnp.maximum(n, 1).astype(F32)
    large = max_exact + (jnp.log(nf / max_exact) / math.log(REL_MAX_DIST / max_exact)
                         * (REL_BUCKETS - max_exact)).astype(I32)
    large = jnp.minimum(large, REL_BUCKETS - 1)
    return jnp.where(n < max_exact, n, large)


def _bias_tiles(table, t):
    assert t >= REL_MAX_DIST
    by_dist = table[_t5_bucket(jnp.arange(2 * t))].astype(F32) * LOG2E
    ahead = jnp.full((t, table.shape[1]), NEG, F32)
    tile0 = _toeplitz(jnp.concatenate([by_dist[:t], ahead], axis=0), t)
    tile1 = _toeplitz(jnp.concatenate([by_dist[t:], by_dist[:t]], axis=0), t)
    tiles = jnp.stack([tile0, tile1]).transpose(3, 0, 1, 2)
    return tiles, table[REL_BUCKETS - 1].astype(F32) * LOG2E


def _split_w_in(w_in):
    offs = [0]
    for s in IN_SIZES:
        offs.append(offs[-1] + s)
    da_q, da_k, da_v, ssm_u, sa_q, sa_k, sa_v, ix_q, ix_k, ix_w = (
        w_in[:, offs[j]:offs[j + 1]] for j in range(len(IN_SIZES)))
    w_scale = IDX_HEADS ** -0.5 * IDX_DIM ** -0.5
    pad = lambda w: jnp.pad(w, ((0, 0), (0, LANES - w.shape[1])))
    weights = [(da_q * (DA_QK_DIM ** -0.5 * LOG2E)).T, da_k, da_v.T, ssm_u,
               (sa_q * (SA_HEAD_DIM ** -0.5 * LOG2E)).T, pad(sa_k), sa_v.T,
               ix_q.T, pad(ix_k), (ix_w * w_scale).T]
    dtypes = [BF16, BF16, BF16, F32, BF16, BF16, BF16, BF16, BF16, F32]
    transposed = [True, False, True, False, True, False, True, True, False, True]
    return [w.astype(BF16) for w in weights], dtypes, transposed


def _with_ones(v_t, rows, bsz, seq, t):
    g = v_t.shape[0] // rows
    v = v_t.reshape(g, rows, bsz, seq // t, t).transpose(2, 0, 3, 1, 4)
    ones = jnp.ones(v.shape[:3] + (BF16_ROWS, t), v.dtype)
    return jnp.concatenate([v, ones], axis=3)


def _plan(bsz, seq):
    rows = bsz * seq
    tm = 512 if rows % 512 == 0 else rows
    t_da = 512 if seq % 512 == 0 and seq >= 2048 else 256 if seq % 256 == 0 else 128
    t_sa = 256 if seq % 256 == 0 else 128
    return dict(tm=tm, t_da=t_da, t_sa=t_sa)


def kernel(x, p, rel_bias, ffn1_w_gate, ffn1_w_up, ffn1_w_down, ln1_g, ln1_b, w_in, w_o, da_lam_q1, da_lam_k1, da_lam_q2, da_lam_k2, da_subln_g, ssm_lam_re, ssm_lam_im, ssm_log_dt, ssm_b_re, ssm_b_im, ssm_c_re, ssm_c_im, ssm_d, ssm_w_glu, ln2_g, ln2_b, ffn2_w_gate, ffn2_w_up, ffn2_w_down, ple_w_proj, ple_w_gate, ln3_g, ln3_b):
    bsz, seq, d = x.shape
    rows = bsz * seq
    plan = _plan(bsz, seq)
    tm, t_da, t_sa = plan["tm"], plan["t_da"], plan["t_sa"]
    topk = min(TOPK_MAX, seq // 4)
    da_near, da_far = _bias_tiles(rel_bias[:, :DA_HEADS], t_da)
    sa_near, sa_far = _bias_tiles(rel_bias[:, DA_HEADS:], t_sa)
    vec = lambda a: a.reshape(1, -1).astype(F32)
    da_w = DA_HEADS * DA_V_DIM

    h = x.reshape(rows, d)
    for i in range(DEPTH):
        lam_init = 0.8 - 0.6 * math.exp(-0.3 * i)
        h = _ffn(h, ffn1_w_gate[i].astype(BF16), ffn1_w_up[i].astype(BF16), ffn1_w_down[i].astype(BF16),
                 vec(ln1_g[i]), vec(ln1_b[i]), tm=tm)

        weights, dtypes, transposed = _split_w_in(w_in[i])
        (da_q, da_k, da_v, ssm_u, sa_q, sa_k, sa_v, ix_q, ix_k, ix_w) = _inproj(
            h, weights, dtypes, transposed, tm=tm)
        b3 = lambda a: a.reshape(bsz, seq, a.shape[-1])

        lam = (jnp.exp(jnp.sum(da_lam_q1[i].astype(F32) * da_lam_k1[i]))
               - jnp.exp(jnp.sum(da_lam_q2[i].astype(F32) * da_lam_k2[i])) + lam_init)
        subln = jnp.broadcast_to(da_subln_g[i].astype(F32)[:, None], (DA_V_DIM, t_da))
        o_da = _diff_attention(da_q, b3(da_k), _with_ones(da_v, DA_V_DIM, bsz, seq, t_da), da_near, da_far,
                               lam.reshape(1).astype(F32), subln, bsz=bsz, seq=seq, t=t_da,
                               post_scale=1.0 - lam_init)

        tables = _ssm_tables(ssm_lam_re[i].astype(F32), ssm_lam_im[i].astype(F32), ssm_log_dt[i].astype(F32),
                             ssm_b_re[i].astype(F32), ssm_b_im[i].astype(F32),
                             ssm_c_re[i].astype(F32), ssm_c_im[i].astype(F32), SSM_CHUNK)
        o_ssm = _ssm(ssm_u, tables, vec(ssm_d[i]), ssm_w_glu[i].astype(BF16), bsz=bsz, seq=seq, tm=tm)

        sav = _with_ones(sa_v, SA_HEAD_DIM, bsz, seq, t_sa)[:, 0]
        o_sa = _sparse_attention(ix_q, ix_w, b3(ix_k), sa_q, b3(sa_k), sav, sa_near, sa_far,
                                 bsz=bsz, seq=seq, t=t_sa, topk=topk)

        wo = w_o[i].astype(BF16)
        h = _outproj(h, o_da.reshape(rows, da_w), o_ssm, o_sa.reshape(rows, SA_WIDTH),
                     wo[:da_w], wo[da_w:da_w + SSM_WIDTH], wo[da_w + SSM_WIDTH:],
                     vec(ln2_g[i]), vec(ln2_b[i]), tm=tm)

        h = _ffn(h, ffn2_w_gate[i].astype(BF16), ffn2_w_up[i].astype(BF16), ffn2_w_down[i].astype(BF16),
                 vec(ln3_g[i]), vec(ln3_b[i]),
                 ple=(p[i].reshape(rows, -1), ple_w_proj[i].astype(BF16), ple_w_gate[i].astype(BF16)), tm=tm)
    return h.reshape(bsz, seq, d)
```

```python
import functools
import math

import jax
import jax.numpy as jnp
from jax import lax
from jax.experimental import pallas as pl
from jax.experimental.pallas import tpu as pltpu

F32 = jnp.float32
BF16 = jnp.bfloat16
I32 = jnp.int32

DEPTH = 2
DA_QK_DIM = 64
DA_V_DIM = 2 * DA_QK_DIM
DA_HEADS = 4
SSM_GROUP = 16
SSM_GROUPS = 16
SSM_STATE = 64
SSM_WIDTH = SSM_GROUP * SSM_GROUPS
SA_HEAD_DIM = 64
SA_HEADS = 4
SA_WIDTH = SA_HEADS * SA_HEAD_DIM
IDX_HEADS = 8
IDX_DIM = 32
TOPK_MAX = 256
REL_BUCKETS = 32
REL_MAX_DIST = 128
ALPHA = (2 * DEPTH) ** 0.25
LN_EPS = 1e-5
IN_SIZES = (DA_HEADS * 2 * DA_QK_DIM, DA_HEADS * 2 * DA_QK_DIM, DA_HEADS * DA_V_DIM,
            SSM_WIDTH, SA_WIDTH, SA_HEAD_DIM, SA_HEAD_DIM,
            IDX_HEADS * IDX_DIM, IDX_DIM, IDX_HEADS)

LANES = 128
SUBLANES = 8
BF16_ROWS = 16
NEG = -1e30
INT_MIN = -2 ** 31
LOG2E = math.log2(math.e)
VMEM_LIMIT = 56 * 1024 * 1024
SSM_CHUNK = 64
DA_COLS = 1024
SA_GROUP = 4


def _params(*sem):
    return pltpu.CompilerParams(dimension_semantics=sem, vmem_limit_bytes=VMEM_LIMIT)


def _resident(shape):
    return pl.BlockSpec(shape, lambda *_: (0,) * len(shape), pipeline_mode=pl.Buffered(1))


def _dot(a, b):
    return jnp.dot(a, b, preferred_element_type=F32)


def _dot_nt(a, b):
    return lax.dot_general(a, b, (((1,), (1,)), ((), ())), preferred_element_type=F32)


def _layer_norm(y, g, b):
    mu = jnp.mean(y, axis=-1, keepdims=True)
    d = y - mu
    var = jnp.mean(d * d, axis=-1, keepdims=True)
    return d * lax.rsqrt(var + LN_EPS) * g + b


def _fold_rows(x):
    while x.shape[0] > SUBLANES:
        half = x.shape[0] // 2
        x = x[:half] + x[half:]
    return x


def _ffn_kernel(*refs, has_ple, ff_chunk):
    if has_ple:
        x_ref, p_ref, wg, wu, wd, wpp, wpg, g_ref, b_ref, o_ref = refs
    else:
        x_ref, wg, wu, wd, g_ref, b_ref, o_ref = refs
    x = x_ref[...]
    xb = x.astype(BF16)
    d_ff = wg.shape[1]
    acc = None
    for c0 in range(0, d_ff, ff_chunk):
        c1 = min(c0 + ff_chunk, d_ff)
        gate = _dot(xb, wg[:, c0:c1])
        up = _dot(xb, wu[:, c0:c1])
        hid = (gate * jax.nn.sigmoid(gate) * up).astype(BF16)
        part = _dot(hid, wd[c0:c1, :])
        acc = part if acc is None else acc + part
    y = ALPHA * x + 0.5 * acc
    if has_ple:
        y = y + _dot(p_ref[...].astype(BF16), wpp[...]) * jax.nn.sigmoid(_dot(xb, wpg[...]))
    o_ref[...] = _layer_norm(y, g_ref[...], b_ref[...])


def _ffn(x, wg, wu, wd, g, b, ple=None, *, tm):
    t_rows, d = x.shape
    d_ff = wg.shape[1]
    row = lambda w: pl.BlockSpec((tm, w), lambda i: (i, 0))
    args = [x]
    specs = [row(d)]
    if ple is not None:
        p, wpp, wpg = ple
        args.append(p)
        specs.append(row(p.shape[1]))
    args += [wg, wu, wd]
    specs += [_resident(wg.shape), _resident(wu.shape), _resident(wd.shape)]
    if ple is not None:
        args += [wpp, wpg]
        specs += [_resident(wpp.shape), _resident(wpg.shape)]
    args += [g, b]
    specs += [_resident(g.shape), _resident(b.shape)]
    return pl.pallas_call(
        functools.partial(_ffn_kernel, has_ple=ple is not None, ff_chunk=min(512, d_ff)),
        grid=(t_rows // tm,),
        in_specs=specs,
        out_specs=row(d),
        out_shape=jax.ShapeDtypeStruct((t_rows, d), F32),
        compiler_params=_params("parallel"),
        name="ffn_ple_ln" if ple is not None else "ffn_ln",
    )(*args)


def _inproj_kernel(x_ref, *refs, transposed):
    n = len(refs) // 2
    xb = x_ref[...].astype(BF16)
    for w_ref, o_ref, tr in zip(refs[:n], refs[n:], transposed):
        out = _dot_nt(w_ref[...], xb) if tr else _dot(xb, w_ref[...])
        o_ref[...] = out.astype(o_ref.dtype)


def _inproj(x, weights, dtypes, transposed, *, tm):
    t_rows, d = x.shape
    out_specs, out_shape = [], []
    for w, dt, tr in zip(weights, dtypes, transposed):
        if tr:
            out_specs.append(pl.BlockSpec((w.shape[0], tm), lambda i: (0, i)))
            out_shape.append(jax.ShapeDtypeStruct((w.shape[0], t_rows), dt))
        else:
            out_specs.append(pl.BlockSpec((tm, w.shape[1]), lambda i: (i, 0)))
            out_shape.append(jax.ShapeDtypeStruct((t_rows, w.shape[1]), dt))
    return pl.pallas_call(
        functools.partial(_inproj_kernel, transposed=tuple(transposed)),
        grid=(t_rows // tm,),
        in_specs=[pl.BlockSpec((tm, d), lambda i: (i, 0))] + [_resident(w.shape) for w in weights],
        out_specs=out_specs,
        out_shape=out_shape,
        compiler_params=_params("parallel"),
        name="in_proj",
    )(x, *weights)


def _softmax_probs(s, m_ref, shift):
    m_prev = m_ref[...]
    m_cur = jnp.max(s, axis=0, keepdims=True)
    if shift is not None:
        m_cur = m_cur + shift
    m_next = jnp.maximum(m_prev, m_cur)
    m_ref[...] = m_next
    p = jnp.exp2(s - (m_next if shift is None else m_next - shift))
    return p.astype(BF16), jnp.exp2(m_prev - m_next)


def _accumulate(acc_ref, rescale, v_aug, p):
    acc_ref[...] = rescale * acc_ref[...] + _dot(v_aug, p)


def _softmax_step(s, v_aug, m_ref, acc_ref, shift):
    p, rescale = _softmax_probs(s, m_ref, shift)
    _accumulate(acc_ref, rescale, v_aug, p)


def _far_chunks(n, scores, values, m_ref, acc_ref, shift):
    def body(i, carry):
        _softmax_step(scores(i), values(i), m_ref, acc_ref, shift)
        return carry

    lax.fori_loop(0, n, body, 0)


def _da_kernel(far_ref, lam_ref, q_ref, k_ref, v_ref, bn_ref, g_ref, o_ref,
               qd_ref, m_ref, acc_ref, *, t, post_scale):
    h = pl.program_id(1)
    qi = pl.program_id(2)
    q = q_ref[...].astype(F32)
    rowq = lax.broadcasted_iota(I32, q.shape, 0)
    qd_ref[:, 0:t] = jnp.where(rowq < DA_QK_DIM, q, 0.0).astype(BF16)
    qd_ref[:, t:2 * t] = jnp.where(rowq >= DA_QK_DIM, q, 0.0).astype(BF16)
    m_ref[...] = jnp.full(m_ref.shape, NEG, F32)
    acc_ref[...] = jnp.zeros(acc_ref.shape, F32)
    far = far_ref[h]

    def scores(kb):
        return _dot(k_ref[pl.ds(pl.multiple_of(kb * t, t), t), :], qd_ref[...])

    def near_block(kb, near):
        bias = bn_ref[near]
        _softmax_step(scores(kb) + jnp.concatenate([bias, bias], axis=1), v_ref[kb], m_ref, acc_ref, None)

    _far_chunks(jnp.maximum(qi - 1, 0), scores, lambda kb: v_ref[kb], m_ref, acc_ref, far)

    @pl.when(qi >= 1)
    def _():
        near_block(qi - 1, 1)

    near_block(qi, 0)

    acc = acc_ref[...]
    o = acc[0:DA_V_DIM] / acc[DA_V_DIM:DA_V_DIM + 1]
    o = o[:, 0:t] - lam_ref[0] * o[:, t:2 * t]
    ms = jnp.mean(o * o, axis=0, keepdims=True)
    o = o * lax.rsqrt(ms + LN_EPS) * g_ref[...] * post_scale
    o_ref[...] = o.T.astype(o_ref.dtype)


def _diff_attention(q_t, k, v_aug, bias_near, bias_far, lam, subln_g, *, bsz, seq, t, post_scale):
    heads = q_t.shape[0] // DA_V_DIM
    nq = seq // t
    smem = pl.BlockSpec(memory_space=pltpu.SMEM)
    va = DA_V_DIM + BF16_ROWS
    return pl.pallas_call(
        functools.partial(_da_kernel, t=t, post_scale=post_scale),
        grid=(bsz, heads, nq),
        in_specs=[
            smem, smem,
            pl.BlockSpec((DA_V_DIM, t), lambda b, h, i: (h, b * nq + i)),
            pl.BlockSpec((None, seq, DA_V_DIM), lambda b, h, i: (b, 0, h)),
            pl.BlockSpec((None, None, nq, va, t), lambda b, h, i: (b, h, 0, 0, 0)),
            pl.BlockSpec((None, 2, t, t), lambda b, h, i: (h, 0, 0, 0)),
            pl.BlockSpec((DA_V_DIM, t), lambda b, h, i: (0, 0)),
        ],
        out_specs=pl.BlockSpec((None, t, DA_V_DIM), lambda b, h, i: (b, i, h)),
        out_shape=jax.ShapeDtypeStruct((bsz, seq, heads * DA_V_DIM), BF16),
        scratch_shapes=[
            pltpu.VMEM((DA_V_DIM, 2 * t), BF16),
            pltpu.VMEM((1, 2 * t), F32),
            pltpu.VMEM((va, 2 * t), F32),
        ],
        compiler_params=_params("parallel", "parallel", "arbitrary"),
        name="diff_attention",
    )(bias_far, lam, q_t, k, v_aug, bias_near, subln_g)


def _dsa_kernel(far_ref, ixq_ref, ixw_ref, ixk_ref, saq_ref, sak_ref, sav_ref, bn_ref, o_ref,
                qi8_ref, keys_ref, qs4_ref, m_ref, acc_ref, *, t, topk, idx_bits):
    qi = pl.program_id(1)
    kf = float(topk)

    qi8_ref[...] = jnp.zeros(qi8_ref.shape, BF16)
    for h in range(IDX_HEADS):
        qi8_ref[0:IDX_DIM, h * t:(h + 1) * t] = ixq_ref[h * IDX_DIM:(h + 1) * IDX_DIM, :]

    def index_keys(kb):
        kblk = ixk_ref[pl.ds(pl.multiple_of(kb * t, t), t), :]
        sc = None
        for h in range(IDX_HEADS):
            r = _dot(kblk, qi8_ref[:, h * t:(h + 1) * t])
            term = ixw_ref[h:h + 1, :] * jnp.maximum(r, 0.0)
            sc = term if sc is None else sc + term
        sc = jnp.where(sc == 0.0, 0.0, sc)
        bits = lax.bitcast_convert_type(sc, I32)
        return bits ^ ((bits >> 31) & 0x7FFFFFFF)

    def chunk(kb):
        return keys_ref[pl.ds(pl.multiple_of(kb * t, t), t), :]

    def fill(kb, carry):
        keys_ref[pl.ds(pl.multiple_of(kb * t, t), t), :] = index_keys(kb)
        return carry

    lax.fori_loop(0, qi, fill, 0)
    krow = lax.broadcasted_iota(I32, (t, t), 0)
    qcol = lax.broadcasted_iota(I32, (t, t), 1)
    keys_ref[pl.ds(pl.multiple_of(qi * t, t), t), :] = jnp.where(krow <= qcol, index_keys(qi), INT_MIN)

    def count(fn):
        def body(kb, cnt):
            return cnt + _fold_rows(fn(kb, chunk(kb)))
        cnt = lax.fori_loop(0, qi + 1, body, jnp.zeros((SUBLANES, t), I32))
        return jnp.sum(cnt.astype(F32), axis=0, keepdims=True)

    def search_cond(st):
        i, _, _, n_open = st
        return jnp.logical_and(i < 32, n_open > 0.0)

    def search_body(st):
        i, tau, open_, _ = st
        cand = tau + jnp.left_shift(jnp.int32(1), 31 - i)
        c = count(lambda kb, blk: jnp.where(blk >= cand, 1, 0))
        tau = jnp.where(open_ > 0.0, jnp.where(c >= kf, cand, tau), tau)
        open_ = jnp.where(c == kf, 0.0, open_)
        return i + 1, tau, open_, jnp.max(open_)

    _, tau, _, _ = lax.while_loop(
        search_cond, search_body,
        (jnp.int32(0), jnp.full((1, t), INT_MIN, I32), jnp.ones((1, t), F32), jnp.float32(1.0)))
    c_ge = count(lambda kb, blk: jnp.where(blk >= tau, 1, 0))

    @pl.when(jnp.max(c_ge) > kf)
    def _():
        need = kf - count(lambda kb, blk: jnp.where(blk > tau, 1, 0))

        def j_bit(i, ju):
            cand = ju | jnp.left_shift(jnp.int32(1), idx_bits - 1 - i)
            c = count(lambda kb, blk: jnp.where(blk == tau, jnp.where((krow + kb * t) < cand, 1, 0), 0))
            return jnp.where(c < need, cand, ju)

        ju = lax.fori_loop(0, idx_bits, j_bit, jnp.zeros((1, t), I32))

        def retire(kb, carry):
            blk = chunk(kb)
            late = jnp.where((krow + kb * t) > ju, INT_MIN, blk)
            keys_ref[pl.ds(pl.multiple_of(kb * t, t), t), :] = jnp.where(blk == tau, late, blk)
            return carry

        lax.fori_loop(0, qi + 1, retire, 0)

    qs4_ref[...] = jnp.zeros(qs4_ref.shape, BF16)
    for h in range(SA_HEADS):
        qs4_ref[0:SA_HEAD_DIM, h * t:(h + 1) * t] = saq_ref[h * SA_HEAD_DIM:(h + 1) * SA_HEAD_DIM, :]
    m_ref[...] = jnp.full(m_ref.shape, NEG, F32)
    acc_ref[...] = jnp.zeros(acc_ref.shape, F32)

    def scores(kb, near=None):
        s = _dot(sak_ref[pl.ds(pl.multiple_of(kb * t, t), t), :], qs4_ref[...])
        sel = chunk(kb) >= tau
        parts = []
        for h in range(SA_HEADS):
            sh = s[:, h * t:(h + 1) * t]
            if near is not None:
                sh = sh + bn_ref[h, near]
            parts.append(jnp.where(sel, sh, NEG))
        return jnp.concatenate(parts, axis=1)

    far = jnp.concatenate([jnp.full((1, t), far_ref[h], F32) for h in range(SA_HEADS)], axis=1)
    _far_chunks(jnp.maximum(qi - 1, 0), scores, lambda kb: sav_ref[kb], m_ref, acc_ref, far)

    @pl.when(qi >= 1)
    def _():
        _softmax_step(scores(qi - 1, 1), sav_ref[qi - 1], m_ref, acc_ref, None)

    _softmax_step(scores(qi, 0), sav_ref[qi], m_ref, acc_ref, None)

    acc = acc_ref[...]
    o = acc[0:SA_HEAD_DIM] / acc[SA_HEAD_DIM:SA_HEAD_DIM + 1]
    o = jnp.concatenate([o[:, h * t:(h + 1) * t] for h in range(SA_HEADS)], axis=0)
    o_ref[...] = o.T.astype(o_ref.dtype)


def _sparse_attention(ixq_t, ixw_t, ixk, saq_t, sak, sav_aug, bias_near, bias_far, *, bsz, seq, t, topk):
    nq = seq // t
    smem = pl.BlockSpec(memory_space=pltpu.SMEM)
    qcols = lambda r: pl.BlockSpec((r, t), lambda b, i: (0, b * nq + i))
    full = lambda w: pl.BlockSpec((None, seq, w), lambda b, i: (b, 0, 0))
    va = SA_HEAD_DIM + BF16_ROWS
    return pl.pallas_call(
        functools.partial(_dsa_kernel, t=t, topk=topk, idx_bits=max(1, (seq - 1).bit_length())),
        grid=(bsz, nq),
        in_specs=[
            smem,
            qcols(IDX_HEADS * IDX_DIM), qcols(IDX_HEADS), full(LANES),
            qcols(SA_WIDTH), full(LANES),
            pl.BlockSpec((None, nq, va, t), lambda b, i: (b, 0, 0, 0)),
            _resident((SA_HEADS, 2, t, t)),
        ],
        out_specs=pl.BlockSpec((None, t, SA_WIDTH), lambda b, i: (b, i, 0)),
        out_shape=jax.ShapeDtypeStruct((bsz, seq, SA_WIDTH), BF16),
        scratch_shapes=[
            pltpu.VMEM((LANES, IDX_HEADS * t), BF16),
            pltpu.VMEM((seq, t), I32),
            pltpu.VMEM((LANES, SA_HEADS * t), BF16),
            pltpu.VMEM((1, SA_HEADS * t), F32),
            pltpu.VMEM((va, SA_HEADS * t), F32),
        ],
        compiler_params=_params("parallel", "arbitrary"),
        name="sparse_attention",
    )(bias_far, ixq_t, ixw_t, ixk, saq_t, sak, sav_aug, bias_near)


def _ssm_state_kernel(u_ref, w_ref, o_ref):
    o_ref[...] = _dot(u_ref[...].astype(BF16), w_ref[...])


def _ssm_scan_kernel(loc_ref, a1_ref, a2_ref, o_ref):
    a1 = a1_ref[...]
    a2 = a2_ref[...]

    def body(n, s):
        o_ref[n] = s
        return a1 * s + a2 * pltpu.roll(s, SSM_STATE, axis=1) + loc_ref[n]

    lax.fori_loop(0, loc_ref.shape[0], body, jnp.zeros(a1.shape, F32))


def _ssm_out_kernel(u_ref, s_ref, toep_ref, wout_ref, o_ref):
    o_ref[...] = (_dot(u_ref[...].astype(BF16), toep_ref[...])
                  + _dot(s_ref[...].astype(BF16), wout_ref[...]))


def _ssm_gate_kernel(y_ref, u_ref, d_ref, w_ref, o_ref):
    y = jax.nn.gelu(y_ref[...] + d_ref[...] * u_ref[...])
    o_ref[...] = (y * jax.nn.sigmoid(_dot(y.astype(BF16), w_ref[...]))).astype(o_ref.dtype)


def _ssm_tables(lam_re, lam_im, log_dt, b_re, b_im, c_re, c_im, tc):
    hp = lax.Precision.HIGHEST
    dt = jnp.exp(log_dt)[:, None]
    n = jnp.arange(tc + 1, dtype=F32)[:, None, None]
    mag = jnp.exp(lam_re * dt * n)
    pw_re = mag * jnp.cos(lam_im * dt * n)
    pw_im = mag * jnp.sin(lam_im * dt * n)
    den = lam_re * lam_re + lam_im * lam_im
    nr, ni = pw_re[1] - 1.0, pw_im[1]
    f_re = (nr * lam_re + ni * lam_im) / den
    f_im = (ni * lam_re - nr * lam_im) / den
    bb_re = f_re[..., None] * b_re - f_im[..., None] * b_im
    bb_im = f_re[..., None] * b_im + f_im[..., None] * b_re
    ca_re = c_re[None] * pw_re[:, :, None, :] - c_im[None] * pw_im[:, :, None, :]
    ca_im = c_re[None] * pw_im[:, :, None, :] + c_im[None] * pw_re[:, :, None, :]
    kern = (jnp.einsum('tgcp,gpd->tgcd', ca_re[:tc], bb_re, precision=hp)
            - jnp.einsum('tgcp,gpd->tgcd', ca_im[:tc], bb_im, precision=hp))
    groups = lam_re.shape[0]
    toep = _toeplitz(jnp.concatenate([kern, jnp.zeros_like(kern)], axis=0), tc)
    toep = toep.transpose(2, 0, 4, 1, 3).reshape(groups, tc * SSM_GROUP, tc * SSM_GROUP)
    rev_re, rev_im = pw_re[tc - 1::-1][:tc], pw_im[tc - 1::-1][:tc]
    ws_re = rev_re[..., None] * bb_re[None] - rev_im[..., None] * bb_im[None]
    ws_im = rev_re[..., None] * bb_im[None] + rev_im[..., None] * bb_re[None]
    wstate = jnp.concatenate([ws_re, ws_im], axis=2)
    wstate = wstate.transpose(1, 0, 3, 2).reshape(groups, tc * SSM_GROUP, 2 * SSM_STATE)
    wout = jnp.concatenate([ca_re[1:], -ca_im[1:]], axis=3)
    wout = wout.transpose(1, 3, 0, 2).reshape(groups, 2 * SSM_STATE, tc * SSM_GROUP)
    dec_re, dec_im = pw_re[tc], pw_im[tc]
    a1 = jnp.concatenate([dec_re, dec_re], axis=1)
    a2 = jnp.concatenate([-dec_im, dec_im], axis=1)
    return toep.astype(BF16), wstate.astype(BF16), wout.astype(BF16), a1, a2


def _ssm(u, tables, d_skip, w_glu, *, bsz, seq, tm):
    toep, wstate, wout, a1, a2 = tables
    groups = toep.shape[0]
    tc = SSM_CHUNK
    nc = seq // tc
    rows = bsz * nc
    cw = tc * SSM_GROUP
    ug = u.reshape(bsz, nc, tc, groups, SSM_GROUP).transpose(3, 0, 1, 2, 4).reshape(groups, rows, cw)
    gspec = lambda r, c: pl.BlockSpec((None, r, c), lambda g: (g, 0, 0))
    loc = pl.pallas_call(
        _ssm_state_kernel,
        grid=(groups,),
        in_specs=[gspec(rows, cw), gspec(cw, 2 * SSM_STATE)],
        out_specs=gspec(rows, 2 * SSM_STATE),
        out_shape=jax.ShapeDtypeStruct((groups, rows, 2 * SSM_STATE), F32),
        compiler_params=_params("parallel"),
        name="ssm_chunk_state",
    )(ug, wstate)
    gb = groups * bsz
    loc_t = loc.reshape(groups, bsz, nc, 2 * SSM_STATE).transpose(2, 0, 1, 3).reshape(nc, gb, 2 * SSM_STATE)
    a1r = jnp.repeat(a1, bsz, axis=0)
    a2r = jnp.repeat(a2, bsz, axis=0)
    rb = 8 if gb % 8 == 0 else gb
    prev = pl.pallas_call(
        _ssm_scan_kernel,
        grid=(gb // rb,),
        in_specs=[pl.BlockSpec((nc, rb, 2 * SSM_STATE), lambda i: (0, i, 0)),
                  pl.BlockSpec((rb, 2 * SSM_STATE), lambda i: (i, 0)),
                  pl.BlockSpec((rb, 2 * SSM_STATE), lambda i: (i, 0))],
        out_specs=pl.BlockSpec((nc, rb, 2 * SSM_STATE), lambda i: (0, i, 0)),
        out_shape=jax.ShapeDtypeStruct((nc, gb, 2 * SSM_STATE), F32),
        compiler_params=_params("parallel"),
        name="ssm_chunk_scan",
    )(loc_t, a1r, a2r)
    prev_g = prev.reshape(nc, groups, bsz, 2 * SSM_STATE).transpose(1, 2, 0, 3).reshape(groups, rows, 2 * SSM_STATE)
    y = pl.pallas_call(
        _ssm_out_kernel,
        grid=(groups,),
        in_specs=[gspec(rows, cw), gspec(rows, 2 * SSM_STATE), gspec(cw, cw), gspec(2 * SSM_STATE, cw)],
        out_specs=gspec(rows, cw),
        out_shape=jax.ShapeDtypeStruct((groups, rows, cw), F32),
        compiler_params=_params("parallel"),
        name="ssm_chunk_out",
    )(ug, prev_g, toep, wout)
    y = y.reshape(groups, bsz, nc, tc, SSM_GROUP).transpose(1, 2, 3, 0, 4).reshape(bsz * seq, groups * SSM_GROUP)
    width = groups * SSM_GROUP
    return pl.pallas_call(
        _ssm_gate_kernel,
        grid=(bsz * seq // tm,),
        in_specs=[pl.BlockSpec((tm, width), lambda i: (i, 0)),
                  pl.BlockSpec((tm, width), lambda i: (i, 0)),
                  _resident((1, width)), _resident((width, width))],
        out_specs=pl.BlockSpec((tm, width), lambda i: (i, 0)),
        out_shape=jax.ShapeDtypeStruct((bsz * seq, width), BF16),
        compiler_params=_params("parallel"),
        name="ssm_gate",
    )(y, u, d_skip, w_glu)


def _outproj_kernel(x_ref, da_ref, ssm_ref, sa_ref, w1, w2, w3, g_ref, b_ref, o_ref):
    mix = _dot(da_ref[...], w1[...]) + _dot(ssm_ref[...], w2[...]) + _dot(sa_ref[...], w3[...])
    o_ref[...] = _layer_norm(ALPHA * x_ref[...] + mix, g_ref[...], b_ref[...])


def _outproj(x, o_da, o_ssm, o_sa, w1, w2, w3, g, b, *, tm):
    t_rows, d = x.shape
    row = lambda w: pl.BlockSpec((tm, w), lambda i: (i, 0))
    return pl.pallas_call(
        _outproj_kernel,
        grid=(t_rows // tm,),
        in_specs=[row(d), row(o_da.shape[1]), row(o_ssm.shape[1]), row(o_sa.shape[1]),
                  _resident(w1.shape), _resident(w2.shape), _resident(w3.shape),
                  _resident(g.shape), _resident(b.shape)],
        out_specs=row(d),
        out_shape=jax.ShapeDtypeStruct((t_rows, d), F32),
        compiler_params=_params("parallel"),
        name="out_proj_ln",
    )(x, o_da, o_ssm, o_sa, w1, w2, w3, g, b)


def _toeplitz(w, n):
    m = w.shape[0]
    flat = jnp.tile(w, (n,) + (1,) * (w.ndim - 1))[:n * (m - 1)]
    return flat.reshape((n, m - 1) + w.shape[1:])[:, :n]


def _t5_bucket(n):
    max_exact = REL_BUCKETS // 2
    nf = jnp.maximum(n, 1).astype(F32)
    large = max_exact + (jnp.log(nf / max_exact) / math.log(REL_MAX_DIST / max_exact)
                         * (REL_BUCKETS - max_exact)).astype(I32)
    large = jnp.minimum(large, REL_BUCKETS - 1)
    return jnp.where(n < max_exact, n, large)


def _bias_tiles(table, t):
    assert t >= REL_MAX_DIST
    by_dist = table[_t5_bucket(jnp.arange(2 * t))].astype(F32) * LOG2E
    ahead = jnp.full((t, table.shape[1]), NEG, F32)
    tile0 = _toeplitz(jnp.concatenate([by_dist[:t], ahead], axis=0), t)
    tile1 = _toeplitz(jnp.concatenate([by_dist[t:], by_dist[:t]], axis=0), t)
    tiles = jnp.stack([tile0, tile1]).transpose(3, 0, 1, 2)
    return tiles, table[REL_BUCKETS - 1].astype(F32) * LOG2E


def _split_w_in(w_in):
    offs = [0]
    for s in IN_SIZES:
        offs.append(offs[-1] + s)
    da_q, da_k, da_v, ssm_u, sa_q, sa_k, sa_v, ix_q, ix_k, ix_w = (
        w_in[:, offs[j]:offs[j + 1]] for j in range(len(IN_SIZES)))
    w_scale = IDX_HEADS ** -0.5 * IDX_DIM ** -0.5
    pad = lambda w: jnp.pad(w, ((0, 0), (0, LANES - w.shape[1])))
    weights = [(da_q * (DA_QK_DIM ** -0.5 * LOG2E)).T, da_k, da_v.T, ssm_u,
               (sa_q * (SA_HEAD_DIM ** -0.5 * LOG2E)).T, pad(sa_k), sa_v.T,
               ix_q.T, pad(ix_k), (ix_w * w_scale).T]
    dtypes = [BF16, BF16, BF16, F32, BF16, BF16, BF16, BF16, BF16, F32]
    transposed = [True, False, True, False, True, False, True, True, False, True]
    return [w.astype(BF16) for w in weights], dtypes, transposed


def _with_ones(v_t, rows, bsz, seq, t):
    g = v_t.shape[0] // rows
    v = v_t.reshape(g, rows, bsz, seq // t, t).transpose(2, 0, 3, 1, 4)
    ones = jnp.ones(v.shape[:3] + (BF16_ROWS, t), v.dtype)
    return jnp.concatenate([v, ones], axis=3)


def _plan(bsz, seq):
    rows = bsz * seq
    tm = 512 if rows % 512 == 0 else rows
    t_da = 1024 if seq % 1024 == 0 and seq >= 4096 else 256 if seq % 256 == 0 else 128
    t_sa = 512 if seq % 512 == 0 and seq >= 2048 else 256 if seq % 256 == 0 else 128
    return dict(tm=tm, t_da=t_da, t_sa=t_sa)


def kernel(x, p, rel_bias, ffn1_w_gate, ffn1_w_up, ffn1_w_down, ln1_g, ln1_b, w_in, w_o, da_lam_q1, da_lam_k1, da_lam_q2, da_lam_k2, da_subln_g, ssm_lam_re, ssm_lam_im, ssm_log_dt, ssm_b_re, ssm_b_im, ssm_c_re, ssm_c_im, ssm_d, ssm_w_glu, ln2_g, ln2_b, ffn2_w_gate, ffn2_w_up, ffn2_w_down, ple_w_proj, ple_w_gate, ln3_g, ln3_b):
    bsz, seq, d = x.shape
    rows = bsz * seq
    plan = _plan(bsz, seq)
    tm, t_da, t_sa = plan["tm"], plan["t_da"], plan["t_sa"]
    topk = min(TOPK_MAX, seq // 4)
    da_near, da_far = _bias_tiles(rel_bias[:, :DA_HEADS], t_da)
    sa_near, sa_far = _bias_tiles(rel_bias[:, DA_HEADS:], t_sa)
    vec = lambda a: a.reshape(1, -1).astype(F32)
    da_w = DA_HEADS * DA_V_DIM

    h = x.reshape(rows, d)
    for i in range(DEPTH):
        lam_init = 0.8 - 0.6 * math.exp(-0.3 * i)
        h = _ffn(h, ffn1_w_gate[i].astype(BF16), ffn1_w_up[i].astype(BF16), ffn1_w_down[i].astype(BF16),
                 vec(ln1_g[i]), vec(ln1_b[i]), tm=tm)

        weights, dtypes, transposed = _split_w_in(w_in[i])
        (da_q, da_k, da_v, ssm_u, sa_q, sa_k, sa_v, ix_q, ix_k, ix_w) = _inproj(
            h, weights, dtypes, transposed, tm=tm)
        b3 = lambda a: a.reshape(bsz, seq, a.shape[-1])

        lam = (jnp.exp(jnp.sum(da_lam_q1[i].astype(F32) * da_lam_k1[i]))
               - jnp.exp(jnp.sum(da_lam_q2[i].astype(F32) * da_lam_k2[i])) + lam_init)
        subln = jnp.broadcast_to(da_subln_g[i].astype(F32)[:, None], (DA_V_DIM, t_da))
        o_da = _diff_attention(da_q, b3(da_k), _with_ones(da_v, DA_V_DIM, bsz, seq, t_da), da_near, da_far,
                               lam.reshape(1).astype(F32), subln, bsz=bsz, seq=seq, t=t_da,
                               post_scale=1.0 - lam_init)

        tables = _ssm_tables(ssm_lam_re[i].astype(F32), ssm_lam_im[i].astype(F32), ssm_log_dt[i].astype(F32),
                             ssm_b_re[i].astype(F32), ssm_b_im[i].astype(F32),
                             ssm_c_re[i].astype(F32), ssm_c_im[i].astype(F32), SSM_CHUNK)
        o_ssm = _ssm(ssm_u, tables, vec(ssm_d[i]), ssm_w_glu[i].astype(BF16), bsz=bsz, seq=seq, tm=tm)

        sav = _with_ones(sa_v, SA_HEAD_DIM, bsz, seq, t_sa)[:, 0]
        o_sa = _sparse_attention(ix_q, ix_w, b3(ix_k), sa_q, b3(sa_k), sav, sa_near, sa_far,
                                 bsz=bsz, seq=seq, t=t_sa, topk=topk)

        wo = w_o[i].astype(BF16)
        h = _outproj(h, o_da.reshape(rows, da_w), o_ssm, o_sa.reshape(rows, SA_WIDTH),
                     wo[:da_w], wo[da_w:da_w + SSM_WIDTH], wo[da_w + SSM_WIDTH:],
                     vec(ln2_g[i]), vec(ln2_b[i]), tm=tm)

        h = _ffn(h, ffn2_w_gate[i].astype(BF16), ffn2_w_up[i].astype(BF16), ffn2_w_down[i].astype(BF16),
                 vec(ln3_g[i]), vec(ln3_b[i]),
                 ple=(p[i].reshape(rows, -1), ple_w_proj[i].astype(BF16), ple_w_gate[i].astype(BF16)), tm=tm)
    return h.reshape(bsz, seq, d)
```

```python
import functools
import math

import jax
import jax.numpy as jnp
from jax import lax
from jax.experimental import pallas as pl
from jax.experimental.pallas import tpu as pltpu

F32 = jnp.float32
BF16 = jnp.bfloat16
I32 = jnp.int32

DEPTH = 2
DA_QK_DIM = 64
DA_V_DIM = 2 * DA_QK_DIM
DA_HEADS = 4
SSM_GROUP = 16
SSM_GROUPS = 16
SSM_STATE = 64
SSM_WIDTH = SSM_GROUP * SSM_GROUPS
SA_HEAD_DIM = 64
SA_HEADS = 4
SA_WIDTH = SA_HEADS * SA_HEAD_DIM
IDX_HEADS = 8
IDX_DIM = 32
TOPK_MAX = 256
REL_BUCKETS = 32
REL_MAX_DIST = 128
ALPHA = (2 * DEPTH) ** 0.25
LN_EPS = 1e-5
IN_SIZES = (DA_HEADS * 2 * DA_QK_DIM, DA_HEADS * 2 * DA_QK_DIM, DA_HEADS * DA_V_DIM,
            SSM_WIDTH, SA_WIDTH, SA_HEAD_DIM, SA_HEAD_DIM,
            IDX_HEADS * IDX_DIM, IDX_DIM, IDX_HEADS)

LANES = 128
SUBLANES = 8
BF16_ROWS = 16
NEG = -1e30
INT_MIN = -2 ** 31
LOG2E = math.log2(math.e)
VMEM_LIMIT = 56 * 1024 * 1024
SSM_CHUNK = 64


def _params(*sem):
    return pltpu.CompilerParams(dimension_semantics=sem, vmem_limit_bytes=VMEM_LIMIT)


def _resident(shape):
    return pl.BlockSpec(shape, lambda *_: (0,) * len(shape), pipeline_mode=pl.Buffered(1))


def _dot(a, b):
    return jnp.dot(a, b, preferred_element_type=F32)


def _dot_nt(a, b):
    return lax.dot_general(a, b, (((1,), (1,)), ((), ())), preferred_element_type=F32)


def _layer_norm(y, g, b):
    mu = jnp.mean(y, axis=-1, keepdims=True)
    d = y - mu
    var = jnp.mean(d * d, axis=-1, keepdims=True)
    return d * lax.rsqrt(var + LN_EPS) * g + b


def _fold_rows(x):
    while x.shape[0] > SUBLANES:
        half = x.shape[0] // 2
        x = x[:half] + x[half:]
    return x


def _ffn_kernel(*refs, has_ple, ff_chunk):
    if has_ple:
        x_ref, p_ref, wg, wu, wd, wpp, wpg, g_ref, b_ref, o_ref = refs
    else:
        x_ref, wg, wu, wd, g_ref, b_ref, o_ref = refs
    x = x_ref[...]
    xb = x.astype(BF16)
    d_ff = wg.shape[1]
    acc = None
    for c0 in range(0, d_ff, ff_chunk):
        c1 = min(c0 + ff_chunk, d_ff)
        gate = _dot(xb, wg[:, c0:c1])
        up = _dot(xb, wu[:, c0:c1])
        hid = (gate * jax.nn.sigmoid(gate) * up).astype(BF16)
        part = _dot(hid, wd[c0:c1, :])
        acc = part if acc is None else acc + part
    y = ALPHA * x + 0.5 * acc
    if has_ple:
        y = y + _dot(p_ref[...].astype(BF16), wpp[...]) * jax.nn.sigmoid(_dot(xb, wpg[...]))
    o_ref[...] = _layer_norm(y, g_ref[...], b_ref[...])


def _ffn(x, wg, wu, wd, g, b, ple=None, *, tm):
    t_rows, d = x.shape
    d_ff = wg.shape[1]
    row = lambda w: pl.BlockSpec((tm, w), lambda i: (i, 0))
    args = [x]
    specs = [row(d)]
    if ple is not None:
        p, wpp, wpg = ple
        args.append(p)
        specs.append(row(p.shape[1]))
    args += [wg, wu, wd]
    specs += [_resident(wg.shape), _resident(wu.shape), _resident(wd.shape)]
    if ple is not None:
        args += [wpp, wpg]
        specs += [_resident(wpp.shape), _resident(wpg.shape)]
    args += [g, b]
    specs += [_resident(g.shape), _resident(b.shape)]
    return pl.pallas_call(
        functools.partial(_ffn_kernel, has_ple=ple is not None, ff_chunk=min(512, d_ff)),
        grid=(t_rows // tm,),
        in_specs=specs,
        out_specs=row(d),
        out_shape=jax.ShapeDtypeStruct((t_rows, d), F32),
        compiler_params=_params("parallel"),
        name="ffn_ple_ln" if ple is not None else "ffn_ln",
    )(*args)


def _inproj_kernel(x_ref, *refs, transposed):
    n = len(refs) // 2
    xb = x_ref[...].astype(BF16)
    for w_ref, o_ref, tr in zip(refs[:n], refs[n:], transposed):
        out = _dot_nt(w_ref[...], xb) if tr else _dot(xb, w_ref[...])
        o_ref[...] = out.astype(o_ref.dtype)


def _inproj(x, weights, dtypes, transposed, *, tm):
    t_rows, d = x.shape
    out_specs, out_shape = [], []
    for w, dt, tr in zip(weights, dtypes, transposed):
        if tr:
            out_specs.append(pl.BlockSpec((w.shape[0], tm), lambda i: (0, i)))
            out_shape.append(jax.ShapeDtypeStruct((w.shape[0], t_rows), dt))
        else:
            out_specs.append(pl.BlockSpec((tm, w.shape[1]), lambda i: (i, 0)))
            out_shape.append(jax.ShapeDtypeStruct((t_rows, w.shape[1]), dt))
    return pl.pallas_call(
        functools.partial(_inproj_kernel, transposed=tuple(transposed)),
        grid=(t_rows // tm,),
        in_specs=[pl.BlockSpec((tm, d), lambda i: (i, 0))] + [_resident(w.shape) for w in weights],
        out_specs=out_specs,
        out_shape=out_shape,
        compiler_params=_params("parallel"),
        name="in_proj",
    )(x, *weights)


def _softmax_probs(s, m_ref, shift):
    m_prev = m_ref[...]
    m_cur = jnp.max(s, axis=0, keepdims=True)
    if shift is not None:
        m_cur = m_cur + shift
    m_next = jnp.maximum(m_prev, m_cur)
    m_ref[...] = m_next
    p = jnp.exp2(s - (m_next if shift is None else m_next - shift))
    return p.astype(BF16), jnp.exp2(m_prev - m_next)


def _accumulate(acc_ref, rescale, v_t, p):
    v_aug = jnp.concatenate([v_t, jnp.ones((BF16_ROWS, v_t.shape[1]), v_t.dtype)], axis=0)
    acc_ref[...] = rescale * acc_ref[...] + _dot(v_aug, p)


def _softmax_step(s, v_t, m_ref, acc_ref, shift):
    p, rescale = _softmax_probs(s, m_ref, shift)
    _accumulate(acc_ref, rescale, v_t, p)


def _far_chunks(n, scores, values, m_ref, acc_ref, shift):
    def body(i, carry):
        _softmax_step(scores(i), values(i), m_ref, acc_ref, shift)
        return carry

    lax.fori_loop(0, n, body, 0)


def _da_kernel(far_ref, lam_ref, q_ref, k_ref, v_ref, bn_ref, g_ref, o_ref,
               qd_ref, m_ref, acc_ref, *, t, post_scale):
    h = pl.program_id(1)
    qi = pl.program_id(2)
    q = q_ref[...].astype(F32)
    rowq = lax.broadcasted_iota(I32, q.shape, 0)
    qd_ref[:, 0:t] = jnp.where(rowq < DA_QK_DIM, q, 0.0).astype(BF16)
    qd_ref[:, t:2 * t] = jnp.where(rowq >= DA_QK_DIM, q, 0.0).astype(BF16)
    m_ref[...] = jnp.full(m_ref.shape, NEG, F32)
    acc_ref[...] = jnp.zeros(acc_ref.shape, F32)
    far = far_ref[h]

    def scores(kb):
        return _dot(k_ref[pl.ds(pl.multiple_of(kb * t, t), t), :], qd_ref[...])

    def near_block(kb, near):
        bias = bn_ref[near]
        _softmax_step(scores(kb) + jnp.concatenate([bias, bias], axis=1), v_ref[kb], m_ref, acc_ref, None)

    _far_chunks(jnp.maximum(qi - 1, 0), scores, lambda kb: v_ref[kb], m_ref, acc_ref, far)

    @pl.when(qi >= 1)
    def _():
        near_block(qi - 1, 1)

    near_block(qi, 0)

    acc = acc_ref[...]
    o = acc[0:DA_V_DIM] / acc[DA_V_DIM:DA_V_DIM + 1]
    o = o[:, 0:t] - lam_ref[0] * o[:, t:2 * t]
    ms = jnp.mean(o * o, axis=0, keepdims=True)
    o = o * lax.rsqrt(ms + LN_EPS) * g_ref[...] * post_scale
    o_ref[...] = o.T.astype(o_ref.dtype)


def _diff_attention(q_t, k, v_t, bias_near, bias_far, lam, subln_g, *, bsz, seq, t, post_scale):
    heads = q_t.shape[0] // DA_V_DIM
    nq = seq // t
    smem = pl.BlockSpec(memory_space=pltpu.SMEM)
    va = DA_V_DIM + BF16_ROWS
    return pl.pallas_call(
        functools.partial(_da_kernel, t=t, post_scale=post_scale),
        grid=(bsz, heads, nq),
        in_specs=[
            smem, smem,
            pl.BlockSpec((DA_V_DIM, t), lambda b, h, i: (h, b * nq + i)),
            pl.BlockSpec((None, seq, DA_V_DIM), lambda b, h, i: (b, 0, h)),
            pl.BlockSpec((None, None, nq, DA_V_DIM, t), lambda b, h, i: (b, h, 0, 0, 0)),
            pl.BlockSpec((None, 2, t, t), lambda b, h, i: (h, 0, 0, 0)),
            pl.BlockSpec((DA_V_DIM, t), lambda b, h, i: (0, 0)),
        ],
        out_specs=pl.BlockSpec((None, t, DA_V_DIM), lambda b, h, i: (b, i, h)),
        out_shape=jax.ShapeDtypeStruct((bsz, seq, heads * DA_V_DIM), BF16),
        scratch_shapes=[
            pltpu.VMEM((DA_V_DIM, 2 * t), BF16),
            pltpu.VMEM((1, 2 * t), F32),
            pltpu.VMEM((va, 2 * t), F32),
        ],
        compiler_params=_params("parallel", "parallel", "arbitrary"),
        name="diff_attention",
    )(bias_far, lam, q_t, k, v_t, bias_near, subln_g)


def _dsa_kernel(far_ref, ixq_ref, ixw_ref, ixk_ref, saq_ref, sak_ref, sav_ref, bn_ref, o_ref,
                qi8_ref, keys_ref, qs4_ref, m_ref, acc_ref, *, t, topk, idx_bits):
    qi = pl.program_id(1)
    kf = float(topk)

    qi8_ref[...] = jnp.zeros(qi8_ref.shape, BF16)
    for h in range(IDX_HEADS):
        qi8_ref[0:IDX_DIM, h * t:(h + 1) * t] = ixq_ref[h * IDX_DIM:(h + 1) * IDX_DIM, :]

    def index_keys(kb):
        kblk = ixk_ref[pl.ds(pl.multiple_of(kb * t, t), t), :]
        sc = None
        for h in range(IDX_HEADS):
            r = _dot(kblk, qi8_ref[:, h * t:(h + 1) * t])
            term = ixw_ref[h:h + 1, :] * jnp.maximum(r, 0.0)
            sc = term if sc is None else sc + term
        sc = jnp.where(sc == 0.0, 0.0, sc)
        bits = lax.bitcast_convert_type(sc, I32)
        return bits ^ ((bits >> 31) & 0x7FFFFFFF)

    def chunk(kb):
        return keys_ref[pl.ds(pl.multiple_of(kb * t, t), t), :]

    def fill(kb, carry):
        keys_ref[pl.ds(pl.multiple_of(kb * t, t), t), :] = index_keys(kb)
        return carry

    lax.fori_loop(0, qi, fill, 0)
    krow = lax.broadcasted_iota(I32, (t, t), 0)
    qcol = lax.broadcasted_iota(I32, (t, t), 1)
    keys_ref[pl.ds(pl.multiple_of(qi * t, t), t), :] = jnp.where(krow <= qcol, index_keys(qi), INT_MIN)

    def count(fn):
        def body(kb, cnt):
            return cnt + _fold_rows(fn(kb, chunk(kb)))
        cnt = lax.fori_loop(0, qi + 1, body, jnp.zeros((SUBLANES, t), I32))
        return jnp.sum(cnt.astype(F32), axis=0, keepdims=True)

    def search_cond(st):
        i, _, _, n_open = st
        return jnp.logical_and(i < 32, n_open > 0.0)

    def search_body(st):
        i, tau, open_, _ = st
        cand = tau + jnp.left_shift(jnp.int32(1), 31 - i)
        c = count(lambda kb, blk: jnp.where(blk >= cand, 1, 0))
        tau = jnp.where(open_ > 0.0, jnp.where(c >= kf, cand, tau), tau)
        open_ = jnp.where(c == kf, 0.0, open_)
        return i + 1, tau, open_, jnp.max(open_)

    _, tau, _, _ = lax.while_loop(
        search_cond, search_body,
        (jnp.int32(0), jnp.full((1, t), INT_MIN, I32), jnp.ones((1, t), F32), jnp.float32(1.0)))
    c_ge = count(lambda kb, blk: jnp.where(blk >= tau, 1, 0))

    @pl.when(jnp.max(c_ge) > kf)
    def _():
        need = kf - count(lambda kb, blk: jnp.where(blk > tau, 1, 0))

        def j_bit(i, ju):
            cand = ju | jnp.left_shift(jnp.int32(1), idx_bits - 1 - i)
            c = count(lambda kb, blk: jnp.where(blk == tau, jnp.where((krow + kb * t) < cand, 1, 0), 0))
            return jnp.where(c < need, cand, ju)

        ju = lax.fori_loop(0, idx_bits, j_bit, jnp.zeros((1, t), I32))

        def retire(kb, carry):
            blk = chunk(kb)
            late = jnp.where((krow + kb * t) > ju, INT_MIN, blk)
            keys_ref[pl.ds(pl.multiple_of(kb * t, t), t), :] = jnp.where(blk == tau, late, blk)
            return carry

        lax.fori_loop(0, qi + 1, retire, 0)

    qs4_ref[...] = jnp.zeros(qs4_ref.shape, BF16)
    for h in range(SA_HEADS):
        qs4_ref[0:SA_HEAD_DIM, h * t:(h + 1) * t] = saq_ref[h * SA_HEAD_DIM:(h + 1) * SA_HEAD_DIM, :]
    m_ref[...] = jnp.full(m_ref.shape, NEG, F32)
    acc_ref[...] = jnp.zeros(acc_ref.shape, F32)

    def scores(kb, near=None):
        s = _dot(sak_ref[pl.ds(pl.multiple_of(kb * t, t), t), :], qs4_ref[...])
        sel = chunk(kb) >= tau
        parts = []
        for h in range(SA_HEADS):
            sh = s[:, h * t:(h + 1) * t]
            if near is not None:
                sh = sh + bn_ref[h, near]
            parts.append(jnp.where(sel, sh, NEG))
        return jnp.concatenate(parts, axis=1)

    far = jnp.concatenate([jnp.full((1, t), far_ref[h], F32) for h in range(SA_HEADS)], axis=1)
    _far_chunks(jnp.maximum(qi - 1, 0), scores, lambda kb: sav_ref[kb], m_ref, acc_ref, far)

    @pl.when(qi >= 1)
    def _():
        _softmax_step(scores(qi - 1, 1), sav_ref[qi - 1], m_ref, acc_ref, None)

    _softmax_step(scores(qi, 0), sav_ref[qi], m_ref, acc_ref, None)

    acc = acc_ref[...]
    o = acc[0:SA_HEAD_DIM] / acc[SA_HEAD_DIM:SA_HEAD_DIM + 1]
    o = jnp.concatenate([o[:, h * t:(h + 1) * t] for h in range(SA_HEADS)], axis=0)
    o_ref[...] = o.T.astype(o_ref.dtype)


def _sparse_attention(ixq_t, ixw_t, ixk, saq_t, sak, sav_t, bias_near, bias_far, *, bsz, seq, t, topk):
    nq = seq // t
    smem = pl.BlockSpec(memory_space=pltpu.SMEM)
    qcols = lambda r: pl.BlockSpec((r, t), lambda b, i: (0, b * nq + i))
    full = lambda w: pl.BlockSpec((None, seq, w), lambda b, i: (b, 0, 0))
    va = SA_HEAD_DIM + BF16_ROWS
    return pl.pallas_call(
        functools.partial(_dsa_kernel, t=t, topk=topk, idx_bits=max(1, (seq - 1).bit_length())),
        grid=(bsz, nq),
        in_specs=[
            smem,
            qcols(IDX_HEADS * IDX_DIM), qcols(IDX_HEADS), full(LANES),
            qcols(SA_WIDTH), full(LANES),
            pl.BlockSpec((None, nq, SA_HEAD_DIM, t), lambda b, i: (b, 0, 0, 0)),
            _resident((SA_HEADS, 2, t, t)),
        ],
        out_specs=pl.BlockSpec((None, t, SA_WIDTH), lambda b, i: (b, i, 0)),
        out_shape=jax.ShapeDtypeStruct((bsz, seq, SA_WIDTH), BF16),
        scratch_shapes=[
            pltpu.VMEM((LANES, IDX_HEADS * t), BF16),
            pltpu.VMEM((seq, t), I32),
            pltpu.VMEM((LANES, SA_HEADS * t), BF16),
            pltpu.VMEM((1, SA_HEADS * t), F32),
            pltpu.VMEM((va, SA_HEADS * t), F32),
        ],
        compiler_params=_params("parallel", "arbitrary"),
        name="sparse_attention",
    )(bias_far, ixq_t, ixw_t, ixk, saq_t, sak, sav_t, bias_near)


def _ssm_state_kernel(u_ref, w_ref, o_ref):
    o_ref[...] = _dot(u_ref[...].astype(BF16), w_ref[...])


def _ssm_scan_kernel(loc_ref, a1_ref, a2_ref, o_ref):
    a1 = a1_ref[...]
    a2 = a2_ref[...]

    def body(n, s):
        o_ref[n] = s
        return a1 * s + a2 * pltpu.roll(s, SSM_STATE, axis=1) + loc_ref[n]

    lax.fori_loop(0, loc_ref.shape[0], body, jnp.zeros(a1.shape, F32))


def _ssm_out_kernel(u_ref, s_ref, toep_ref, wout_ref, o_ref):
    o_ref[...] = (_dot(u_ref[...].astype(BF16), toep_ref[...])
                  + _dot(s_ref[...].astype(BF16), wout_ref[...]))


def _ssm_gate_kernel(y_ref, u_ref, d_ref, w_ref, o_ref):
    y = jax.nn.gelu(y_ref[...] + d_ref[...] * u_ref[...])
    o_ref[...] = (y * jax.nn.sigmoid(_dot(y.astype(BF16), w_ref[...]))).astype(o_ref.dtype)


def _ssm_tables(lam_re, lam_im, log_dt, b_re, b_im, c_re, c_im, tc):
    hp = lax.Precision.HIGHEST
    dt = jnp.exp(log_dt)[:, None]
    n = jnp.arange(tc + 1, dtype=F32)[:, None, None]
    mag = jnp.exp(lam_re * dt * n)
    pw_re = mag * jnp.cos(lam_im * dt * n)
    pw_im = mag * jnp.sin(lam_im * dt * n)
    den = lam_re * lam_re + lam_im * lam_im
    nr, ni = pw_re[1] - 1.0, pw_im[1]
    f_re = (nr * lam_re + ni * lam_im) / den
    f_im = (ni * lam_re - nr * lam_im) / den
    bb_re = f_re[..., None] * b_re - f_im[..., None] * b_im
    bb_im = f_re[..., None] * b_im + f_im[..., None] * b_re
    ca_re = c_re[None] * pw_re[:, :, None, :] - c_im[None] * pw_im[:, :, None, :]
    ca_im = c_re[None] * pw_im[:, :, None, :] + c_im[None] * pw_re[:, :, None, :]
    kern = (jnp.einsum('tgcp,gpd->tgcd', ca_re[:tc], bb_re, precision=hp)
            - jnp.einsum('tgcp,gpd->tgcd', ca_im[:tc], bb_im, precision=hp))
    groups = lam_re.shape[0]
    lag = jnp.arange(tc)[None, :] - jnp.arange(tc)[:, None]
    onehot = (lag[:, :, None] == jnp.arange(tc)[None, None, :]).astype(F32)
    toep = jnp.einsum('stk,kgcd->gsdtc', onehot, kern, precision=hp)
    toep = toep.reshape(groups, tc * SSM_GROUP, tc * SSM_GROUP)
    rev_re, rev_im = pw_re[tc - 1::-1][:tc], pw_im[tc - 1::-1][:tc]
    ws_re = rev_re[..., None] * bb_re[None] - rev_im[..., None] * bb_im[None]
    ws_im = rev_re[..., None] * bb_im[None] + rev_im[..., None] * bb_re[None]
    wstate = jnp.concatenate([ws_re, ws_im], axis=2)
    wstate = wstate.transpose(1, 0, 3, 2).reshape(groups, tc * SSM_GROUP, 2 * SSM_STATE)
    wout = jnp.concatenate([ca_re[1:], -ca_im[1:]], axis=3)
    wout = wout.transpose(1, 3, 0, 2).reshape(groups, 2 * SSM_STATE, tc * SSM_GROUP)
    dec_re, dec_im = pw_re[tc], pw_im[tc]
    a1 = jnp.concatenate([dec_re, dec_re], axis=1)
    a2 = jnp.concatenate([-dec_im, dec_im], axis=1)
    return toep.astype(BF16), wstate.astype(BF16), wout.astype(BF16), a1, a2


def _ssm(u, tables, d_skip, w_glu, *, bsz, seq, tm):
    toep, wstate, wout, a1, a2 = tables
    groups = toep.shape[0]
    tc = SSM_CHUNK
    nc = seq // tc
    rows = bsz * nc
    cw = tc * SSM_GROUP
    ug = u.reshape(bsz, nc, tc, groups, SSM_GROUP).transpose(3, 0, 1, 2, 4).reshape(groups, rows, cw)
    gspec = lambda r, c: pl.BlockSpec((None, r, c), lambda g: (g, 0, 0))
    loc = pl.pallas_call(
        _ssm_state_kernel,
        grid=(groups,),
        in_specs=[gspec(rows, cw), gspec(cw, 2 * SSM_STATE)],
        out_specs=gspec(rows, 2 * SSM_STATE),
        out_shape=jax.ShapeDtypeStruct((groups, rows, 2 * SSM_STATE), F32),
        compiler_params=_params("parallel"),
        name="ssm_chunk_state",
    )(ug, wstate)
    gb = groups * bsz
    loc_t = loc.reshape(groups, bsz, nc, 2 * SSM_STATE).transpose(2, 0, 1, 3).reshape(nc, gb, 2 * SSM_STATE)
    a1r = jnp.repeat(a1, bsz, axis=0)
    a2r = jnp.repeat(a2, bsz, axis=0)
    rb = 8 if gb % 8 == 0 else gb
    prev = pl.pallas_call(
        _ssm_scan_kernel,
        grid=(gb // rb,),
        in_specs=[pl.BlockSpec((nc, rb, 2 * SSM_STATE), lambda i: (0, i, 0)),
                  pl.BlockSpec((rb, 2 * SSM_STATE), lambda i: (i, 0)),
                  pl.BlockSpec((rb, 2 * SSM_STATE), lambda i: (i, 0))],
        out_specs=pl.BlockSpec((nc, rb, 2 * SSM_STATE), lambda i: (0, i, 0)),
        out_shape=jax.ShapeDtypeStruct((nc, gb, 2 * SSM_STATE), F32),
        compiler_params=_params("parallel"),
        name="ssm_chunk_scan",
    )(loc_t, a1r, a2r)
    prev_g = prev.reshape(nc, groups, bsz, 2 * SSM_STATE).transpose(1, 2, 0, 3).reshape(groups, rows, 2 * SSM_STATE)
    y = pl.pallas_call(
        _ssm_out_kernel,
        grid=(groups,),
        in_specs=[gspec(rows, cw), gspec(rows, 2 * SSM_STATE), gspec(cw, cw), gspec(2 * SSM_STATE, cw)],
        out_specs=gspec(rows, cw),
        out_shape=jax.ShapeDtypeStruct((groups, rows, cw), F32),
        compiler_params=_params("parallel"),
        name="ssm_chunk_out",
    )(ug, prev_g, toep, wout)
    y = y.reshape(groups, bsz, nc, tc, SSM_GROUP).transpose(1, 2, 3, 0, 4).reshape(bsz * seq, groups * SSM_GROUP)
    width = groups * SSM_GROUP
    return pl.pallas_call(
        _ssm_gate_kernel,
        grid=(bsz * seq // tm,),
        in_specs=[pl.BlockSpec((tm, width), lambda i: (i, 0)),
                  pl.BlockSpec((tm, width), lambda i: (i, 0)),
                  _resident((1, width)), _resident((width, width))],
        out_specs=pl.BlockSpec((tm, width), lambda i: (i, 0)),
        out_shape=jax.ShapeDtypeStruct((bsz * seq, width), BF16),
        compiler_params=_params("parallel"),
        name="ssm_gate",
    )(y, u, d_skip, w_glu)


def _outproj_kernel(x_ref, da_ref, ssm_ref, sa_ref, w1, w2, w3, g_ref, b_ref, o_ref):
    mix = _dot(da_ref[...], w1[...]) + _dot(ssm_ref[...], w2[...]) + _dot(sa_ref[...], w3[...])
    o_ref[...] = _layer_norm(ALPHA * x_ref[...] + mix, g_ref[...], b_ref[...])


def _outproj(x, o_da, o_ssm, o_sa, w1, w2, w3, g, b, *, tm):
    t_rows, d = x.shape
    row = lambda w: pl.BlockSpec((tm, w), lambda i: (i, 0))
    return pl.pallas_call(
        _outproj_kernel,
        grid=(t_rows // tm,),
        in_specs=[row(d), row(o_da.shape[1]), row(o_ssm.shape[1]), row(o_sa.shape[1]),
                  _resident(w1.shape), _resident(w2.shape), _resident(w3.shape),
                  _resident(g.shape), _resident(b.shape)],
        out_specs=row(d),
        out_shape=jax.ShapeDtypeStruct((t_rows, d), F32),
        compiler_params=_params("parallel"),
        name="out_proj_ln",
    )(x, o_da, o_ssm, o_sa, w1, w2, w3, g, b)


def _t5_bucket(n):
    max_exact = REL_BUCKETS // 2
    nf = jnp.maximum(n, 1).astype(F32)
    large = max_exact + (jnp.log(nf / max_exact) / math.log(REL_MAX_DIST / max_exact)
                         * (REL_BUCKETS - max_exact)).astype(I32)
    large = jnp.minimum(large, REL_BUCKETS - 1)
    return jnp.where(n < max_exact, n, large)


def _bias_tiles(table, t):
    assert t >= REL_MAX_DIST
    table = table.astype(F32) * LOG2E
    key = jnp.arange(t)[:, None]
    query = jnp.arange(t)[None, :]
    dist = jnp.stack([query - key, t + query - key])
    bucket = _t5_bucket(jnp.maximum(dist, 0))
    tiles = jnp.zeros((table.shape[1],) + dist.shape, F32)
    for b in range(REL_BUCKETS):
        tiles = jnp.where(bucket[None] == b, table[b][:, None, None, None], tiles)
    return jnp.where(dist[None] >= 0, tiles, NEG), table[REL_BUCKETS - 1]


def _split_w_in(w_in):
    offs = [0]
    for s in IN_SIZES:
        offs.append(offs[-1] + s)
    da_q, da_k, da_v, ssm_u, sa_q, sa_k, sa_v, ix_q, ix_k, ix_w = (
        w_in[:, offs[j]:offs[j + 1]] for j in range(len(IN_SIZES)))
    w_scale = IDX_HEADS ** -0.5 * IDX_DIM ** -0.5
    pad = lambda w: jnp.pad(w, ((0, 0), (0, LANES - w.shape[1])))
    weights = [(da_q * (DA_QK_DIM ** -0.5 * LOG2E)).T, da_k, da_v.T, ssm_u,
               (sa_q * (SA_HEAD_DIM ** -0.5 * LOG2E)).T, pad(sa_k), sa_v.T,
               ix_q.T, pad(ix_k), (ix_w * w_scale).T]
    dtypes = [BF16, BF16, BF16, F32, BF16, BF16, BF16, BF16, BF16, F32]
    transposed = [True, False, True, False, True, False, True, True, False, True]
    return [w.astype(BF16) for w in weights], dtypes, transposed


def _value_chunks(v_t, rows, bsz, seq, t):
    g = v_t.shape[0] // rows
    return v_t.reshape(g, rows, bsz, seq // t, t).transpose(2, 0, 3, 1, 4)


def _plan(bsz, seq):
    rows = bsz * seq
    tm = 512 if rows % 512 == 0 else rows
    t_da = 1024 if seq % 1024 == 0 and seq >= 4096 else 256 if seq % 256 == 0 else 128
    t_sa = 512 if seq % 512 == 0 and seq >= 2048 else 256 if seq % 256 == 0 else 128
    return dict(tm=tm, t_da=t_da, t_sa=t_sa)


def kernel(x, p, rel_bias, ffn1_w_gate, ffn1_w_up, ffn1_w_down, ln1_g, ln1_b, w_in, w_o, da_lam_q1, da_lam_k1, da_lam_q2, da_lam_k2, da_subln_g, ssm_lam_re, ssm_lam_im, ssm_log_dt, ssm_b_re, ssm_b_im, ssm_c_re, ssm_c_im, ssm_d, ssm_w_glu, ln2_g, ln2_b, ffn2_w_gate, ffn2_w_up, ffn2_w_down, ple_w_proj, ple_w_gate, ln3_g, ln3_b):
    bsz, seq, d = x.shape
    rows = bsz * seq
    plan = _plan(bsz, seq)
    tm, t_da, t_sa = plan["tm"], plan["t_da"], plan["t_sa"]
    topk = min(TOPK_MAX, seq // 4)
    da_near, da_far = _bias_tiles(rel_bias[:, :DA_HEADS], t_da)
    sa_near, sa_far = _bias_tiles(rel_bias[:, DA_HEADS:], t_sa)
    vec = lambda a: a.reshape(1, -1).astype(F32)
    da_w = DA_HEADS * DA_V_DIM

    h = x.reshape(rows, d)
    for i in range(DEPTH):
        lam_init = 0.8 - 0.6 * math.exp(-0.3 * i)
        h = _ffn(h, ffn1_w_gate[i].astype(BF16), ffn1_w_up[i].astype(BF16), ffn1_w_down[i].astype(BF16),
                 vec(ln1_g[i]), vec(ln1_b[i]), tm=tm)

        weights, dtypes, transposed = _split_w_in(w_in[i])
        (da_q, da_k, da_v, ssm_u, sa_q, sa_k, sa_v, ix_q, ix_k, ix_w) = _inproj(
            h, weights, dtypes, transposed, tm=tm)
        b3 = lambda a: a.reshape(bsz, seq, a.shape[-1])

        lam = (jnp.exp(jnp.sum(da_lam_q1[i].astype(F32) * da_lam_k1[i]))
               - jnp.exp(jnp.sum(da_lam_q2[i].astype(F32) * da_lam_k2[i])) + lam_init)
        subln = jnp.broadcast_to(da_subln_g[i].astype(F32)[:, None], (DA_V_DIM, t_da))
        o_da = _diff_attention(da_q, b3(da_k), _value_chunks(da_v, DA_V_DIM, bsz, seq, t_da), da_near, da_far,
                               lam.reshape(1).astype(F32), subln, bsz=bsz, seq=seq, t=t_da,
                               post_scale=1.0 - lam_init)

        tables = _ssm_tables(ssm_lam_re[i].astype(F32), ssm_lam_im[i].astype(F32), ssm_log_dt[i].astype(F32),
                             ssm_b_re[i].astype(F32), ssm_b_im[i].astype(F32),
                             ssm_c_re[i].astype(F32), ssm_c_im[i].astype(F32), SSM_CHUNK)
        o_ssm = _ssm(ssm_u, tables, vec(ssm_d[i]), ssm_w_glu[i].astype(BF16), bsz=bsz, seq=seq, tm=tm)

        sav = _value_chunks(sa_v, SA_HEAD_DIM, bsz, seq, t_sa)[:, 0]
        o_sa = _sparse_attention(ix_q, ix_w, b3(ix_k), sa_q, b3(sa_k), sav, sa_near, sa_far,
                                 bsz=bsz, seq=seq, t=t_sa, topk=topk)

        wo = w_o[i].astype(BF16)
        h = _outproj(h, o_da.reshape(rows, da_w), o_ssm, o_sa.reshape(rows, SA_WIDTH),
                     wo[:da_w], wo[da_w:da_w + SSM_WIDTH], wo[da_w + SSM_WIDTH:],
                     vec(ln2_g[i]), vec(ln2_b[i]), tm=tm)

        h = _ffn(h, ffn2_w_gate[i].astype(BF16), ffn2_w_up[i].astype(BF16), ffn2_w_down[i].astype(BF16),
                 vec(ln3_g[i]), vec(ln3_b[i]),
                 ple=(p[i].reshape(rows, -1), ple_w_proj[i].astype(BF16), ple_w_gate[i].astype(BF16)), tm=tm)
    return h.reshape(bsz, seq, d)
```

```python
import functools
import math

import jax
import jax.numpy as jnp
from jax import lax
from jax.experimental import pallas as pl
from jax.experimental.pallas import tpu as pltpu

F32 = jnp.float32
BF16 = jnp.bfloat16
I32 = jnp.int32

DEPTH = 2
DA_QK_DIM = 64
DA_V_DIM = 2 * DA_QK_DIM
DA_HEADS = 4
SSM_GROUP = 16
SSM_GROUPS = 16
SSM_STATE = 64
SSM_WIDTH = SSM_GROUP * SSM_GROUPS
SA_HEAD_DIM = 64
SA_HEADS = 4
SA_WIDTH = SA_HEADS * SA_HEAD_DIM
IDX_HEADS = 8
IDX_DIM = 32
TOPK_MAX = 256
REL_BUCKETS = 32
REL_MAX_DIST = 128
ALPHA = (2 * DEPTH) ** 0.25
LN_EPS = 1e-5
IN_SIZES = (DA_HEADS * 2 * DA_QK_DIM, DA_HEADS * 2 * DA_QK_DIM, DA_HEADS * DA_V_DIM,
            SSM_WIDTH, SA_WIDTH, SA_HEAD_DIM, SA_HEAD_DIM,
            IDX_HEADS * IDX_DIM, IDX_DIM, IDX_HEADS)

LANES = 128
SUBLANES = 8
BF16_ROWS = 16
NEG = -1e30
INT_MIN = -2 ** 31
HI_MASK = -2 ** 16
LOG2E = math.log2(math.e)
VMEM_LIMIT = 56 * 1024 * 1024
SSM_CHUNK = 64


def _params(*sem):
    return pltpu.CompilerParams(dimension_semantics=sem, vmem_limit_bytes=VMEM_LIMIT)


def _resident(shape):
    return pl.BlockSpec(shape, lambda *_: (0,) * len(shape), pipeline_mode=pl.Buffered(1))


def _dot(a, b):
    return jnp.dot(a, b, preferred_element_type=F32)


def _dot_nt(a, b):
    return lax.dot_general(a, b, (((1,), (1,)), ((), ())), preferred_element_type=F32)


def _layer_norm(y, g, b):
    mu = jnp.mean(y, axis=-1, keepdims=True)
    d = y - mu
    var = jnp.mean(d * d, axis=-1, keepdims=True)
    return d * lax.rsqrt(var + LN_EPS) * g + b


def _fold_rows(x, rows):
    while x.shape[0] > rows:
        half = x.shape[0] // 2
        x = x[:half] + x[half:]
    return x


def _ffn_kernel(*refs, has_ple, ff_chunk):
    if has_ple:
        x_ref, p_ref, wg, wu, wd, wpp, wpg, g_ref, b_ref, o_ref = refs
    else:
        x_ref, wg, wu, wd, g_ref, b_ref, o_ref = refs
    x = x_ref[...]
    xb = x.astype(BF16)
    d_ff = wg.shape[1]
    acc = None
    for c0 in range(0, d_ff, ff_chunk):
        c1 = min(c0 + ff_chunk, d_ff)
        gate = _dot(xb, wg[:, c0:c1])
        up = _dot(xb, wu[:, c0:c1])
        hid = (gate * jax.nn.sigmoid(gate) * up).astype(BF16)
        part = _dot(hid, wd[c0:c1, :])
        acc = part if acc is None else acc + part
    y = ALPHA * x + 0.5 * acc
    if has_ple:
        y = y + _dot(p_ref[...].astype(BF16), wpp[...]) * jax.nn.sigmoid(_dot(xb, wpg[...]))
    o_ref[...] = _layer_norm(y, g_ref[...], b_ref[...])


def _ffn(x, wg, wu, wd, g, b, ple=None, *, tm):
    t_rows, d = x.shape
    d_ff = wg.shape[1]
    row = lambda w: pl.BlockSpec((tm, w), lambda i: (i, 0))
    args = [x]
    specs = [row(d)]
    if ple is not None:
        p, wpp, wpg = ple
        args.append(p)
        specs.append(row(p.shape[1]))
    args += [wg, wu, wd]
    specs += [_resident(wg.shape), _resident(wu.shape), _resident(wd.shape)]
    if ple is not None:
        args += [wpp, wpg]
        specs += [_resident(wpp.shape), _resident(wpg.shape)]
    args += [g, b]
    specs += [_resident(g.shape), _resident(b.shape)]
    return pl.pallas_call(
        functools.partial(_ffn_kernel, has_ple=ple is not None, ff_chunk=min(512, d_ff)),
        grid=(t_rows // tm,),
        in_specs=specs,
        out_specs=row(d),
        out_shape=jax.ShapeDtypeStruct((t_rows, d), F32),
        compiler_params=_params("parallel"),
        name="ffn_ple_ln" if ple is not None else "ffn_ln",
    )(*args)


def _inproj_kernel(x_ref, *refs, transposed):
    n = len(refs) // 2
    xb = x_ref[...].astype(BF16)
    for w_ref, o_ref, tr in zip(refs[:n], refs[n:], transposed):
        out = _dot_nt(w_ref[...], xb) if tr else _dot(xb, w_ref[...])
        o_ref[...] = out.astype(o_ref.dtype)


def _inproj(x, weights, dtypes, transposed, *, tm):
    t_rows, d = x.shape
    out_specs, out_shape = [], []
    for w, dt, tr in zip(weights, dtypes, transposed):
        if tr:
            out_specs.append(pl.BlockSpec((w.shape[0], tm), lambda i: (0, i)))
            out_shape.append(jax.ShapeDtypeStruct((w.shape[0], t_rows), dt))
        else:
            out_specs.append(pl.BlockSpec((tm, w.shape[1]), lambda i: (i, 0)))
            out_shape.append(jax.ShapeDtypeStruct((t_rows, w.shape[1]), dt))
    return pl.pallas_call(
        functools.partial(_inproj_kernel, transposed=tuple(transposed)),
        grid=(t_rows // tm,),
        in_specs=[pl.BlockSpec((tm, d), lambda i: (i, 0))] + [_resident(w.shape) for w in weights],
        out_specs=out_specs,
        out_shape=out_shape,
        compiler_params=_params("parallel"),
        name="in_proj",
    )(x, *weights)


def _softmax_probs(s, m_ref, shift):
    m_prev = m_ref[...]
    m_cur = jnp.max(s, axis=0, keepdims=True)
    if shift is not None:
        m_cur = m_cur + shift
    m_next = jnp.maximum(m_prev, m_cur)
    m_ref[...] = m_next
    p = jnp.exp2(s - (m_next if shift is None else m_next - shift))
    return p.astype(BF16), jnp.exp2(m_prev - m_next)


def _accumulate(acc_ref, rescale, v_t, p):
    v_aug = jnp.concatenate([v_t, jnp.ones((BF16_ROWS, v_t.shape[1]), v_t.dtype)], axis=0)
    acc_ref[...] = rescale * acc_ref[...] + _dot(v_aug, p)


def _softmax_step(s, v_t, m_ref, acc_ref, shift):
    p, rescale = _softmax_probs(s, m_ref, shift)
    _accumulate(acc_ref, rescale, v_t, p)


def _far_chunks(n, scores, values, m_ref, acc_ref, shift):
    def body(i, carry):
        _softmax_step(scores(i), values(i), m_ref, acc_ref, shift)
        return carry

    lax.fori_loop(0, n, body, 0)


def _da_kernel(far_ref, lam_ref, q_ref, k_ref, v_ref, bn_ref, g_ref, o_ref,
               qd_ref, m_ref, acc_ref, *, t, post_scale):
    h = pl.program_id(1)
    qi = pl.program_id(2)
    q = q_ref[...].astype(F32)
    rowq = lax.broadcasted_iota(I32, q.shape, 0)
    qd_ref[:, 0:t] = jnp.where(rowq < DA_QK_DIM, q, 0.0).astype(BF16)
    qd_ref[:, t:2 * t] = jnp.where(rowq >= DA_QK_DIM, q, 0.0).astype(BF16)
    m_ref[...] = jnp.full(m_ref.shape, NEG, F32)
    acc_ref[...] = jnp.zeros(acc_ref.shape, F32)
    far = far_ref[h]

    def scores(kb):
        return _dot(k_ref[pl.ds(pl.multiple_of(kb * t, t), t), :], qd_ref[...])

    def near_block(kb, near):
        bias = bn_ref[near]
        _softmax_step(scores(kb) + jnp.concatenate([bias, bias], axis=1), v_ref[kb], m_ref, acc_ref, None)

    _far_chunks(jnp.maximum(qi - 1, 0), scores, lambda kb: v_ref[kb], m_ref, acc_ref, far)

    @pl.when(qi >= 1)
    def _():
        near_block(qi - 1, 1)

    near_block(qi, 0)

    acc = acc_ref[...]
    o = acc[0:DA_V_DIM] / acc[DA_V_DIM:DA_V_DIM + 1]
    o = o[:, 0:t] - lam_ref[0] * o[:, t:2 * t]
    ms = jnp.mean(o * o, axis=0, keepdims=True)
    o = o * lax.rsqrt(ms + LN_EPS) * g_ref[...] * post_scale
    o_ref[...] = o.T.astype(o_ref.dtype)


def _diff_attention(q_t, k, v_t, bias_near, bias_far, lam, subln_g, *, bsz, seq, t, post_scale):
    heads = q_t.shape[0] // DA_V_DIM
    nq = seq // t
    smem = pl.BlockSpec(memory_space=pltpu.SMEM)
    va = DA_V_DIM + BF16_ROWS
    return pl.pallas_call(
        functools.partial(_da_kernel, t=t, post_scale=post_scale),
        grid=(bsz, heads, nq),
        in_specs=[
            smem, smem,
            pl.BlockSpec((DA_V_DIM, t), lambda b, h, i: (h, b * nq + i)),
            pl.BlockSpec((None, seq, DA_V_DIM), lambda b, h, i: (b, 0, h)),
            pl.BlockSpec((None, None, nq, DA_V_DIM, t), lambda b, h, i: (b, h, 0, 0, 0)),
            pl.BlockSpec((None, 2, t, t), lambda b, h, i: (h, 0, 0, 0)),
            pl.BlockSpec((DA_V_DIM, t), lambda b, h, i: (0, 0)),
        ],
        out_specs=pl.BlockSpec((None, t, DA_V_DIM), lambda b, h, i: (b, i, h)),
        out_shape=jax.ShapeDtypeStruct((bsz, seq, heads * DA_V_DIM), BF16),
        scratch_shapes=[
            pltpu.VMEM((DA_V_DIM, 2 * t), BF16),
            pltpu.VMEM((1, 2 * t), F32),
            pltpu.VMEM((va, 2 * t), F32),
        ],
        compiler_params=_params("parallel", "parallel", "arbitrary"),
        name="diff_attention",
    )(bias_far, lam, q_t, k, v_t, bias_near, subln_g)


def _dsa_kernel(far_ref, ixq_ref, ixw_ref, ixk_ref, saq_ref, sak_ref, sav_ref, bn_ref, o_ref,
                qi8_ref, keys_ref, hi_ref, qs4_ref, m_ref, acc_ref, *, t, topk, idx_bits):
    qi = pl.program_id(1)
    kf = float(topk)

    qi8_ref[...] = jnp.zeros(qi8_ref.shape, BF16)
    for h in range(IDX_HEADS):
        qi8_ref[0:IDX_DIM, h * t:(h + 1) * t] = ixq_ref[h * IDX_DIM:(h + 1) * IDX_DIM, :]

    def index_keys(kb):
        kblk = ixk_ref[pl.ds(pl.multiple_of(kb * t, t), t), :]
        sc = None
        for h in range(IDX_HEADS):
            r = _dot(kblk, qi8_ref[:, h * t:(h + 1) * t])
            term = ixw_ref[h:h + 1, :] * jnp.maximum(r, 0.0)
            sc = term if sc is None else sc + term
        return jnp.where(sc == 0.0, 0.0, sc)

    def rows(kb):
        return pl.ds(pl.multiple_of(kb * t, t), t)

    def chunk(kb):
        return keys_ref[rows(kb), :]

    def store_keys(kb, sc):
        bits = lax.bitcast_convert_type(sc, I32)
        keys_ref[rows(kb), :] = bits ^ ((bits >> 31) & 0x7FFFFFFF)
        hi_ref[rows(kb), :] = lax.bitcast_convert_type(bits & HI_MASK, F32).astype(BF16)

    def fill(kb, carry):
        store_keys(kb, index_keys(kb))
        return carry

    lax.fori_loop(0, qi, fill, 0)
    krow = lax.broadcasted_iota(I32, (t, t), 0)
    qcol = lax.broadcasted_iota(I32, (t, t), 1)
    store_keys(qi, jnp.where(krow <= qcol, index_keys(qi), -jnp.inf))

    def count(fn):
        def body(kb, cnt):
            return cnt + _fold_rows(fn(kb, chunk(kb)), SUBLANES)
        cnt = lax.fori_loop(0, qi + 1, body, jnp.zeros((SUBLANES, t), I32))
        return jnp.sum(cnt.astype(F32), axis=0, keepdims=True)

    def count_hi(thr):
        def body(kb, cnt):
            hit = jnp.where(hi_ref[rows(kb), :] >= thr, jnp.ones((), BF16), jnp.zeros((), BF16))
            return cnt + _fold_rows(hit, BF16_ROWS).astype(F32)
        cnt = lax.fori_loop(0, qi + 1, body, jnp.zeros((BF16_ROWS, t), F32))
        return jnp.sum(cnt, axis=0, keepdims=True)

    def hi_bit(i, st):
        code, open_ = st
        cand = code + jnp.left_shift(jnp.int32(1), 15 - i)
        pattern = cand ^ ((cand >> 31) & 0x7FFF)
        thr = lax.bitcast_convert_type(jnp.left_shift(pattern, 16), F32).astype(BF16)
        c = count_hi(thr)
        code = jnp.where(open_ > 0.0, jnp.where(c >= kf, cand, code), code)
        return code, jnp.where(c == kf, 0.0, open_)

    code, open_ = lax.fori_loop(0, 16, hi_bit, (jnp.full((1, t), -2 ** 15, I32), jnp.ones((1, t), F32)))

    def search_cond(st):
        i, _, _, n_open = st
        return jnp.logical_and(i < 32, n_open > 0.0)

    def search_body(st):
        i, tau, open_, _ = st
        cand = tau + jnp.left_shift(jnp.int32(1), 31 - i)
        c = count(lambda kb, blk: jnp.where(blk >= cand, 1, 0))
        tau = jnp.where(open_ > 0.0, jnp.where(c >= kf, cand, tau), tau)
        open_ = jnp.where(c == kf, 0.0, open_)
        return i + 1, tau, open_, jnp.max(open_)

    _, tau, _, _ = lax.while_loop(
        search_cond, search_body, (jnp.int32(16), jnp.left_shift(code, 16), open_, jnp.max(open_)))
    c_ge = count(lambda kb, blk: jnp.where(blk >= tau, 1, 0))

    @pl.when(jnp.max(c_ge) > kf)
    def _():
        need = kf - count(lambda kb, blk: jnp.where(blk > tau, 1, 0))

        def j_bit(i, ju):
            cand = ju | jnp.left_shift(jnp.int32(1), idx_bits - 1 - i)
            c = count(lambda kb, blk: jnp.where(blk == tau, jnp.where((krow + kb * t) < cand, 1, 0), 0))
            return jnp.where(c < need, cand, ju)

        ju = lax.fori_loop(0, idx_bits, j_bit, jnp.zeros((1, t), I32))

        def retire(kb, carry):
            blk = chunk(kb)
            late = jnp.where((krow + kb * t) > ju, INT_MIN, blk)
            keys_ref[pl.ds(pl.multiple_of(kb * t, t), t), :] = jnp.where(blk == tau, late, blk)
            return carry

        lax.fori_loop(0, qi + 1, retire, 0)

    qs4_ref[...] = jnp.zeros(qs4_ref.shape, BF16)
    for h in range(SA_HEADS):
        qs4_ref[0:SA_HEAD_DIM, h * t:(h + 1) * t] = saq_ref[h * SA_HEAD_DIM:(h + 1) * SA_HEAD_DIM, :]
    m_ref[...] = jnp.full(m_ref.shape, NEG, F32)
    acc_ref[...] = jnp.zeros(acc_ref.shape, F32)

    def scores(kb, near=None):
        s = _dot(sak_ref[pl.ds(pl.multiple_of(kb * t, t), t), :], qs4_ref[...])
        sel = chunk(kb) >= tau
        parts = []
        for h in range(SA_HEADS):
            sh = s[:, h * t:(h + 1) * t]
            if near is not None:
                sh = sh + bn_ref[h, near]
            parts.append(jnp.where(sel, sh, NEG))
        return jnp.concatenate(parts, axis=1)

    far = jnp.concatenate([jnp.full((1, t), far_ref[h], F32) for h in range(SA_HEADS)], axis=1)
    _far_chunks(jnp.maximum(qi - 1, 0), scores, lambda kb: sav_ref[kb], m_ref, acc_ref, far)

    @pl.when(qi >= 1)
    def _():
        _softmax_step(scores(qi - 1, 1), sav_ref[qi - 1], m_ref, acc_ref, None)

    _softmax_step(scores(qi, 0), sav_ref[qi], m_ref, acc_ref, None)

    acc = acc_ref[...]
    o = acc[0:SA_HEAD_DIM] / acc[SA_HEAD_DIM:SA_HEAD_DIM + 1]
    o = jnp.concatenate([o[:, h * t:(h + 1) * t] for h in range(SA_HEADS)], axis=0)
    o_ref[...] = o.T.astype(o_ref.dtype)


def _sparse_attention(ixq_t, ixw_t, ixk, saq_t, sak, sav_t, bias_near, bias_far, *, bsz, seq, t, topk):
    nq = seq // t
    smem = pl.BlockSpec(memory_space=pltpu.SMEM)
    qcols = lambda r: pl.BlockSpec((r, t), lambda b, i: (0, b * nq + i))
    full = lambda w: pl.BlockSpec((None, seq, w), lambda b, i: (b, 0, 0))
    va = SA_HEAD_DIM + BF16_ROWS
    return pl.pallas_call(
        functools.partial(_dsa_kernel, t=t, topk=topk, idx_bits=max(1, (seq - 1).bit_length())),
        grid=(bsz, nq),
        in_specs=[
            smem,
            qcols(IDX_HEADS * IDX_DIM), qcols(IDX_HEADS), full(LANES),
            qcols(SA_WIDTH), full(LANES),
            pl.BlockSpec((None, nq, SA_HEAD_DIM, t), lambda b, i: (b, 0, 0, 0)),
            _resident((SA_HEADS, 2, t, t)),
        ],
        out_specs=pl.BlockSpec((None, t, SA_WIDTH), lambda b, i: (b, i, 0)),
        out_shape=jax.ShapeDtypeStruct((bsz, seq, SA_WIDTH), BF16),
        scratch_shapes=[
            pltpu.VMEM((LANES, IDX_HEADS * t), BF16),
            pltpu.VMEM((seq, t), I32),
            pltpu.VMEM((seq, t), BF16),
            pltpu.VMEM((LANES, SA_HEADS * t), BF16),
            pltpu.VMEM((1, SA_HEADS * t), F32),
            pltpu.VMEM((va, SA_HEADS * t), F32),
        ],
        compiler_params=_params("parallel", "arbitrary"),
        name="sparse_attention",
    )(bias_far, ixq_t, ixw_t, ixk, saq_t, sak, sav_t, bias_near)


def _ssm_state_kernel(u_ref, w_ref, o_ref):
    o_ref[...] = _dot(u_ref[...].astype(BF16), w_ref[...])


def _ssm_scan_kernel(loc_ref, a1_ref, a2_ref, o_ref):
    a1 = a1_ref[...]
    a2 = a2_ref[...]

    def body(n, s):
        o_ref[n] = s
        return a1 * s + a2 * pltpu.roll(s, SSM_STATE, axis=1) + loc_ref[n]

    lax.fori_loop(0, loc_ref.shape[0], body, jnp.zeros(a1.shape, F32))


def _ssm_out_kernel(u_ref, s_ref, toep_ref, wout_ref, o_ref):
    o_ref[...] = (_dot(u_ref[...].astype(BF16), toep_ref[...])
                  + _dot(s_ref[...].astype(BF16), wout_ref[...]))


def _ssm_gate_kernel(y_ref, u_ref, d_ref, w_ref, o_ref):
    y = jax.nn.gelu(y_ref[...] + d_ref[...] * u_ref[...])
    o_ref[...] = (y * jax.nn.sigmoid(_dot(y.astype(BF16), w_ref[...]))).astype(o_ref.dtype)


def _ssm_tables(lam_re, lam_im, log_dt, b_re, b_im, c_re, c_im, tc):
    hp = lax.Precision.HIGHEST
    dt = jnp.exp(log_dt)[:, None]
    n = jnp.arange(tc + 1, dtype=F32)[:, None, None]
    mag = jnp.exp(lam_re * dt * n)
    pw_re = mag * jnp.cos(lam_im * dt * n)
    pw_im = mag * jnp.sin(lam_im * dt * n)
    den = lam_re * lam_re + lam_im * lam_im
    nr, ni = pw_re[1] - 1.0, pw_im[1]
    f_re = (nr * lam_re + ni * lam_im) / den
    f_im = (ni * lam_re - nr * lam_im) / den
    bb_re = f_re[..., None] * b_re - f_im[..., None] * b_im
    bb_im = f_re[..., None] * b_im + f_im[..., None] * b_re
    ca_re = c_re[None] * pw_re[:, :, None, :] - c_im[None] * pw_im[:, :, None, :]
    ca_im = c_re[None] * pw_im[:, :, None, :] + c_im[None] * pw_re[:, :, None, :]
    kern = (jnp.einsum('tgcp,gpd->tgcd', ca_re[:tc], bb_re, precision=hp)
            - jnp.einsum('tgcp,gpd->tgcd', ca_im[:tc], bb_im, precision=hp))
    groups = lam_re.shape[0]
    lag = jnp.arange(tc)[None, :] - jnp.arange(tc)[:, None]
    onehot = (lag[:, :, None] == jnp.arange(tc)[None, None, :]).astype(F32)
    toep = jnp.einsum('stk,kgcd->gsdtc', onehot, kern, precision=hp)
    toep = toep.reshape(groups, tc * SSM_GROUP, tc * SSM_GROUP)
    rev_re, rev_im = pw_re[tc - 1::-1][:tc], pw_im[tc - 1::-1][:tc]
    ws_re = rev_re[..., None] * bb_re[None] - rev_im[..., None] * bb_im[None]
    ws_im = rev_re[..., None] * bb_im[None] + rev_im[..., None] * bb_re[None]
    wstate = jnp.concatenate([ws_re, ws_im], axis=2)
    wstate = wstate.transpose(1, 0, 3, 2).reshape(groups, tc * SSM_GROUP, 2 * SSM_STATE)
    wout = jnp.concatenate([ca_re[1:], -ca_im[1:]], axis=3)
    wout = wout.transpose(1, 3, 0, 2).reshape(groups, 2 * SSM_STATE, tc * SSM_GROUP)
    dec_re, dec_im = pw_re[tc], pw_im[tc]
    a1 = jnp.concatenate([dec_re, dec_re], axis=1)
    a2 = jnp.concatenate([-dec_im, dec_im], axis=1)
    return toep.astype(BF16), wstate.astype(BF16), wout.astype(BF16), a1, a2


def _ssm(u, tables, d_skip, w_glu, *, bsz, seq, tm):
    toep, wstate, wout, a1, a2 = tables
    groups = toep.shape[0]
    tc = SSM_CHUNK
    nc = seq // tc
    rows = bsz * nc
    cw = tc * SSM_GROUP
    ug = u.reshape(bsz, nc, tc, groups, SSM_GROUP).transpose(3, 0, 1, 2, 4).reshape(groups, rows, cw)
    gspec = lambda r, c: pl.BlockSpec((None, r, c), lambda g: (g, 0, 0))
    loc = pl.pallas_call(
        _ssm_state_kernel,
        grid=(groups,),
        in_specs=[gspec(rows, cw), gspec(cw, 2 * SSM_STATE)],
        out_specs=gspec(rows, 2 * SSM_STATE),
        out_shape=jax.ShapeDtypeStruct((groups, rows, 2 * SSM_STATE), F32),
        compiler_params=_params("parallel"),
        name="ssm_chunk_state",
    )(ug, wstate)
    gb = groups * bsz
    loc_t = loc.reshape(groups, bsz, nc, 2 * SSM_STATE).transpose(2, 0, 1, 3).reshape(nc, gb, 2 * SSM_STATE)
    a1r = jnp.repeat(a1, bsz, axis=0)
    a2r = jnp.repeat(a2, bsz, axis=0)
    rb = 8 if gb % 8 == 0 else gb
    prev = pl.pallas_call(
        _ssm_scan_kernel,
        grid=(gb // rb,),
        in_specs=[pl.BlockSpec((nc, rb, 2 * SSM_STATE), lambda i: (0, i, 0)),
                  pl.BlockSpec((rb, 2 * SSM_STATE), lambda i: (i, 0)),
                  pl.BlockSpec((rb, 2 * SSM_STATE), lambda i: (i, 0))],
        out_specs=pl.BlockSpec((nc, rb, 2 * SSM_STATE), lambda i: (0, i, 0)),
        out_shape=jax.ShapeDtypeStruct((nc, gb, 2 * SSM_STATE), F32),
        compiler_params=_params("parallel"),
        name="ssm_chunk_scan",
    )(loc_t, a1r, a2r)
    prev_g = prev.reshape(nc, groups, bsz, 2 * SSM_STATE).transpose(1, 2, 0, 3).reshape(groups, rows, 2 * SSM_STATE)
    y = pl.pallas_call(
        _ssm_out_kernel,
        grid=(groups,),
        in_specs=[gspec(rows, cw), gspec(rows, 2 * SSM_STATE), gspec(cw, cw), gspec(2 * SSM_STATE, cw)],
        out_specs=gspec(rows, cw),
        out_shape=jax.ShapeDtypeStruct((groups, rows, cw), F32),
        compiler_params=_params("parallel"),
        name="ssm_chunk_out",
    )(ug, prev_g, toep, wout)
    y = y.reshape(groups, bsz, nc, tc, SSM_GROUP).transpose(1, 2, 3, 0, 4).reshape(bsz * seq, groups * SSM_GROUP)
    width = groups * SSM_GROUP
    return pl.pallas_call(
        _ssm_gate_kernel,
        grid=(bsz * seq // tm,),
        in_specs=[pl.BlockSpec((tm, width), lambda i: (i, 0)),
                  pl.BlockSpec((tm, width), lambda i: (i, 0)),
                  _resident((1, width)), _resident((width, width))],
        out_specs=pl.BlockSpec((tm, width), lambda i: (i, 0)),
        out_shape=jax.ShapeDtypeStruct((bsz * seq, width), BF16),
        compiler_params=_params("parallel"),
        name="ssm_gate",
    )(y, u, d_skip, w_glu)


def _outproj_kernel(x_ref, da_ref, ssm_ref, sa_ref, w1, w2, w3, g_ref, b_ref, o_ref):
    mix = _dot(da_ref[...], w1[...]) + _dot(ssm_ref[...], w2[...]) + _dot(sa_ref[...], w3[...])
    o_ref[...] = _layer_norm(ALPHA * x_ref[...] + mix, g_ref[...], b_ref[...])


def _outproj(x, o_da, o_ssm, o_sa, w1, w2, w3, g, b, *, tm):
    t_rows, d = x.shape
    row = lambda w: pl.BlockSpec((tm, w), lambda i: (i, 0))
    return pl.pallas_call(
        _outproj_kernel,
        grid=(t_rows // tm,),
        in_specs=[row(d), row(o_da.shape[1]), row(o_ssm.shape[1]), row(o_sa.shape[1]),
                  _resident(w1.shape), _resident(w2.shape), _resident(w3.shape),
                  _resident(g.shape), _resident(b.shape)],
        out_specs=row(d),
        out_shape=jax.ShapeDtypeStruct((t_rows, d), F32),
        compiler_params=_params("parallel"),
        name="out_proj_ln",
    )(x, o_da, o_ssm, o_sa, w1, w2, w3, g, b)


def _t5_bucket(n):
    max_exact = REL_BUCKETS // 2
    nf = jnp.maximum(n, 1).astype(F32)
    large = max_exact + (jnp.log(nf / max_exact) / math.log(REL_MAX_DIST / max_exact)
                         * (REL_BUCKETS - max_exact)).astype(I32)
    large = jnp.minimum(large, REL_BUCKETS - 1)
    return jnp.where(n < max_exact, n, large)


def _bias_tiles(table, t):
    assert t >= REL_MAX_DIST
    table = table.astype(F32) * LOG2E
    key = jnp.arange(t)[:, None]
    query = jnp.arange(t)[None, :]
    dist = jnp.stack([query - key, t + query - key])
    bucket = _t5_bucket(jnp.maximum(dist, 0))
    tiles = jnp.zeros((table.shape[1],) + dist.shape, F32)
    for b in range(REL_BUCKETS):
        tiles = jnp.where(bucket[None] == b, table[b][:, None, None, None], tiles)
    return jnp.where(dist[None] >= 0, tiles, NEG), table[REL_BUCKETS - 1]


def _split_w_in(w_in):
    offs = [0]
    for s in IN_SIZES:
        offs.append(offs[-1] + s)
    da_q, da_k, da_v, ssm_u, sa_q, sa_k, sa_v, ix_q, ix_k, ix_w = (
        w_in[:, offs[j]:offs[j + 1]] for j in range(len(IN_SIZES)))
    w_scale = IDX_HEADS ** -0.5 * IDX_DIM ** -0.5
    pad = lambda w: jnp.pad(w, ((0, 0), (0, LANES - w.shape[1])))
    weights = [(da_q * (DA_QK_DIM ** -0.5 * LOG2E)).T, da_k, da_v.T, ssm_u,
               (sa_q * (SA_HEAD_DIM ** -0.5 * LOG2E)).T, pad(sa_k), sa_v.T,
               ix_q.T, pad(ix_k), (ix_w * w_scale).T]
    dtypes = [BF16, BF16, BF16, F32, BF16, BF16, BF16, BF16, BF16, F32]
    transposed = [True, False, True, False, True, False, True, True, False, True]
    return [w.astype(BF16) for w in weights], dtypes, transposed


def _value_chunks(v_t, rows, bsz, seq, t):
    g = v_t.shape[0] // rows
    return v_t.reshape(g, rows, bsz, seq // t, t).transpose(2, 0, 3, 1, 4)


def _plan(bsz, seq):
    rows = bsz * seq
    tm = 512 if rows % 512 == 0 else rows
    t_da = 1024 if seq % 1024 == 0 and seq >= 4096 else 256 if seq % 256 == 0 else 128
    t_sa = 512 if seq % 512 == 0 and seq >= 2048 else 256 if seq % 256 == 0 else 128
    return dict(tm=tm, t_da=t_da, t_sa=t_sa)


def kernel(x, p, rel_bias, ffn1_w_gate, ffn1_w_up, ffn1_w_down, ln1_g, ln1_b, w_in, w_o, da_lam_q1, da_lam_k1, da_lam_q2, da_lam_k2, da_subln_g, ssm_lam_re, ssm_lam_im, ssm_log_dt, ssm_b_re, ssm_b_im, ssm_c_re, ssm_c_im, ssm_d, ssm_w_glu, ln2_g, ln2_b, ffn2_w_gate, ffn2_w_up, ffn2_w_down, ple_w_proj, ple_w_gate, ln3_g, ln3_b):
    bsz, seq, d = x.shape
    rows = bsz * seq
    plan = _plan(bsz, seq)
    tm, t_da, t_sa = plan["tm"], plan["t_da"], plan["t_sa"]
    topk = min(TOPK_MAX, seq // 4)
    da_near, da_far = _bias_tiles(rel_bias[:, :DA_HEADS], t_da)
    sa_near, sa_far = _bias_tiles(rel_bias[:, DA_HEADS:], t_sa)
    vec = lambda a: a.reshape(1, -1).astype(F32)
    da_w = DA_HEADS * DA_V_DIM

    h = x.reshape(rows, d)
    for i in range(DEPTH):
        lam_init = 0.8 - 0.6 * math.exp(-0.3 * i)
        h = _ffn(h, ffn1_w_gate[i].astype(BF16), ffn1_w_up[i].astype(BF16), ffn1_w_down[i].astype(BF16),
                 vec(ln1_g[i]), vec(ln1_b[i]), tm=tm)

        weights, dtypes, transposed = _split_w_in(w_in[i])
        (da_q, da_k, da_v, ssm_u, sa_q, sa_k, sa_v, ix_q, ix_k, ix_w) = _inproj(
            h, weights, dtypes, transposed, tm=tm)
        b3 = lambda a: a.reshape(bsz, seq, a.shape[-1])

        lam = (jnp.exp(jnp.sum(da_lam_q1[i].astype(F32) * da_lam_k1[i]))
               - jnp.exp(jnp.sum(da_lam_q2[i].astype(F32) * da_lam_k2[i])) + lam_init)
        subln = jnp.broadcast_to(da_subln_g[i].astype(F32)[:, None], (DA_V_DIM, t_da))
        o_da = _diff_attention(da_q, b3(da_k), _value_chunks(da_v, DA_V_DIM, bsz, seq, t_da), da_near, da_far,
                               lam.reshape(1).astype(F32), subln, bsz=bsz, seq=seq, t=t_da,
                               post_scale=1.0 - lam_init)

        tables = _ssm_tables(ssm_lam_re[i].astype(F32), ssm_lam_im[i].astype(F32), ssm_log_dt[i].astype(F32),
                             ssm_b_re[i].astype(F32), ssm_b_im[i].astype(F32),
                             ssm_c_re[i].astype(F32), ssm_c_im[i].astype(F32), SSM_CHUNK)
        o_ssm = _ssm(ssm_u, tables, vec(ssm_d[i]), ssm_w_glu[i].astype(BF16), bsz=bsz, seq=seq, tm=tm)

        sav = _value_chunks(sa_v, SA_HEAD_DIM, bsz, seq, t_sa)[:, 0]
        o_sa = _sparse_attention(ix_q, ix_w, b3(ix_k), sa_q, b3(sa_k), sav, sa_near, sa_far,
                                 bsz=bsz, seq=seq, t=t_sa, topk=topk)

        wo = w_o[i].astype(BF16)
        h = _outproj(h, o_da.reshape(rows, da_w), o_ssm, o_sa.reshape(rows, SA_WIDTH),
                     wo[:da_w], wo[da_w:da_w + SSM_WIDTH], wo[da_w + SSM_WIDTH:],
                     vec(ln2_g[i]), vec(ln2_b[i]), tm=tm)

        h = _ffn(h, ffn2_w_gate[i].astype(BF16), ffn2_w_up[i].astype(BF16), ffn2_w_down[i].astype(BF16),
                 vec(ln3_g[i]), vec(ln3_b[i]),
                 ple=(p[i].reshape(rows, -1), ple_w_proj[i].astype(BF16), ple_w_gate[i].astype(BF16)), tm=tm)
    return h.reshape(bsz, seq, d)
```

```python
import functools
import math

import jax
import jax.numpy as jnp
from jax import lax
from jax.experimental import pallas as pl
from jax.experimental.pallas import tpu as pltpu

F32 = jnp.float32
BF16 = jnp.bfloat16
I32 = jnp.int32

DEPTH = 2
DA_QK_DIM = 64
DA_V_DIM = 2 * DA_QK_DIM
DA_HEADS = 4
SSM_GROUP = 16
SSM_GROUPS = 16
SSM_STATE = 64
SSM_WIDTH = SSM_GROUP * SSM_GROUPS
SA_HEAD_DIM = 64
SA_HEADS = 4
SA_WIDTH = SA_HEADS * SA_HEAD_DIM
IDX_HEADS = 8
IDX_DIM = 32
TOPK_MAX = 256
REL_BUCKETS = 32
REL_MAX_DIST = 128
ALPHA = (2 * DEPTH) ** 0.25
LN_EPS = 1e-5
IN_SIZES = (DA_HEADS * 2 * DA_QK_DIM, DA_HEADS * 2 * DA_QK_DIM, DA_HEADS * DA_V_DIM,
            SSM_WIDTH, SA_WIDTH, SA_HEAD_DIM, SA_HEAD_DIM,
            IDX_HEADS * IDX_DIM, IDX_DIM, IDX_HEADS)

LANES = 128
SUBLANES = 8
BF16_ROWS = 16
NEG = -1e30
INT_MIN = -2 ** 31
HI_MASK = -2 ** 16
LOG2E = math.log2(math.e)
VMEM_LIMIT = 56 * 1024 * 1024
SSM_CHUNK = 64

def _params(*sem):
    return pltpu.CompilerParams(dimension_semantics=sem, vmem_limit_bytes=VMEM_LIMIT)


def _resident(shape):
    return pl.BlockSpec(shape, lambda *_: (0,) * len(shape), pipeline_mode=pl.Buffered(1))


def _dot(a, b):
    return jnp.dot(a, b, preferred_element_type=F32)


def _dot_nt(a, b):
    return lax.dot_general(a, b, (((1,), (1,)), ((), ())), preferred_element_type=F32)


def _layer_norm(y, g, b):
    mu = jnp.mean(y, axis=-1, keepdims=True)
    d = y - mu
    var = jnp.mean(d * d, axis=-1, keepdims=True)
    return d * lax.rsqrt(var + LN_EPS) * g + b


def _fold_rows(x, rows):
    while x.shape[0] > rows:
        half = x.shape[0] // 2
        x = x[:half] + x[half:]
    return x


def _ffn_kernel(*refs, has_ple, ff_chunk):
    if has_ple:
        x_ref, p_ref, wg, wu, wd, wpp, wpg, g_ref, b_ref, o_ref = refs
    else:
        x_ref, wg, wu, wd, g_ref, b_ref, o_ref = refs
    x = x_ref[...]
    xb = x.astype(BF16)
    d_ff = wg.shape[1]
    acc = None
    for c0 in range(0, d_ff, ff_chunk):
        c1 = min(c0 + ff_chunk, d_ff)
        gate = _dot(xb, wg[:, c0:c1])
        up = _dot(xb, wu[:, c0:c1])
        hid = (gate * jax.nn.sigmoid(gate) * up).astype(BF16)
        part = _dot(hid, wd[c0:c1, :])
        acc = part if acc is None else acc + part
    y = ALPHA * x + 0.5 * acc
    if has_ple:
        y = y + _dot(p_ref[...].astype(BF16), wpp[...]) * jax.nn.sigmoid(_dot(xb, wpg[...]))
    o_ref[...] = _layer_norm(y, g_ref[...], b_ref[...])


def _ffn(x, wg, wu, wd, g, b, ple=None, *, tm):
    t_rows, d = x.shape
    d_ff = wg.shape[1]
    row = lambda w: pl.BlockSpec((tm, w), lambda i: (i, 0))
    args = [x]
    specs = [row(d)]
    if ple is not None:
        p, wpp, wpg = ple
        args.append(p)
        specs.append(row(p.shape[1]))
    args += [wg, wu, wd]
    specs += [_resident(wg.shape), _resident(wu.shape), _resident(wd.shape)]
    if ple is not None:
        args += [wpp, wpg]
        specs += [_resident(wpp.shape), _resident(wpg.shape)]
    args += [g, b]
    specs += [_resident(g.shape), _resident(b.shape)]
    return pl.pallas_call(
        functools.partial(_ffn_kernel, has_ple=ple is not None, ff_chunk=min(512, d_ff)),
        grid=(t_rows // tm,),
        in_specs=specs,
        out_specs=row(d),
        out_shape=jax.ShapeDtypeStruct((t_rows, d), F32),
        compiler_params=_params("parallel"),
        name="ffn_ple_ln" if ple is not None else "ffn_ln",
    )(*args)


def _inproj_kernel(x_ref, *refs, transposed):
    n = len(refs) // 2
    xb = x_ref[...].astype(BF16)
    for w_ref, o_ref, tr in zip(refs[:n], refs[n:], transposed):
        out = _dot_nt(w_ref[...], xb) if tr else _dot(xb, w_ref[...])
        o_ref[...] = out.astype(o_ref.dtype)


def _inproj(x, weights, dtypes, transposed, *, tm):
    t_rows, d = x.shape
    out_specs, out_shape = [], []
    for w, dt, tr in zip(weights, dtypes, transposed):
        if tr:
            out_specs.append(pl.BlockSpec((w.shape[0], tm), lambda i: (0, i)))
            out_shape.append(jax.ShapeDtypeStruct((w.shape[0], t_rows), dt))
        else:
            out_specs.append(pl.BlockSpec((tm, w.shape[1]), lambda i: (i, 0)))
            out_shape.append(jax.ShapeDtypeStruct((t_rows, w.shape[1]), dt))
    return pl.pallas_call(
        functools.partial(_inproj_kernel, transposed=tuple(transposed)),
        grid=(t_rows // tm,),
        in_specs=[pl.BlockSpec((tm, d), lambda i: (i, 0))] + [_resident(w.shape) for w in weights],
        out_specs=out_specs,
        out_shape=out_shape,
        compiler_params=_params("parallel"),
        name="in_proj",
    )(x, *weights)


def _softmax_probs(s, m_ref, shift):
    m_prev = m_ref[...]
    m_cur = jnp.max(s, axis=0, keepdims=True)
    if shift is not None:
        m_cur = m_cur + shift
    m_next = jnp.maximum(m_prev, m_cur)
    m_ref[...] = m_next
    p = jnp.exp2(s - (m_next if shift is None else m_next - shift))
    return p.astype(BF16), jnp.exp2(m_prev - m_next)


def _accumulate(acc_ref, rescale, v_t, p):
    v_aug = jnp.concatenate([v_t, jnp.ones((BF16_ROWS, v_t.shape[1]), v_t.dtype)], axis=0)
    acc_ref[...] = rescale * acc_ref[...] + _dot(v_aug, p)


def _softmax_step(s, v_t, m_ref, acc_ref, shift):
    p, rescale = _softmax_probs(s, m_ref, shift)
    _accumulate(acc_ref, rescale, v_t, p)


def _far_chunks(n, scores, values, m_ref, acc_ref, shift):
    def body(i, carry):
        _softmax_step(scores(i), values(i), m_ref, acc_ref, shift)
        return carry

    lax.fori_loop(0, n, body, 0)


def _da_kernel(far_ref, lam_ref, q_ref, k_ref, v_ref, bn_ref, g_ref, o_ref,
               qd_ref, m_ref, acc_ref, *, t, post_scale):
    h = pl.program_id(1)
    qi = pl.program_id(2)
    q = q_ref[...].astype(F32)
    rowq = lax.broadcasted_iota(I32, q.shape, 0)
    qd_ref[:, 0:t] = jnp.where(rowq < DA_QK_DIM, q, 0.0).astype(BF16)
    qd_ref[:, t:2 * t] = jnp.where(rowq >= DA_QK_DIM, q, 0.0).astype(BF16)
    m_ref[...] = jnp.full(m_ref.shape, NEG, F32)
    acc_ref[...] = jnp.zeros(acc_ref.shape, F32)
    far = far_ref[h]

    def scores(kb):
        return _dot(k_ref[pl.ds(pl.multiple_of(kb * t, t), t), :], qd_ref[...])

    def near_block(kb, near):
        bias = bn_ref[near]
        _softmax_step(scores(kb) + jnp.concatenate([bias, bias], axis=1), v_ref[kb], m_ref, acc_ref, None)

    _far_chunks(jnp.maximum(qi - 1, 0), scores, lambda kb: v_ref[kb], m_ref, acc_ref, far)

    @pl.when(qi >= 1)
    def _():
        near_block(qi - 1, 1)

    near_block(qi, 0)

    acc = acc_ref[...]
    o = acc[0:DA_V_DIM] / acc[DA_V_DIM:DA_V_DIM + 1]
    o = o[:, 0:t] - lam_ref[0] * o[:, t:2 * t]
    ms = jnp.mean(o * o, axis=0, keepdims=True)
    o = o * lax.rsqrt(ms + LN_EPS) * g_ref[...] * post_scale
    o_ref[...] = o.T.astype(o_ref.dtype)


def _diff_attention(q_t, k, v_t, bias_near, bias_far, lam, subln_g, *, bsz, seq, t, post_scale):
    heads = q_t.shape[0] // DA_V_DIM
    nq = seq // t
    smem = pl.BlockSpec(memory_space=pltpu.SMEM)
    va = DA_V_DIM + BF16_ROWS
    return pl.pallas_call(
        functools.partial(_da_kernel, t=t, post_scale=post_scale),
        grid=(bsz, heads, nq),
        in_specs=[
            smem, smem,
            pl.BlockSpec((DA_V_DIM, t), lambda b, h, i: (h, b * nq + i)),
            pl.BlockSpec((None, seq, DA_V_DIM), lambda b, h, i: (b, 0, h)),
            pl.BlockSpec((None, None, nq, DA_V_DIM, t), lambda b, h, i: (b, h, 0, 0, 0)),
            pl.BlockSpec((None, 2, t, t), lambda b, h, i: (h, 0, 0, 0), pipeline_mode=pl.Buffered(1)),
            pl.BlockSpec((DA_V_DIM, t), lambda b, h, i: (0, 0)),
        ],
        out_specs=pl.BlockSpec((None, t, DA_V_DIM), lambda b, h, i: (b, i, h)),
        out_shape=jax.ShapeDtypeStruct((bsz, seq, heads * DA_V_DIM), BF16),
        scratch_shapes=[
            pltpu.VMEM((DA_V_DIM, 2 * t), BF16),
            pltpu.VMEM((1, 2 * t), F32),
            pltpu.VMEM((va, 2 * t), F32),
        ],
        compiler_params=_params("parallel", "parallel", "arbitrary"),
        name="diff_attention",
    )(bias_far, lam, q_t, k, v_t, bias_near, subln_g)


def _dsa_kernel(far_ref, ixq_ref, ixw_ref, ixk_ref, saq_ref, sak_ref, sav_ref, bn_ref, o_ref,
                qi8_ref, keys_ref, hi_ref, qs4_ref, m_ref, acc_ref, *, t, topk, idx_bits):
    qi = pl.program_id(1)
    kf = float(topk)

    qi8_ref[...] = jnp.zeros(qi8_ref.shape, BF16)
    for h in range(IDX_HEADS):
        qi8_ref[0:IDX_DIM, h * t:(h + 1) * t] = ixq_ref[h * IDX_DIM:(h + 1) * IDX_DIM, :]

    def index_keys(kb):
        kblk = ixk_ref[pl.ds(pl.multiple_of(kb * t, t), t), :]
        sc = None
        for h in range(IDX_HEADS):
            r = _dot(kblk, qi8_ref[:, h * t:(h + 1) * t])
            term = ixw_ref[h:h + 1, :] * jnp.maximum(r, 0.0)
            sc = term if sc is None else sc + term
        return jnp.where(sc == 0.0, 0.0, sc)

    def rows(kb):
        return pl.ds(pl.multiple_of(kb * t, t), t)

    def chunk(kb):
        return keys_ref[rows(kb), :]

    def store_keys(kb, sc):
        bits = lax.bitcast_convert_type(sc, I32)
        keys_ref[rows(kb), :] = bits ^ ((bits >> 31) & 0x7FFFFFFF)
        hi_ref[rows(kb), :] = lax.bitcast_convert_type(bits & HI_MASK, F32).astype(BF16)

    def fill(kb, carry):
        store_keys(kb, index_keys(kb))
        return carry

    lax.fori_loop(0, qi, fill, 0)
    krow = lax.broadcasted_iota(I32, (t, t), 0)
    qcol = lax.broadcasted_iota(I32, (t, t), 1)
    store_keys(qi, jnp.where(krow <= qcol, index_keys(qi), -jnp.inf))

    def count(fn):
        def body(kb, cnt):
            return cnt + _fold_rows(fn(kb, chunk(kb)), SUBLANES)
        cnt = lax.fori_loop(0, qi + 1, body, jnp.zeros((SUBLANES, t), I32))
        return jnp.sum(cnt.astype(F32), axis=0, keepdims=True)

    def count_hi(thr):
        def body(kb, cnt):
            hit = jnp.where(hi_ref[rows(kb), :] >= thr, jnp.ones((), BF16), jnp.zeros((), BF16))
            return cnt + _fold_rows(hit, BF16_ROWS).astype(F32)
        cnt = lax.fori_loop(0, qi + 1, body, jnp.zeros((BF16_ROWS, t), F32))
        return jnp.sum(cnt, axis=0, keepdims=True)

    def hi_bit(i, st):
        code, open_ = st
        cand = code + jnp.left_shift(jnp.int32(1), 15 - i)
        pattern = cand ^ ((cand >> 31) & 0x7FFF)
        thr = lax.bitcast_convert_type(jnp.left_shift(pattern, 16), F32).astype(BF16)
        c = count_hi(thr)
        code = jnp.where(open_ > 0.0, jnp.where(c >= kf, cand, code), code)
        return code, jnp.where(c == kf, 0.0, open_)

    code, open_ = lax.fori_loop(0, 16, hi_bit, (jnp.full((1, t), -2 ** 15, I32), jnp.ones((1, t), F32)))

    def search_cond(st):
        i, _, _, n_open = st
        return jnp.logical_and(i < 32, n_open > 0.0)

    def search_body(st):
        i, tau, open_, _ = st
        cand = tau + jnp.left_shift(jnp.int32(1), 31 - i)
        c = count(lambda kb, blk: jnp.where(blk >= cand, 1, 0))
        tau = jnp.where(open_ > 0.0, jnp.where(c >= kf, cand, tau), tau)
        open_ = jnp.where(c == kf, 0.0, open_)
        return i + 1, tau, open_, jnp.max(open_)

    _, tau, _, n_open = lax.while_loop(
        search_cond, search_body, (jnp.int32(16), jnp.left_shift(code, 16), open_, jnp.max(open_)))

    @pl.when(n_open > 0.0)
    def _():
        need = kf - count(lambda kb, blk: jnp.where(blk > tau, 1, 0))

        def j_bit(i, ju):
            cand = ju | jnp.left_shift(jnp.int32(1), idx_bits - 1 - i)
            c = count(lambda kb, blk: jnp.where(blk == tau, jnp.where((krow + kb * t) < cand, 1, 0), 0))
            return jnp.where(c < need, cand, ju)

        ju = lax.fori_loop(0, idx_bits, j_bit, jnp.zeros((1, t), I32))

        def retire(kb, carry):
            blk = chunk(kb)
            late = jnp.where((krow + kb * t) > ju, INT_MIN, blk)
            keys_ref[pl.ds(pl.multiple_of(kb * t, t), t), :] = jnp.where(blk == tau, late, blk)
            return carry

        lax.fori_loop(0, qi + 1, retire, 0)

    qs4_ref[...] = jnp.zeros(qs4_ref.shape, BF16)
    for h in range(SA_HEADS):
        qs4_ref[0:SA_HEAD_DIM, h * t:(h + 1) * t] = saq_ref[h * SA_HEAD_DIM:(h + 1) * SA_HEAD_DIM, :]
    m_ref[...] = jnp.full(m_ref.shape, NEG, F32)
    acc_ref[...] = jnp.zeros(acc_ref.shape, F32)

    def scores(kb, near=None):
        s = _dot(sak_ref[pl.ds(pl.multiple_of(kb * t, t), t), :], qs4_ref[...])
        sel = chunk(kb) >= tau
        parts = []
        for h in range(SA_HEADS):
            sh = s[:, h * t:(h + 1) * t]
            if near is not None:
                sh = sh + bn_ref[h, near]
            parts.append(jnp.where(sel, sh, NEG))
        return jnp.concatenate(parts, axis=1)

    far = jnp.concatenate([jnp.full((1, t), far_ref[h], F32) for h in range(SA_HEADS)], axis=1)
    _far_chunks(jnp.maximum(qi - 1, 0), scores, lambda kb: sav_ref[kb], m_ref, acc_ref, far)

    @pl.when(qi >= 1)
    def _():
        _softmax_step(scores(qi - 1, 1), sav_ref[qi - 1], m_ref, acc_ref, None)

    _softmax_step(scores(qi, 0), sav_ref[qi], m_ref, acc_ref, None)

    acc = acc_ref[...]
    o = acc[0:SA_HEAD_DIM] / acc[SA_HEAD_DIM:SA_HEAD_DIM + 1]
    o = jnp.concatenate([o[:, h * t:(h + 1) * t] for h in range(SA_HEADS)], axis=0)
    o_ref[...] = o.T.astype(o_ref.dtype)


def _sparse_attention(ixq_t, ixw_t, ixk, saq_t, sak, sav_t, bias_near, bias_far, *, bsz, seq, t, topk):
    nq = seq // t
    smem = pl.BlockSpec(memory_space=pltpu.SMEM)
    qcols = lambda r: pl.BlockSpec((r, t), lambda b, i: (0, b * nq + i))
    full = lambda w: pl.BlockSpec((None, seq, w), lambda b, i: (b, 0, 0))
    va = SA_HEAD_DIM + BF16_ROWS
    return pl.pallas_call(
        functools.partial(_dsa_kernel, t=t, topk=topk, idx_bits=max(1, (seq - 1).bit_length())),
        grid=(bsz, nq),
        in_specs=[
            smem,
            qcols(IDX_HEADS * IDX_DIM), qcols(IDX_HEADS), full(LANES),
            qcols(SA_WIDTH), full(LANES),
            pl.BlockSpec((None, nq, SA_HEAD_DIM, t), lambda b, i: (b, 0, 0, 0)),
            _resident((SA_HEADS, 2, t, t)),
        ],
        out_specs=pl.BlockSpec((None, t, SA_WIDTH), lambda b, i: (b, i, 0)),
        out_shape=jax.ShapeDtypeStruct((bsz, seq, SA_WIDTH), BF16),
        scratch_shapes=[
            pltpu.VMEM((LANES, IDX_HEADS * t), BF16),
            pltpu.VMEM((seq, t), I32),
            pltpu.VMEM((seq, t), BF16),
            pltpu.VMEM((LANES, SA_HEADS * t), BF16),
            pltpu.VMEM((1, SA_HEADS * t), F32),
            pltpu.VMEM((va, SA_HEADS * t), F32),
        ],
        compiler_params=_params("parallel", "arbitrary"),
        name="sparse_attention",
    )(bias_far, ixq_t, ixw_t, ixk, saq_t, sak, sav_t, bias_near)


def _ssm_state_kernel(u_ref, w_ref, o_ref):
    o_ref[...] = _dot(u_ref[...].astype(BF16), w_ref[...])


def _ssm_scan_kernel(loc_ref, a1_ref, a2_ref, o_ref):
    a1 = a1_ref[...]
    a2 = a2_ref[...]

    def body(n, s):
        o_ref[n] = s
        return a1 * s + a2 * pltpu.roll(s, SSM_STATE, axis=1) + loc_ref[n]

    lax.fori_loop(0, loc_ref.shape[0], body, jnp.zeros(a1.shape, F32))


def _ssm_out_kernel(u_ref, s_ref, toep_ref, wout_ref, o_ref):
    o_ref[...] = (_dot(u_ref[...].astype(BF16), toep_ref[...])
                  + _dot(s_ref[...].astype(BF16), wout_ref[...]))


def _ssm_gate_kernel(y_ref, u_ref, d_ref, w_ref, o_ref):
    y = jax.nn.gelu(y_ref[...] + d_ref[...] * u_ref[...])
    o_ref[...] = (y * jax.nn.sigmoid(_dot(y.astype(BF16), w_ref[...]))).astype(o_ref.dtype)


def _ssm_tables(lam_re, lam_im, log_dt, b_re, b_im, c_re, c_im, tc):
    hp = lax.Precision.HIGHEST
    dt = jnp.exp(log_dt)[:, None]
    n = jnp.arange(tc + 1, dtype=F32)[:, None, None]
    mag = jnp.exp(lam_re * dt * n)
    pw_re = mag * jnp.cos(lam_im * dt * n)
    pw_im = mag * jnp.sin(lam_im * dt * n)
    den = lam_re * lam_re + lam_im * lam_im
    nr, ni = pw_re[1] - 1.0, pw_im[1]
    f_re = (nr * lam_re + ni * lam_im) / den
    f_im = (ni * lam_re - nr * lam_im) / den
    bb_re = f_re[..., None] * b_re - f_im[..., None] * b_im
    bb_im = f_re[..., None] * b_im + f_im[..., None] * b_re
    ca_re = c_re[None] * pw_re[:, :, None, :] - c_im[None] * pw_im[:, :, None, :]
    ca_im = c_re[None] * pw_im[:, :, None, :] + c_im[None] * pw_re[:, :, None, :]
    groups = lam_re.shape[0]
    inv_mag = jnp.exp(-lam_re * dt * n[:tc])
    iw_re = inv_mag * jnp.cos(lam_im * dt * n[:tc])
    iw_im = -inv_mag * jnp.sin(lam_im * dt * n[:tc])
    l_re = iw_re[..., None] * bb_re[None] - iw_im[..., None] * bb_im[None]
    l_im = iw_re[..., None] * bb_im[None] + iw_im[..., None] * bb_re[None]
    left = jnp.concatenate([l_re, l_im], axis=2).transpose(1, 0, 3, 2)
    right = jnp.concatenate([ca_re[:tc], -ca_im[:tc]], axis=3).transpose(1, 0, 2, 3)
    cw = tc * SSM_GROUP
    toep = jnp.einsum('gik,gjk->gij', left.reshape(groups, cw, 2 * SSM_STATE),
                      right.reshape(groups, cw, 2 * SSM_STATE), precision=hp)
    step = jnp.arange(cw) // SSM_GROUP
    toep = jnp.where(step[:, None] <= step[None, :], toep, 0.0)
    rev_re, rev_im = pw_re[tc - 1::-1][:tc], pw_im[tc - 1::-1][:tc]
    ws_re = rev_re[..., None] * bb_re[None] - rev_im[..., None] * bb_im[None]
    ws_im = rev_re[..., None] * bb_im[None] + rev_im[..., None] * bb_re[None]
    wstate = jnp.concatenate([ws_re, ws_im], axis=2)
    wstate = wstate.transpose(1, 0, 3, 2).reshape(groups, tc * SSM_GROUP, 2 * SSM_STATE)
    wout = jnp.concatenate([ca_re[1:], -ca_im[1:]], axis=3)
    wout = wout.transpose(1, 3, 0, 2).reshape(groups, 2 * SSM_STATE, tc * SSM_GROUP)
    dec_re, dec_im = pw_re[tc], pw_im[tc]
    a1 = jnp.concatenate([dec_re, dec_re], axis=1)
    a2 = jnp.concatenate([-dec_im, dec_im], axis=1)
    return toep.astype(BF16), wstate.astype(BF16), wout.astype(BF16), a1, a2


def _ssm(u, tables, d_skip, w_glu, *, bsz, seq, tm):
    toep, wstate, wout, a1, a2 = tables
    groups = toep.shape[0]
    tc = SSM_CHUNK
    nc = seq // tc
    rows = bsz * nc
    cw = tc * SSM_GROUP
    ug = u.reshape(bsz, nc, tc, groups, SSM_GROUP).transpose(3, 0, 1, 2, 4).reshape(groups, rows, cw)
    gspec = lambda r, c: pl.BlockSpec((None, r, c), lambda g: (g, 0, 0))
    loc = pl.pallas_call(
        _ssm_state_kernel,
        grid=(groups,),
        in_specs=[gspec(rows, cw), gspec(cw, 2 * SSM_STATE)],
        out_specs=gspec(rows, 2 * SSM_STATE),
        out_shape=jax.ShapeDtypeStruct((groups, rows, 2 * SSM_STATE), F32),
        compiler_params=_params("parallel"),
        name="ssm_chunk_state",
    )(ug, wstate)
    gb = groups * bsz
    loc_t = loc.reshape(groups, bsz, nc, 2 * SSM_STATE).transpose(2, 0, 1, 3).reshape(nc, gb, 2 * SSM_STATE)
    a1r = jnp.repeat(a1, bsz, axis=0)
    a2r = jnp.repeat(a2, bsz, axis=0)
    rb = 8 if gb % 8 == 0 else gb
    prev = pl.pallas_call(
        _ssm_scan_kernel,
        grid=(gb // rb,),
        in_specs=[pl.BlockSpec((nc, rb, 2 * SSM_STATE), lambda i: (0, i, 0)),
                  pl.BlockSpec((rb, 2 * SSM_STATE), lambda i: (i, 0)),
                  pl.BlockSpec((rb, 2 * SSM_STATE), lambda i: (i, 0))],
        out_specs=pl.BlockSpec((nc, rb, 2 * SSM_STATE), lambda i: (0, i, 0)),
        out_shape=jax.ShapeDtypeStruct((nc, gb, 2 * SSM_STATE), F32),
        compiler_params=_params("parallel"),
        name="ssm_chunk_scan",
    )(loc_t, a1r, a2r)
    prev_g = prev.reshape(nc, groups, bsz, 2 * SSM_STATE).transpose(1, 2, 0, 3).reshape(groups, rows, 2 * SSM_STATE)
    y = pl.pallas_call(
        _ssm_out_kernel,
        grid=(groups,),
        in_specs=[gspec(rows, cw), gspec(rows, 2 * SSM_STATE), gspec(cw, cw), gspec(2 * SSM_STATE, cw)],
        out_specs=gspec(rows, cw),
        out_shape=jax.ShapeDtypeStruct((groups, rows, cw), F32),
        compiler_params=_params("parallel"),
        name="ssm_chunk_out",
    )(ug, prev_g, toep, wout)
    y = y.reshape(groups, bsz, nc, tc, SSM_GROUP).transpose(1, 2, 3, 0, 4).reshape(bsz * seq, groups * SSM_GROUP)
    width = groups * SSM_GROUP
    return pl.pallas_call(
        _ssm_gate_kernel,
        grid=(bsz * seq // tm,),
        in_specs=[pl.BlockSpec((tm, width), lambda i: (i, 0)),
                  pl.BlockSpec((tm, width), lambda i: (i, 0)),
                  _resident((1, width)), _resident((width, width))],
        out_specs=pl.BlockSpec((tm, width), lambda i: (i, 0)),
        out_shape=jax.ShapeDtypeStruct((bsz * seq, width), BF16),
        compiler_params=_params("parallel"),
        name="ssm_gate",
    )(y, u, d_skip, w_glu)


def _outproj_kernel(x_ref, da_ref, ssm_ref, sa_ref, w1, w2, w3, g_ref, b_ref, o_ref):
    mix = _dot(da_ref[...], w1[...]) + _dot(ssm_ref[...], w2[...]) + _dot(sa_ref[...], w3[...])
    o_ref[...] = _layer_norm(ALPHA * x_ref[...] + mix, g_ref[...], b_ref[...])


def _outproj(x, o_da, o_ssm, o_sa, w1, w2, w3, g, b, *, tm):
    t_rows, d = x.shape
    row = lambda w: pl.BlockSpec((tm, w), lambda i: (i, 0))
    return pl.pallas_call(
        _outproj_kernel,
        grid=(t_rows // tm,),
        in_specs=[row(d), row(o_da.shape[1]), row(o_ssm.shape[1]), row(o_sa.shape[1]),
                  _resident(w1.shape), _resident(w2.shape), _resident(w3.shape),
                  _resident(g.shape), _resident(b.shape)],
        out_specs=row(d),
        out_shape=jax.ShapeDtypeStruct((t_rows, d), F32),
        compiler_params=_params("parallel"),
        name="out_proj_ln",
    )(x, o_da, o_ssm, o_sa, w1, w2, w3, g, b)


def _t5_bucket(n):
    max_exact = REL_BUCKETS // 2
    nf = jnp.maximum(n, 1).astype(F32)
    large = max_exact + (jnp.log(nf / max_exact) / math.log(REL_MAX_DIST / max_exact)
                         * (REL_BUCKETS - max_exact)).astype(I32)
    large = jnp.minimum(large, REL_BUCKETS - 1)
    return jnp.where(n < max_exact, n, large)


def _bias_tiles(table, t):
    assert t >= REL_MAX_DIST
    table = table.astype(F32) * LOG2E
    key = jnp.arange(t)[:, None]
    query = jnp.arange(t)[None, :]
    dist = jnp.stack([query - key, t + query - key])
    bucket = _t5_bucket(jnp.maximum(dist, 0))
    tiles = jnp.zeros((table.shape[1],) + dist.shape, F32)
    for b in range(REL_BUCKETS):
        tiles = jnp.where(bucket[None] == b, table[b][:, None, None, None], tiles)
    return jnp.where(dist[None] >= 0, tiles, NEG), table[REL_BUCKETS - 1]


def _split_w_in(w_in):
    offs = [0]
    for s in IN_SIZES:
        offs.append(offs[-1] + s)
    da_q, da_k, da_v, ssm_u, sa_q, sa_k, sa_v, ix_q, ix_k, ix_w = (
        w_in[:, offs[j]:offs[j + 1]] for j in range(len(IN_SIZES)))
    w_scale = IDX_HEADS ** -0.5 * IDX_DIM ** -0.5
    pad = lambda w: jnp.pad(w, ((0, 0), (0, LANES - w.shape[1])))
    weights = [(da_q * (DA_QK_DIM ** -0.5 * LOG2E)).T, da_k, da_v.T, ssm_u,
               (sa_q * (SA_HEAD_DIM ** -0.5 * LOG2E)).T, pad(sa_k), sa_v.T,
               ix_q.T, pad(ix_k), (ix_w * w_scale).T]
    dtypes = [BF16, BF16, BF16, F32, BF16, BF16, BF16, BF16, BF16, F32]
    transposed = [True, False, True, False, True, False, True, True, False, True]
    return [w.astype(BF16) for w in weights], dtypes, transposed


def _value_chunks(v_t, rows, bsz, seq, t):
    g = v_t.shape[0] // rows
    return v_t.reshape(g, rows, bsz, seq // t, t).transpose(2, 0, 3, 1, 4)


def _plan(bsz, seq):
    rows = bsz * seq
    tm = 512 if rows % 512 == 0 else rows
    t_da = 1024 if seq % 1024 == 0 and seq >= 4096 else 256 if seq % 256 == 0 else 128
    t_sa = 512 if seq % 512 == 0 and seq >= 2048 else 256 if seq % 256 == 0 else 128
    return dict(tm=tm, t_da=t_da, t_sa=t_sa)


def kernel(x, p, rel_bias, ffn1_w_gate, ffn1_w_up, ffn1_w_down, ln1_g, ln1_b, w_in, w_o, da_lam_q1, da_lam_k1, da_lam_q2, da_lam_k2, da_subln_g, ssm_lam_re, ssm_lam_im, ssm_log_dt, ssm_b_re, ssm_b_im, ssm_c_re, ssm_c_im, ssm_d, ssm_w_glu, ln2_g, ln2_b, ffn2_w_gate, ffn2_w_up, ffn2_w_down, ple_w_proj, ple_w_gate, ln3_g, ln3_b):
    bsz, seq, d = x.shape
    rows = bsz * seq
    plan = _plan(bsz, seq)
    tm, t_da, t_sa = plan["tm"], plan["t_da"], plan["t_sa"]
    topk = min(TOPK_MAX, seq // 4)
    da_near, da_far = _bias_tiles(rel_bias[:, :DA_HEADS], t_da)
    sa_near, sa_far = _bias_tiles(rel_bias[:, DA_HEADS:], t_sa)
    vec = lambda a: a.reshape(1, -1).astype(F32)
    da_w = DA_HEADS * DA_V_DIM

    h = x.reshape(rows, d)
    for i in range(DEPTH):
        lam_init = 0.8 - 0.6 * math.exp(-0.3 * i)
        h = _ffn(h, ffn1_w_gate[i].astype(BF16), ffn1_w_up[i].astype(BF16), ffn1_w_down[i].astype(BF16),
                 vec(ln1_g[i]), vec(ln1_b[i]), tm=tm)

        weights, dtypes, transposed = _split_w_in(w_in[i])
        (da_q, da_k, da_v, ssm_u, sa_q, sa_k, sa_v, ix_q, ix_k, ix_w) = _inproj(
            h, weights, dtypes, transposed, tm=tm)
        b3 = lambda a: a.reshape(bsz, seq, a.shape[-1])

        lam = (jnp.exp(jnp.sum(da_lam_q1[i].astype(F32) * da_lam_k1[i]))
               - jnp.exp(jnp.sum(da_lam_q2[i].astype(F32) * da_lam_k2[i])) + lam_init)
        subln = jnp.broadcast_to(da_subln_g[i].astype(F32)[:, None], (DA_V_DIM, t_da))
        o_da = _diff_attention(da_q, b3(da_k), _value_chunks(da_v, DA_V_DIM, bsz, seq, t_da), da_near, da_far,
                               lam.reshape(1).astype(F32), subln, bsz=bsz, seq=seq, t=t_da,
                               post_scale=1.0 - lam_init)

        tables = _ssm_tables(ssm_lam_re[i].astype(F32), ssm_lam_im[i].astype(F32), ssm_log_dt[i].astype(F32),
                             ssm_b_re[i].astype(F32), ssm_b_im[i].astype(F32),
                             ssm_c_re[i].astype(F32), ssm_c_im[i].astype(F32), SSM_CHUNK)
        o_ssm = _ssm(ssm_u, tables, vec(ssm_d[i]), ssm_w_glu[i].astype(BF16), bsz=bsz, seq=seq, tm=tm)

        sav = _value_chunks(sa_v, SA_HEAD_DIM, bsz, seq, t_sa)[:, 0]
        o_sa = _sparse_attention(ix_q, ix_w, b3(ix_k), sa_q, b3(sa_k), sav, sa_near, sa_far,
                                 bsz=bsz, seq=seq, t=t_sa, topk=topk)

        wo = w_o[i].astype(BF16)
        h = _outproj(h, o_da.reshape(rows, da_w), o_ssm, o_sa.reshape(rows, SA_WIDTH),
                     wo[:da_w], wo[da_w:da_w + SSM_WIDTH], wo[da_w + SSM_WIDTH:],
                     vec(ln2_g[i]), vec(ln2_b[i]), tm=tm)

        h = _ffn(h, ffn2_w_gate[i].astype(BF16), ffn2_w_up[i].astype(BF16), ffn2_w_down[i].astype(BF16),
                 vec(ln3_g[i]), vec(ln3_b[i]),
                 ple=(p[i].reshape(rows, -1), ple_w_proj[i].astype(BF16), ple_w_gate[i].astype(BF16)), tm=tm)
    return h.reshape(bsz, seq, d)
```

```python
import functools
import math

import jax
import jax.numpy as jnp
from jax import lax
from jax.experimental import pallas as pl
from jax.experimental.pallas import tpu as pltpu

F32 = jnp.float32
BF16 = jnp.bfloat16
I32 = jnp.int32

DEPTH = 2
DA_QK_DIM = 64
DA_V_DIM = 2 * DA_QK_DIM
DA_HEADS = 4
SSM_GROUP = 16
SSM_GROUPS = 16
SSM_STATE = 64
SSM_WIDTH = SSM_GROUP * SSM_GROUPS
SA_HEAD_DIM = 64
SA_HEADS = 4
SA_WIDTH = SA_HEADS * SA_HEAD_DIM
IDX_HEADS = 8
IDX_DIM = 32
TOPK_MAX = 256
REL_BUCKETS = 32
REL_MAX_DIST = 128
ALPHA = (2 * DEPTH) ** 0.25
LN_EPS = 1e-5
IN_SIZES = (DA_HEADS * 2 * DA_QK_DIM, DA_HEADS * 2 * DA_QK_DIM, DA_HEADS * DA_V_DIM,
            SSM_WIDTH, SA_WIDTH, SA_HEAD_DIM, SA_HEAD_DIM,
            IDX_HEADS * IDX_DIM, IDX_DIM, IDX_HEADS)

LANES = 128
SUBLANES = 8
BF16_ROWS = 16
NEG = -1e30
INT_MIN = -2 ** 31
HI_MASK = -2 ** 16
TINY = 2.0 ** -126
BF16_MIN_NORMAL = 0x0080
LOG2E = math.log2(math.e)
VMEM_LIMIT = 56 * 1024 * 1024
SSM_CHUNK = 64

def _params(*sem):
    return pltpu.CompilerParams(dimension_semantics=sem, vmem_limit_bytes=VMEM_LIMIT)


def _resident(shape):
    return pl.BlockSpec(shape, lambda *_: (0,) * len(shape), pipeline_mode=pl.Buffered(1))


def _dot(a, b):
    return jnp.dot(a, b, preferred_element_type=F32)


def _dot_nt(a, b):
    return lax.dot_general(a, b, (((1,), (1,)), ((), ())), preferred_element_type=F32)


def _layer_norm(y, g, b):
    mu = jnp.mean(y, axis=-1, keepdims=True)
    d = y - mu
    var = jnp.mean(d * d, axis=-1, keepdims=True)
    return d * lax.rsqrt(var + LN_EPS) * g + b


def _fold_rows(x, rows):
    while x.shape[0] > rows:
        half = x.shape[0] // 2
        x = x[:half] + x[half:]
    return x


def _ffn_kernel(*refs, has_ple, ff_chunk):
    if has_ple:
        x_ref, p_ref, wg, wu, wd, wpp, wpg, g_ref, b_ref, o_ref = refs
    else:
        x_ref, wg, wu, wd, g_ref, b_ref, o_ref = refs
    x = x_ref[...]
    xb = x.astype(BF16)
    d_ff = wg.shape[1]
    acc = None
    for c0 in range(0, d_ff, ff_chunk):
        c1 = min(c0 + ff_chunk, d_ff)
        gate = _dot(xb, wg[:, c0:c1])
        up = _dot(xb, wu[:, c0:c1])
        hid = (gate * jax.nn.sigmoid(gate) * up).astype(BF16)
        part = _dot(hid, wd[c0:c1, :])
        acc = part if acc is None else acc + part
    y = ALPHA * x + 0.5 * acc
    if has_ple:
        y = y + _dot(p_ref[...].astype(BF16), wpp[...]) * jax.nn.sigmoid(_dot(xb, wpg[...]))
    o_ref[...] = _layer_norm(y, g_ref[...], b_ref[...])


def _ffn(x, wg, wu, wd, g, b, ple=None, *, tm):
    t_rows, d = x.shape
    d_ff = wg.shape[1]
    row = lambda w: pl.BlockSpec((tm, w), lambda i: (i, 0))
    args = [x]
    specs = [row(d)]
    if ple is not None:
        p, wpp, wpg = ple
        args.append(p)
        specs.append(row(p.shape[1]))
    args += [wg, wu, wd]
    specs += [_resident(wg.shape), _resident(wu.shape), _resident(wd.shape)]
    if ple is not None:
        args += [wpp, wpg]
        specs += [_resident(wpp.shape), _resident(wpg.shape)]
    args += [g, b]
    specs += [_resident(g.shape), _resident(b.shape)]
    return pl.pallas_call(
        functools.partial(_ffn_kernel, has_ple=ple is not None, ff_chunk=min(512, d_ff)),
        grid=(t_rows // tm,),
        in_specs=specs,
        out_specs=row(d),
        out_shape=jax.ShapeDtypeStruct((t_rows, d), F32),
        compiler_params=_params("parallel"),
        name="ffn_ple_ln" if ple is not None else "ffn_ln",
    )(*args)


def _inproj_kernel(x_ref, *refs, transposed):
    n = len(refs) // 2
    xb = x_ref[...].astype(BF16)
    for w_ref, o_ref, tr in zip(refs[:n], refs[n:], transposed):
        out = _dot_nt(w_ref[...], xb) if tr else _dot(xb, w_ref[...])
        o_ref[...] = out.astype(o_ref.dtype)


def _inproj(x, weights, dtypes, transposed, *, tm):
    t_rows, d = x.shape
    out_specs, out_shape = [], []
    for w, dt, tr in zip(weights, dtypes, transposed):
        if tr:
            out_specs.append(pl.BlockSpec((w.shape[0], tm), lambda i: (0, i)))
            out_shape.append(jax.ShapeDtypeStruct((w.shape[0], t_rows), dt))
        else:
            out_specs.append(pl.BlockSpec((tm, w.shape[1]), lambda i: (i, 0)))
            out_shape.append(jax.ShapeDtypeStruct((t_rows, w.shape[1]), dt))
    return pl.pallas_call(
        functools.partial(_inproj_kernel, transposed=tuple(transposed)),
        grid=(t_rows // tm,),
        in_specs=[pl.BlockSpec((tm, d), lambda i: (i, 0))] + [_resident(w.shape) for w in weights],
        out_specs=out_specs,
        out_shape=out_shape,
        compiler_params=_params("parallel"),
        name="in_proj",
    )(x, *weights)


def _softmax_probs(s, m_ref, shift):
    m_prev = m_ref[...]
    m_cur = jnp.max(s, axis=0, keepdims=True)
    if shift is not None:
        m_cur = m_cur + shift
    m_next = jnp.maximum(m_prev, m_cur)
    m_ref[...] = m_next
    p = jnp.exp2(s - (m_next if shift is None else m_next - shift))
    return p.astype(BF16), jnp.exp2(m_prev - m_next)


def _accumulate(acc_ref, rescale, v_t, p):
    v_aug = jnp.concatenate([v_t, jnp.ones((BF16_ROWS, v_t.shape[1]), v_t.dtype)], axis=0)
    acc_ref[...] = rescale * acc_ref[...] + _dot(v_aug, p)


def _softmax_step(s, v_t, m_ref, acc_ref, shift):
    p, rescale = _softmax_probs(s, m_ref, shift)
    _accumulate(acc_ref, rescale, v_t, p)


def _far_chunks(n, scores, values, m_ref, acc_ref, shift):
    def body(i, carry):
        _softmax_step(scores(i), values(i), m_ref, acc_ref, shift)
        return carry

    lax.fori_loop(0, n, body, 0)


def _da_kernel(far_ref, lam_ref, q_ref, k_ref, v_ref, bn_ref, g_ref, o_ref,
               qd_ref, m_ref, acc_ref, *, t, post_scale):
    h = pl.program_id(1)
    qi = pl.program_id(2)
    q = q_ref[...].astype(F32)
    rowq = lax.broadcasted_iota(I32, q.shape, 0)
    qd_ref[:, 0:t] = jnp.where(rowq < DA_QK_DIM, q, 0.0).astype(BF16)
    qd_ref[:, t:2 * t] = jnp.where(rowq >= DA_QK_DIM, q, 0.0).astype(BF16)
    m_ref[...] = jnp.full(m_ref.shape, NEG, F32)
    acc_ref[...] = jnp.zeros(acc_ref.shape, F32)
    far = far_ref[h]

    def scores(kb):
        return _dot(k_ref[pl.ds(pl.multiple_of(kb * t, t), t), :], qd_ref[...])

    def near_block(kb, near):
        bias = bn_ref[near]
        _softmax_step(scores(kb) + jnp.concatenate([bias, bias], axis=1), v_ref[kb], m_ref, acc_ref, None)

    _far_chunks(jnp.maximum(qi - 1, 0), scores, lambda kb: v_ref[kb], m_ref, acc_ref, far)

    @pl.when(qi >= 1)
    def _():
        near_block(qi - 1, 1)

    near_block(qi, 0)

    acc = acc_ref[...]
    o = acc[0:DA_V_DIM] / acc[DA_V_DIM:DA_V_DIM + 1]
    o = o[:, 0:t] - lam_ref[0] * o[:, t:2 * t]
    ms = jnp.mean(o * o, axis=0, keepdims=True)
    o = o * lax.rsqrt(ms + LN_EPS) * g_ref[...] * post_scale
    o_ref[...] = o.T.astype(o_ref.dtype)


def _diff_attention(q_t, k, v_t, bias_near, bias_far, lam, subln_g, *, bsz, seq, t, post_scale):
    heads = q_t.shape[0] // DA_V_DIM
    nq = seq // t
    smem = pl.BlockSpec(memory_space=pltpu.SMEM)
    va = DA_V_DIM + BF16_ROWS
    return pl.pallas_call(
        functools.partial(_da_kernel, t=t, post_scale=post_scale),
        grid=(bsz, heads, nq),
        in_specs=[
            smem, smem,
            pl.BlockSpec((DA_V_DIM, t), lambda b, h, i: (h, b * nq + i)),
            pl.BlockSpec((None, seq, DA_V_DIM), lambda b, h, i: (b, 0, h)),
            pl.BlockSpec((None, None, nq, DA_V_DIM, t), lambda b, h, i: (b, h, 0, 0, 0)),
            pl.BlockSpec((None, 2, t, t), lambda b, h, i: (h, 0, 0, 0)),
            pl.BlockSpec((DA_V_DIM, t), lambda b, h, i: (0, 0)),
        ],
        out_specs=pl.BlockSpec((None, t, DA_V_DIM), lambda b, h, i: (b, i, h)),
        out_shape=jax.ShapeDtypeStruct((bsz, seq, heads * DA_V_DIM), BF16),
        scratch_shapes=[
            pltpu.VMEM((DA_V_DIM, 2 * t), BF16),
            pltpu.VMEM((1, 2 * t), F32),
            pltpu.VMEM((va, 2 * t), F32),
        ],
        compiler_params=_params("parallel", "parallel", "arbitrary"),
        name="diff_attention",
    )(bias_far, lam, q_t, k, v_t, bias_near, subln_g)


def _dsa_kernel(far_ref, ixq_ref, ixw_ref, ixk_ref, saq_ref, sak_ref, sav_ref, bn_ref, o_ref,
                qi8_ref, keys_ref, hi_ref, qs4_ref, m_ref, acc_ref, *, t, topk):
    qi = pl.program_id(1)
    kf = float(topk)

    qi8_ref[...] = jnp.zeros(qi8_ref.shape, BF16)
    for h in range(IDX_HEADS):
        qi8_ref[0:IDX_DIM, h * t:(h + 1) * t] = ixq_ref[h * IDX_DIM:(h + 1) * IDX_DIM, :]

    def index_keys(kb):
        kblk = ixk_ref[pl.ds(pl.multiple_of(kb * t, t), t), :]
        sc = None
        for h in range(IDX_HEADS):
            r = _dot(kblk, qi8_ref[:, h * t:(h + 1) * t])
            term = ixw_ref[h:h + 1, :] * jnp.maximum(r, 0.0)
            sc = term if sc is None else sc + term
        return jnp.where(jnp.abs(sc) < TINY, 0.0, sc)

    def rows(kb):
        return pl.ds(pl.multiple_of(kb * t, t), t)

    def chunk(kb):
        return keys_ref[rows(kb), :]

    def store_keys(kb, sc):
        bits = lax.bitcast_convert_type(sc, I32)
        keys_ref[rows(kb), :] = bits ^ ((bits >> 31) & 0x7FFFFFFF)
        hi_ref[rows(kb), :] = lax.bitcast_convert_type(bits & HI_MASK, F32).astype(BF16)

    def fill(kb, carry):
        store_keys(kb, index_keys(kb))
        return carry

    lax.fori_loop(0, qi, fill, 0)
    krow = lax.broadcasted_iota(I32, (t, t), 0)
    qcol = lax.broadcasted_iota(I32, (t, t), 1)
    store_keys(qi, jnp.where(krow <= qcol, index_keys(qi), -jnp.inf))

    def count(fn):
        def body(kb, cnt):
            return cnt + _fold_rows(fn(kb, chunk(kb)), SUBLANES)
        cnt = lax.fori_loop(0, qi + 1, body, jnp.zeros((SUBLANES, t), I32))
        return jnp.sum(cnt.astype(F32), axis=0, keepdims=True)

    def count_hi(thr):
        def body(kb, cnt):
            hit = jnp.where(hi_ref[rows(kb), :] >= thr, jnp.ones((), BF16), jnp.zeros((), BF16))
            return cnt + _fold_rows(hit, BF16_ROWS).astype(F32)
        cnt = lax.fori_loop(0, qi + 1, body, jnp.zeros((BF16_ROWS, t), F32))
        return jnp.sum(cnt, axis=0, keepdims=True)

    def hi_bit(i, st):
        code, open_ = st
        cand = code + jnp.left_shift(jnp.int32(1), 15 - i)
        pattern = cand ^ ((cand >> 31) & 0x7FFF)
        pattern = jnp.where(cand > 0, jnp.maximum(pattern, BF16_MIN_NORMAL), pattern)
        thr = lax.bitcast_convert_type(jnp.left_shift(pattern, 16), F32).astype(BF16)
        c = count_hi(thr)
        code = jnp.where(open_ > 0.0, jnp.where(c >= kf, cand, code), code)
        return code, jnp.where(c == kf, 0.0, open_)

    code, open_ = lax.fori_loop(0, 16, hi_bit, (jnp.full((1, t), -2 ** 15, I32), jnp.ones((1, t), F32)))
    open_ = jnp.where(code == 0, 0.0, open_)

    def search_cond(st):
        i, _, _, n_open = st
        return jnp.logical_and(i < 32, n_open > 0.0)

    def search_body(st):
        i, tau, open_, _ = st
        cand = tau + jnp.left_shift(jnp.int32(1), 31 - i)
        c = count(lambda kb, blk: jnp.where(blk >= cand, 1, 0))
        tau = jnp.where(open_ > 0.0, jnp.where(c >= kf, cand, tau), tau)
        open_ = jnp.where(c == kf, 0.0, open_)
        return i + 1, tau, open_, jnp.max(open_)

    _, tau, _, _ = lax.while_loop(
        search_cond, search_body, (jnp.int32(16), jnp.left_shift(code, 16), open_, jnp.max(open_)))
    c_ge = count(lambda kb, blk: jnp.where(blk >= tau, 1, 0))

    @pl.when(jnp.max(c_ge) > kf)
    def _():
        need = kf - count(lambda kb, blk: jnp.where(blk > tau, 1, 0))
        below = jnp.where(qcol <= krow, 1.0, 0.0).astype(BF16)

        def retire(kb, seen):
            blk = chunk(kb)
            tied = jnp.where(blk == tau, 1.0, 0.0).astype(BF16)
            rank = _dot(below, tied) + seen
            keys_ref[rows(kb), :] = jnp.where(blk == tau, jnp.where(rank > need, INT_MIN, blk), blk)
            return rank[t - 1:t, :]

        lax.fori_loop(0, qi + 1, retire, jnp.zeros((1, t), F32))

    qs4_ref[...] = jnp.zeros(qs4_ref.shape, BF16)
    for h in range(SA_HEADS):
        qs4_ref[0:SA_HEAD_DIM, h * t:(h + 1) * t] = saq_ref[h * SA_HEAD_DIM:(h + 1) * SA_HEAD_DIM, :]
    m_ref[...] = jnp.full(m_ref.shape, NEG, F32)
    acc_ref[...] = jnp.zeros(acc_ref.shape, F32)

    def scores(kb, near=None):
        s = _dot(sak_ref[pl.ds(pl.multiple_of(kb * t, t), t), :], qs4_ref[...])
        sel = chunk(kb) >= tau
        parts = []
        for h in range(SA_HEADS):
            sh = s[:, h * t:(h + 1) * t]
            if near is not None:
                sh = sh + bn_ref[h, near]
            parts.append(jnp.where(sel, sh, NEG))
        return jnp.concatenate(parts, axis=1)

    far = jnp.concatenate([jnp.full((1, t), far_ref[h], F32) for h in range(SA_HEADS)], axis=1)
    _far_chunks(jnp.maximum(qi - 1, 0), scores, lambda kb: sav_ref[kb], m_ref, acc_ref, far)

    @pl.when(qi >= 1)
    def _():
        _softmax_step(scores(qi - 1, 1), sav_ref[qi - 1], m_ref, acc_ref, None)

    _softmax_step(scores(qi, 0), sav_ref[qi], m_ref, acc_ref, None)

    acc = acc_ref[...]
    o = acc[0:SA_HEAD_DIM] / acc[SA_HEAD_DIM:SA_HEAD_DIM + 1]
    o = jnp.concatenate([o[:, h * t:(h + 1) * t] for h in range(SA_HEADS)], axis=0)
    o_ref[...] = o.T.astype(o_ref.dtype)


def _sparse_attention(ixq_t, ixw_t, ixk, saq_t, sak, sav_t, bias_near, bias_far, *, bsz, seq, t, topk):
    nq = seq // t
    smem = pl.BlockSpec(memory_space=pltpu.SMEM)
    qcols = lambda r: pl.BlockSpec((r, t), lambda b, i: (0, b * nq + i))
    full = lambda w: pl.BlockSpec((None, seq, w), lambda b, i: (b, 0, 0))
    va = SA_HEAD_DIM + BF16_ROWS
    return pl.pallas_call(
        functools.partial(_dsa_kernel, t=t, topk=topk),
        grid=(bsz, nq),
        in_specs=[
            smem,
            qcols(IDX_HEADS * IDX_DIM), qcols(IDX_HEADS), full(LANES),
            qcols(SA_WIDTH), full(LANES),
            pl.BlockSpec((None, nq, SA_HEAD_DIM, t), lambda b, i: (b, 0, 0, 0)),
            _resident((SA_HEADS, 2, t, t)),
        ],
        out_specs=pl.BlockSpec((None, t, SA_WIDTH), lambda b, i: (b, i, 0)),
        out_shape=jax.ShapeDtypeStruct((bsz, seq, SA_WIDTH), BF16),
        scratch_shapes=[
            pltpu.VMEM((LANES, IDX_HEADS * t), BF16),
            pltpu.VMEM((seq, t), I32),
            pltpu.VMEM((seq, t), BF16),
            pltpu.VMEM((LANES, SA_HEADS * t), BF16),
            pltpu.VMEM((1, SA_HEADS * t), F32),
            pltpu.VMEM((va, SA_HEADS * t), F32),
        ],
        compiler_params=_params("parallel", "arbitrary"),
        name="sparse_attention",
    )(bias_far, ixq_t, ixw_t, ixk, saq_t, sak, sav_t, bias_near)


def _ssm_state_kernel(u_ref, w_ref, o_ref):
    o_ref[...] = _dot(u_ref[...].astype(BF16), w_ref[...])


def _ssm_scan_kernel(loc_ref, a1_ref, a2_ref, o_ref):
    a1 = a1_ref[...]
    a2 = a2_ref[...]

    def body(n, s):
        o_ref[n] = s
        return a1 * s + a2 * pltpu.roll(s, SSM_STATE, axis=1) + loc_ref[n]

    lax.fori_loop(0, loc_ref.shape[0], body, jnp.zeros(a1.shape, F32))


def _ssm_out_kernel(u_ref, s_ref, toep_ref, wout_ref, o_ref):
    o_ref[...] = (_dot(u_ref[...].astype(BF16), toep_ref[...])
                  + _dot(s_ref[...].astype(BF16), wout_ref[...]))


def _ssm_gate_kernel(y_ref, u_ref, d_ref, w_ref, o_ref):
    y = jax.nn.gelu(y_ref[...] + d_ref[...] * u_ref[...])
    o_ref[...] = (y * jax.nn.sigmoid(_dot(y.astype(BF16), w_ref[...]))).astype(o_ref.dtype)


def _ssm_tables(lam_re, lam_im, log_dt, b_re, b_im, c_re, c_im, tc):
    hp = lax.Precision.HIGHEST
    dt = jnp.exp(log_dt)[:, None]
    n = jnp.arange(tc + 1, dtype=F32)[:, None, None]
    mag = jnp.exp(lam_re * dt * n)
    pw_re = mag * jnp.cos(lam_im * dt * n)
    pw_im = mag * jnp.sin(lam_im * dt * n)
    den = lam_re * lam_re + lam_im * lam_im
    nr, ni = pw_re[1] - 1.0, pw_im[1]
    f_re = (nr * lam_re + ni * lam_im) / den
    f_im = (ni * lam_re - nr * lam_im) / den
    bb_re = f_re[..., None] * b_re - f_im[..., None] * b_im
    bb_im = f_re[..., None] * b_im + f_im[..., None] * b_re
    ca_re = c_re[None] * pw_re[:, :, None, :] - c_im[None] * pw_im[:, :, None, :]
    ca_im = c_re[None] * pw_im[:, :, None, :] + c_im[None] * pw_re[:, :, None, :]
    groups = lam_re.shape[0]
    inv_mag = jnp.exp(-lam_re * dt * n[:tc])
    iw_re = inv_mag * jnp.cos(lam_im * dt * n[:tc])
    iw_im = -inv_mag * jnp.sin(lam_im * dt * n[:tc])
    l_re = iw_re[..., None] * bb_re[None] - iw_im[..., None] * bb_im[None]
    l_im = iw_re[..., None] * bb_im[None] + iw_im[..., None] * bb_re[None]
    left = jnp.concatenate([l_re, l_im], axis=2).transpose(1, 0, 3, 2)
    right = jnp.concatenate([ca_re[:tc], -ca_im[:tc]], axis=3).transpose(1, 0, 2, 3)
    cw = tc * SSM_GROUP
    toep = jnp.einsum('gik,gjk->gij', left.reshape(groups, cw, 2 * SSM_STATE),
                      right.reshape(groups, cw, 2 * SSM_STATE), precision=hp)
    step = jnp.arange(cw) // SSM_GROUP
    toep = jnp.where(step[:, None] <= step[None, :], toep, 0.0)
    rev_re, rev_im = pw_re[tc - 1::-1][:tc], pw_im[tc - 1::-1][:tc]
    ws_re = rev_re[..., None] * bb_re[None] - rev_im[..., None] * bb_im[None]
    ws_im = rev_re[..., None] * bb_im[None] + rev_im[..., None] * bb_re[None]
    wstate = jnp.concatenate([ws_re, ws_im], axis=2)
    wstate = wstate.transpose(1, 0, 3, 2).reshape(groups, tc * SSM_GROUP, 2 * SSM_STATE)
    wout = jnp.concatenate([ca_re[1:], -ca_im[1:]], axis=3)
    wout = wout.transpose(1, 3, 0, 2).reshape(groups, 2 * SSM_STATE, tc * SSM_GROUP)
    dec_re, dec_im = pw_re[tc], pw_im[tc]
    a1 = jnp.concatenate([dec_re, dec_re], axis=1)
    a2 = jnp.concatenate([-dec_im, dec_im], axis=1)
    return toep.astype(BF16), wstate.astype(BF16), wout.astype(BF16), a1, a2


def _ssm(u, tables, d_skip, w_glu, *, bsz, seq, tm):
    toep, wstate, wout, a1, a2 = tables
    groups = toep.shape[0]
    tc = SSM_CHUNK
    nc = seq // tc
    rows = bsz * nc
    cw = tc * SSM_GROUP
    ug = u.reshape(bsz, nc, tc, groups, SSM_GROUP).transpose(3, 0, 1, 2, 4).reshape(groups, rows, cw)
    gspec = lambda r, c: pl.BlockSpec((None, r, c), lambda g: (g, 0, 0))
    loc = pl.pallas_call(
        _ssm_state_kernel,
        grid=(groups,),
        in_specs=[gspec(rows, cw), gspec(cw, 2 * SSM_STATE)],
        out_specs=gspec(rows, 2 * SSM_STATE),
        out_shape=jax.ShapeDtypeStruct((groups, rows, 2 * SSM_STATE), F32),
        compiler_params=_params("parallel"),
        name="ssm_chunk_state",
    )(ug, wstate)
    gb = groups * bsz
    loc_t = loc.reshape(groups, bsz, nc, 2 * SSM_STATE).transpose(2, 0, 1, 3).reshape(nc, gb, 2 * SSM_STATE)
    a1r = jnp.repeat(a1, bsz, axis=0)
    a2r = jnp.repeat(a2, bsz, axis=0)
    rb = 8 if gb % 8 == 0 else gb
    prev = pl.pallas_call(
        _ssm_scan_kernel,
        grid=(gb // rb,),
        in_specs=[pl.BlockSpec((nc, rb, 2 * SSM_STATE), lambda i: (0, i, 0)),
                  pl.BlockSpec((rb, 2 * SSM_STATE), lambda i: (i, 0)),
                  pl.BlockSpec((rb, 2 * SSM_STATE), lambda i: (i, 0))],
        out_specs=pl.BlockSpec((nc, rb, 2 * SSM_STATE), lambda i: (0, i, 0)),
        out_shape=jax.ShapeDtypeStruct((nc, gb, 2 * SSM_STATE), F32),
        compiler_params=_params("parallel"),
        name="ssm_chunk_scan",
    )(loc_t, a1r, a2r)
    prev_g = prev.reshape(nc, groups, bsz, 2 * SSM_STATE).transpose(1, 2, 0, 3).reshape(groups, rows, 2 * SSM_STATE)
    y = pl.pallas_call(
        _ssm_out_kernel,
        grid=(groups,),
        in_specs=[gspec(rows, cw), gspec(rows, 2 * SSM_STATE), gspec(cw, cw), gspec(2 * SSM_STATE, cw)],
        out_specs=gspec(rows, cw),
        out_shape=jax.ShapeDtypeStruct((groups, rows, cw), F32),
        compiler_params=_params("parallel"),
        name="ssm_chunk_out",
    )(ug, prev_g, toep, wout)
    y = y.reshape(groups, bsz, nc, tc, SSM_GROUP).transpose(1, 2, 3, 0, 4).reshape(bsz * seq, groups * SSM_GROUP)
    width = groups * SSM_GROUP
    return pl.pallas_call(
        _ssm_gate_kernel,
        grid=(bsz * seq // tm,),
        in_specs=[pl.BlockSpec((tm, width), lambda i: (i, 0)),
                  pl.BlockSpec((tm, width), lambda i: (i, 0)),
                  _resident((1, width)), _resident((width, width))],
        out_specs=pl.BlockSpec((tm, width), lambda i: (i, 0)),
        out_shape=jax.ShapeDtypeStruct((bsz * seq, width), BF16),
        compiler_params=_params("parallel"),
        name="ssm_gate",
    )(y, u, d_skip, w_glu)


def _outproj_kernel(x_ref, da_ref, ssm_ref, sa_ref, w1, w2, w3, g_ref, b_ref, o_ref):
    mix = _dot(da_ref[...], w1[...]) + _dot(ssm_ref[...], w2[...]) + _dot(sa_ref[...], w3[...])
    o_ref[...] = _layer_norm(ALPHA * x_ref[...] + mix, g_ref[...], b_ref[...])


def _outproj(x, o_da, o_ssm, o_sa, w1, w2, w3, g, b, *, tm):
    t_rows, d = x.shape
    row = lambda w: pl.BlockSpec((tm, w), lambda i: (i, 0))
    return pl.pallas_call(
        _outproj_kernel,
        grid=(t_rows // tm,),
        in_specs=[row(d), row(o_da.shape[1]), row(o_ssm.shape[1]), row(o_sa.shape[1]),
                  _resident(w1.shape), _resident(w2.shape), _resident(w3.shape),
                  _resident(g.shape), _resident(b.shape)],
        out_specs=row(d),
        out_shape=jax.ShapeDtypeStruct((t_rows, d), F32),
        compiler_params=_params("parallel"),
        name="out_proj_ln",
    )(x, o_da, o_ssm, o_sa, w1, w2, w3, g, b)


def _t5_bucket(n):
    max_exact = REL_BUCKETS // 2
    nf = jnp.maximum(n, 1).astype(F32)
    large = max_exact + (jnp.log(nf / max_exact) / math.log(REL_MAX_DIST / max_exact)
                         * (REL_BUCKETS - max_exact)).astype(I32)
    large = jnp.minimum(large, REL_BUCKETS - 1)
    return jnp.where(n < max_exact, n, large)


def _bias_tiles(table, t):
    assert t >= REL_MAX_DIST
    table = table.astype(F32) * LOG2E
    key = jnp.arange(t)[:, None]
    query = jnp.arange(t)[None, :]
    dist = jnp.stack([query - key, t + query - key])
    bucket = _t5_bucket(jnp.maximum(dist, 0))
    tiles = jnp.zeros((table.shape[1],) + dist.shape, F32)
    for b in range(REL_BUCKETS):
        tiles = jnp.where(bucket[None] == b, table[b][:, None, None, None], tiles)
    return jnp.where(dist[None] >= 0, tiles, NEG), table[REL_BUCKETS - 1]


def _split_w_in(w_in):
    offs = [0]
    for s in IN_SIZES:
        offs.append(offs[-1] + s)
    da_q, da_k, da_v, ssm_u, sa_q, sa_k, sa_v, ix_q, ix_k, ix_w = (
        w_in[:, offs[j]:offs[j + 1]] for j in range(len(IN_SIZES)))
    w_scale = IDX_HEADS ** -0.5 * IDX_DIM ** -0.5
    pad = lambda w: jnp.pad(w, ((0, 0), (0, LANES - w.shape[1])))
    weights = [(da_q * (DA_QK_DIM ** -0.5 * LOG2E)).T, da_k, da_v.T, ssm_u,
               (sa_q * (SA_HEAD_DIM ** -0.5 * LOG2E)).T, pad(sa_k), sa_v.T,
               ix_q.T, pad(ix_k), (ix_w * w_scale).T]
    dtypes = [BF16, BF16, BF16, F32, BF16, BF16, BF16, BF16, BF16, F32]
    transposed = [True, False, True, False, True, False, True, True, False, True]
    return [w.astype(BF16) for w in weights], dtypes, transposed


def _value_chunks(v_t, rows, bsz, seq, t):
    g = v_t.shape[0] // rows
    return v_t.reshape(g, rows, bsz, seq // t, t).transpose(2, 0, 3, 1, 4)


def _plan(bsz, seq):
    rows = bsz * seq
    tm = 512 if rows % 512 == 0 else rows
    t_da = 1024 if seq % 1024 == 0 and seq >= 4096 else 256 if seq % 256 == 0 else 128
    t_sa = 512 if seq % 512 == 0 and seq >= 2048 else 256 if seq % 256 == 0 else 128
    return dict(tm=tm, t_da=t_da, t_sa=t_sa)


def kernel(x, p, rel_bias, ffn1_w_gate, ffn1_w_up, ffn1_w_down, ln1_g, ln1_b, w_in, w_o, da_lam_q1, da_lam_k1, da_lam_q2, da_lam_k2, da_subln_g, ssm_lam_re, ssm_lam_im, ssm_log_dt, ssm_b_re, ssm_b_im, ssm_c_re, ssm_c_im, ssm_d, ssm_w_glu, ln2_g, ln2_b, ffn2_w_gate, ffn2_w_up, ffn2_w_down, ple_w_proj, ple_w_gate, ln3_g, ln3_b):
    bsz, seq, d = x.shape
    rows = bsz * seq
    plan = _plan(bsz, seq)
    tm, t_da, t_sa = plan["tm"], plan["t_da"], plan["t_sa"]
    topk = min(TOPK_MAX, seq // 4)
    da_near, da_far = _bias_tiles(rel_bias[:, :DA_HEADS], t_da)
    sa_near, sa_far = _bias_tiles(rel_bias[:, DA_HEADS:], t_sa)
    vec = lambda a: a.reshape(1, -1).astype(F32)
    da_w = DA_HEADS * DA_V_DIM

    h = x.reshape(rows, d)
    for i in range(DEPTH):
        lam_init = 0.8 - 0.6 * math.exp(-0.3 * i)
        h = _ffn(h, ffn1_w_gate[i].astype(BF16), ffn1_w_up[i].astype(BF16), ffn1_w_down[i].astype(BF16),
                 vec(ln1_g[i]), vec(ln1_b[i]), tm=tm)

        weights, dtypes, transposed = _split_w_in(w_in[i])
        (da_q, da_k, da_v, ssm_u, sa_q, sa_k, sa_v, ix_q, ix_k, ix_w) = _inproj(
            h, weights, dtypes, transposed, tm=tm)
        b3 = lambda a: a.reshape(bsz, seq, a.shape[-1])

        lam = (jnp.exp(jnp.sum(da_lam_q1[i].astype(F32) * da_lam_k1[i]))
               - jnp.exp(jnp.sum(da_lam_q2[i].astype(F32) * da_lam_k2[i])) + lam_init)
        subln = jnp.broadcast_to(da_subln_g[i].astype(F32)[:, None], (DA_V_DIM, t_da))
        o_da = _diff_attention(da_q, b3(da_k), _value_chunks(da_v, DA_V_DIM, bsz, seq, t_da), da_near, da_far,
                               lam.reshape(1).astype(F32), subln, bsz=bsz, seq=seq, t=t_da,
                               post_scale=1.0 - lam_init)

        tables = _ssm_tables(ssm_lam_re[i].astype(F32), ssm_lam_im[i].astype(F32), ssm_log_dt[i].astype(F32),
                             ssm_b_re[i].astype(F32), ssm_b_im[i].astype(F32),
                             ssm_c_re[i].astype(F32), ssm_c_im[i].astype(F32), SSM_CHUNK)
        o_ssm = _ssm(ssm_u, tables, vec(ssm_d[i]), ssm_w_glu[i].astype(BF16), bsz=bsz, seq=seq, tm=tm)

        sav = _value_chunks(sa_v, SA_HEAD_DIM, bsz, seq, t_sa)[:, 0]
        o_sa = _sparse_attention(ix_q, ix_w, b3(ix_k), sa_q, b3(sa_k), sav, sa_near, sa_far,
                                 bsz=bsz, seq=seq, t=t_sa, topk=topk)

        wo = w_o[i].astype(BF16)
        h = _outproj(h, o_da.reshape(rows, da_w), o_ssm, o_sa.reshape(rows, SA_WIDTH),
                     wo[:da_w], wo[da_w:da_w + SSM_WIDTH], wo[da_w + SSM_WIDTH:],
                     vec(ln2_g[i]), vec(ln2_b[i]), tm=tm)

        h = _ffn(h, ffn2_w_gate[i].astype(BF16), ffn2_w_up[i].astype(BF16), ffn2_w_down[i].astype(BF16),
                 vec(ln3_g[i]), vec(ln3_b[i]),
                 ple=(p[i].reshape(rows, -1), ple_w_proj[i].astype(BF16), ple_w_gate[i].astype(BF16)), tm=tm)
    return h.reshape(bsz, seq, d)
```

```python
import functools
import math

import jax
import jax.numpy as jnp
from jax import lax
from jax.experimental import pallas as pl
from jax.experimental.pallas import tpu as pltpu

F32 = jnp.float32
BF16 = jnp.bfloat16
I32 = jnp.int32

DEPTH = 2
DA_QK_DIM = 64
DA_V_DIM = 2 * DA_QK_DIM
DA_HEADS = 4
SSM_GROUP = 16
SSM_GROUPS = 16
SSM_STATE = 64
SSM_WIDTH = SSM_GROUP * SSM_GROUPS
SA_HEAD_DIM = 64
SA_HEADS = 4
SA_WIDTH = SA_HEADS * SA_HEAD_DIM
IDX_HEADS = 8
IDX_DIM = 32
TOPK_MAX = 256
REL_BUCKETS = 32
REL_MAX_DIST = 128
ALPHA = (2 * DEPTH) ** 0.25
LN_EPS = 1e-5
IN_SIZES = (DA_HEADS * 2 * DA_QK_DIM, DA_HEADS * 2 * DA_QK_DIM, DA_HEADS * DA_V_DIM,
            SSM_WIDTH, SA_WIDTH, SA_HEAD_DIM, SA_HEAD_DIM,
            IDX_HEADS * IDX_DIM, IDX_DIM, IDX_HEADS)

LANES = 128
SUBLANES = 8
BF16_ROWS = 16
NEG = -1e30
INT_MIN = -2 ** 31
HI_MASK = -2 ** 16
TINY = 2.0 ** -126
BF16_MIN_NORMAL = 0x0080
LOG2E = math.log2(math.e)
VMEM_LIMIT = 56 * 1024 * 1024
SSM_CHUNK = 64

def _params(*sem):
    return pltpu.CompilerParams(dimension_semantics=sem, vmem_limit_bytes=VMEM_LIMIT)


def _resident(shape):
    return pl.BlockSpec(shape, lambda *_: (0,) * len(shape), pipeline_mode=pl.Buffered(1))


def _dot(a, b):
    return jnp.dot(a, b, preferred_element_type=F32)


def _dot_nt(a, b):
    return lax.dot_general(a, b, (((1,), (1,)), ((), ())), preferred_element_type=F32)


def _layer_norm(y, g, b):
    mu = jnp.mean(y, axis=-1, keepdims=True)
    d = y - mu
    var = jnp.mean(d * d, axis=-1, keepdims=True)
    return d * lax.rsqrt(var + LN_EPS) * g + b


def _fold_rows(x, rows):
    while x.shape[0] > rows:
        half = x.shape[0] // 2
        x = x[:half] + x[half:]
    return x


FF_CHUNK = 512


def _swiglu(xb, wg, wu, wd):
    d_ff = wg.shape[1]
    acc = None
    for c0 in range(0, d_ff, FF_CHUNK):
        c1 = min(c0 + FF_CHUNK, d_ff)
        gate = _dot(xb, wg[:, c0:c1])
        up = _dot(xb, wu[:, c0:c1])
        hid = (gate * jax.nn.sigmoid(gate) * up).astype(BF16)
        part = _dot(hid, wd[c0:c1, :])
        acc = part if acc is None else acc + part
    return acc


def _ffn_inproj_kernel(x_ref, wg, wu, wd, g_ref, b_ref, *refs, transposed):
    n = len(transposed)
    w_refs, x1_ref, o_refs = refs[:n], refs[n], refs[n + 1:]
    x = x_ref[...]
    x1 = _layer_norm(ALPHA * x + 0.5 * _swiglu(x.astype(BF16), wg, wu, wd), g_ref[...], b_ref[...])
    x1_ref[...] = x1
    xb = x1.astype(BF16)
    for w_ref, o_ref, tr in zip(w_refs, o_refs, transposed):
        out = _dot_nt(w_ref[...], xb) if tr else _dot(xb, w_ref[...])
        o_ref[...] = out.astype(o_ref.dtype)


def _ffn_inproj(x, wg, wu, wd, g, b, weights, dtypes, transposed, *, tm):
    t_rows, d = x.shape
    row = lambda w: pl.BlockSpec((tm, w), lambda i: (i, 0))
    out_specs, out_shape = [row(d)], [jax.ShapeDtypeStruct((t_rows, d), F32)]
    for w, dt, tr in zip(weights, dtypes, transposed):
        if tr:
            out_specs.append(pl.BlockSpec((w.shape[0], tm), lambda i: (0, i)))
            out_shape.append(jax.ShapeDtypeStruct((w.shape[0], t_rows), dt))
        else:
            out_specs.append(row(w.shape[1]))
            out_shape.append(jax.ShapeDtypeStruct((t_rows, w.shape[1]), dt))
    consts = [wg, wu, wd, g, b] + list(weights)
    outs = pl.pallas_call(
        functools.partial(_ffn_inproj_kernel, transposed=tuple(transposed)),
        grid=(t_rows // tm,),
        in_specs=[row(d)] + [_resident(c.shape) for c in consts],
        out_specs=out_specs,
        out_shape=out_shape,
        compiler_params=_params("parallel"),
        name="ffn_ln_in_proj",
    )(x, *consts)
    return outs[0], outs[1:]


def _mix_ffn_kernel(x_ref, da_ref, ssm_ref, sa_ref, p_ref, w1, w2, w3, g2_ref, b2_ref,
                    wg, wu, wd, wpp, wpg, g3_ref, b3_ref, o_ref):
    mix = _dot(da_ref[...], w1[...]) + _dot(ssm_ref[...], w2[...]) + _dot(sa_ref[...], w3[...])
    x = _layer_norm(ALPHA * x_ref[...] + mix, g2_ref[...], b2_ref[...])
    xb = x.astype(BF16)
    y = ALPHA * x + 0.5 * _swiglu(xb, wg, wu, wd)
    y = y + _dot(p_ref[...].astype(BF16), wpp[...]) * jax.nn.sigmoid(_dot(xb, wpg[...]))
    o_ref[...] = _layer_norm(y, g3_ref[...], b3_ref[...])


def _mix_ffn(x, o_da, o_ssm, o_sa, p, consts, *, tm):
    t_rows, d = x.shape
    row = lambda a: pl.BlockSpec((tm, a.shape[1]), lambda i: (i, 0))
    acts = [x, o_da, o_ssm, o_sa, p]
    return pl.pallas_call(
        _mix_ffn_kernel,
        grid=(t_rows // tm,),
        in_specs=[row(a) for a in acts] + [_resident(c.shape) for c in consts],
        out_specs=pl.BlockSpec((tm, d), lambda i: (i, 0)),
        out_shape=jax.ShapeDtypeStruct((t_rows, d), F32),
        compiler_params=_params("parallel"),
        name="out_proj_ln_ffn_ple_ln",
    )(*acts, *consts)


def _softmax_probs(s, m_ref, shift):
    m_prev = m_ref[...]
    m_cur = jnp.max(s, axis=0, keepdims=True)
    if shift is not None:
        m_cur = m_cur + shift
    m_next = jnp.maximum(m_prev, m_cur)
    m_ref[...] = m_next
    p = jnp.exp2(s - (m_next if shift is None else m_next - shift))
    return p.astype(BF16), jnp.exp2(m_prev - m_next)


def _accumulate(acc_ref, rescale, v_t, p):
    v_aug = jnp.concatenate([v_t, jnp.ones((BF16_ROWS, v_t.shape[1]), v_t.dtype)], axis=0)
    acc_ref[...] = rescale * acc_ref[...] + _dot(v_aug, p)


def _softmax_step(s, v_t, m_ref, acc_ref, shift):
    p, rescale = _softmax_probs(s, m_ref, shift)
    _accumulate(acc_ref, rescale, v_t, p)


def _far_chunks(n, scores, values, m_ref, acc_ref, shift):
    def body(i, carry):
        _softmax_step(scores(i), values(i), m_ref, acc_ref, shift)
        return carry

    lax.fori_loop(0, n, body, 0)


def _da_kernel(far_ref, lam_ref, q_ref, k_ref, v_ref, bn_ref, g_ref, o_ref,
               qd_ref, m_ref, acc_ref, *, t, post_scale):
    h = pl.program_id(1)
    qi = pl.program_id(2)
    q = q_ref[...].astype(F32)
    rowq = lax.broadcasted_iota(I32, q.shape, 0)
    qd_ref[:, 0:t] = jnp.where(rowq < DA_QK_DIM, q, 0.0).astype(BF16)
    qd_ref[:, t:2 * t] = jnp.where(rowq >= DA_QK_DIM, q, 0.0).astype(BF16)
    m_ref[...] = jnp.full(m_ref.shape, NEG, F32)
    acc_ref[...] = jnp.zeros(acc_ref.shape, F32)
    far = far_ref[h]

    def scores(kb):
        return _dot(k_ref[pl.ds(pl.multiple_of(kb * t, t), t), :], qd_ref[...])

    def near_block(kb, near):
        bias = bn_ref[near]
        _softmax_step(scores(kb) + jnp.concatenate([bias, bias], axis=1), v_ref[kb], m_ref, acc_ref, None)

    _far_chunks(jnp.maximum(qi - 1, 0), scores, lambda kb: v_ref[kb], m_ref, acc_ref, far)

    @pl.when(qi >= 1)
    def _():
        near_block(qi - 1, 1)

    near_block(qi, 0)

    acc = acc_ref[...]
    o = acc[0:DA_V_DIM] / acc[DA_V_DIM:DA_V_DIM + 1]
    o = o[:, 0:t] - lam_ref[0] * o[:, t:2 * t]
    ms = jnp.mean(o * o, axis=0, keepdims=True)
    o = o * lax.rsqrt(ms + LN_EPS) * g_ref[...] * post_scale
    o_ref[...] = o.T.astype(o_ref.dtype)


def _diff_attention(q_t, k, v_t, bias_near, bias_far, lam, subln_g, *, bsz, seq, t, post_scale):
    heads = q_t.shape[0] // DA_V_DIM
    nq = seq // t
    smem = pl.BlockSpec(memory_space=pltpu.SMEM)
    va = DA_V_DIM + BF16_ROWS
    return pl.pallas_call(
        functools.partial(_da_kernel, t=t, post_scale=post_scale),
        grid=(bsz, heads, nq),
        in_specs=[
            smem, smem,
            pl.BlockSpec((DA_V_DIM, t), lambda b, h, i: (h, b * nq + i)),
            pl.BlockSpec((None, seq, DA_V_DIM), lambda b, h, i: (b, 0, h)),
            pl.BlockSpec((None, None, nq, DA_V_DIM, t), lambda b, h, i: (b, h, 0, 0, 0)),
            pl.BlockSpec((None, 2, t, t), lambda b, h, i: (h, 0, 0, 0)),
            pl.BlockSpec((DA_V_DIM, t), lambda b, h, i: (0, 0)),
        ],
        out_specs=pl.BlockSpec((None, t, DA_V_DIM), lambda b, h, i: (b, i, h)),
        out_shape=jax.ShapeDtypeStruct((bsz, seq, heads * DA_V_DIM), BF16),
        scratch_shapes=[
            pltpu.VMEM((DA_V_DIM, 2 * t), BF16),
            pltpu.VMEM((1, 2 * t), F32),
            pltpu.VMEM((va, 2 * t), F32),
        ],
        compiler_params=_params("parallel", "parallel", "arbitrary"),
        name="diff_attention",
    )(bias_far, lam, q_t, k, v_t, bias_near, subln_g)


def _dsa_kernel(far_ref, ixq_ref, ixw_ref, ixk_ref, saq_ref, sak_ref, sav_ref, bn_ref, o_ref,
                qi8_ref, keys_ref, hi_ref, qs4_ref, m_ref, acc_ref, *, t, topk):
    qi = pl.program_id(1)
    kf = float(topk)

    qi8_ref[...] = jnp.zeros(qi8_ref.shape, BF16)
    for h in range(IDX_HEADS):
        qi8_ref[0:IDX_DIM, h * t:(h + 1) * t] = ixq_ref[h * IDX_DIM:(h + 1) * IDX_DIM, :]

    def index_keys(kb):
        kblk = ixk_ref[pl.ds(pl.multiple_of(kb * t, t), t), :]
        sc = None
        for h in range(IDX_HEADS):
            r = _dot(kblk, qi8_ref[:, h * t:(h + 1) * t])
            term = ixw_ref[h:h + 1, :] * jnp.maximum(r, 0.0)
            sc = term if sc is None else sc + term
        return jnp.where(jnp.abs(sc) < TINY, 0.0, sc)

    def rows(kb):
        return pl.ds(pl.multiple_of(kb * t, t), t)

    def chunk(kb):
        return keys_ref[rows(kb), :]

    def store_keys(kb, sc):
        bits = lax.bitcast_convert_type(sc, I32)
        keys_ref[rows(kb), :] = bits ^ ((bits >> 31) & 0x7FFFFFFF)
        hi_ref[rows(kb), :] = lax.bitcast_convert_type(bits & HI_MASK, F32).astype(BF16)

    def fill(kb, carry):
        store_keys(kb, index_keys(kb))
        return carry

    lax.fori_loop(0, qi, fill, 0)
    krow = lax.broadcasted_iota(I32, (t, t), 0)
    qcol = lax.broadcasted_iota(I32, (t, t), 1)
    store_keys(qi, jnp.where(krow <= qcol, index_keys(qi), -jnp.inf))

    def count(fn):
        def body(kb, cnt):
            return cnt + _fold_rows(fn(kb, chunk(kb)), SUBLANES)
        cnt = lax.fori_loop(0, qi + 1, body, jnp.zeros((SUBLANES, t), I32))
        return jnp.sum(cnt.astype(F32), axis=0, keepdims=True)

    def count_hi(thr):
        def body(kb, cnt):
            hit = jnp.where(hi_ref[rows(kb), :] >= thr, jnp.ones((), BF16), jnp.zeros((), BF16))
            return cnt + _fold_rows(hit, BF16_ROWS).astype(F32)
        cnt = lax.fori_loop(0, qi + 1, body, jnp.zeros((BF16_ROWS, t), F32))
        return jnp.sum(cnt, axis=0, keepdims=True)

    def hi_bit(i, st):
        code, open_ = st
        cand = code + jnp.left_shift(jnp.int32(1), 15 - i)
        pattern = cand ^ ((cand >> 31) & 0x7FFF)
        pattern = jnp.where(cand > 0, jnp.maximum(pattern, BF16_MIN_NORMAL), pattern)
        thr = lax.bitcast_convert_type(jnp.left_shift(pattern, 16), F32).astype(BF16)
        c = count_hi(thr)
        code = jnp.where(open_ > 0.0, jnp.where(c >= kf, cand, code), code)
        return code, jnp.where(c == kf, 0.0, open_)

    code, open_ = lax.fori_loop(0, 16, hi_bit, (jnp.full((1, t), -2 ** 15, I32), jnp.ones((1, t), F32)))
    zero_tie = jnp.where(code == 0, open_, 0.0)
    open_ = jnp.where(code == 0, 0.0, open_)

    def search_cond(st):
        i, _, _, n_open = st
        return jnp.logical_and(i < 32, n_open > 0.0)

    def search_body(st):
        i, tau, open_, _ = st
        cand = tau + jnp.left_shift(jnp.int32(1), 31 - i)
        c = count(lambda kb, blk: jnp.where(blk >= cand, 1, 0))
        tau = jnp.where(open_ > 0.0, jnp.where(c >= kf, cand, tau), tau)
        open_ = jnp.where(c == kf, 0.0, open_)
        return i + 1, tau, open_, jnp.max(open_)

    _, tau, open_, _ = lax.while_loop(
        search_cond, search_body, (jnp.int32(16), jnp.left_shift(code, 16), open_, jnp.max(open_)))

    @pl.when(jnp.max(jnp.maximum(open_, zero_tie)) > 0.0)
    def _():
        need = kf - count(lambda kb, blk: jnp.where(blk > tau, 1, 0))
        below = jnp.where(qcol <= krow, 1.0, 0.0).astype(BF16)

        def retire(kb, seen):
            blk = chunk(kb)
            tied = jnp.where(blk == tau, 1.0, 0.0).astype(BF16)
            rank = _dot(below, tied) + seen
            keys_ref[rows(kb), :] = jnp.where(blk == tau, jnp.where(rank > need, INT_MIN, blk), blk)
            return rank[t - 1:t, :]

        lax.fori_loop(0, qi + 1, retire, jnp.zeros((1, t), F32))

    qs4_ref[...] = jnp.zeros(qs4_ref.shape, BF16)
    for h in range(SA_HEADS):
        qs4_ref[0:SA_HEAD_DIM, h * t:(h + 1) * t] = saq_ref[h * SA_HEAD_DIM:(h + 1) * SA_HEAD_DIM, :]
    m_ref[...] = jnp.full(m_ref.shape, NEG, F32)
    acc_ref[...] = jnp.zeros(acc_ref.shape, F32)

    def scores(kb, near=None):
        s = _dot(sak_ref[pl.ds(pl.multiple_of(kb * t, t), t), :], qs4_ref[...])
        sel = chunk(kb) >= tau
        parts = []
        for h in range(SA_HEADS):
            sh = s[:, h * t:(h + 1) * t]
            if near is not None:
                sh = sh + bn_ref[h, near]
            parts.append(jnp.where(sel, sh, NEG))
        return jnp.concatenate(parts, axis=1)

    far = jnp.concatenate([jnp.full((1, t), far_ref[h], F32) for h in range(SA_HEADS)], axis=1)
    _far_chunks(jnp.maximum(qi - 1, 0), scores, lambda kb: sav_ref[kb], m_ref, acc_ref, far)

    @pl.when(qi >= 1)
    def _():
        _softmax_step(scores(qi - 1, 1), sav_ref[qi - 1], m_ref, acc_ref, None)

    _softmax_step(scores(qi, 0), sav_ref[qi], m_ref, acc_ref, None)

    acc = acc_ref[...]
    o = acc[0:SA_HEAD_DIM] / acc[SA_HEAD_DIM:SA_HEAD_DIM + 1]
    o = jnp.concatenate([o[:, h * t:(h + 1) * t] for h in range(SA_HEADS)], axis=0)
    o_ref[...] = o.T.astype(o_ref.dtype)


def _sparse_attention(ixq_t, ixw_t, ixk, saq_t, sak, sav_t, bias_near, bias_far, *, bsz, seq, t, topk):
    nq = seq // t
    smem = pl.BlockSpec(memory_space=pltpu.SMEM)
    qcols = lambda r: pl.BlockSpec((r, t), lambda b, i: (0, b * nq + i))
    full = lambda w: pl.BlockSpec((None, seq, w), lambda b, i: (b, 0, 0))
    va = SA_HEAD_DIM + BF16_ROWS
    return pl.pallas_call(
        functools.partial(_dsa_kernel, t=t, topk=topk),
        grid=(bsz, nq),
        in_specs=[
            smem,
            qcols(IDX_HEADS * IDX_DIM), qcols(IDX_HEADS), full(LANES),
            qcols(SA_WIDTH), full(LANES),
            pl.BlockSpec((None, nq, SA_HEAD_DIM, t), lambda b, i: (b, 0, 0, 0)),
            _resident((SA_HEADS, 2, t, t)),
        ],
        out_specs=pl.BlockSpec((None, t, SA_WIDTH), lambda b, i: (b, i, 0)),
        out_shape=jax.ShapeDtypeStruct((bsz, seq, SA_WIDTH), BF16),
        scratch_shapes=[
            pltpu.VMEM((LANES, IDX_HEADS * t), BF16),
            pltpu.VMEM((seq, t), I32),
            pltpu.VMEM((seq, t), BF16),
            pltpu.VMEM((LANES, SA_HEADS * t), BF16),
            pltpu.VMEM((1, SA_HEADS * t), F32),
            pltpu.VMEM((va, SA_HEADS * t), F32),
        ],
        compiler_params=_params("parallel", "arbitrary"),
        name="sparse_attention",
    )(bias_far, ixq_t, ixw_t, ixk, saq_t, sak, sav_t, bias_near)


def _ssm_state_kernel(u_ref, w_ref, o_ref):
    o_ref[...] = _dot(u_ref[...].astype(BF16), w_ref[...])


def _ssm_scan_kernel(loc_ref, a1_ref, a2_ref, o_ref):
    a1 = a1_ref[...]
    a2 = a2_ref[...]

    def body(n, s):
        o_ref[n] = s
        return a1 * s + a2 * pltpu.roll(s, SSM_STATE, axis=1) + loc_ref[n]

    lax.fori_loop(0, loc_ref.shape[0], body, jnp.zeros(a1.shape, F32))


def _ssm_out_kernel(u_ref, s_ref, toep_ref, wout_ref, o_ref):
    o_ref[...] = (_dot(u_ref[...].astype(BF16), toep_ref[...])
                  + _dot(s_ref[...].astype(BF16), wout_ref[...]))


def _ssm_gate_kernel(y_ref, u_ref, d_ref, w_ref, o_ref):
    y = jax.nn.gelu(y_ref[...] + d_ref[...] * u_ref[...])
    o_ref[...] = (y * jax.nn.sigmoid(_dot(y.astype(BF16), w_ref[...]))).astype(o_ref.dtype)


def _ssm_tables(lam_re, lam_im, log_dt, b_re, b_im, c_re, c_im, tc):
    hp = lax.Precision.HIGHEST
    dt = jnp.exp(log_dt)[:, None]
    n = jnp.arange(tc + 1, dtype=F32)[:, None, None]
    mag = jnp.exp(lam_re * dt * n)
    pw_re = mag * jnp.cos(lam_im * dt * n)
    pw_im = mag * jnp.sin(lam_im * dt * n)
    den = lam_re * lam_re + lam_im * lam_im
    nr, ni = pw_re[1] - 1.0, pw_im[1]
    f_re = (nr * lam_re + ni * lam_im) / den
    f_im = (ni * lam_re - nr * lam_im) / den
    bb_re = f_re[..., None] * b_re - f_im[..., None] * b_im
    bb_im = f_re[..., None] * b_im + f_im[..., None] * b_re
    ca_re = c_re[None] * pw_re[:, :, None, :] - c_im[None] * pw_im[:, :, None, :]
    ca_im = c_re[None] * pw_im[:, :, None, :] + c_im[None] * pw_re[:, :, None, :]
    groups = lam_re.shape[0]
    inv_mag = jnp.exp(-lam_re * dt * n[:tc])
    iw_re = inv_mag * jnp.cos(lam_im * dt * n[:tc])
    iw_im = -inv_mag * jnp.sin(lam_im * dt * n[:tc])
    l_re = iw_re[..., None] * bb_re[None] - iw_im[..., None] * bb_im[None]
    l_im = iw_re[..., None] * bb_im[None] + iw_im[..., None] * bb_re[None]
    left = jnp.concatenate([l_re, l_im], axis=2).transpose(1, 0, 3, 2)
    right = jnp.concatenate([ca_re[:tc], -ca_im[:tc]], axis=3).transpose(1, 0, 2, 3)
    cw = tc * SSM_GROUP
    toep = jnp.einsum('gik,gjk->gij', left.reshape(groups, cw, 2 * SSM_STATE),
                      right.reshape(groups, cw, 2 * SSM_STATE), precision=hp)
    step = jnp.arange(cw) // SSM_GROUP
    toep = jnp.where(step[:, None] <= step[None, :], toep, 0.0)
    rev_re, rev_im = pw_re[tc - 1::-1][:tc], pw_im[tc - 1::-1][:tc]
    ws_re = rev_re[..., None] * bb_re[None] - rev_im[..., None] * bb_im[None]
    ws_im = rev_re[..., None] * bb_im[None] + rev_im[..., None] * bb_re[None]
    wstate = jnp.concatenate([ws_re, ws_im], axis=2)
    wstate = wstate.transpose(1, 0, 3, 2).reshape(groups, tc * SSM_GROUP, 2 * SSM_STATE)
    wout = jnp.concatenate([ca_re[1:], -ca_im[1:]], axis=3)
    wout = wout.transpose(1, 3, 0, 2).reshape(groups, 2 * SSM_STATE, tc * SSM_GROUP)
    dec_re, dec_im = pw_re[tc], pw_im[tc]
    a1 = jnp.concatenate([dec_re, dec_re], axis=1)
    a2 = jnp.concatenate([-dec_im, dec_im], axis=1)
    return toep.astype(BF16), wstate.astype(BF16), wout.astype(BF16), a1, a2


def _ssm(u, tables, d_skip, w_glu, *, bsz, seq, tm):
    toep, wstate, wout, a1, a2 = tables
    groups = toep.shape[0]
    tc = SSM_CHUNK
    nc = seq // tc
    rows = bsz * nc
    cw = tc * SSM_GROUP
    ug = u.reshape(bsz, nc, tc, groups, SSM_GROUP).transpose(3, 0, 1, 2, 4).reshape(groups, rows, cw)
    gspec = lambda r, c: pl.BlockSpec((None, r, c), lambda g: (g, 0, 0))
    loc = pl.pallas_call(
        _ssm_state_kernel,
        grid=(groups,),
        in_specs=[gspec(rows, cw), gspec(cw, 2 * SSM_STATE)],
        out_specs=gspec(rows, 2 * SSM_STATE),
        out_shape=jax.ShapeDtypeStruct((groups, rows, 2 * SSM_STATE), F32),
        compiler_params=_params("parallel"),
        name="ssm_chunk_state",
    )(ug, wstate)
    gb = groups * bsz
    loc_t = loc.reshape(groups, bsz, nc, 2 * SSM_STATE).transpose(2, 0, 1, 3).reshape(nc, gb, 2 * SSM_STATE)
    a1r = jnp.repeat(a1, bsz, axis=0)
    a2r = jnp.repeat(a2, bsz, axis=0)
    rb = 8 if gb % 8 == 0 else gb
    prev = pl.pallas_call(
        _ssm_scan_kernel,
        grid=(gb // rb,),
        in_specs=[pl.BlockSpec((nc, rb, 2 * SSM_STATE), lambda i: (0, i, 0)),
                  pl.BlockSpec((rb, 2 * SSM_STATE), lambda i: (i, 0)),
                  pl.BlockSpec((rb, 2 * SSM_STATE), lambda i: (i, 0))],
        out_specs=pl.BlockSpec((nc, rb, 2 * SSM_STATE), lambda i: (0, i, 0)),
        out_shape=jax.ShapeDtypeStruct((nc, gb, 2 * SSM_STATE), F32),
        compiler_params=_params("parallel"),
        name="ssm_chunk_scan",
    )(loc_t, a1r, a2r)
    prev_g = prev.reshape(nc, groups, bsz, 2 * SSM_STATE).transpose(1, 2, 0, 3).reshape(groups, rows, 2 * SSM_STATE)
    y = pl.pallas_call(
        _ssm_out_kernel,
        grid=(groups,),
        in_specs=[gspec(rows, cw), gspec(rows, 2 * SSM_STATE), gspec(cw, cw), gspec(2 * SSM_STATE, cw)],
        out_specs=gspec(rows, cw),
        out_shape=jax.ShapeDtypeStruct((groups, rows, cw), F32),
        compiler_params=_params("parallel"),
        name="ssm_chunk_out",
    )(ug, prev_g, toep, wout)
    y = y.reshape(groups, bsz, nc, tc, SSM_GROUP).transpose(1, 2, 3, 0, 4).reshape(bsz * seq, groups * SSM_GROUP)
    width = groups * SSM_GROUP
    return pl.pallas_call(
        _ssm_gate_kernel,
        grid=(bsz * seq // tm,),
        in_specs=[pl.BlockSpec((tm, width), lambda i: (i, 0)),
                  pl.BlockSpec((tm, width), lambda i: (i, 0)),
                  _resident((1, width)), _resident((width, width))],
        out_specs=pl.BlockSpec((tm, width), lambda i: (i, 0)),
        out_shape=jax.ShapeDtypeStruct((bsz * seq, width), BF16),
        compiler_params=_params("parallel"),
        name="ssm_gate",
    )(y, u, d_skip, w_glu)


def _t5_bucket(n):
    max_exact = REL_BUCKETS // 2
    nf = jnp.maximum(n, 1).astype(F32)
    large = max_exact + (jnp.log(nf / max_exact) / math.log(REL_MAX_DIST / max_exact)
                         * (REL_BUCKETS - max_exact)).astype(I32)
    large = jnp.minimum(large, REL_BUCKETS - 1)
    return jnp.where(n < max_exact, n, large)


def _bias_tiles(table, t):
    assert t >= REL_MAX_DIST
    table = table.astype(F32) * LOG2E
    key = jnp.arange(t)[:, None]
    query = jnp.arange(t)[None, :]
    dist = jnp.stack([query - key, t + query - key])
    bucket = _t5_bucket(jnp.maximum(dist, 0))
    tiles = jnp.zeros((table.shape[1],) + dist.shape, F32)
    for b in range(REL_BUCKETS):
        tiles = jnp.where(bucket[None] == b, table[b][:, None, None, None], tiles)
    return jnp.where(dist[None] >= 0, tiles, NEG), table[REL_BUCKETS - 1]


def _split_w_in(w_in):
    offs = [0]
    for s in IN_SIZES:
        offs.append(offs[-1] + s)
    da_q, da_k, da_v, ssm_u, sa_q, sa_k, sa_v, ix_q, ix_k, ix_w = (
        w_in[:, offs[j]:offs[j + 1]] for j in range(len(IN_SIZES)))
    w_scale = IDX_HEADS ** -0.5 * IDX_DIM ** -0.5
    pad = lambda w: jnp.pad(w, ((0, 0), (0, LANES - w.shape[1])))
    weights = [(da_q * (DA_QK_DIM ** -0.5 * LOG2E)).T, da_k, da_v.T, ssm_u,
               (sa_q * (SA_HEAD_DIM ** -0.5 * LOG2E)).T, pad(sa_k), sa_v.T,
               ix_q.T, pad(ix_k), (ix_w * w_scale).T]
    dtypes = [BF16, BF16, BF16, F32, BF16, BF16, BF16, BF16, BF16, F32]
    transposed = [True, False, True, False, True, False, True, True, False, True]
    return [w.astype(BF16) for w in weights], dtypes, transposed


def _value_chunks(v_t, rows, bsz, seq, t):
    g = v_t.shape[0] // rows
    return v_t.reshape(g, rows, bsz, seq // t, t).transpose(2, 0, 3, 1, 4)


def _plan(bsz, seq):
    rows = bsz * seq
    tm = 512 if rows % 512 == 0 else rows
    t_da = 1024 if seq % 1024 == 0 and seq >= 4096 else 256 if seq % 256 == 0 else 128
    t_sa = 512 if seq % 512 == 0 and seq >= 2048 else 256 if seq % 256 == 0 else 128
    return dict(tm=tm, t_da=t_da, t_sa=t_sa)


def kernel(x, p, rel_bias, ffn1_w_gate, ffn1_w_up, ffn1_w_down, ln1_g, ln1_b, w_in, w_o, da_lam_q1, da_lam_k1, da_lam_q2, da_lam_k2, da_subln_g, ssm_lam_re, ssm_lam_im, ssm_log_dt, ssm_b_re, ssm_b_im, ssm_c_re, ssm_c_im, ssm_d, ssm_w_glu, ln2_g, ln2_b, ffn2_w_gate, ffn2_w_up, ffn2_w_down, ple_w_proj, ple_w_gate, ln3_g, ln3_b):
    bsz, seq, d = x.shape
    rows = bsz * seq
    plan = _plan(bsz, seq)
    tm, t_da, t_sa = plan["tm"], plan["t_da"], plan["t_sa"]
    topk = min(TOPK_MAX, seq // 4)
    da_near, da_far = _bias_tiles(rel_bias[:, :DA_HEADS], t_da)
    sa_near, sa_far = _bias_tiles(rel_bias[:, DA_HEADS:], t_sa)
    vec = lambda a: a.reshape(1, -1).astype(F32)
    da_w = DA_HEADS * DA_V_DIM

    h = x.reshape(rows, d)
    for i in range(DEPTH):
        lam_init = 0.8 - 0.6 * math.exp(-0.3 * i)
        weights, dtypes, transposed = _split_w_in(w_in[i])
        h, (da_q, da_k, da_v, ssm_u, sa_q, sa_k, sa_v, ix_q, ix_k, ix_w) = _ffn_inproj(
            h, ffn1_w_gate[i].astype(BF16), ffn1_w_up[i].astype(BF16), ffn1_w_down[i].astype(BF16),
            vec(ln1_g[i]), vec(ln1_b[i]), weights, dtypes, transposed, tm=tm)
        b3 = lambda a: a.reshape(bsz, seq, a.shape[-1])

        lam = (jnp.exp(jnp.sum(da_lam_q1[i].astype(F32) * da_lam_k1[i]))
               - jnp.exp(jnp.sum(da_lam_q2[i].astype(F32) * da_lam_k2[i])) + lam_init)
        subln = jnp.broadcast_to(da_subln_g[i].astype(F32)[:, None], (DA_V_DIM, t_da))
        o_da = _diff_attention(da_q, b3(da_k), _value_chunks(da_v, DA_V_DIM, bsz, seq, t_da), da_near, da_far,
                               lam.reshape(1).astype(F32), subln, bsz=bsz, seq=seq, t=t_da,
                               post_scale=1.0 - lam_init)

        tables = _ssm_tables(ssm_lam_re[i].astype(F32), ssm_lam_im[i].astype(F32), ssm_log_dt[i].astype(F32),
                             ssm_b_re[i].astype(F32), ssm_b_im[i].astype(F32),
                             ssm_c_re[i].astype(F32), ssm_c_im[i].astype(F32), SSM_CHUNK)
        o_ssm = _ssm(ssm_u, tables, vec(ssm_d[i]), ssm_w_glu[i].astype(BF16), bsz=bsz, seq=seq, tm=tm)

        sav = _value_chunks(sa_v, SA_HEAD_DIM, bsz, seq, t_sa)[:, 0]
        o_sa = _sparse_attention(ix_q, ix_w, b3(ix_k), sa_q, b3(sa_k), sav, sa_near, sa_far,
                                 bsz=bsz, seq=seq, t=t_sa, topk=topk)

        wo = w_o[i].astype(BF16)
        h = _mix_ffn(h, o_da.reshape(rows, da_w), o_ssm, o_sa.reshape(rows, SA_WIDTH), p[i].reshape(rows, -1),
                     [wo[:da_w], wo[da_w:da_w + SSM_WIDTH], wo[da_w + SSM_WIDTH:], vec(ln2_g[i]), vec(ln2_b[i]),
                      ffn2_w_gate[i].astype(BF16), ffn2_w_up[i].astype(BF16), ffn2_w_down[i].astype(BF16),
                      ple_w_proj[i].astype(BF16), ple_w_gate[i].astype(BF16), vec(ln3_g[i]), vec(ln3_b[i])],
                     tm=tm)
    return h.reshape(bsz, seq, d)
```

```python
import functools
import math

import jax
import jax.numpy as jnp
from jax import lax
from jax.experimental import pallas as pl
from jax.experimental.pallas import tpu as pltpu

F32 = jnp.float32
BF16 = jnp.bfloat16
I32 = jnp.int32

DEPTH = 2
DA_QK_DIM = 64
DA_V_DIM = 2 * DA_QK_DIM
DA_HEADS = 4
SSM_GROUP = 16
SSM_GROUPS = 16
SSM_STATE = 64
SSM_WIDTH = SSM_GROUP * SSM_GROUPS
SA_HEAD_DIM = 64
SA_HEADS = 4
SA_WIDTH = SA_HEADS * SA_HEAD_DIM
IDX_HEADS = 8
IDX_DIM = 32
TOPK_MAX = 256
REL_BUCKETS = 32
REL_MAX_DIST = 128
ALPHA = (2 * DEPTH) ** 0.25
LN_EPS = 1e-5
IN_SIZES = (DA_HEADS * 2 * DA_QK_DIM, DA_HEADS * 2 * DA_QK_DIM, DA_HEADS * DA_V_DIM,
            SSM_WIDTH, SA_WIDTH, SA_HEAD_DIM, SA_HEAD_DIM,
            IDX_HEADS * IDX_DIM, IDX_DIM, IDX_HEADS)

LANES = 128
SUBLANES = 8
BF16_ROWS = 16
NEG = -1e30
INT_MIN = -2 ** 31
HI_MASK = -2 ** 16
TINY = 2.0 ** -126
BF16_MIN_NORMAL = 0x0080
LOG2E = math.log2(math.e)
VMEM_LIMIT = 56 * 1024 * 1024
SSM_CHUNK = 64

def _params(*sem):
    return pltpu.CompilerParams(dimension_semantics=sem, vmem_limit_bytes=VMEM_LIMIT)


def _resident(shape):
    return pl.BlockSpec(shape, lambda *_: (0,) * len(shape), pipeline_mode=pl.Buffered(1))


def _dot(a, b):
    return jnp.dot(a, b, preferred_element_type=F32)


def _dot_nt(a, b):
    return lax.dot_general(a, b, (((1,), (1,)), ((), ())), preferred_element_type=F32)


def _layer_norm(y, g, b):
    mu = jnp.mean(y, axis=-1, keepdims=True)
    d = y - mu
    var = jnp.mean(d * d, axis=-1, keepdims=True)
    return d * lax.rsqrt(var + LN_EPS) * g + b


def _fold_rows(x, rows):
    while x.shape[0] > rows:
        half = x.shape[0] // 2
        x = x[:half] + x[half:]
    return x


FF_CHUNK = 512


def _swiglu(xb, wg, wu, wd):
    d_ff = wg.shape[1]
    acc = None
    for c0 in range(0, d_ff, FF_CHUNK):
        c1 = min(c0 + FF_CHUNK, d_ff)
        gate = _dot(xb, wg[:, c0:c1])
        up = _dot(xb, wu[:, c0:c1])
        hid = (gate * jax.nn.sigmoid(gate) * up).astype(BF16)
        part = _dot(hid, wd[c0:c1, :])
        acc = part if acc is None else acc + part
    return acc


def _ffn_inproj_kernel(x_ref, wg, wu, wd, g_ref, b_ref, *refs, transposed):
    n = len(transposed)
    w_refs, x1_ref, o_refs = refs[:n], refs[n], refs[n + 1:]
    x = x_ref[...]
    x1 = _layer_norm(ALPHA * x + 0.5 * _swiglu(x.astype(BF16), wg, wu, wd), g_ref[...], b_ref[...])
    x1_ref[...] = x1
    xb = x1.astype(BF16)
    for w_ref, o_ref, tr in zip(w_refs, o_refs, transposed):
        out = _dot_nt(w_ref[...], xb) if tr else _dot(xb, w_ref[...])
        o_ref[...] = out.astype(o_ref.dtype)


def _ffn_inproj(x, wg, wu, wd, g, b, weights, dtypes, transposed, *, tm):
    t_rows, d = x.shape
    row = lambda w: pl.BlockSpec((tm, w), lambda i: (i, 0))
    out_specs, out_shape = [row(d)], [jax.ShapeDtypeStruct((t_rows, d), F32)]
    for w, dt, tr in zip(weights, dtypes, transposed):
        if tr:
            out_specs.append(pl.BlockSpec((w.shape[0], tm), lambda i: (0, i)))
            out_shape.append(jax.ShapeDtypeStruct((w.shape[0], t_rows), dt))
        else:
            out_specs.append(row(w.shape[1]))
            out_shape.append(jax.ShapeDtypeStruct((t_rows, w.shape[1]), dt))
    consts = [wg, wu, wd, g, b] + list(weights)
    outs = pl.pallas_call(
        functools.partial(_ffn_inproj_kernel, transposed=tuple(transposed)),
        grid=(t_rows // tm,),
        in_specs=[row(d)] + [_resident(c.shape) for c in consts],
        out_specs=out_specs,
        out_shape=out_shape,
        compiler_params=_params("parallel"),
        name="ffn_ln_in_proj",
    )(x, *consts)
    return outs[0], outs[1:]


def _mix_ffn_kernel(x_ref, da_ref, ssm_ref, sa_ref, p_ref, w1, w2, w3, g2_ref, b2_ref,
                    wg, wu, wd, wpp, wpg, g3_ref, b3_ref, o_ref):
    mix = _dot(da_ref[...], w1[...]) + _dot(ssm_ref[...], w2[...]) + _dot(sa_ref[...], w3[...])
    x = _layer_norm(ALPHA * x_ref[...] + mix, g2_ref[...], b2_ref[...])
    xb = x.astype(BF16)
    y = ALPHA * x + 0.5 * _swiglu(xb, wg, wu, wd)
    y = y + _dot(p_ref[...].astype(BF16), wpp[...]) * jax.nn.sigmoid(_dot(xb, wpg[...]))
    o_ref[...] = _layer_norm(y, g3_ref[...], b3_ref[...])


def _mix_ffn(x, o_da, o_ssm, o_sa, p, consts, *, tm):
    t_rows, d = x.shape
    row = lambda a: pl.BlockSpec((tm, a.shape[1]), lambda i: (i, 0))
    acts = [x, o_da, o_ssm, o_sa, p]
    return pl.pallas_call(
        _mix_ffn_kernel,
        grid=(t_rows // tm,),
        in_specs=[row(a) for a in acts] + [_resident(c.shape) for c in consts],
        out_specs=pl.BlockSpec((tm, d), lambda i: (i, 0)),
        out_shape=jax.ShapeDtypeStruct((t_rows, d), F32),
        compiler_params=_params("parallel"),
        name="out_proj_ln_ffn_ple_ln",
    )(*acts, *consts)


def _softmax_probs(s, m_ref, shift):
    m_prev = m_ref[...]
    m_cur = jnp.max(s, axis=0, keepdims=True)
    if shift is not None:
        m_cur = m_cur + shift
    m_next = jnp.maximum(m_prev, m_cur)
    m_ref[...] = m_next
    p = jnp.exp2(s - (m_next if shift is None else m_next - shift))
    return p.astype(BF16), jnp.exp2(m_prev - m_next)


def _accumulate(acc_ref, rescale, v_t, p):
    v_aug = jnp.concatenate([v_t, jnp.ones((BF16_ROWS, v_t.shape[1]), v_t.dtype)], axis=0)
    acc_ref[...] = rescale * acc_ref[...] + _dot(v_aug, p)


def _softmax_step(s, v_t, m_ref, acc_ref, shift):
    p, rescale = _softmax_probs(s, m_ref, shift)
    _accumulate(acc_ref, rescale, v_t, p)


def _far_chunks(n, scores, values, m_ref, acc_ref, shift):
    def body(i, carry):
        _softmax_step(scores(i), values(i), m_ref, acc_ref, shift)
        return carry

    lax.fori_loop(0, n, body, 0)


def _da_kernel(far_ref, lam_ref, q_ref, k_ref, v_ref, bn_ref, g_ref, o_ref,
               qd_ref, m_ref, acc_ref, *, t, post_scale):
    h = pl.program_id(1)
    qi = pl.program_id(2)
    q = q_ref[...].astype(F32)
    rowq = lax.broadcasted_iota(I32, q.shape, 0)
    qd_ref[:, 0:t] = jnp.where(rowq < DA_QK_DIM, q, 0.0).astype(BF16)
    qd_ref[:, t:2 * t] = jnp.where(rowq >= DA_QK_DIM, q, 0.0).astype(BF16)
    m_ref[...] = jnp.full(m_ref.shape, NEG, F32)
    acc_ref[...] = jnp.zeros(acc_ref.shape, F32)
    far = far_ref[h]

    def scores(kb):
        return _dot(k_ref[pl.ds(pl.multiple_of(kb * t, t), t), :], qd_ref[...])

    def values(kb):
        return v_ref[:, pl.ds(pl.multiple_of(kb * t, t), t)]

    def near_block(kb, near):
        bias = bn_ref[near]
        _softmax_step(scores(kb) + jnp.concatenate([bias, bias], axis=1), values(kb), m_ref, acc_ref, None)

    _far_chunks(jnp.maximum(qi - 1, 0), scores, values, m_ref, acc_ref, far)

    @pl.when(qi >= 1)
    def _():
        near_block(qi - 1, 1)

    near_block(qi, 0)

    acc = acc_ref[...]
    o = acc[0:DA_V_DIM] / acc[DA_V_DIM:DA_V_DIM + 1]
    o = o[:, 0:t] - lam_ref[0] * o[:, t:2 * t]
    ms = jnp.mean(o * o, axis=0, keepdims=True)
    o = o * lax.rsqrt(ms + LN_EPS) * g_ref[...] * post_scale
    o_ref[...] = o.T.astype(o_ref.dtype)


def _diff_attention(q_t, k, v_t, bias_near, bias_far, lam, subln_g, *, bsz, seq, t, post_scale):
    heads = q_t.shape[0] // DA_V_DIM
    nq = seq // t
    smem = pl.BlockSpec(memory_space=pltpu.SMEM)
    va = DA_V_DIM + BF16_ROWS
    return pl.pallas_call(
        functools.partial(_da_kernel, t=t, post_scale=post_scale),
        grid=(bsz, heads, nq),
        in_specs=[
            smem, smem,
            pl.BlockSpec((DA_V_DIM, t), lambda b, h, i: (h, b * nq + i)),
            pl.BlockSpec((None, seq, DA_V_DIM), lambda b, h, i: (b, 0, h)),
            pl.BlockSpec((DA_V_DIM, seq), lambda b, h, i: (h, b)),
            pl.BlockSpec((None, 2, t, t), lambda b, h, i: (h, 0, 0, 0)),
            pl.BlockSpec((DA_V_DIM, t), lambda b, h, i: (0, 0)),
        ],
        out_specs=pl.BlockSpec((None, t, DA_V_DIM), lambda b, h, i: (b, i, h)),
        out_shape=jax.ShapeDtypeStruct((bsz, seq, heads * DA_V_DIM), BF16),
        scratch_shapes=[
            pltpu.VMEM((DA_V_DIM, 2 * t), BF16),
            pltpu.VMEM((1, 2 * t), F32),
            pltpu.VMEM((va, 2 * t), F32),
        ],
        compiler_params=_params("parallel", "parallel", "arbitrary"),
        name="diff_attention",
    )(bias_far, lam, q_t, k, v_t, bias_near, subln_g)


def _dsa_kernel(far_ref, ixq_ref, ixw_ref, ixk_ref, saq_ref, sak_ref, sav_ref, bn_ref, o_ref,
                qi8_ref, keys_ref, hi_ref, qs4_ref, m_ref, acc_ref, *, t, topk):
    qi = pl.program_id(1)
    kf = float(topk)

    qi8_ref[...] = jnp.zeros(qi8_ref.shape, BF16)
    for h in range(IDX_HEADS):
        qi8_ref[0:IDX_DIM, h * t:(h + 1) * t] = ixq_ref[h * IDX_DIM:(h + 1) * IDX_DIM, :]

    def index_keys(kb):
        kblk = ixk_ref[pl.ds(pl.multiple_of(kb * t, t), t), :]
        sc = None
        for h in range(IDX_HEADS):
            r = _dot(kblk, qi8_ref[:, h * t:(h + 1) * t])
            term = ixw_ref[h:h + 1, :] * jnp.maximum(r, 0.0)
            sc = term if sc is None else sc + term
        return jnp.where(jnp.abs(sc) < TINY, 0.0, sc)

    def rows(kb):
        return pl.ds(pl.multiple_of(kb * t, t), t)

    def chunk(kb):
        return keys_ref[rows(kb), :]

    def store_keys(kb, sc):
        bits = lax.bitcast_convert_type(sc, I32)
        keys_ref[rows(kb), :] = bits ^ ((bits >> 31) & 0x7FFFFFFF)
        hi_ref[rows(kb), :] = lax.bitcast_convert_type(bits & HI_MASK, F32).astype(BF16)

    def fill(kb, carry):
        store_keys(kb, index_keys(kb))
        return carry

    lax.fori_loop(0, qi, fill, 0)
    krow = lax.broadcasted_iota(I32, (t, t), 0)
    qcol = lax.broadcasted_iota(I32, (t, t), 1)
    store_keys(qi, jnp.where(krow <= qcol, index_keys(qi), -jnp.inf))

    def count(fn):
        def body(kb, cnt):
            return cnt + _fold_rows(fn(kb, chunk(kb)), SUBLANES)
        cnt = lax.fori_loop(0, qi + 1, body, jnp.zeros((SUBLANES, t), I32))
        return jnp.sum(cnt.astype(F32), axis=0, keepdims=True)

    def count_hi(thr):
        def body(kb, cnt):
            hit = jnp.where(hi_ref[rows(kb), :] >= thr, jnp.ones((), BF16), jnp.zeros((), BF16))
            return cnt + _fold_rows(hit, BF16_ROWS).astype(F32)
        cnt = lax.fori_loop(0, qi + 1, body, jnp.zeros((BF16_ROWS, t), F32))
        return jnp.sum(cnt, axis=0, keepdims=True)

    def hi_bit(i, st):
        code, open_ = st
        cand = code + jnp.left_shift(jnp.int32(1), 15 - i)
        pattern = cand ^ ((cand >> 31) & 0x7FFF)
        pattern = jnp.where(cand > 0, jnp.maximum(pattern, BF16_MIN_NORMAL), pattern)
        thr = lax.bitcast_convert_type(jnp.left_shift(pattern, 16), F32).astype(BF16)
        c = count_hi(thr)
        code = jnp.where(open_ > 0.0, jnp.where(c >= kf, cand, code), code)
        return code, jnp.where(c == kf, 0.0, open_)

    code, open_ = lax.fori_loop(0, 16, hi_bit, (jnp.full((1, t), -2 ** 15, I32), jnp.ones((1, t), F32)))
    zero_tie = jnp.where(code == 0, open_, 0.0)
    open_ = jnp.where(code == 0, 0.0, open_)

    def search_cond(st):
        i, _, _, n_open = st
        return jnp.logical_and(i < 32, n_open > 0.0)

    def search_body(st):
        i, tau, open_, _ = st
        cand = tau + jnp.left_shift(jnp.int32(1), 31 - i)
        c = count(lambda kb, blk: jnp.where(blk >= cand, 1, 0))
        tau = jnp.where(open_ > 0.0, jnp.where(c >= kf, cand, tau), tau)
        open_ = jnp.where(c == kf, 0.0, open_)
        return i + 1, tau, open_, jnp.max(open_)

    _, tau, open_, _ = lax.while_loop(
        search_cond, search_body, (jnp.int32(16), jnp.left_shift(code, 16), open_, jnp.max(open_)))

    @pl.when(jnp.max(jnp.maximum(open_, zero_tie)) > 0.0)
    def _():
        need = kf - count(lambda kb, blk: jnp.where(blk > tau, 1, 0))
        below = jnp.where(qcol <= krow, 1.0, 0.0).astype(BF16)

        def retire(kb, seen):
            blk = chunk(kb)
            tied = jnp.where(blk == tau, 1.0, 0.0).astype(BF16)
            rank = _dot(below, tied) + seen
            keys_ref[rows(kb), :] = jnp.where(blk == tau, jnp.where(rank > need, INT_MIN, blk), blk)
            return rank[t - 1:t, :]

        lax.fori_loop(0, qi + 1, retire, jnp.zeros((1, t), F32))

    qs4_ref[...] = jnp.zeros(qs4_ref.shape, BF16)
    for h in range(SA_HEADS):
        qs4_ref[0:SA_HEAD_DIM, h * t:(h + 1) * t] = saq_ref[h * SA_HEAD_DIM:(h + 1) * SA_HEAD_DIM, :]
    m_ref[...] = jnp.full(m_ref.shape, NEG, F32)
    acc_ref[...] = jnp.zeros(acc_ref.shape, F32)

    def scores(kb, near=None):
        s = _dot(sak_ref[pl.ds(pl.multiple_of(kb * t, t), t), :], qs4_ref[...])
        sel = chunk(kb) >= tau
        parts = []
        for h in range(SA_HEADS):
            sh = s[:, h * t:(h + 1) * t]
            if near is not None:
                sh = sh + bn_ref[h, near]
            parts.append(jnp.where(sel, sh, NEG))
        return jnp.concatenate(parts, axis=1)

    far = jnp.concatenate([jnp.full((1, t), far_ref[h], F32) for h in range(SA_HEADS)], axis=1)
    def values(kb):
        return sav_ref[:, pl.ds(pl.multiple_of(kb * t, t), t)]

    _far_chunks(jnp.maximum(qi - 1, 0), scores, values, m_ref, acc_ref, far)

    @pl.when(qi >= 1)
    def _():
        _softmax_step(scores(qi - 1, 1), values(qi - 1), m_ref, acc_ref, None)

    _softmax_step(scores(qi, 0), values(qi), m_ref, acc_ref, None)

    acc = acc_ref[...]
    o = acc[0:SA_HEAD_DIM] / acc[SA_HEAD_DIM:SA_HEAD_DIM + 1]
    o = jnp.concatenate([o[:, h * t:(h + 1) * t] for h in range(SA_HEADS)], axis=0)
    o_ref[...] = o.T.astype(o_ref.dtype)


def _sparse_attention(ixq_t, ixw_t, ixk, saq_t, sak, sav_t, bias_near, bias_far, *, bsz, seq, t, topk):
    nq = seq // t
    smem = pl.BlockSpec(memory_space=pltpu.SMEM)
    qcols = lambda r: pl.BlockSpec((r, t), lambda b, i: (0, b * nq + i))
    full = lambda w: pl.BlockSpec((None, seq, w), lambda b, i: (b, 0, 0))
    va = SA_HEAD_DIM + BF16_ROWS
    return pl.pallas_call(
        functools.partial(_dsa_kernel, t=t, topk=topk),
        grid=(bsz, nq),
        in_specs=[
            smem,
            qcols(IDX_HEADS * IDX_DIM), qcols(IDX_HEADS), full(LANES),
            qcols(SA_WIDTH), full(LANES),
            pl.BlockSpec((SA_HEAD_DIM, seq), lambda b, i: (0, b)),
            _resident((SA_HEADS, 2, t, t)),
        ],
        out_specs=pl.BlockSpec((None, t, SA_WIDTH), lambda b, i: (b, i, 0)),
        out_shape=jax.ShapeDtypeStruct((bsz, seq, SA_WIDTH), BF16),
        scratch_shapes=[
            pltpu.VMEM((LANES, IDX_HEADS * t), BF16),
            pltpu.VMEM((seq, t), I32),
            pltpu.VMEM((seq, t), BF16),
            pltpu.VMEM((LANES, SA_HEADS * t), BF16),
            pltpu.VMEM((1, SA_HEADS * t), F32),
            pltpu.VMEM((va, SA_HEADS * t), F32),
        ],
        compiler_params=_params("parallel", "arbitrary"),
        name="sparse_attention",
    )(bias_far, ixq_t, ixw_t, ixk, saq_t, sak, sav_t, bias_near)


def _ssm_state_kernel(u_ref, w_ref, o_ref):
    o_ref[...] = _dot(u_ref[...].astype(BF16), w_ref[...])


def _ssm_scan_kernel(loc_ref, a1_ref, a2_ref, o_ref):
    a1 = a1_ref[...]
    a2 = a2_ref[...]

    def body(n, s):
        o_ref[n] = s
        return a1 * s + a2 * pltpu.roll(s, SSM_STATE, axis=1) + loc_ref[n]

    lax.fori_loop(0, loc_ref.shape[0], body, jnp.zeros(a1.shape, F32))


def _ssm_out_kernel(u_ref, s_ref, toep_ref, wout_ref, o_ref):
    o_ref[...] = (_dot(u_ref[...].astype(BF16), toep_ref[...])
                  + _dot(s_ref[...].astype(BF16), wout_ref[...]))


def _ssm_gate_kernel(y_ref, u_ref, d_ref, w_ref, o_ref):
    y = jax.nn.gelu(y_ref[...] + d_ref[...] * u_ref[...])
    o_ref[...] = (y * jax.nn.sigmoid(_dot(y.astype(BF16), w_ref[...]))).astype(o_ref.dtype)


def _ssm_tables(lam_re, lam_im, log_dt, b_re, b_im, c_re, c_im, tc):
    hp = lax.Precision.HIGHEST
    dt = jnp.exp(log_dt)[:, None]
    n = jnp.arange(tc + 1, dtype=F32)[:, None, None]
    mag = jnp.exp(lam_re * dt * n)
    pw_re = mag * jnp.cos(lam_im * dt * n)
    pw_im = mag * jnp.sin(lam_im * dt * n)
    den = lam_re * lam_re + lam_im * lam_im
    nr, ni = pw_re[1] - 1.0, pw_im[1]
    f_re = (nr * lam_re + ni * lam_im) / den
    f_im = (ni * lam_re - nr * lam_im) / den
    bb_re = f_re[..., None] * b_re - f_im[..., None] * b_im
    bb_im = f_re[..., None] * b_im + f_im[..., None] * b_re
    ca_re = c_re[None] * pw_re[:, :, None, :] - c_im[None] * pw_im[:, :, None, :]
    ca_im = c_re[None] * pw_im[:, :, None, :] + c_im[None] * pw_re[:, :, None, :]
    groups = lam_re.shape[0]
    inv_mag = jnp.exp(-lam_re * dt * n[:tc])
    iw_re = inv_mag * jnp.cos(lam_im * dt * n[:tc])
    iw_im = -inv_mag * jnp.sin(lam_im * dt * n[:tc])
    l_re = iw_re[..., None] * bb_re[None] - iw_im[..., None] * bb_im[None]
    l_im = iw_re[..., None] * bb_im[None] + iw_im[..., None] * bb_re[None]
    left = jnp.concatenate([l_re, l_im], axis=2).transpose(1, 0, 3, 2)
    right = jnp.concatenate([ca_re[:tc], -ca_im[:tc]], axis=3).transpose(1, 0, 2, 3)
    cw = tc * SSM_GROUP
    toep = jnp.einsum('gik,gjk->gij', left.reshape(groups, cw, 2 * SSM_STATE),
                      right.reshape(groups, cw, 2 * SSM_STATE), precision=hp)
    step = jnp.arange(cw) // SSM_GROUP
    toep = jnp.where(step[:, None] <= step[None, :], toep, 0.0)
    rev_re, rev_im = pw_re[tc - 1::-1][:tc], pw_im[tc - 1::-1][:tc]
    ws_re = rev_re[..., None] * bb_re[None] - rev_im[..., None] * bb_im[None]
    ws_im = rev_re[..., None] * bb_im[None] + rev_im[..., None] * bb_re[None]
    wstate = jnp.concatenate([ws_re, ws_im], axis=2)
    wstate = wstate.transpose(1, 0, 3, 2).reshape(groups, tc * SSM_GROUP, 2 * SSM_STATE)
    wout = jnp.concatenate([ca_re[1:], -ca_im[1:]], axis=3)
    wout = wout.transpose(1, 3, 0, 2).reshape(groups, 2 * SSM_STATE, tc * SSM_GROUP)
    dec_re, dec_im = pw_re[tc], pw_im[tc]
    a1 = jnp.concatenate([dec_re, dec_re], axis=1)
    a2 = jnp.concatenate([-dec_im, dec_im], axis=1)
    return toep.astype(BF16), wstate.astype(BF16), wout.astype(BF16), a1, a2


def _ssm(u, tables, d_skip, w_glu, *, bsz, seq, tm):
    toep, wstate, wout, a1, a2 = tables
    groups = toep.shape[0]
    tc = SSM_CHUNK
    nc = seq // tc
    rows = bsz * nc
    cw = tc * SSM_GROUP
    ug = u.reshape(bsz, nc, tc, groups, SSM_GROUP).transpose(3, 0, 1, 2, 4).reshape(groups, rows, cw)
    gspec = lambda r, c: pl.BlockSpec((None, r, c), lambda g: (g, 0, 0))
    loc = pl.pallas_call(
        _ssm_state_kernel,
        grid=(groups,),
        in_specs=[gspec(rows, cw), gspec(cw, 2 * SSM_STATE)],
        out_specs=gspec(rows, 2 * SSM_STATE),
        out_shape=jax.ShapeDtypeStruct((groups, rows, 2 * SSM_STATE), F32),
        compiler_params=_params("parallel"),
        name="ssm_chunk_state",
    )(ug, wstate)
    gb = groups * bsz
    loc_t = loc.reshape(groups, bsz, nc, 2 * SSM_STATE).transpose(2, 0, 1, 3).reshape(nc, gb, 2 * SSM_STATE)
    a1r = jnp.repeat(a1, bsz, axis=0)
    a2r = jnp.repeat(a2, bsz, axis=0)
    rb = gb
    prev = pl.pallas_call(
        _ssm_scan_kernel,
        grid=(gb // rb,),
        in_specs=[pl.BlockSpec((nc, rb, 2 * SSM_STATE), lambda i: (0, i, 0)),
                  pl.BlockSpec((rb, 2 * SSM_STATE), lambda i: (i, 0)),
                  pl.BlockSpec((rb, 2 * SSM_STATE), lambda i: (i, 0))],
        out_specs=pl.BlockSpec((nc, rb, 2 * SSM_STATE), lambda i: (0, i, 0)),
        out_shape=jax.ShapeDtypeStruct((nc, gb, 2 * SSM_STATE), F32),
        compiler_params=_params("parallel"),
        name="ssm_chunk_scan",
    )(loc_t, a1r, a2r)
    prev_g = prev.reshape(nc, groups, bsz, 2 * SSM_STATE).transpose(1, 2, 0, 3).reshape(groups, rows, 2 * SSM_STATE)
    y = pl.pallas_call(
        _ssm_out_kernel,
        grid=(groups,),
        in_specs=[gspec(rows, cw), gspec(rows, 2 * SSM_STATE), gspec(cw, cw), gspec(2 * SSM_STATE, cw)],
        out_specs=gspec(rows, cw),
        out_shape=jax.ShapeDtypeStruct((groups, rows, cw), F32),
        compiler_params=_params("parallel"),
        name="ssm_chunk_out",
    )(ug, prev_g, toep, wout)
    y = y.reshape(groups, bsz, nc, tc, SSM_GROUP).transpose(1, 2, 3, 0, 4).reshape(bsz * seq, groups * SSM_GROUP)
    width = groups * SSM_GROUP
    return pl.pallas_call(
        _ssm_gate_kernel,
        grid=(bsz * seq // tm,),
        in_specs=[pl.BlockSpec((tm, width), lambda i: (i, 0)),
                  pl.BlockSpec((tm, width), lambda i: (i, 0)),
                  _resident((1, width)), _resident((width, width))],
        out_specs=pl.BlockSpec((tm, width), lambda i: (i, 0)),
        out_shape=jax.ShapeDtypeStruct((bsz * seq, width), BF16),
        compiler_params=_params("parallel"),
        name="ssm_gate",
    )(y, u, d_skip, w_glu)


def _t5_bucket(n):
    max_exact = REL_BUCKETS // 2
    nf = jnp.maximum(n, 1).astype(F32)
    large = max_exact + (jnp.log(nf / max_exact) / math.log(REL_MAX_DIST / max_exact)
                         * (REL_BUCKETS - max_exact)).astype(I32)
    large = jnp.minimum(large, REL_BUCKETS - 1)
    return jnp.where(n < max_exact, n, large)


def _bias_tiles(table, t):
    assert t >= REL_MAX_DIST
    table = table.astype(F32) * LOG2E
    key = jnp.arange(t)[:, None]
    query = jnp.arange(t)[None, :]
    dist = jnp.stack([query - key, t + query - key])
    bucket = _t5_bucket(jnp.maximum(dist, 0))
    tiles = jnp.zeros((table.shape[1],) + dist.shape, F32)
    for b in range(REL_BUCKETS):
        tiles = jnp.where(bucket[None] == b, table[b][:, None, None, None], tiles)
    return jnp.where(dist[None] >= 0, tiles, NEG), table[REL_BUCKETS - 1]


def _split_w_in(w_in):
    offs = [0]
    for s in IN_SIZES:
        offs.append(offs[-1] + s)
    da_q, da_k, da_v, ssm_u, sa_q, sa_k, sa_v, ix_q, ix_k, ix_w = (
        w_in[:, offs[j]:offs[j + 1]] for j in range(len(IN_SIZES)))
    w_scale = IDX_HEADS ** -0.5 * IDX_DIM ** -0.5
    pad = lambda w: jnp.pad(w, ((0, 0), (0, LANES - w.shape[1])))
    weights = [(da_q * (DA_QK_DIM ** -0.5 * LOG2E)).T, da_k, da_v.T, ssm_u,
               (sa_q * (SA_HEAD_DIM ** -0.5 * LOG2E)).T, pad(sa_k), sa_v.T,
               ix_q.T, pad(ix_k), (ix_w * w_scale).T]
    dtypes = [BF16, BF16, BF16, F32, BF16, BF16, BF16, BF16, BF16, F32]
    transposed = [True, False, True, False, True, False, True, True, False, True]
    return [w.astype(BF16) for w in weights], dtypes, transposed


def _plan(bsz, seq):
    rows = bsz * seq
    tm = 512 if rows % 512 == 0 else rows
    t_da = 1024 if seq % 1024 == 0 and seq >= 4096 else 256 if seq % 256 == 0 else 128
    t_sa = 512 if seq % 512 == 0 and seq >= 2048 else 256 if seq % 256 == 0 else 128
    return dict(tm=tm, t_da=t_da, t_sa=t_sa)


def kernel(x, p, rel_bias, ffn1_w_gate, ffn1_w_up, ffn1_w_down, ln1_g, ln1_b, w_in, w_o, da_lam_q1, da_lam_k1, da_lam_q2, da_lam_k2, da_subln_g, ssm_lam_re, ssm_lam_im, ssm_log_dt, ssm_b_re, ssm_b_im, ssm_c_re, ssm_c_im, ssm_d, ssm_w_glu, ln2_g, ln2_b, ffn2_w_gate, ffn2_w_up, ffn2_w_down, ple_w_proj, ple_w_gate, ln3_g, ln3_b):
    bsz, seq, d = x.shape
    rows = bsz * seq
    plan = _plan(bsz, seq)
    tm, t_da, t_sa = plan["tm"], plan["t_da"], plan["t_sa"]
    topk = min(TOPK_MAX, seq // 4)
    da_near, da_far = _bias_tiles(rel_bias[:, :DA_HEADS], t_da)
    sa_near, sa_far = _bias_tiles(rel_bias[:, DA_HEADS:], t_sa)
    vec = lambda a: a.reshape(1, -1).astype(F32)
    da_w = DA_HEADS * DA_V_DIM

    h = x.reshape(rows, d)
    for i in range(DEPTH):
        lam_init = 0.8 - 0.6 * math.exp(-0.3 * i)
        weights, dtypes, transposed = _split_w_in(w_in[i])
        h, (da_q, da_k, da_v, ssm_u, sa_q, sa_k, sa_v, ix_q, ix_k, ix_w) = _ffn_inproj(
            h, ffn1_w_gate[i].astype(BF16), ffn1_w_up[i].astype(BF16), ffn1_w_down[i].astype(BF16),
            vec(ln1_g[i]), vec(ln1_b[i]), weights, dtypes, transposed, tm=tm)
        b3 = lambda a: a.reshape(bsz, seq, a.shape[-1])

        lam = (jnp.exp(jnp.sum(da_lam_q1[i].astype(F32) * da_lam_k1[i]))
               - jnp.exp(jnp.sum(da_lam_q2[i].astype(F32) * da_lam_k2[i])) + lam_init)
        subln = jnp.broadcast_to(da_subln_g[i].astype(F32)[:, None], (DA_V_DIM, t_da))
        o_da = _diff_attention(da_q, b3(da_k), da_v, da_near, da_far,
                               lam.reshape(1).astype(F32), subln, bsz=bsz, seq=seq, t=t_da,
                               post_scale=1.0 - lam_init)

        tables = _ssm_tables(ssm_lam_re[i].astype(F32), ssm_lam_im[i].astype(F32), ssm_log_dt[i].astype(F32),
                             ssm_b_re[i].astype(F32), ssm_b_im[i].astype(F32),
                             ssm_c_re[i].astype(F32), ssm_c_im[i].astype(F32), SSM_CHUNK)
        o_ssm = _ssm(ssm_u, tables, vec(ssm_d[i]), ssm_w_glu[i].astype(BF16), bsz=bsz, seq=seq, tm=tm)

        o_sa = _sparse_attention(ix_q, ix_w, b3(ix_k), sa_q, b3(sa_k), sa_v, sa_near, sa_far,
                                 bsz=bsz, seq=seq, t=t_sa, topk=topk)

        wo = w_o[i].astype(BF16)
        h = _mix_ffn(h, o_da.reshape(rows, da_w), o_ssm, o_sa.reshape(rows, SA_WIDTH), p[i].reshape(rows, -1),
                     [wo[:da_w], wo[da_w:da_w + SSM_WIDTH], wo[da_w + SSM_WIDTH:], vec(ln2_g[i]), vec(ln2_b[i]),
                      ffn2_w_gate[i].astype(BF16), ffn2_w_up[i].astype(BF16), ffn2_w_down[i].astype(BF16),
                      ple_w_proj[i].astype(BF16), ple_w_gate[i].astype(BF16), vec(ln3_g[i]), vec(ln3_b[i])],
                     tm=tm)
    return h.reshape(bsz, seq, d)
```

```python
import functools
import math

import jax
import jax.numpy as jnp
from jax import lax
from jax.experimental import pallas as pl
from jax.experimental.pallas import tpu as pltpu

F32 = jnp.float32
BF16 = jnp.bfloat16
I32 = jnp.int32

DEPTH = 2
DA_QK_DIM = 64
DA_V_DIM = 2 * DA_QK_DIM
DA_HEADS = 4
SSM_GROUP = 16
SSM_GROUPS = 16
SSM_STATE = 64
SSM_WIDTH = SSM_GROUP * SSM_GROUPS
SA_HEAD_DIM = 64
SA_HEADS = 4
SA_WIDTH = SA_HEADS * SA_HEAD_DIM
IDX_HEADS = 8
IDX_DIM = 32
TOPK_MAX = 256
REL_BUCKETS = 32
REL_MAX_DIST = 128
ALPHA = (2 * DEPTH) ** 0.25
LN_EPS = 1e-5
IN_SIZES = (DA_HEADS * 2 * DA_QK_DIM, DA_HEADS * 2 * DA_QK_DIM, DA_HEADS * DA_V_DIM,
            SSM_WIDTH, SA_WIDTH, SA_HEAD_DIM, SA_HEAD_DIM,
            IDX_HEADS * IDX_DIM, IDX_DIM, IDX_HEADS)

LANES = 128
SUBLANES = 8
BF16_ROWS = 16
NEG = -1e30
INT_MIN = -2 ** 31
HI_MASK = -2 ** 16
TINY = 2.0 ** -126
BF16_MIN_NORMAL = 0x0080
LOG2E = math.log2(math.e)
VMEM_LIMIT = 56 * 1024 * 1024
SSM_CHUNK = 64

def _params(*sem):
    return pltpu.CompilerParams(dimension_semantics=sem, vmem_limit_bytes=VMEM_LIMIT)


def _resident(shape):
    return pl.BlockSpec(shape, lambda *_: (0,) * len(shape), pipeline_mode=pl.Buffered(1))


def _dot(a, b):
    return jnp.dot(a, b, preferred_element_type=F32)


def _dot_nt(a, b):
    return lax.dot_general(a, b, (((1,), (1,)), ((), ())), preferred_element_type=F32)


def _layer_norm(y, g, b):
    mu = jnp.mean(y, axis=-1, keepdims=True)
    d = y - mu
    var = jnp.mean(d * d, axis=-1, keepdims=True)
    return d * lax.rsqrt(var + LN_EPS) * g + b


def _fold_rows(x, rows):
    while x.shape[0] > rows:
        half = x.shape[0] // 2
        x = x[:half] + x[half:]
    return x


FF_CHUNK = 256


def _swiglu(xb, wg, wu, wd):
    d_ff = wg.shape[1]
    acc = None
    for c0 in range(0, d_ff, FF_CHUNK):
        c1 = min(c0 + FF_CHUNK, d_ff)
        gate = _dot(xb, wg[:, c0:c1])
        up = _dot(xb, wu[:, c0:c1])
        hid = (gate * jax.nn.sigmoid(gate) * up).astype(BF16)
        part = _dot(hid, wd[c0:c1, :])
        acc = part if acc is None else acc + part
    return acc


def _ffn_inproj_kernel(x_ref, wg, wu, wd, g_ref, b_ref, *refs, transposed):
    n = len(transposed)
    w_refs, x1_ref, o_refs = refs[:n], refs[n], refs[n + 1:]
    x = x_ref[...]
    x1 = _layer_norm(ALPHA * x + 0.5 * _swiglu(x.astype(BF16), wg, wu, wd), g_ref[...], b_ref[...])
    x1_ref[...] = x1
    xb = x1.astype(BF16)
    for w_ref, o_ref, tr in zip(w_refs, o_refs, transposed):
        out = _dot_nt(w_ref[...], xb) if tr else _dot(xb, w_ref[...])
        o_ref[...] = out.astype(o_ref.dtype)


def _ffn_inproj(x, wg, wu, wd, g, b, weights, dtypes, transposed, *, tm):
    t_rows, d = x.shape
    row = lambda w: pl.BlockSpec((tm, w), lambda i: (i, 0))
    out_specs, out_shape = [row(d)], [jax.ShapeDtypeStruct((t_rows, d), F32)]
    for w, dt, tr in zip(weights, dtypes, transposed):
        if tr:
            out_specs.append(pl.BlockSpec((w.shape[0], tm), lambda i: (0, i)))
            out_shape.append(jax.ShapeDtypeStruct((w.shape[0], t_rows), dt))
        else:
            out_specs.append(row(w.shape[1]))
            out_shape.append(jax.ShapeDtypeStruct((t_rows, w.shape[1]), dt))
    consts = [wg, wu, wd, g, b] + list(weights)
    outs = pl.pallas_call(
        functools.partial(_ffn_inproj_kernel, transposed=tuple(transposed)),
        grid=(t_rows // tm,),
        in_specs=[row(d)] + [_resident(c.shape) for c in consts],
        out_specs=out_specs,
        out_shape=out_shape,
        compiler_params=_params("parallel"),
        name="ffn_ln_in_proj",
    )(x, *consts)
    return outs[0], outs[1:]


def _mix_ffn_kernel(x_ref, da_ref, ssm_ref, sa_ref, p_ref, w1, w2, w3, g2_ref, b2_ref,
                    wg, wu, wd, wpp, wpg, g3_ref, b3_ref, o_ref):
    mix = _dot(da_ref[...], w1[...]) + _dot(ssm_ref[...], w2[...]) + _dot(sa_ref[...], w3[...])
    x = _layer_norm(ALPHA * x_ref[...] + mix, g2_ref[...], b2_ref[...])
    xb = x.astype(BF16)
    y = ALPHA * x + 0.5 * _swiglu(xb, wg, wu, wd)
    y = y + _dot(p_ref[...].astype(BF16), wpp[...]) * jax.nn.sigmoid(_dot(xb, wpg[...]))
    o_ref[...] = _layer_norm(y, g3_ref[...], b3_ref[...])


def _mix_ffn(x, o_da, o_ssm, o_sa, p, consts, *, tm):
    t_rows, d = x.shape
    row = lambda a: pl.BlockSpec((tm, a.shape[1]), lambda i: (i, 0))
    acts = [x, o_da, o_ssm, o_sa, p]
    return pl.pallas_call(
        _mix_ffn_kernel,
        grid=(t_rows // tm,),
        in_specs=[row(a) for a in acts] + [_resident(c.shape) for c in consts],
        out_specs=pl.BlockSpec((tm, d), lambda i: (i, 0)),
        out_shape=jax.ShapeDtypeStruct((t_rows, d), F32),
        compiler_params=_params("parallel"),
        name="out_proj_ln_ffn_ple_ln",
    )(*acts, *consts)


def _softmax_probs(s, m_ref, shift):
    m_prev = m_ref[...]
    m_cur = jnp.max(s, axis=0, keepdims=True)
    if shift is not None:
        m_cur = m_cur + shift
    m_next = jnp.maximum(m_prev, m_cur)
    m_ref[...] = m_next
    p = jnp.exp2(s - (m_next if shift is None else m_next - shift))
    return p.astype(BF16), jnp.exp2(m_prev - m_next)


def _accumulate(acc_ref, rescale, v_t, p):
    v_aug = jnp.concatenate([v_t, jnp.ones((BF16_ROWS, v_t.shape[1]), v_t.dtype)], axis=0)
    acc_ref[...] = rescale * acc_ref[...] + _dot(v_aug, p)


def _softmax_step(s, v_t, m_ref, acc_ref, shift):
    p, rescale = _softmax_probs(s, m_ref, shift)
    _accumulate(acc_ref, rescale, v_t, p)


def _far_chunks(n, scores, values, m_ref, acc_ref, shift):
    def body(i, carry):
        _softmax_step(scores(i), values(i), m_ref, acc_ref, shift)
        return carry

    lax.fori_loop(0, n, body, 0)


def _da_kernel(far_ref, lam_ref, q_ref, k_ref, v_ref, bn_ref, g_ref, o_ref,
               qd_ref, m_ref, acc_ref, *, t, post_scale):
    h = pl.program_id(1)
    qi = pl.program_id(2)
    q = q_ref[...].astype(F32)
    rowq = lax.broadcasted_iota(I32, q.shape, 0)
    qd_ref[:, 0:t] = jnp.where(rowq < DA_QK_DIM, q, 0.0).astype(BF16)
    qd_ref[:, t:2 * t] = jnp.where(rowq >= DA_QK_DIM, q, 0.0).astype(BF16)
    m_ref[...] = jnp.full(m_ref.shape, NEG, F32)
    acc_ref[...] = jnp.zeros(acc_ref.shape, F32)
    far = far_ref[h]

    def scores(kb):
        return _dot(k_ref[pl.ds(pl.multiple_of(kb * t, t), t), :], qd_ref[...])

    def values(kb):
        return v_ref[:, pl.ds(pl.multiple_of(kb * t, t), t)]

    def near_block(kb, near):
        bias = bn_ref[near]
        _softmax_step(scores(kb) + jnp.concatenate([bias, bias], axis=1), values(kb), m_ref, acc_ref, None)

    _far_chunks(jnp.maximum(qi - 1, 0), scores, values, m_ref, acc_ref, far)

    @pl.when(qi >= 1)
    def _():
        near_block(qi - 1, 1)

    near_block(qi, 0)

    acc = acc_ref[...]
    o = acc[0:DA_V_DIM] / acc[DA_V_DIM:DA_V_DIM + 1]
    o = o[:, 0:t] - lam_ref[0] * o[:, t:2 * t]
    ms = jnp.mean(o * o, axis=0, keepdims=True)
    o = o * lax.rsqrt(ms + LN_EPS) * g_ref[...] * post_scale
    o_ref[...] = o.T.astype(o_ref.dtype)


def _diff_attention(q_t, k, v_t, bias_near, bias_far, lam, subln_g, *, bsz, seq, t, post_scale):
    heads = q_t.shape[0] // DA_V_DIM
    nq = seq // t
    smem = pl.BlockSpec(memory_space=pltpu.SMEM)
    va = DA_V_DIM + BF16_ROWS
    return pl.pallas_call(
        functools.partial(_da_kernel, t=t, post_scale=post_scale),
        grid=(bsz, heads, nq),
        in_specs=[
            smem, smem,
            pl.BlockSpec((DA_V_DIM, t), lambda b, h, i: (h, b * nq + i)),
            pl.BlockSpec((None, seq, DA_V_DIM), lambda b, h, i: (b, 0, h)),
            pl.BlockSpec((DA_V_DIM, seq), lambda b, h, i: (h, b)),
            pl.BlockSpec((None, 2, t, t), lambda b, h, i: (h, 0, 0, 0)),
            pl.BlockSpec((DA_V_DIM, t), lambda b, h, i: (0, 0)),
        ],
        out_specs=pl.BlockSpec((None, t, DA_V_DIM), lambda b, h, i: (b, i, h)),
        out_shape=jax.ShapeDtypeStruct((bsz, seq, heads * DA_V_DIM), BF16),
        scratch_shapes=[
            pltpu.VMEM((DA_V_DIM, 2 * t), BF16),
            pltpu.VMEM((1, 2 * t), F32),
            pltpu.VMEM((va, 2 * t), F32),
        ],
        compiler_params=_params("parallel", "parallel", "arbitrary"),
        name="diff_attention",
    )(bias_far, lam, q_t, k, v_t, bias_near, subln_g)


def _dsa_kernel(far_ref, ixq_ref, ixw_ref, ixk_ref, saq_ref, sak_ref, sav_ref, bn_ref, o_ref,
                qi8_ref, keys_ref, hi_ref, qs4_ref, m_ref, acc_ref, *, t, topk):
    qi = pl.program_id(1)
    kf = float(topk)

    qi8_ref[...] = jnp.zeros(qi8_ref.shape, BF16)
    for h in range(IDX_HEADS):
        qi8_ref[0:IDX_DIM, h * t:(h + 1) * t] = ixq_ref[h * IDX_DIM:(h + 1) * IDX_DIM, :]

    def index_keys(kb):
        kblk = ixk_ref[pl.ds(pl.multiple_of(kb * t, t), t), :]
        sc = None
        for h in range(IDX_HEADS):
            r = _dot(kblk, qi8_ref[:, h * t:(h + 1) * t])
            term = ixw_ref[h:h + 1, :] * jnp.maximum(r, 0.0)
            sc = term if sc is None else sc + term
        return jnp.where(jnp.abs(sc) < TINY, 0.0, sc)

    def rows(kb):
        return pl.ds(pl.multiple_of(kb * t, t), t)

    def chunk(kb):
        return keys_ref[rows(kb), :]

    def store_keys(kb, sc):
        bits = lax.bitcast_convert_type(sc, I32)
        keys_ref[rows(kb), :] = bits ^ ((bits >> 31) & 0x7FFFFFFF)
        hi_ref[rows(kb), :] = lax.bitcast_convert_type(bits & HI_MASK, F32).astype(BF16)

    def fill(kb, carry):
        store_keys(kb, index_keys(kb))
        return carry

    lax.fori_loop(0, qi, fill, 0)
    krow = lax.broadcasted_iota(I32, (t, t), 0)
    qcol = lax.broadcasted_iota(I32, (t, t), 1)
    store_keys(qi, jnp.where(krow <= qcol, index_keys(qi), -jnp.inf))

    def count(fn):
        def body(kb, cnt):
            return cnt + _fold_rows(fn(kb, chunk(kb)), SUBLANES)
        cnt = lax.fori_loop(0, qi + 1, body, jnp.zeros((SUBLANES, t), I32))
        return jnp.sum(cnt.astype(F32), axis=0, keepdims=True)

    def count_hi(thr):
        def body(kb, cnt):
            hit = jnp.where(hi_ref[rows(kb), :] >= thr, jnp.ones((), BF16), jnp.zeros((), BF16))
            return cnt + _fold_rows(hit, BF16_ROWS).astype(F32)
        cnt = lax.fori_loop(0, qi + 1, body, jnp.zeros((BF16_ROWS, t), F32))
        return jnp.sum(cnt, axis=0, keepdims=True)

    def hi_bit(i, st):
        code, open_ = st
        cand = code + jnp.left_shift(jnp.int32(1), 15 - i)
        pattern = cand ^ ((cand >> 31) & 0x7FFF)
        pattern = jnp.where(cand > 0, jnp.maximum(pattern, BF16_MIN_NORMAL), pattern)
        thr = lax.bitcast_convert_type(jnp.left_shift(pattern, 16), F32).astype(BF16)
        c = count_hi(thr)
        code = jnp.where(open_ > 0.0, jnp.where(c >= kf, cand, code), code)
        return code, jnp.where(c == kf, 0.0, open_)

    code, open_ = lax.fori_loop(0, 16, hi_bit, (jnp.full((1, t), -2 ** 15, I32), jnp.ones((1, t), F32)))
    zero_tie = jnp.where(code == 0, open_, 0.0)
    open_ = jnp.where(code == 0, 0.0, open_)

    def search_cond(st):
        i, _, _, n_open = st
        return jnp.logical_and(i < 32, n_open > 0.0)

    def search_body(st):
        i, tau, open_, _ = st
        cand = tau + jnp.left_shift(jnp.int32(1), 31 - i)
        c = count(lambda kb, blk: jnp.where(blk >= cand, 1, 0))
        tau = jnp.where(open_ > 0.0, jnp.where(c >= kf, cand, tau), tau)
        open_ = jnp.where(c == kf, 0.0, open_)
        return i + 1, tau, open_, jnp.max(open_)

    _, tau, open_, _ = lax.while_loop(
        search_cond, search_body, (jnp.int32(16), jnp.left_shift(code, 16), open_, jnp.max(open_)))

    @pl.when(jnp.max(jnp.maximum(open_, zero_tie)) > 0.0)
    def _():
        need = kf - count(lambda kb, blk: jnp.where(blk > tau, 1, 0))
        below = jnp.where(qcol <= krow, 1.0, 0.0).astype(BF16)

        def retire(kb, seen):
            blk = chunk(kb)
            tied = jnp.where(blk == tau, 1.0, 0.0).astype(BF16)
            rank = _dot(below, tied) + seen
            keys_ref[rows(kb), :] = jnp.where(blk == tau, jnp.where(rank > need, INT_MIN, blk), blk)
            return rank[t - 1:t, :]

        lax.fori_loop(0, qi + 1, retire, jnp.zeros((1, t), F32))

    qs4_ref[...] = jnp.zeros(qs4_ref.shape, BF16)
    for h in range(SA_HEADS):
        qs4_ref[0:SA_HEAD_DIM, h * t:(h + 1) * t] = saq_ref[h * SA_HEAD_DIM:(h + 1) * SA_HEAD_DIM, :]
    m_ref[...] = jnp.full(m_ref.shape, NEG, F32)
    acc_ref[...] = jnp.zeros(acc_ref.shape, F32)

    def scores(kb, near=None):
        s = _dot(sak_ref[pl.ds(pl.multiple_of(kb * t, t), t), :], qs4_ref[...])
        sel = chunk(kb) >= tau
        parts = []
        for h in range(SA_HEADS):
            sh = s[:, h * t:(h + 1) * t]
            if near is not None:
                sh = sh + bn_ref[h, near]
            parts.append(jnp.where(sel, sh, NEG))
        return jnp.concatenate(parts, axis=1)

    far = jnp.concatenate([jnp.full((1, t), far_ref[h], F32) for h in range(SA_HEADS)], axis=1)
    def values(kb):
        return sav_ref[:, pl.ds(pl.multiple_of(kb * t, t), t)]

    _far_chunks(jnp.maximum(qi - 1, 0), scores, values, m_ref, acc_ref, far)

    @pl.when(qi >= 1)
    def _():
        _softmax_step(scores(qi - 1, 1), values(qi - 1), m_ref, acc_ref, None)

    _softmax_step(scores(qi, 0), values(qi), m_ref, acc_ref, None)

    acc = acc_ref[...]
    o = acc[0:SA_HEAD_DIM] / acc[SA_HEAD_DIM:SA_HEAD_DIM + 1]
    o = jnp.concatenate([o[:, h * t:(h + 1) * t] for h in range(SA_HEADS)], axis=0)
    o_ref[...] = o.T.astype(o_ref.dtype)


def _sparse_attention(ixq_t, ixw_t, ixk, saq_t, sak, sav_t, bias_near, bias_far, *, bsz, seq, t, topk):
    nq = seq // t
    smem = pl.BlockSpec(memory_space=pltpu.SMEM)
    qcols = lambda r: pl.BlockSpec((r, t), lambda b, i: (0, b * nq + i))
    full = lambda w: pl.BlockSpec((None, seq, w), lambda b, i: (b, 0, 0))
    va = SA_HEAD_DIM + BF16_ROWS
    return pl.pallas_call(
        functools.partial(_dsa_kernel, t=t, topk=topk),
        grid=(bsz, nq),
        in_specs=[
            smem,
            qcols(IDX_HEADS * IDX_DIM), qcols(IDX_HEADS), full(LANES),
            qcols(SA_WIDTH), full(LANES),
            pl.BlockSpec((SA_HEAD_DIM, seq), lambda b, i: (0, b)),
            _resident((SA_HEADS, 2, t, t)),
        ],
        out_specs=pl.BlockSpec((None, t, SA_WIDTH), lambda b, i: (b, i, 0)),
        out_shape=jax.ShapeDtypeStruct((bsz, seq, SA_WIDTH), BF16),
        scratch_shapes=[
            pltpu.VMEM((LANES, IDX_HEADS * t), BF16),
            pltpu.VMEM((seq, t), I32),
            pltpu.VMEM((seq, t), BF16),
            pltpu.VMEM((LANES, SA_HEADS * t), BF16),
            pltpu.VMEM((1, SA_HEADS * t), F32),
            pltpu.VMEM((va, SA_HEADS * t), F32),
        ],
        compiler_params=_params("parallel", "arbitrary"),
        name="sparse_attention",
    )(bias_far, ixq_t, ixw_t, ixk, saq_t, sak, sav_t, bias_near)


def _ssm_state_kernel(u_ref, w_ref, o_ref):
    o_ref[...] = _dot(u_ref[...], w_ref[...])


def _ssm_scan_kernel(loc_ref, a1_ref, a2_ref, o_ref):
    a1 = a1_ref[...]
    a2 = a2_ref[...]

    def body(n, s):
        o_ref[n] = s
        return a1 * s + a2 * pltpu.roll(s, SSM_STATE, axis=1) + loc_ref[n]

    lax.fori_loop(0, loc_ref.shape[0], body, jnp.zeros(a1.shape, F32))


def _ssm_out_kernel(u_ref, s_ref, toep_ref, wout_ref, o_ref):
    o_ref[...] = (_dot(u_ref[...], toep_ref[...])
                  + _dot(s_ref[...].astype(BF16), wout_ref[...]))


def _ssm_gate_kernel(y_ref, u_ref, d_ref, w_ref, o_ref):
    y = jax.nn.gelu(y_ref[...] + d_ref[...] * u_ref[...])
    o_ref[...] = (y * jax.nn.sigmoid(_dot(y.astype(BF16), w_ref[...]))).astype(o_ref.dtype)


def _ssm_tables(lam_re, lam_im, log_dt, b_re, b_im, c_re, c_im, tc):
    hp = lax.Precision.HIGHEST
    dt = jnp.exp(log_dt)[:, None]
    n = jnp.arange(tc + 1, dtype=F32)[:, None, None]
    mag = jnp.exp(lam_re * dt * n)
    pw_re = mag * jnp.cos(lam_im * dt * n)
    pw_im = mag * jnp.sin(lam_im * dt * n)
    den = lam_re * lam_re + lam_im * lam_im
    nr, ni = pw_re[1] - 1.0, pw_im[1]
    f_re = (nr * lam_re + ni * lam_im) / den
    f_im = (ni * lam_re - nr * lam_im) / den
    bb_re = f_re[..., None] * b_re - f_im[..., None] * b_im
    bb_im = f_re[..., None] * b_im + f_im[..., None] * b_re
    ca_re = c_re[None] * pw_re[:, :, None, :] - c_im[None] * pw_im[:, :, None, :]
    ca_im = c_re[None] * pw_im[:, :, None, :] + c_im[None] * pw_re[:, :, None, :]
    groups = lam_re.shape[0]
    inv_mag = jnp.exp(-lam_re * dt * n[:tc])
    iw_re = inv_mag * jnp.cos(lam_im * dt * n[:tc])
    iw_im = -inv_mag * jnp.sin(lam_im * dt * n[:tc])
    l_re = iw_re[..., None] * bb_re[None] - iw_im[..., None] * bb_im[None]
    l_im = iw_re[..., None] * bb_im[None] + iw_im[..., None] * bb_re[None]
    left = jnp.concatenate([l_re, l_im], axis=2).transpose(1, 0, 3, 2)
    right = jnp.concatenate([ca_re[:tc], -ca_im[:tc]], axis=3).transpose(1, 0, 2, 3)
    cw = tc * SSM_GROUP
    toep = jnp.einsum('gik,gjk->gij', left.reshape(groups, cw, 2 * SSM_STATE),
                      right.reshape(groups, cw, 2 * SSM_STATE), precision=hp)
    step = jnp.arange(cw) // SSM_GROUP
    toep = jnp.where(step[:, None] <= step[None, :], toep, 0.0)
    rev_re, rev_im = pw_re[tc - 1::-1][:tc], pw_im[tc - 1::-1][:tc]
    ws_re = rev_re[..., None] * bb_re[None] - rev_im[..., None] * bb_im[None]
    ws_im = rev_re[..., None] * bb_im[None] + rev_im[..., None] * bb_re[None]
    wstate = jnp.concatenate([ws_re, ws_im], axis=2)
    wstate = wstate.transpose(1, 0, 3, 2).reshape(groups, tc * SSM_GROUP, 2 * SSM_STATE)
    wout = jnp.concatenate([ca_re[1:], -ca_im[1:]], axis=3)
    wout = wout.transpose(1, 3, 0, 2).reshape(groups, 2 * SSM_STATE, tc * SSM_GROUP)
    dec_re, dec_im = pw_re[tc], pw_im[tc]
    a1 = jnp.concatenate([dec_re, dec_re], axis=1)
    a2 = jnp.concatenate([-dec_im, dec_im], axis=1)
    return toep.astype(BF16), wstate.astype(BF16), wout.astype(BF16), a1, a2


def _ssm(u, tables, d_skip, w_glu, *, bsz, seq, tm):
    toep, wstate, wout, a1, a2 = tables
    groups = toep.shape[0]
    tc = SSM_CHUNK
    nc = seq // tc
    rows = bsz * nc
    cw = tc * SSM_GROUP
    ug = u.astype(BF16).reshape(bsz, nc, tc, groups, SSM_GROUP).transpose(3, 0, 1, 2, 4).reshape(groups, rows, cw)
    gspec = lambda r, c: pl.BlockSpec((None, r, c), lambda g: (g, 0, 0))
    loc = pl.pallas_call(
        _ssm_state_kernel,
        grid=(groups,),
        in_specs=[gspec(rows, cw), gspec(cw, 2 * SSM_STATE)],
        out_specs=gspec(rows, 2 * SSM_STATE),
        out_shape=jax.ShapeDtypeStruct((groups, rows, 2 * SSM_STATE), F32),
        compiler_params=_params("parallel"),
        name="ssm_chunk_state",
    )(ug, wstate)
    gb = groups * bsz
    loc_t = loc.reshape(groups, bsz, nc, 2 * SSM_STATE).transpose(2, 0, 1, 3).reshape(nc, gb, 2 * SSM_STATE)
    a1r = jnp.repeat(a1, bsz, axis=0)
    a2r = jnp.repeat(a2, bsz, axis=0)
    rb = gb
    prev = pl.pallas_call(
        _ssm_scan_kernel,
        grid=(gb // rb,),
        in_specs=[pl.BlockSpec((nc, rb, 2 * SSM_STATE), lambda i: (0, i, 0)),
                  pl.BlockSpec((rb, 2 * SSM_STATE), lambda i: (i, 0)),
                  pl.BlockSpec((rb, 2 * SSM_STATE), lambda i: (i, 0))],
        out_specs=pl.BlockSpec((nc, rb, 2 * SSM_STATE), lambda i: (0, i, 0)),
        out_shape=jax.ShapeDtypeStruct((nc, gb, 2 * SSM_STATE), F32),
        compiler_params=_params("parallel"),
        name="ssm_chunk_scan",
    )(loc_t, a1r, a2r)
    prev_g = prev.reshape(nc, groups, bsz, 2 * SSM_STATE).transpose(1, 2, 0, 3).reshape(groups, rows, 2 * SSM_STATE)
    y = pl.pallas_call(
        _ssm_out_kernel,
        grid=(groups,),
        in_specs=[gspec(rows, cw), gspec(rows, 2 * SSM_STATE), gspec(cw, cw), gspec(2 * SSM_STATE, cw)],
        out_specs=gspec(rows, cw),
        out_shape=jax.ShapeDtypeStruct((groups, rows, cw), F32),
        compiler_params=_params("parallel"),
        name="ssm_chunk_out",
    )(ug, prev_g, toep, wout)
    y = y.reshape(groups, bsz, nc, tc, SSM_GROUP).transpose(1, 2, 3, 0, 4).reshape(bsz * seq, groups * SSM_GROUP)
    width = groups * SSM_GROUP
    return pl.pallas_call(
        _ssm_gate_kernel,
        grid=(bsz * seq // tm,),
        in_specs=[pl.BlockSpec((tm, width), lambda i: (i, 0)),
                  pl.BlockSpec((tm, width), lambda i: (i, 0)),
                  _resident((1, width)), _resident((width, width))],
        out_specs=pl.BlockSpec((tm, width), lambda i: (i, 0)),
        out_shape=jax.ShapeDtypeStruct((bsz * seq, width), BF16),
        compiler_params=_params("parallel"),
        name="ssm_gate",
    )(y, u, d_skip, w_glu)


def _t5_bucket(n):
    max_exact = REL_BUCKETS // 2
    nf = jnp.maximum(n, 1).astype(F32)
    large = max_exact + (jnp.log(nf / max_exact) / math.log(REL_MAX_DIST / max_exact)
                         * (REL_BUCKETS - max_exact)).astype(I32)
    large = jnp.minimum(large, REL_BUCKETS - 1)
    return jnp.where(n < max_exact, n, large)


def _bias_tiles(table, t):
    assert t >= REL_MAX_DIST
    table = table.astype(F32) * LOG2E
    key = jnp.arange(t)[:, None]
    query = jnp.arange(t)[None, :]
    dist = jnp.stack([query - key, t + query - key])
    bucket = _t5_bucket(jnp.maximum(dist, 0))
    tiles = jnp.zeros((table.shape[1],) + dist.shape, F32)
    for b in range(REL_BUCKETS):
        tiles = jnp.where(bucket[None] == b, table[b][:, None, None, None], tiles)
    return jnp.where(dist[None] >= 0, tiles, NEG), table[REL_BUCKETS - 1]


def _split_w_in(w_in):
    offs = [0]
    for s in IN_SIZES:
        offs.append(offs[-1] + s)
    da_q, da_k, da_v, ssm_u, sa_q, sa_k, sa_v, ix_q, ix_k, ix_w = (
        w_in[:, offs[j]:offs[j + 1]] for j in range(len(IN_SIZES)))
    w_scale = IDX_HEADS ** -0.5 * IDX_DIM ** -0.5
    pad = lambda w: jnp.pad(w, ((0, 0), (0, LANES - w.shape[1])))
    weights = [(da_q * (DA_QK_DIM ** -0.5 * LOG2E)).T, da_k, da_v.T, ssm_u,
               (sa_q * (SA_HEAD_DIM ** -0.5 * LOG2E)).T, pad(sa_k), sa_v.T,
               ix_q.T, pad(ix_k), (ix_w * w_scale).T]
    dtypes = [BF16, BF16, BF16, F32, BF16, BF16, BF16, BF16, BF16, F32]
    transposed = [True, False, True, False, True, False, True, True, False, True]
    return [w.astype(BF16) for w in weights], dtypes, transposed


def _plan(bsz, seq):
    rows = bsz * seq
    tm = 512 if rows % 512 == 0 else rows
    t_da = 1024 if seq % 1024 == 0 and seq >= 4096 else 256 if seq % 256 == 0 else 128
    t_sa = 512 if seq % 512 == 0 and seq >= 2048 else 256 if seq % 256 == 0 else 128
    return dict(tm=tm, t_da=t_da, t_sa=t_sa)


def kernel(x, p, rel_bias, ffn1_w_gate, ffn1_w_up, ffn1_w_down, ln1_g, ln1_b, w_in, w_o, da_lam_q1, da_lam_k1, da_lam_q2, da_lam_k2, da_subln_g, ssm_lam_re, ssm_lam_im, ssm_log_dt, ssm_b_re, ssm_b_im, ssm_c_re, ssm_c_im, ssm_d, ssm_w_glu, ln2_g, ln2_b, ffn2_w_gate, ffn2_w_up, ffn2_w_down, ple_w_proj, ple_w_gate, ln3_g, ln3_b):
    bsz, seq, d = x.shape
    rows = bsz * seq
    plan = _plan(bsz, seq)
    tm, t_da, t_sa = plan["tm"], plan["t_da"], plan["t_sa"]
    topk = min(TOPK_MAX, seq // 4)
    da_near, da_far = _bias_tiles(rel_bias[:, :DA_HEADS], t_da)
    sa_near, sa_far = _bias_tiles(rel_bias[:, DA_HEADS:], t_sa)
    vec = lambda a: a.reshape(1, -1).astype(F32)
    da_w = DA_HEADS * DA_V_DIM

    h = x.reshape(rows, d)
    for i in range(DEPTH):
        lam_init = 0.8 - 0.6 * math.exp(-0.3 * i)
        weights, dtypes, transposed = _split_w_in(w_in[i])
        h, (da_q, da_k, da_v, ssm_u, sa_q, sa_k, sa_v, ix_q, ix_k, ix_w) = _ffn_inproj(
            h, ffn1_w_gate[i].astype(BF16), ffn1_w_up[i].astype(BF16), ffn1_w_down[i].astype(BF16),
            vec(ln1_g[i]), vec(ln1_b[i]), weights, dtypes, transposed, tm=tm)
        b3 = lambda a: a.reshape(bsz, seq, a.shape[-1])

        lam = (jnp.exp(jnp.sum(da_lam_q1[i].astype(F32) * da_lam_k1[i]))
               - jnp.exp(jnp.sum(da_lam_q2[i].astype(F32) * da_lam_k2[i])) + lam_init)
        subln = jnp.broadcast_to(da_subln_g[i].astype(F32)[:, None], (DA_V_DIM, t_da))
        o_da = _diff_attention(da_q, b3(da_k), da_v, da_near, da_far,
                               lam.reshape(1).astype(F32), subln, bsz=bsz, seq=seq, t=t_da,
                               post_scale=1.0 - lam_init)

        tables = _ssm_tables(ssm_lam_re[i].astype(F32), ssm_lam_im[i].astype(F32), ssm_log_dt[i].astype(F32),
                             ssm_b_re[i].astype(F32), ssm_b_im[i].astype(F32),
                             ssm_c_re[i].astype(F32), ssm_c_im[i].astype(F32), SSM_CHUNK)
        o_ssm = _ssm(ssm_u, tables, vec(ssm_d[i]), ssm_w_glu[i].astype(BF16), bsz=bsz, seq=seq, tm=tm)

        o_sa = _sparse_attention(ix_q, ix_w, b3(ix_k), sa_q, b3(sa_k), sa_v, sa_near, sa_far,
                                 bsz=bsz, seq=seq, t=t_sa, topk=topk)

        wo = w_o[i].astype(BF16)
        h = _mix_ffn(h, o_da.reshape(rows, da_w), o_ssm, o_sa.reshape(rows, SA_WIDTH), p[i].reshape(rows, -1),
                     [wo[:da_w], wo[da_w:da_w + SSM_WIDTH], wo[da_w + SSM_WIDTH:], vec(ln2_g[i]), vec(ln2_b[i]),
                      ffn2_w_gate[i].astype(BF16), ffn2_w_up[i].astype(BF16), ffn2_w_down[i].astype(BF16),
                      ple_w_proj[i].astype(BF16), ple_w_gate[i].astype(BF16), vec(ln3_g[i]), vec(ln3_b[i])],
                     tm=tm)
    return h.reshape(bsz, seq, d)
```

```python
import functools
import math

import jax
import jax.numpy as jnp
from jax import lax
from jax.experimental import pallas as pl
from jax.experimental.pallas import tpu as pltpu

F32 = jnp.float32
BF16 = jnp.bfloat16
I32 = jnp.int32

DEPTH = 2
DA_QK_DIM = 64
DA_V_DIM = 2 * DA_QK_DIM
DA_HEADS = 4
SSM_GROUP = 16
SSM_GROUPS = 16
SSM_STATE = 64
SSM_WIDTH = SSM_GROUP * SSM_GROUPS
SA_HEAD_DIM = 64
SA_HEADS = 4
SA_WIDTH = SA_HEADS * SA_HEAD_DIM
IDX_HEADS = 8
IDX_DIM = 32
TOPK_MAX = 256
REL_BUCKETS = 32
REL_MAX_DIST = 128
ALPHA = (2 * DEPTH) ** 0.25
LN_EPS = 1e-5
IN_SIZES = (DA_HEADS * 2 * DA_QK_DIM, DA_HEADS * 2 * DA_QK_DIM, DA_HEADS * DA_V_DIM,
            SSM_WIDTH, SA_WIDTH, SA_HEAD_DIM, SA_HEAD_DIM,
            IDX_HEADS * IDX_DIM, IDX_DIM, IDX_HEADS)

LANES = 128
SUBLANES = 8
BF16_ROWS = 16
NEG = -1e30
INT_MIN = -2 ** 31
HI_MASK = -2 ** 16
TINY = 2.0 ** -126
BF16_MIN_NORMAL = 0x0080
LOG2E = math.log2(math.e)
VMEM_LIMIT = 56 * 1024 * 1024
SSM_CHUNK = 64

def _params(*sem):
    return pltpu.CompilerParams(dimension_semantics=sem, vmem_limit_bytes=VMEM_LIMIT)


def _resident(shape):
    return pl.BlockSpec(shape, lambda *_: (0,) * len(shape), pipeline_mode=pl.Buffered(1))


def _dot(a, b):
    return jnp.dot(a, b, preferred_element_type=F32)


def _dot_nt(a, b):
    return lax.dot_general(a, b, (((1,), (1,)), ((), ())), preferred_element_type=F32)


def _layer_norm(y, g, b):
    mu = jnp.mean(y, axis=-1, keepdims=True)
    d = y - mu
    var = jnp.mean(d * d, axis=-1, keepdims=True)
    return d * lax.rsqrt(var + LN_EPS) * g + b


def _fold_rows(x, rows):
    while x.shape[0] > rows:
        half = x.shape[0] // 2
        x = x[:half] + x[half:]
    return x


FF_CHUNK = 256


def _swiglu(xb, wg, wu, wd):
    d_ff = wg.shape[1]
    acc = None
    for c0 in range(0, d_ff, FF_CHUNK):
        c1 = min(c0 + FF_CHUNK, d_ff)
        gate = _dot(xb, wg[:, c0:c1])
        up = _dot(xb, wu[:, c0:c1])
        hid = (gate * jax.nn.sigmoid(gate) * up).astype(BF16)
        part = _dot(hid, wd[c0:c1, :])
        acc = part if acc is None else acc + part
    return acc


def _ffn_inproj_kernel(x_ref, wg, wu, wd, g_ref, b_ref, *refs, transposed):
    n = len(transposed)
    w_refs, x1_ref, o_refs = refs[:n], refs[n], refs[n + 1:]
    x = x_ref[...]
    x1 = _layer_norm(ALPHA * x + 0.5 * _swiglu(x.astype(BF16), wg, wu, wd), g_ref[...], b_ref[...])
    x1_ref[...] = x1
    xb = x1.astype(BF16)
    for w_ref, o_ref, tr in zip(w_refs, o_refs, transposed):
        out = _dot_nt(w_ref[...], xb) if tr else _dot(xb, w_ref[...])
        o_ref[...] = out.astype(o_ref.dtype)


def _ffn_inproj(x, wg, wu, wd, g, b, weights, dtypes, transposed, *, tm):
    t_rows, d = x.shape
    row = lambda w: pl.BlockSpec((tm, w), lambda i: (i, 0))
    out_specs, out_shape = [row(d)], [jax.ShapeDtypeStruct((t_rows, d), F32)]
    for w, dt, tr in zip(weights, dtypes, transposed):
        if tr:
            out_specs.append(pl.BlockSpec((w.shape[0], tm), lambda i: (0, i)))
            out_shape.append(jax.ShapeDtypeStruct((w.shape[0], t_rows), dt))
        else:
            out_specs.append(row(w.shape[1]))
            out_shape.append(jax.ShapeDtypeStruct((t_rows, w.shape[1]), dt))
    consts = [wg, wu, wd, g, b] + list(weights)
    outs = pl.pallas_call(
        functools.partial(_ffn_inproj_kernel, transposed=tuple(transposed)),
        grid=(t_rows // tm,),
        in_specs=[row(d)] + [_resident(c.shape) for c in consts],
        out_specs=out_specs,
        out_shape=out_shape,
        compiler_params=_params("parallel"),
        name="ffn_ln_in_proj",
    )(x, *consts)
    return outs[0], outs[1:]


def _mix_ffn_kernel(x_ref, da_ref, ssm_ref, sa_ref, p_ref, w1, w2, w3, g2_ref, b2_ref,
                    wg, wu, wd, wpp, wpg, g3_ref, b3_ref, o_ref):
    mix = _dot(da_ref[...], w1[...]) + _dot(ssm_ref[...], w2[...]) + _dot(sa_ref[...], w3[...])
    x = _layer_norm(ALPHA * x_ref[...] + mix, g2_ref[...], b2_ref[...])
    xb = x.astype(BF16)
    y = ALPHA * x + 0.5 * _swiglu(xb, wg, wu, wd)
    y = y + _dot(p_ref[...].astype(BF16), wpp[...]) * jax.nn.sigmoid(_dot(xb, wpg[...]))
    o_ref[...] = _layer_norm(y, g3_ref[...], b3_ref[...])


def _mix_ffn(x, o_da, o_ssm, o_sa, p, consts, *, tm):
    t_rows, d = x.shape
    row = lambda a: pl.BlockSpec((tm, a.shape[1]), lambda i: (i, 0))
    acts = [x, o_da, o_ssm, o_sa, p]
    return pl.pallas_call(
        _mix_ffn_kernel,
        grid=(t_rows // tm,),
        in_specs=[row(a) for a in acts] + [_resident(c.shape) for c in consts],
        out_specs=pl.BlockSpec((tm, d), lambda i: (i, 0)),
        out_shape=jax.ShapeDtypeStruct((t_rows, d), F32),
        compiler_params=_params("parallel"),
        name="out_proj_ln_ffn_ple_ln",
    )(*acts, *consts)


def _softmax_probs(s, m_ref, shift):
    m_prev = m_ref[...]
    m_cur = jnp.max(s, axis=0, keepdims=True)
    if shift is not None:
        m_cur = m_cur + shift
    m_next = jnp.maximum(m_prev, m_cur)
    m_ref[...] = m_next
    p = jnp.exp2(s - (m_next if shift is None else m_next - shift))
    return p.astype(BF16), jnp.exp2(m_prev - m_next)


def _accumulate(acc_ref, rescale, v_t, p):
    v_aug = jnp.concatenate([v_t, jnp.ones((BF16_ROWS, v_t.shape[1]), v_t.dtype)], axis=0)
    acc_ref[...] = rescale * acc_ref[...] + _dot(v_aug, p)


def _softmax_step(s, v_t, m_ref, acc_ref, shift):
    p, rescale = _softmax_probs(s, m_ref, shift)
    _accumulate(acc_ref, rescale, v_t, p)


def _far_chunks(n, scores, values, m_ref, acc_ref, shift):
    def body(i, carry):
        _softmax_step(scores(i), values(i), m_ref, acc_ref, shift)
        return carry

    lax.fori_loop(0, n, body, 0)


def _da_kernel(far_ref, lam_ref, q_ref, k_ref, v_ref, bn_ref, g_ref, o_ref,
               qd_ref, m_ref, acc_ref, *, t, post_scale):
    h = pl.program_id(1)
    qi = pl.program_id(2)
    q = q_ref[...].astype(F32)
    rowq = lax.broadcasted_iota(I32, q.shape, 0)
    qd_ref[:, 0:t] = jnp.where(rowq < DA_QK_DIM, q, 0.0).astype(BF16)
    qd_ref[:, t:2 * t] = jnp.where(rowq >= DA_QK_DIM, q, 0.0).astype(BF16)
    m_ref[...] = jnp.full(m_ref.shape, NEG, F32)
    acc_ref[...] = jnp.zeros(acc_ref.shape, F32)
    far = far_ref[h]

    def scores(kb):
        return _dot(k_ref[pl.ds(pl.multiple_of(kb * t, t), t), :], qd_ref[...])

    def values(kb):
        return v_ref[:, pl.ds(pl.multiple_of(kb * t, t), t)]

    def near_block(kb, near):
        bias = bn_ref[near]
        _softmax_step(scores(kb) + jnp.concatenate([bias, bias], axis=1), values(kb), m_ref, acc_ref, None)

    _far_chunks(jnp.maximum(qi - 1, 0), scores, values, m_ref, acc_ref, far)

    @pl.when(qi >= 1)
    def _():
        near_block(qi - 1, 1)

    near_block(qi, 0)

    acc = acc_ref[...]
    o = acc[0:DA_V_DIM] / acc[DA_V_DIM:DA_V_DIM + 1]
    o = o[:, 0:t] - lam_ref[0] * o[:, t:2 * t]
    ms = jnp.mean(o * o, axis=0, keepdims=True)
    o = o * lax.rsqrt(ms + LN_EPS) * g_ref[...] * post_scale
    o_ref[...] = o.T.astype(o_ref.dtype)


def _diff_attention(q_t, k, v_t, bias_near, bias_far, lam, subln_g, *, bsz, seq, t, post_scale):
    heads = q_t.shape[0] // DA_V_DIM
    nq = seq // t
    smem = pl.BlockSpec(memory_space=pltpu.SMEM)
    va = DA_V_DIM + BF16_ROWS
    return pl.pallas_call(
        functools.partial(_da_kernel, t=t, post_scale=post_scale),
        grid=(bsz, heads, nq),
        in_specs=[
            smem, smem,
            pl.BlockSpec((DA_V_DIM, t), lambda b, h, i: (h, b * nq + i)),
            pl.BlockSpec((None, seq, DA_V_DIM), lambda b, h, i: (b, 0, h)),
            pl.BlockSpec((DA_V_DIM, seq), lambda b, h, i: (h, b)),
            pl.BlockSpec((None, 2, t, t), lambda b, h, i: (h, 0, 0, 0)),
            pl.BlockSpec((DA_V_DIM, t), lambda b, h, i: (0, 0)),
        ],
        out_specs=pl.BlockSpec((None, t, DA_V_DIM), lambda b, h, i: (b, i, h)),
        out_shape=jax.ShapeDtypeStruct((bsz, seq, heads * DA_V_DIM), BF16),
        scratch_shapes=[
            pltpu.VMEM((DA_V_DIM, 2 * t), BF16),
            pltpu.VMEM((1, 2 * t), F32),
            pltpu.VMEM((va, 2 * t), F32),
        ],
        compiler_params=_params("parallel", "parallel", "arbitrary"),
        name="diff_attention",
    )(bias_far, lam, q_t, k, v_t, bias_near, subln_g)


def _dsa_kernel(far_ref, ixq_ref, ixw_ref, ixk_ref, saq_ref, sak_ref, sav_ref, bn_ref, o_ref,
                qi8_ref, keys_ref, hi_ref, qs4_ref, m_ref, acc_ref, *, t, topk):
    qi = pl.program_id(1)
    kf = float(topk)

    qi8_ref[...] = jnp.zeros(qi8_ref.shape, BF16)
    for h in range(IDX_HEADS):
        qi8_ref[0:IDX_DIM, h * t:(h + 1) * t] = ixq_ref[h * IDX_DIM:(h + 1) * IDX_DIM, :]

    def rows(kb, n=1):
        return pl.ds(pl.multiple_of(kb * t, t), n * t)

    def chunk(kb):
        return keys_ref[rows(kb), :]

    def index_keys(kb, n=1):
        kblk = ixk_ref[rows(kb, n), :]
        sc = None
        for h in range(IDX_HEADS):
            r = _dot(kblk, qi8_ref[:, h * t:(h + 1) * t])
            term = ixw_ref[h:h + 1, :] * jnp.maximum(r, 0.0)
            sc = term if sc is None else sc + term
        return jnp.where(jnp.abs(sc) < TINY, 0.0, sc)

    def store_keys(kb, sc, n=1):
        bits = lax.bitcast_convert_type(sc, I32)
        keys_ref[rows(kb, n), :] = bits ^ ((bits >> 31) & 0x7FFFFFFF)
        hi_ref[rows(kb, n), :] = lax.bitcast_convert_type(bits & HI_MASK, F32).astype(BF16)

    def in_pairs(count, step):
        def pair(j, carry):
            step(2 * j, 2)
            return carry
        lax.fori_loop(0, count // 2, pair, 0)

        @pl.when(count % 2 == 1)
        def _():
            step(count - 1, 1)

    in_pairs(qi, lambda kb, n: store_keys(kb, index_keys(kb, n), n))
    krow = lax.broadcasted_iota(I32, (t, t), 0)
    qcol = lax.broadcasted_iota(I32, (t, t), 1)
    store_keys(qi, jnp.where(krow <= qcol, index_keys(qi), -jnp.inf))

    def count(fn):
        def body(kb, cnt):
            return cnt + _fold_rows(fn(kb, chunk(kb)), SUBLANES)
        cnt = lax.fori_loop(0, qi + 1, body, jnp.zeros((SUBLANES, t), I32))
        return jnp.sum(cnt.astype(F32), axis=0, keepdims=True)

    def count_hi(thr):
        def body(kb, cnt):
            hit = jnp.where(hi_ref[rows(kb), :] >= thr, jnp.ones((), BF16), jnp.zeros((), BF16))
            return cnt + _fold_rows(hit, BF16_ROWS).astype(F32)
        cnt = lax.fori_loop(0, qi + 1, body, jnp.zeros((BF16_ROWS, t), F32))
        return jnp.sum(cnt, axis=0, keepdims=True)

    def hi_bit(i, st):
        code, open_ = st
        cand = code + jnp.left_shift(jnp.int32(1), 15 - i)
        pattern = cand ^ ((cand >> 31) & 0x7FFF)
        pattern = jnp.where(cand > 0, jnp.maximum(pattern, BF16_MIN_NORMAL), pattern)
        thr = lax.bitcast_convert_type(jnp.left_shift(pattern, 16), F32).astype(BF16)
        c = count_hi(thr)
        code = jnp.where(open_ > 0.0, jnp.where(c >= kf, cand, code), code)
        return code, jnp.where(c == kf, 0.0, open_)

    code, open_ = lax.fori_loop(0, 16, hi_bit, (jnp.full((1, t), -2 ** 15, I32), jnp.ones((1, t), F32)))
    zero_tie = jnp.where(code == 0, open_, 0.0)
    open_ = jnp.where(code == 0, 0.0, open_)

    def search_cond(st):
        i, _, _, n_open = st
        return jnp.logical_and(i < 32, n_open > 0.0)

    def search_body(st):
        i, tau, open_, _ = st
        cand = tau + jnp.left_shift(jnp.int32(1), 31 - i)
        c = count(lambda kb, blk: jnp.where(blk >= cand, 1, 0))
        tau = jnp.where(open_ > 0.0, jnp.where(c >= kf, cand, tau), tau)
        open_ = jnp.where(c == kf, 0.0, open_)
        return i + 1, tau, open_, jnp.max(open_)

    _, tau, open_, _ = lax.while_loop(
        search_cond, search_body, (jnp.int32(16), jnp.left_shift(code, 16), open_, jnp.max(open_)))

    @pl.when(jnp.max(jnp.maximum(open_, zero_tie)) > 0.0)
    def _():
        need = kf - count(lambda kb, blk: jnp.where(blk > tau, 1, 0))
        below = jnp.where(qcol <= krow, 1.0, 0.0).astype(BF16)

        def retire(kb, seen):
            blk = chunk(kb)
            tied = jnp.where(blk == tau, 1.0, 0.0).astype(BF16)
            rank = _dot(below, tied) + seen
            keys_ref[rows(kb), :] = jnp.where(blk == tau, jnp.where(rank > need, INT_MIN, blk), blk)
            return rank[t - 1:t, :]

        lax.fori_loop(0, qi + 1, retire, jnp.zeros((1, t), F32))

    qs4_ref[...] = jnp.zeros(qs4_ref.shape, BF16)
    for h in range(SA_HEADS):
        qs4_ref[0:SA_HEAD_DIM, h * t:(h + 1) * t] = saq_ref[h * SA_HEAD_DIM:(h + 1) * SA_HEAD_DIM, :]
    m_ref[...] = jnp.full(m_ref.shape, NEG, F32)
    acc_ref[...] = jnp.zeros(acc_ref.shape, F32)

    def scores(kb, near=None, n=1):
        s = _dot(sak_ref[rows(kb, n), :], qs4_ref[...])
        sel = keys_ref[rows(kb, n), :] >= tau
        parts = []
        for h in range(SA_HEADS):
            sh = s[:, h * t:(h + 1) * t]
            if near is not None:
                sh = sh + bn_ref[h, near]
            parts.append(jnp.where(sel, sh, NEG))
        return jnp.concatenate(parts, axis=1)

    far = jnp.concatenate([jnp.full((1, t), far_ref[h], F32) for h in range(SA_HEADS)], axis=1)

    def values(kb, n=1):
        return sav_ref[:, rows(kb, n)]

    in_pairs(jnp.maximum(qi - 1, 0),
             lambda kb, n: _softmax_step(scores(kb, None, n), values(kb, n), m_ref, acc_ref, far))

    @pl.when(qi >= 1)
    def _():
        _softmax_step(scores(qi - 1, 1), values(qi - 1), m_ref, acc_ref, None)

    _softmax_step(scores(qi, 0), values(qi), m_ref, acc_ref, None)

    acc = acc_ref[...]
    o = acc[0:SA_HEAD_DIM] / acc[SA_HEAD_DIM:SA_HEAD_DIM + 1]
    o = jnp.concatenate([o[:, h * t:(h + 1) * t] for h in range(SA_HEADS)], axis=0)
    o_ref[...] = o.T.astype(o_ref.dtype)


def _sparse_attention(ixq_t, ixw_t, ixk, saq_t, sak, sav_t, bias_near, bias_far, *, bsz, seq, t, topk):
    nq = seq // t
    smem = pl.BlockSpec(memory_space=pltpu.SMEM)
    qcols = lambda r: pl.BlockSpec((r, t), lambda b, i: (0, b * nq + i))
    full = lambda w: pl.BlockSpec((None, seq, w), lambda b, i: (b, 0, 0), pipeline_mode=pl.Buffered(1))
    va = SA_HEAD_DIM + BF16_ROWS
    return pl.pallas_call(
        functools.partial(_dsa_kernel, t=t, topk=topk),
        grid=(bsz, nq),
        in_specs=[
            smem,
            qcols(IDX_HEADS * IDX_DIM), qcols(IDX_HEADS), full(LANES),
            qcols(SA_WIDTH), full(LANES),
            pl.BlockSpec((SA_HEAD_DIM, seq), lambda b, i: (0, b), pipeline_mode=pl.Buffered(1)),
            _resident((SA_HEADS, 2, t, t)),
        ],
        out_specs=pl.BlockSpec((None, t, SA_WIDTH), lambda b, i: (b, i, 0)),
        out_shape=jax.ShapeDtypeStruct((bsz, seq, SA_WIDTH), BF16),
        scratch_shapes=[
            pltpu.VMEM((LANES, IDX_HEADS * t), BF16),
            pltpu.VMEM((seq, t), I32),
            pltpu.VMEM((seq, t), BF16),
            pltpu.VMEM((LANES, SA_HEADS * t), BF16),
            pltpu.VMEM((1, SA_HEADS * t), F32),
            pltpu.VMEM((va, SA_HEADS * t), F32),
        ],
        compiler_params=_params("parallel", "arbitrary"),
        name="sparse_attention",
    )(bias_far, ixq_t, ixw_t, ixk, saq_t, sak, sav_t, bias_near)


def _ssm_state_kernel(u_ref, w_ref, o_ref):
    o_ref[...] = _dot(u_ref[...], w_ref[...])


def _ssm_scan_kernel(loc_ref, a1_ref, a2_ref, o_ref):
    a1 = a1_ref[...]
    a2 = a2_ref[...]

    def body(n, s):
        o_ref[n] = s
        return a1 * s + a2 * pltpu.roll(s, SSM_STATE, axis=1) + loc_ref[n]

    lax.fori_loop(0, loc_ref.shape[0], body, jnp.zeros(a1.shape, F32))


def _ssm_out_kernel(u_ref, s_ref, toep_ref, wout_ref, o_ref):
    o_ref[...] = (_dot(u_ref[...], toep_ref[...])
                  + _dot(s_ref[...].astype(BF16), wout_ref[...]))


def _ssm_gate_kernel(y_ref, u_ref, d_ref, w_ref, o_ref):
    y = jax.nn.gelu(y_ref[...] + d_ref[...] * u_ref[...])
    o_ref[...] = (y * jax.nn.sigmoid(_dot(y.astype(BF16), w_ref[...]))).astype(o_ref.dtype)


def _ssm_tables(lam_re, lam_im, log_dt, b_re, b_im, c_re, c_im, tc):
    hp = lax.Precision.HIGHEST
    dt = jnp.exp(log_dt)[:, None]
    n = jnp.arange(tc + 1, dtype=F32)[:, None, None]
    mag = jnp.exp(lam_re * dt * n)
    pw_re = mag * jnp.cos(lam_im * dt * n)
    pw_im = mag * jnp.sin(lam_im * dt * n)
    den = lam_re * lam_re + lam_im * lam_im
    nr, ni = pw_re[1] - 1.0, pw_im[1]
    f_re = (nr * lam_re + ni * lam_im) / den
    f_im = (ni * lam_re - nr * lam_im) / den
    bb_re = f_re[..., None] * b_re - f_im[..., None] * b_im
    bb_im = f_re[..., None] * b_im + f_im[..., None] * b_re
    ca_re = c_re[None] * pw_re[:, :, None, :] - c_im[None] * pw_im[:, :, None, :]
    ca_im = c_re[None] * pw_im[:, :, None, :] + c_im[None] * pw_re[:, :, None, :]
    groups = lam_re.shape[0]
    inv_mag = jnp.exp(-lam_re * dt * n[:tc])
    iw_re = inv_mag * jnp.cos(lam_im * dt * n[:tc])
    iw_im = -inv_mag * jnp.sin(lam_im * dt * n[:tc])
    l_re = iw_re[..., None] * bb_re[None] - iw_im[..., None] * bb_im[None]
    l_im = iw_re[..., None] * bb_im[None] + iw_im[..., None] * bb_re[None]
    left = jnp.concatenate([l_re, l_im], axis=2).transpose(1, 0, 3, 2)
    right = jnp.concatenate([ca_re[:tc], -ca_im[:tc]], axis=3).transpose(1, 0, 2, 3)
    cw = tc * SSM_GROUP
    toep = jnp.einsum('gik,gjk->gij', left.reshape(groups, cw, 2 * SSM_STATE),
                      right.reshape(groups, cw, 2 * SSM_STATE), precision=hp)
    step = jnp.arange(cw) // SSM_GROUP
    toep = jnp.where(step[:, None] <= step[None, :], toep, 0.0)
    rev_re, rev_im = pw_re[tc - 1::-1][:tc], pw_im[tc - 1::-1][:tc]
    ws_re = rev_re[..., None] * bb_re[None] - rev_im[..., None] * bb_im[None]
    ws_im = rev_re[..., None] * bb_im[None] + rev_im[..., None] * bb_re[None]
    wstate = jnp.concatenate([ws_re, ws_im], axis=2)
    wstate = wstate.transpose(1, 0, 3, 2).reshape(groups, tc * SSM_GROUP, 2 * SSM_STATE)
    wout = jnp.concatenate([ca_re[1:], -ca_im[1:]], axis=3)
    wout = wout.transpose(1, 3, 0, 2).reshape(groups, 2 * SSM_STATE, tc * SSM_GROUP)
    dec_re, dec_im = pw_re[tc], pw_im[tc]
    a1 = jnp.concatenate([dec_re, dec_re], axis=1)
    a2 = jnp.concatenate([-dec_im, dec_im], axis=1)
    return toep.astype(BF16), wstate.astype(BF16), wout.astype(BF16), a1, a2


def _ssm(u, tables, d_skip, w_glu, *, bsz, seq, tm):
    toep, wstate, wout, a1, a2 = tables
    groups = toep.shape[0]
    tc = SSM_CHUNK
    nc = seq // tc
    rows = bsz * nc
    cw = tc * SSM_GROUP
    ug = u.astype(BF16).reshape(bsz, nc, tc, groups, SSM_GROUP).transpose(3, 0, 1, 2, 4).reshape(groups, rows, cw)
    gspec = lambda r, c: pl.BlockSpec((None, r, c), lambda g: (g, 0, 0))
    loc = pl.pallas_call(
        _ssm_state_kernel,
        grid=(groups,),
        in_specs=[gspec(rows, cw), gspec(cw, 2 * SSM_STATE)],
        out_specs=gspec(rows, 2 * SSM_STATE),
        out_shape=jax.ShapeDtypeStruct((groups, rows, 2 * SSM_STATE), F32),
        compiler_params=_params("parallel"),
        name="ssm_chunk_state",
    )(ug, wstate)
    gb = groups * bsz
    loc_t = loc.reshape(groups, bsz, nc, 2 * SSM_STATE).transpose(2, 0, 1, 3).reshape(nc, gb, 2 * SSM_STATE)
    a1r = jnp.repeat(a1, bsz, axis=0)
    a2r = jnp.repeat(a2, bsz, axis=0)
    rb = gb
    prev = pl.pallas_call(
        _ssm_scan_kernel,
        grid=(gb // rb,),
        in_specs=[pl.BlockSpec((nc, rb, 2 * SSM_STATE), lambda i: (0, i, 0)),
                  pl.BlockSpec((rb, 2 * SSM_STATE), lambda i: (i, 0)),
                  pl.BlockSpec((rb, 2 * SSM_STATE), lambda i: (i, 0))],
        out_specs=pl.BlockSpec((nc, rb, 2 * SSM_STATE), lambda i: (0, i, 0)),
        out_shape=jax.ShapeDtypeStruct((nc, gb, 2 * SSM_STATE), F32),
        compiler_params=_params("parallel"),
        name="ssm_chunk_scan",
    )(loc_t, a1r, a2r)
    prev_g = prev.reshape(nc, groups, bsz, 2 * SSM_STATE).transpose(1, 2, 0, 3).reshape(groups, rows, 2 * SSM_STATE)
    y = pl.pallas_call(
        _ssm_out_kernel,
        grid=(groups,),
        in_specs=[gspec(rows, cw), gspec(rows, 2 * SSM_STATE), gspec(cw, cw), gspec(2 * SSM_STATE, cw)],
        out_specs=gspec(rows, cw),
        out_shape=jax.ShapeDtypeStruct((groups, rows, cw), F32),
        compiler_params=_params("parallel"),
        name="ssm_chunk_out",
    )(ug, prev_g, toep, wout)
    y = y.reshape(groups, bsz, nc, tc, SSM_GROUP).transpose(1, 2, 3, 0, 4).reshape(bsz * seq, groups * SSM_GROUP)
    width = groups * SSM_GROUP
    return pl.pallas_call(
        _ssm_gate_kernel,
        grid=(bsz * seq // tm,),
        in_specs=[pl.BlockSpec((tm, width), lambda i: (i, 0)),
                  pl.BlockSpec((tm, width), lambda i: (i, 0)),
                  _resident((1, width)), _resident((width, width))],
        out_specs=pl.BlockSpec((tm, width), lambda i: (i, 0)),
        out_shape=jax.ShapeDtypeStruct((bsz * seq, width), BF16),
        compiler_params=_params("parallel"),
        name="ssm_gate",
    )(y, u, d_skip, w_glu)


def _t5_bucket(n):
    max_exact = REL_BUCKETS // 2
    nf = jnp.maximum(n, 1).astype(F32)
    large = max_exact + (jnp.log(nf / max_exact) / math.log(REL_MAX_DIST / max_exact)
                         * (REL_BUCKETS - max_exact)).astype(I32)
    large = jnp.minimum(large, REL_BUCKETS - 1)
    return jnp.where(n < max_exact, n, large)


def _bias_tiles(table, t):
    assert t >= REL_MAX_DIST
    table = table.astype(F32) * LOG2E
    key = jnp.arange(t)[:, None]
    query = jnp.arange(t)[None, :]
    dist = jnp.stack([query - key, t + query - key])
    bucket = _t5_bucket(jnp.maximum(dist, 0))
    tiles = jnp.zeros((table.shape[1],) + dist.shape, F32)
    for b in range(REL_BUCKETS):
        tiles = jnp.where(bucket[None] == b, table[b][:, None, None, None], tiles)
    return jnp.where(dist[None] >= 0, tiles, NEG), table[REL_BUCKETS - 1]


def _split_w_in(w_in):
    offs = [0]
    for s in IN_SIZES:
        offs.append(offs[-1] + s)
    da_q, da_k, da_v, ssm_u, sa_q, sa_k, sa_v, ix_q, ix_k, ix_w = (
        w_in[:, offs[j]:offs[j + 1]] for j in range(len(IN_SIZES)))
    w_scale = IDX_HEADS ** -0.5 * IDX_DIM ** -0.5
    pad = lambda w: jnp.pad(w, ((0, 0), (0, LANES - w.shape[1])))
    weights = [(da_q * (DA_QK_DIM ** -0.5 * LOG2E)).T, da_k, da_v.T, ssm_u,
               (sa_q * (SA_HEAD_DIM ** -0.5 * LOG2E)).T, pad(sa_k), sa_v.T,
               ix_q.T, pad(ix_k), (ix_w * w_scale).T]
    dtypes = [BF16, BF16, BF16, F32, BF16, BF16, BF16, BF16, BF16, F32]
    transposed = [True, False, True, False, True, False, True, True, False, True]
    return [w.astype(BF16) for w in weights], dtypes, transposed


def _plan(bsz, seq):
    rows = bsz * seq
    tm = 512 if rows % 512 == 0 else rows
    t_da = 1024 if seq % 1024 == 0 and seq >= 4096 else 256 if seq % 256 == 0 else 128
    t_sa = 512 if seq % 512 == 0 and seq >= 2048 else 256 if seq % 256 == 0 else 128
    tm_gate = 2048 if rows % 2048 == 0 else tm
    return dict(tm=tm, tm_gate=tm_gate, t_da=t_da, t_sa=t_sa)


def kernel(x, p, rel_bias, ffn1_w_gate, ffn1_w_up, ffn1_w_down, ln1_g, ln1_b, w_in, w_o, da_lam_q1, da_lam_k1, da_lam_q2, da_lam_k2, da_subln_g, ssm_lam_re, ssm_lam_im, ssm_log_dt, ssm_b_re, ssm_b_im, ssm_c_re, ssm_c_im, ssm_d, ssm_w_glu, ln2_g, ln2_b, ffn2_w_gate, ffn2_w_up, ffn2_w_down, ple_w_proj, ple_w_gate, ln3_g, ln3_b):
    bsz, seq, d = x.shape
    rows = bsz * seq
    plan = _plan(bsz, seq)
    tm, t_da, t_sa = plan["tm"], plan["t_da"], plan["t_sa"]
    topk = min(TOPK_MAX, seq // 4)
    da_near, da_far = _bias_tiles(rel_bias[:, :DA_HEADS], t_da)
    sa_near, sa_far = _bias_tiles(rel_bias[:, DA_HEADS:], t_sa)
    vec = lambda a: a.reshape(1, -1).astype(F32)
    da_w = DA_HEADS * DA_V_DIM

    h = x.reshape(rows, d)
    for i in range(DEPTH):
        lam_init = 0.8 - 0.6 * math.exp(-0.3 * i)
        weights, dtypes, transposed = _split_w_in(w_in[i])
        h, (da_q, da_k, da_v, ssm_u, sa_q, sa_k, sa_v, ix_q, ix_k, ix_w) = _ffn_inproj(
            h, ffn1_w_gate[i].astype(BF16), ffn1_w_up[i].astype(BF16), ffn1_w_down[i].astype(BF16),
            vec(ln1_g[i]), vec(ln1_b[i]), weights, dtypes, transposed, tm=tm)
        b3 = lambda a: a.reshape(bsz, seq, a.shape[-1])

        lam = (jnp.exp(jnp.sum(da_lam_q1[i].astype(F32) * da_lam_k1[i]))
               - jnp.exp(jnp.sum(da_lam_q2[i].astype(F32) * da_lam_k2[i])) + lam_init)
        subln = jnp.broadcast_to(da_subln_g[i].astype(F32)[:, None], (DA_V_DIM, t_da))
        o_da = _diff_attention(da_q, b3(da_k), da_v, da_near, da_far,
                               lam.reshape(1).astype(F32), subln, bsz=bsz, seq=seq, t=t_da,
                               post_scale=1.0 - lam_init)

        tables = _ssm_tables(ssm_lam_re[i].astype(F32), ssm_lam_im[i].astype(F32), ssm_log_dt[i].astype(F32),
                             ssm_b_re[i].astype(F32), ssm_b_im[i].astype(F32),
                             ssm_c_re[i].astype(F32), ssm_c_im[i].astype(F32), SSM_CHUNK)
        o_ssm = _ssm(ssm_u, tables, vec(ssm_d[i]), ssm_w_glu[i].astype(BF16), bsz=bsz, seq=seq,
                     tm=plan["tm_gate"])

        o_sa = _sparse_attention(ix_q, ix_w, b3(ix_k), sa_q, b3(sa_k), sa_v, sa_near, sa_far,
                                 bsz=bsz, seq=seq, t=t_sa, topk=topk)

        wo = w_o[i].astype(BF16)
        h = _mix_ffn(h, o_da.reshape(rows, da_w), o_ssm, o_sa.reshape(rows, SA_WIDTH), p[i].reshape(rows, -1),
                     [wo[:da_w], wo[da_w:da_w + SSM_WIDTH], wo[da_w + SSM_WIDTH:], vec(ln2_g[i]), vec(ln2_b[i]),
                      ffn2_w_gate[i].astype(BF16), ffn2_w_up[i].astype(BF16), ffn2_w_down[i].astype(BF16),
                      ple_w_proj[i].astype(BF16), ple_w_gate[i].astype(BF16), vec(ln3_g[i]), vec(ln3_b[i])],
                     tm=tm)
    return h.reshape(bsz, seq, d)
```

```python
import functools
import math

import jax
import jax.numpy as jnp
from jax import lax
from jax.experimental import pallas as pl
from jax.experimental.pallas import tpu as pltpu

F32 = jnp.float32
BF16 = jnp.bfloat16
I32 = jnp.int32

DEPTH = 2
DA_QK_DIM = 64
DA_V_DIM = 2 * DA_QK_DIM
DA_HEADS = 4
SSM_GROUP = 16
SSM_GROUPS = 16
SSM_STATE = 64
SSM_WIDTH = SSM_GROUP * SSM_GROUPS
SA_HEAD_DIM = 64
SA_HEADS = 4
SA_WIDTH = SA_HEADS * SA_HEAD_DIM
IDX_HEADS = 8
IDX_DIM = 32
TOPK_MAX = 256
REL_BUCKETS = 32
REL_MAX_DIST = 128
ALPHA = (2 * DEPTH) ** 0.25
LN_EPS = 1e-5
IN_SIZES = (DA_HEADS * 2 * DA_QK_DIM, DA_HEADS * 2 * DA_QK_DIM, DA_HEADS * DA_V_DIM,
            SSM_WIDTH, SA_WIDTH, SA_HEAD_DIM, SA_HEAD_DIM,
            IDX_HEADS * IDX_DIM, IDX_DIM, IDX_HEADS)

LANES = 128
SUBLANES = 8
BF16_ROWS = 16
NEG = -1e30
INT_MIN = -2 ** 31
HI_MASK = -2 ** 16
TINY = 2.0 ** -126
BF16_MIN_NORMAL = 0x0080
LOW_BITS = 2
MID_BITS = 16 - LOW_BITS
MID_MASK = 2 ** MID_BITS - 1
LOG2E = math.log2(math.e)
VMEM_LIMIT = 56 * 1024 * 1024
SSM_CHUNK = 64

def _params(*sem):
    return pltpu.CompilerParams(dimension_semantics=sem, vmem_limit_bytes=VMEM_LIMIT)


def _resident(shape):
    return pl.BlockSpec(shape, lambda *_: (0,) * len(shape), pipeline_mode=pl.Buffered(1))


def _dot(a, b):
    return jnp.dot(a, b, preferred_element_type=F32)


def _dot_nt(a, b):
    return lax.dot_general(a, b, (((1,), (1,)), ((), ())), preferred_element_type=F32)


def _layer_norm(y, g, b):
    mu = jnp.mean(y, axis=-1, keepdims=True)
    d = y - mu
    var = jnp.mean(d * d, axis=-1, keepdims=True)
    return d * lax.rsqrt(var + LN_EPS) * g + b


def _fold_rows(x, rows):
    while x.shape[0] > rows:
        half = x.shape[0] // 2
        x = x[:half] + x[half:]
    return x


FF_CHUNK = 256


def _swiglu(xb, wg, wu, wd):
    d_ff = wg.shape[1]
    acc = None
    for c0 in range(0, d_ff, FF_CHUNK):
        c1 = min(c0 + FF_CHUNK, d_ff)
        gate = _dot(xb, wg[:, c0:c1])
        up = _dot(xb, wu[:, c0:c1])
        hid = (gate * jax.nn.sigmoid(gate) * up).astype(BF16)
        part = _dot(hid, wd[c0:c1, :])
        acc = part if acc is None else acc + part
    return acc


def _ffn_inproj_kernel(x_ref, wg, wu, wd, g_ref, b_ref, *refs, transposed):
    n = len(transposed)
    w_refs, x1_ref, o_refs = refs[:n], refs[n], refs[n + 1:]
    x = x_ref[...]
    x1 = _layer_norm(ALPHA * x + 0.5 * _swiglu(x.astype(BF16), wg, wu, wd), g_ref[...], b_ref[...])
    x1_ref[...] = x1
    xb = x1.astype(BF16)
    for w_ref, o_ref, tr in zip(w_refs, o_refs, transposed):
        out = _dot_nt(w_ref[...], xb) if tr else _dot(xb, w_ref[...])
        o_ref[...] = out.astype(o_ref.dtype)


def _ffn_inproj(x, wg, wu, wd, g, b, weights, dtypes, transposed, *, tm):
    t_rows, d = x.shape
    row = lambda w: pl.BlockSpec((tm, w), lambda i: (i, 0))
    out_specs, out_shape = [row(d)], [jax.ShapeDtypeStruct((t_rows, d), F32)]
    for w, dt, tr in zip(weights, dtypes, transposed):
        if tr:
            out_specs.append(pl.BlockSpec((w.shape[0], tm), lambda i: (0, i)))
            out_shape.append(jax.ShapeDtypeStruct((w.shape[0], t_rows), dt))
        else:
            out_specs.append(row(w.shape[1]))
            out_shape.append(jax.ShapeDtypeStruct((t_rows, w.shape[1]), dt))
    consts = [wg, wu, wd, g, b] + list(weights)
    outs = pl.pallas_call(
        functools.partial(_ffn_inproj_kernel, transposed=tuple(transposed)),
        grid=(t_rows // tm,),
        in_specs=[row(d)] + [_resident(c.shape) for c in consts],
        out_specs=out_specs,
        out_shape=out_shape,
        compiler_params=_params("parallel"),
        name="ffn_ln_in_proj",
    )(x, *consts)
    return outs[0], outs[1:]


def _mix_ffn_kernel(x_ref, da_ref, ssm_ref, sa_ref, p_ref, w1, w2, w3, g2_ref, b2_ref,
                    wg, wu, wd, wpp, wpg, g3_ref, b3_ref, o_ref):
    mix = _dot(da_ref[...], w1[...]) + _dot(ssm_ref[...], w2[...]) + _dot(sa_ref[...], w3[...])
    x = _layer_norm(ALPHA * x_ref[...] + mix, g2_ref[...], b2_ref[...])
    xb = x.astype(BF16)
    y = ALPHA * x + 0.5 * _swiglu(xb, wg, wu, wd)
    y = y + _dot(p_ref[...].astype(BF16), wpp[...]) * jax.nn.sigmoid(_dot(xb, wpg[...]))
    o_ref[...] = _layer_norm(y, g3_ref[...], b3_ref[...])


def _mix_ffn(x, o_da, o_ssm, o_sa, p, consts, *, tm):
    t_rows, d = x.shape
    row = lambda a: pl.BlockSpec((tm, a.shape[1]), lambda i: (i, 0))
    acts = [x, o_da, o_ssm, o_sa, p]
    return pl.pallas_call(
        _mix_ffn_kernel,
        grid=(t_rows // tm,),
        in_specs=[row(a) for a in acts] + [_resident(c.shape) for c in consts],
        out_specs=pl.BlockSpec((tm, d), lambda i: (i, 0)),
        out_shape=jax.ShapeDtypeStruct((t_rows, d), F32),
        compiler_params=_params("parallel"),
        name="out_proj_ln_ffn_ple_ln",
    )(*acts, *consts)


def _softmax_probs(s, m_ref, shift):
    m_prev = m_ref[...]
    m_cur = jnp.max(s, axis=0, keepdims=True)
    if shift is not None:
        m_cur = m_cur + shift
    m_next = jnp.maximum(m_prev, m_cur)
    m_ref[...] = m_next
    p = jnp.exp2(s - (m_next if shift is None else m_next - shift))
    return p.astype(BF16), jnp.exp2(m_prev - m_next)


def _accumulate(acc_ref, rescale, v_t, p):
    v_aug = jnp.concatenate([v_t, jnp.ones((BF16_ROWS, v_t.shape[1]), v_t.dtype)], axis=0)
    acc_ref[...] = rescale * acc_ref[...] + _dot(v_aug, p)


def _softmax_step(s, v_t, m_ref, acc_ref, shift):
    p, rescale = _softmax_probs(s, m_ref, shift)
    _accumulate(acc_ref, rescale, v_t, p)


def _far_chunks(n, scores, values, m_ref, acc_ref, shift):
    def body(i, carry):
        _softmax_step(scores(i), values(i), m_ref, acc_ref, shift)
        return carry

    lax.fori_loop(0, n, body, 0)


def _da_kernel(far_ref, lam_ref, q_ref, k_ref, v_ref, bn_ref, g_ref, o_ref,
               qd_ref, m_ref, acc_ref, *, t, post_scale):
    h = pl.program_id(1)
    qi = pl.program_id(2)
    q = q_ref[...].astype(F32)
    rowq = lax.broadcasted_iota(I32, q.shape, 0)
    qd_ref[:, 0:t] = jnp.where(rowq < DA_QK_DIM, q, 0.0).astype(BF16)
    qd_ref[:, t:2 * t] = jnp.where(rowq >= DA_QK_DIM, q, 0.0).astype(BF16)
    m_ref[...] = jnp.full(m_ref.shape, NEG, F32)
    acc_ref[...] = jnp.zeros(acc_ref.shape, F32)
    far = far_ref[h]

    def scores(kb):
        return _dot(k_ref[pl.ds(pl.multiple_of(kb * t, t), t), :], qd_ref[...])

    def values(kb):
        return v_ref[:, pl.ds(pl.multiple_of(kb * t, t), t)]

    def near_block(kb, near):
        bias = bn_ref[near]
        _softmax_step(scores(kb) + jnp.concatenate([bias, bias], axis=1), values(kb), m_ref, acc_ref, None)

    _far_chunks(jnp.maximum(qi - 1, 0), scores, values, m_ref, acc_ref, far)

    @pl.when(qi >= 1)
    def _():
        near_block(qi - 1, 1)

    near_block(qi, 0)

    acc = acc_ref[...]
    o = acc[0:DA_V_DIM] / acc[DA_V_DIM:DA_V_DIM + 1]
    o = o[:, 0:t] - lam_ref[0] * o[:, t:2 * t]
    ms = jnp.mean(o * o, axis=0, keepdims=True)
    o = o * lax.rsqrt(ms + LN_EPS) * g_ref[...] * post_scale
    o_ref[...] = o.T.astype(o_ref.dtype)


def _diff_attention(q_t, k, v_t, bias_near, bias_far, lam, subln_g, *, bsz, seq, t, post_scale):
    heads = q_t.shape[0] // DA_V_DIM
    nq = seq // t
    smem = pl.BlockSpec(memory_space=pltpu.SMEM)
    va = DA_V_DIM + BF16_ROWS
    return pl.pallas_call(
        functools.partial(_da_kernel, t=t, post_scale=post_scale),
        grid=(bsz, heads, nq),
        in_specs=[
            smem, smem,
            pl.BlockSpec((DA_V_DIM, t), lambda b, h, i: (h, b * nq + i)),
            pl.BlockSpec((None, seq, DA_V_DIM), lambda b, h, i: (b, 0, h)),
            pl.BlockSpec((DA_V_DIM, seq), lambda b, h, i: (h, b)),
            pl.BlockSpec((None, 2, t, t), lambda b, h, i: (h, 0, 0, 0)),
            pl.BlockSpec((DA_V_DIM, t), lambda b, h, i: (0, 0)),
        ],
        out_specs=pl.BlockSpec((None, t, DA_V_DIM), lambda b, h, i: (b, i, h)),
        out_shape=jax.ShapeDtypeStruct((bsz, seq, heads * DA_V_DIM), BF16),
        scratch_shapes=[
            pltpu.VMEM((DA_V_DIM, 2 * t), BF16),
            pltpu.VMEM((1, 2 * t), F32),
            pltpu.VMEM((va, 2 * t), F32),
        ],
        compiler_params=_params("parallel", "parallel", "arbitrary"),
        name="diff_attention",
    )(bias_far, lam, q_t, k, v_t, bias_near, subln_g)


def _dsa_kernel(far_ref, ixq_ref, ixw_ref, ixk_ref, saq_ref, sak_ref, sav_ref, bn_ref, o_ref,
                qi8_ref, keys_ref, hi_ref, qs4_ref, m_ref, acc_ref, *, t, topk):
    qi = pl.program_id(1)
    kf = float(topk)

    qi8_ref[...] = jnp.zeros(qi8_ref.shape, BF16)
    for h in range(IDX_HEADS):
        qi8_ref[0:IDX_DIM, h * t:(h + 1) * t] = ixq_ref[h * IDX_DIM:(h + 1) * IDX_DIM, :]

    def rows(kb, n=1):
        return pl.ds(pl.multiple_of(kb * t, t), n * t)

    def chunk(kb):
        return keys_ref[rows(kb), :]

    def index_keys(kb, n=1):
        kblk = ixk_ref[rows(kb, n), :]
        sc = None
        for h in range(IDX_HEADS):
            r = _dot(kblk, qi8_ref[:, h * t:(h + 1) * t])
            term = ixw_ref[h:h + 1, :] * jnp.maximum(r, 0.0)
            sc = term if sc is None else sc + term
        return jnp.where(jnp.abs(sc) < TINY, 0.0, sc)

    def store_keys(kb, sc, n=1):
        bits = lax.bitcast_convert_type(sc, I32)
        keys_ref[rows(kb, n), :] = bits ^ ((bits >> 31) & 0x7FFFFFFF)
        hi_ref[rows(kb, n), :] = lax.bitcast_convert_type(bits & HI_MASK, F32).astype(BF16)

    def in_pairs(count, step):
        def pair(j, carry):
            step(2 * j, 2)
            return carry
        lax.fori_loop(0, count // 2, pair, 0)

        @pl.when(count % 2 == 1)
        def _():
            step(count - 1, 1)

    in_pairs(qi, lambda kb, n: store_keys(kb, index_keys(kb, n), n))
    krow = lax.broadcasted_iota(I32, (t, t), 0)
    qcol = lax.broadcasted_iota(I32, (t, t), 1)
    store_keys(qi, jnp.where(krow <= qcol, index_keys(qi), -jnp.inf))

    def count(fn):
        def body(kb, cnt):
            return cnt + _fold_rows(fn(kb, chunk(kb)), SUBLANES)
        cnt = lax.fori_loop(0, qi + 1, body, jnp.zeros((SUBLANES, t), I32))
        return jnp.sum(cnt.astype(F32), axis=0, keepdims=True)

    def count_hi(thr, strict=False):
        def body(kb, cnt):
            half = hi_ref[rows(kb), :]
            hit = jnp.where(half > thr if strict else half >= thr, jnp.ones((), BF16), jnp.zeros((), BF16))
            return cnt + _fold_rows(hit, BF16_ROWS).astype(F32)
        cnt = lax.fori_loop(0, qi + 1, body, jnp.zeros((BF16_ROWS, t), F32))
        return jnp.sum(cnt, axis=0, keepdims=True)

    def bf16_of(pattern):
        return lax.bitcast_convert_type(jnp.left_shift(pattern, 16), F32).astype(BF16)

    def code_value(code):
        pattern = code ^ ((code >> 31) & 0x7FFF)
        return bf16_of(jnp.where(code > 0, jnp.maximum(pattern, BF16_MIN_NORMAL), pattern))

    def hi_bit(i, st):
        code, open_ = st
        cand = code + jnp.left_shift(jnp.int32(1), 15 - i)
        c = count_hi(code_value(cand))
        code = jnp.where(open_ > 0.0, jnp.where(c >= kf, cand, code), code)
        return code, jnp.where(c == kf, 0.0, open_)

    code, open_ = lax.fori_loop(0, 16, hi_bit, (jnp.full((1, t), -2 ** 15, I32), jnp.ones((1, t), F32)))
    zero_tie = jnp.where(code == 0, open_, 0.0)
    open_ = jnp.where(code == 0, 0.0, open_)

    value = code_value(code)
    need = kf - count_hi(value, strict=True)

    def recode(kb, carry):
        mid = ((chunk(kb) >> LOW_BITS) & MID_MASK) + BF16_MIN_NORMAL
        hi_ref[rows(kb), :] = jnp.where(hi_ref[rows(kb), :] == value, bf16_of(mid), jnp.zeros((), BF16))
        return carry

    lax.fori_loop(0, qi + 1, recode, 0)

    def mid_cond(st):
        i, _, _, n_open = st
        return jnp.logical_and(i < MID_BITS, n_open > 0.0)

    def mid_body(st):
        i, mid, open_, _ = st
        cand = mid + jnp.left_shift(jnp.int32(1), MID_BITS - 1 - i)
        c = count_hi(bf16_of(cand + BF16_MIN_NORMAL))
        mid = jnp.where(open_ > 0.0, jnp.where(c >= need, cand, mid), mid)
        open_ = jnp.where(c == need, 0.0, open_)
        return i + 1, mid, open_, jnp.max(open_)

    _, mid, open_, _ = lax.while_loop(
        mid_cond, mid_body, (jnp.int32(0), jnp.zeros((1, t), I32), open_, jnp.max(open_)))

    def search_cond(st):
        i, _, _, n_open = st
        return jnp.logical_and(i < 32, n_open > 0.0)

    def search_body(st):
        i, tau, open_, _ = st
        cand = tau + jnp.left_shift(jnp.int32(1), 31 - i)
        c = count(lambda kb, blk: jnp.where(blk >= cand, 1, 0))
        tau = jnp.where(open_ > 0.0, jnp.where(c >= kf, cand, tau), tau)
        open_ = jnp.where(c == kf, 0.0, open_)
        return i + 1, tau, open_, jnp.max(open_)

    _, tau, open_, _ = lax.while_loop(
        search_cond, search_body,
        (jnp.int32(32 - LOW_BITS), jnp.left_shift(code, 16) + jnp.left_shift(mid, LOW_BITS), open_,
         jnp.max(open_)))

    @pl.when(jnp.max(jnp.maximum(open_, zero_tie)) > 0.0)
    def _():
        need = kf - count(lambda kb, blk: jnp.where(blk > tau, 1, 0))
        below = jnp.where(qcol <= krow, 1.0, 0.0).astype(BF16)

        def retire(kb, seen):
            blk = chunk(kb)
            tied = jnp.where(blk == tau, 1.0, 0.0).astype(BF16)
            rank = _dot(below, tied) + seen
            keys_ref[rows(kb), :] = jnp.where(blk == tau, jnp.where(rank > need, INT_MIN, blk), blk)
            return rank[t - 1:t, :]

        lax.fori_loop(0, qi + 1, retire, jnp.zeros((1, t), F32))

    qs4_ref[...] = jnp.zeros(qs4_ref.shape, BF16)
    for h in range(SA_HEADS):
        qs4_ref[0:SA_HEAD_DIM, h * t:(h + 1) * t] = saq_ref[h * SA_HEAD_DIM:(h + 1) * SA_HEAD_DIM, :]
    m_ref[...] = jnp.full(m_ref.shape, NEG, F32)
    acc_ref[...] = jnp.zeros(acc_ref.shape, F32)

    def scores(kb, near=None, n=1):
        s = _dot(sak_ref[rows(kb, n), :], qs4_ref[...])
        sel = keys_ref[rows(kb, n), :] >= tau
        parts = []
        for h in range(SA_HEADS):
            sh = s[:, h * t:(h + 1) * t]
            if near is not None:
                sh = sh + bn_ref[h, near]
            parts.append(jnp.where(sel, sh, NEG))
        return jnp.concatenate(parts, axis=1)

    far = jnp.concatenate([jnp.full((1, t), far_ref[h], F32) for h in range(SA_HEADS)], axis=1)

    def values(kb, n=1):
        return sav_ref[:, rows(kb, n)]

    in_pairs(jnp.maximum(qi - 1, 0),
             lambda kb, n: _softmax_step(scores(kb, None, n), values(kb, n), m_ref, acc_ref, far))

    @pl.when(qi >= 1)
    def _():
        _softmax_step(scores(qi - 1, 1), values(qi - 1), m_ref, acc_ref, None)

    _softmax_step(scores(qi, 0), values(qi), m_ref, acc_ref, None)

    acc = acc_ref[...]
    o = acc[0:SA_HEAD_DIM] / acc[SA_HEAD_DIM:SA_HEAD_DIM + 1]
    o = jnp.concatenate([o[:, h * t:(h + 1) * t] for h in range(SA_HEADS)], axis=0)
    o_ref[...] = o.T.astype(o_ref.dtype)


def _sparse_attention(ixq_t, ixw_t, ixk, saq_t, sak, sav_t, bias_near, bias_far, *, bsz, seq, t, topk):
    nq = seq // t
    smem = pl.BlockSpec(memory_space=pltpu.SMEM)
    qcols = lambda r: pl.BlockSpec((r, t), lambda b, i: (0, b * nq + i))
    full = lambda w: pl.BlockSpec((None, seq, w), lambda b, i: (b, 0, 0), pipeline_mode=pl.Buffered(1))
    va = SA_HEAD_DIM + BF16_ROWS
    return pl.pallas_call(
        functools.partial(_dsa_kernel, t=t, topk=topk),
        grid=(bsz, nq),
        in_specs=[
            smem,
            qcols(IDX_HEADS * IDX_DIM), qcols(IDX_HEADS), full(LANES),
            qcols(SA_WIDTH), full(LANES),
            pl.BlockSpec((SA_HEAD_DIM, seq), lambda b, i: (0, b), pipeline_mode=pl.Buffered(1)),
            _resident((SA_HEADS, 2, t, t)),
        ],
        out_specs=pl.BlockSpec((None, t, SA_WIDTH), lambda b, i: (b, i, 0)),
        out_shape=jax.ShapeDtypeStruct((bsz, seq, SA_WIDTH), BF16),
        scratch_shapes=[
            pltpu.VMEM((LANES, IDX_HEADS * t), BF16),
            pltpu.VMEM((seq, t), I32),
            pltpu.VMEM((seq, t), BF16),
            pltpu.VMEM((LANES, SA_HEADS * t), BF16),
            pltpu.VMEM((1, SA_HEADS * t), F32),
            pltpu.VMEM((va, SA_HEADS * t), F32),
        ],
        compiler_params=_params("parallel", "arbitrary"),
        name="sparse_attention",
    )(bias_far, ixq_t, ixw_t, ixk, saq_t, sak, sav_t, bias_near)


def _ssm_state_kernel(u_ref, w_ref, o_ref):
    o_ref[...] = _dot(u_ref[...], w_ref[...])


def _ssm_scan_kernel(loc_ref, a1_ref, a2_ref, o_ref):
    a1 = a1_ref[...]
    a2 = a2_ref[...]

    def body(n, s):
        o_ref[n] = s
        return a1 * s + a2 * pltpu.roll(s, SSM_STATE, axis=1) + loc_ref[n]

    lax.fori_loop(0, loc_ref.shape[0], body, jnp.zeros(a1.shape, F32))


def _ssm_out_kernel(u_ref, s_ref, toep_ref, wout_ref, o_ref):
    o_ref[...] = (_dot(u_ref[...], toep_ref[...])
                  + _dot(s_ref[...].astype(BF16), wout_ref[...]))


def _ssm_gate_kernel(y_ref, u_ref, d_ref, w_ref, o_ref):
    y = jax.nn.gelu(y_ref[...] + d_ref[...] * u_ref[...])
    o_ref[...] = (y * jax.nn.sigmoid(_dot(y.astype(BF16), w_ref[...]))).astype(o_ref.dtype)


def _ssm_tables(lam_re, lam_im, log_dt, b_re, b_im, c_re, c_im, tc):
    hp = lax.Precision.HIGHEST
    dt = jnp.exp(log_dt)[:, None]
    n = jnp.arange(tc + 1, dtype=F32)[:, None, None]
    mag = jnp.exp(lam_re * dt * n)
    pw_re = mag * jnp.cos(lam_im * dt * n)
    pw_im = mag * jnp.sin(lam_im * dt * n)
    den = lam_re * lam_re + lam_im * lam_im
    nr, ni = pw_re[1] - 1.0, pw_im[1]
    f_re = (nr * lam_re + ni * lam_im) / den
    f_im = (ni * lam_re - nr * lam_im) / den
    bb_re = f_re[..., None] * b_re - f_im[..., None] * b_im
    bb_im = f_re[..., None] * b_im + f_im[..., None] * b_re
    ca_re = c_re[None] * pw_re[:, :, None, :] - c_im[None] * pw_im[:, :, None, :]
    ca_im = c_re[None] * pw_im[:, :, None, :] + c_im[None] * pw_re[:, :, None, :]
    groups = lam_re.shape[0]
    inv_mag = jnp.exp(-lam_re * dt * n[:tc])
    iw_re = inv_mag * jnp.cos(lam_im * dt * n[:tc])
    iw_im = -inv_mag * jnp.sin(lam_im * dt * n[:tc])
    l_re = iw_re[..., None] * bb_re[None] - iw_im[..., None] * bb_im[None]
    l_im = iw_re[..., None] * bb_im[None] + iw_im[..., None] * bb_re[None]
    left = jnp.concatenate([l_re, l_im], axis=2).transpose(1, 0, 3, 2)
    right = jnp.concatenate([ca_re[:tc], -ca_im[:tc]], axis=3).transpose(1, 0, 2, 3)
    cw = tc * SSM_GROUP
    toep = jnp.einsum('gik,gjk->gij', left.reshape(groups, cw, 2 * SSM_STATE),
                      right.reshape(groups, cw, 2 * SSM_STATE), precision=hp)
    step = jnp.arange(cw) // SSM_GROUP
    toep = jnp.where(step[:, None] <= step[None, :], toep, 0.0)
    rev_re, rev_im = pw_re[tc - 1::-1][:tc], pw_im[tc - 1::-1][:tc]
    ws_re = rev_re[..., None] * bb_re[None] - rev_im[..., None] * bb_im[None]
    ws_im = rev_re[..., None] * bb_im[None] + rev_im[..., None] * bb_re[None]
    wstate = jnp.concatenate([ws_re, ws_im], axis=2)
    wstate = wstate.transpose(1, 0, 3, 2).reshape(groups, tc * SSM_GROUP, 2 * SSM_STATE)
    wout = jnp.concatenate([ca_re[1:], -ca_im[1:]], axis=3)
    wout = wout.transpose(1, 3, 0, 2).reshape(groups, 2 * SSM_STATE, tc * SSM_GROUP)
    dec_re, dec_im = pw_re[tc], pw_im[tc]
    a1 = jnp.concatenate([dec_re, dec_re], axis=1)
    a2 = jnp.concatenate([-dec_im, dec_im], axis=1)
    return toep.astype(BF16), wstate.astype(BF16), wout.astype(BF16), a1, a2


def _ssm(u, tables, d_skip, w_glu, *, bsz, seq, tm):
    toep, wstate, wout, a1, a2 = tables
    groups = toep.shape[0]
    tc = SSM_CHUNK
    nc = seq // tc
    rows = bsz * nc
    cw = tc * SSM_GROUP
    ug = u.astype(BF16).reshape(bsz, nc, tc, groups, SSM_GROUP).transpose(3, 0, 1, 2, 4).reshape(groups, rows, cw)
    gspec = lambda r, c: pl.BlockSpec((None, r, c), lambda g: (g, 0, 0))
    loc = pl.pallas_call(
        _ssm_state_kernel,
        grid=(groups,),
        in_specs=[gspec(rows, cw), gspec(cw, 2 * SSM_STATE)],
        out_specs=gspec(rows, 2 * SSM_STATE),
        out_shape=jax.ShapeDtypeStruct((groups, rows, 2 * SSM_STATE), F32),
        compiler_params=_params("parallel"),
        name="ssm_chunk_state",
    )(ug, wstate)
    gb = groups * bsz
    loc_t = loc.reshape(groups, bsz, nc, 2 * SSM_STATE).transpose(2, 0, 1, 3).reshape(nc, gb, 2 * SSM_STATE)
    a1r = jnp.repeat(a1, bsz, axis=0)
    a2r = jnp.repeat(a2, bsz, axis=0)
    rb = gb
    prev = pl.pallas_call(
        _ssm_scan_kernel,
        grid=(gb // rb,),
        in_specs=[pl.BlockSpec((nc, rb, 2 * SSM_STATE), lambda i: (0, i, 0)),
                  pl.BlockSpec((rb, 2 * SSM_STATE), lambda i: (i, 0)),
                  pl.BlockSpec((rb, 2 * SSM_STATE), lambda i: (i, 0))],
        out_specs=pl.BlockSpec((nc, rb, 2 * SSM_STATE), lambda i: (0, i, 0)),
        out_shape=jax.ShapeDtypeStruct((nc, gb, 2 * SSM_STATE), F32),
        compiler_params=_params("parallel"),
        name="ssm_chunk_scan",
    )(loc_t, a1r, a2r)
    prev_g = prev.reshape(nc, groups, bsz, 2 * SSM_STATE).transpose(1, 2, 0, 3).reshape(groups, rows, 2 * SSM_STATE)
    y = pl.pallas_call(
        _ssm_out_kernel,
        grid=(groups,),
        in_specs=[gspec(rows, cw), gspec(rows, 2 * SSM_STATE), gspec(cw, cw), gspec(2 * SSM_STATE, cw)],
        out_specs=gspec(rows, cw),
        out_shape=jax.ShapeDtypeStruct((groups, rows, cw), F32),
        compiler_params=_params("parallel"),
        name="ssm_chunk_out",
    )(ug, prev_g, toep, wout)
    y = y.reshape(groups, bsz, nc, tc, SSM_GROUP).transpose(1, 2, 3, 0, 4).reshape(bsz * seq, groups * SSM_GROUP)
    width = groups * SSM_GROUP
    return pl.pallas_call(
        _ssm_gate_kernel,
        grid=(bsz * seq // tm,),
        in_specs=[pl.BlockSpec((tm, width), lambda i: (i, 0)),
                  pl.BlockSpec((tm, width), lambda i: (i, 0)),
                  _resident((1, width)), _resident((width, width))],
        out_specs=pl.BlockSpec((tm, width), lambda i: (i, 0)),
        out_shape=jax.ShapeDtypeStruct((bsz * seq, width), BF16),
        compiler_params=_params("parallel"),
        name="ssm_gate",
    )(y, u, d_skip, w_glu)


def _t5_bucket(n):
    max_exact = REL_BUCKETS // 2
    nf = jnp.maximum(n, 1).astype(F32)
    large = max_exact + (jnp.log(nf / max_exact) / math.log(REL_MAX_DIST / max_exact)
                         * (REL_BUCKETS - max_exact)).astype(I32)
    large = jnp.minimum(large, REL_BUCKETS - 1)
    return jnp.where(n < max_exact, n, large)


def _bias_tiles(table, t):
    assert t >= REL_MAX_DIST
    table = table.astype(F32) * LOG2E
    key = jnp.arange(t)[:, None]
    query = jnp.arange(t)[None, :]
    dist = jnp.stack([query - key, t + query - key])
    bucket = _t5_bucket(jnp.maximum(dist, 0))
    tiles = jnp.zeros((table.shape[1],) + dist.shape, F32)
    for b in range(REL_BUCKETS):
        tiles = jnp.where(bucket[None] == b, table[b][:, None, None, None], tiles)
    return jnp.where(dist[None] >= 0, tiles, NEG), table[REL_BUCKETS - 1]


def _split_w_in(w_in):
    offs = [0]
    for s in IN_SIZES:
        offs.append(offs[-1] + s)
    da_q, da_k, da_v, ssm_u, sa_q, sa_k, sa_v, ix_q, ix_k, ix_w = (
        w_in[:, offs[j]:offs[j + 1]] for j in range(len(IN_SIZES)))
    w_scale = IDX_HEADS ** -0.5 * IDX_DIM ** -0.5
    pad = lambda w: jnp.pad(w, ((0, 0), (0, LANES - w.shape[1])))
    weights = [(da_q * (DA_QK_DIM ** -0.5 * LOG2E)).T, da_k, da_v.T, ssm_u,
               (sa_q * (SA_HEAD_DIM ** -0.5 * LOG2E)).T, pad(sa_k), sa_v.T,
               ix_q.T, pad(ix_k), (ix_w * w_scale).T]
    dtypes = [BF16, BF16, BF16, F32, BF16, BF16, BF16, BF16, BF16, F32]
    transposed = [True, False, True, False, True, False, True, True, False, True]
    return [w.astype(BF16) for w in weights], dtypes, transposed


def _plan(bsz, seq):
    rows = bsz * seq
    tm = 512 if rows % 512 == 0 else rows
    t_da = 1024 if seq % 1024 == 0 and seq >= 4096 else 256 if seq % 256 == 0 else 128
    t_sa = 512 if seq % 512 == 0 and seq >= 2048 else 256 if seq % 256 == 0 else 128
    tm_gate = 2048 if rows % 2048 == 0 else tm
    return dict(tm=tm, tm_gate=tm_gate, t_da=t_da, t_sa=t_sa)


def kernel(x, p, rel_bias, ffn1_w_gate, ffn1_w_up, ffn1_w_down, ln1_g, ln1_b, w_in, w_o, da_lam_q1, da_lam_k1, da_lam_q2, da_lam_k2, da_subln_g, ssm_lam_re, ssm_lam_im, ssm_log_dt, ssm_b_re, ssm_b_im, ssm_c_re, ssm_c_im, ssm_d, ssm_w_glu, ln2_g, ln2_b, ffn2_w_gate, ffn2_w_up, ffn2_w_down, ple_w_proj, ple_w_gate, ln3_g, ln3_b):
    bsz, seq, d = x.shape
    rows = bsz * seq
    plan = _plan(bsz, seq)
    tm, t_da, t_sa = plan["tm"], plan["t_da"], plan["t_sa"]
    topk = min(TOPK_MAX, seq // 4)
    da_near, da_far = _bias_tiles(rel_bias[:, :DA_HEADS], t_da)
    sa_near, sa_far = _bias_tiles(rel_bias[:, DA_HEADS:], t_sa)
    vec = lambda a: a.reshape(1, -1).astype(F32)
    da_w = DA_HEADS * DA_V_DIM

    h = x.reshape(rows, d)
    for i in range(DEPTH):
        lam_init = 0.8 - 0.6 * math.exp(-0.3 * i)
        weights, dtypes, transposed = _split_w_in(w_in[i])
        h, (da_q, da_k, da_v, ssm_u, sa_q, sa_k, sa_v, ix_q, ix_k, ix_w) = _ffn_inproj(
            h, ffn1_w_gate[i].astype(BF16), ffn1_w_up[i].astype(BF16), ffn1_w_down[i].astype(BF16),
            vec(ln1_g[i]), vec(ln1_b[i]), weights, dtypes, transposed, tm=tm)
        b3 = lambda a: a.reshape(bsz, seq, a.shape[-1])

        lam = (jnp.exp(jnp.sum(da_lam_q1[i].astype(F32) * da_lam_k1[i]))
               - jnp.exp(jnp.sum(da_lam_q2[i].astype(F32) * da_lam_k2[i])) + lam_init)
        subln = jnp.broadcast_to(da_subln_g[i].astype(F32)[:, None], (DA_V_DIM, t_da))
        o_da = _diff_attention(da_q, b3(da_k), da_v, da_near, da_far,
                               lam.reshape(1).astype(F32), subln, bsz=bsz, seq=seq, t=t_da,
                               post_scale=1.0 - lam_init)

        tables = _ssm_tables(ssm_lam_re[i].astype(F32), ssm_lam_im[i].astype(F32), ssm_log_dt[i].astype(F32),
                             ssm_b_re[i].astype(F32), ssm_b_im[i].astype(F32),
                             ssm_c_re[i].astype(F32), ssm_c_im[i].astype(F32), SSM_CHUNK)
        o_ssm = _ssm(ssm_u, tables, vec(ssm_d[i]), ssm_w_glu[i].astype(BF16), bsz=bsz, seq=seq,
                     tm=plan["tm_gate"])

        o_sa = _sparse_attention(ix_q, ix_w, b3(ix_k), sa_q, b3(sa_k), sa_v, sa_near, sa_far,
                                 bsz=bsz, seq=seq, t=t_sa, topk=topk)

        wo = w_o[i].astype(BF16)
        h = _mix_ffn(h, o_da.reshape(rows, da_w), o_ssm, o_sa.reshape(rows, SA_WIDTH), p[i].reshape(rows, -1),
                     [wo[:da_w], wo[da_w:da_w + SSM_WIDTH], wo[da_w + SSM_WIDTH:], vec(ln2_g[i]), vec(ln2_b[i]),
                      ffn2_w_gate[i].astype(BF16), ffn2_w_up[i].astype(BF16), ffn2_w_down[i].astype(BF16),
                      ple_w_proj[i].astype(BF16), ple_w_gate[i].astype(BF16), vec(ln3_g[i]), vec(ln3_b[i])],
                     tm=tm)
    return h.reshape(bsz, seq, d)
```

```python
import functools
import math

import jax
import jax.numpy as jnp
from jax import lax
from jax.experimental import pallas as pl
from jax.experimental.pallas import tpu as pltpu

F32 = jnp.float32
BF16 = jnp.bfloat16
I32 = jnp.int32

DEPTH = 2
DA_QK_DIM = 64
DA_V_DIM = 2 * DA_QK_DIM
DA_HEADS = 4
SSM_GROUP = 16
SSM_GROUPS = 16
SSM_STATE = 64
SSM_WIDTH = SSM_GROUP * SSM_GROUPS
SA_HEAD_DIM = 64
SA_HEADS = 4
SA_WIDTH = SA_HEADS * SA_HEAD_DIM
IDX_HEADS = 8
IDX_DIM = 32
TOPK_MAX = 256
REL_BUCKETS = 32
REL_MAX_DIST = 128
ALPHA = (2 * DEPTH) ** 0.25
LN_EPS = 1e-5
IN_SIZES = (DA_HEADS * 2 * DA_QK_DIM, DA_HEADS * 2 * DA_QK_DIM, DA_HEADS * DA_V_DIM,
            SSM_WIDTH, SA_WIDTH, SA_HEAD_DIM, SA_HEAD_DIM,
            IDX_HEADS * IDX_DIM, IDX_DIM, IDX_HEADS)

LANES = 128
SUBLANES = 8
BF16_ROWS = 16
NEG = -1e30
INT_MIN = -2 ** 31
HI_MASK = -2 ** 16
TINY = 2.0 ** -126
BF16_MIN_NORMAL = 0x0080
LOW_BITS = 2
MID_BITS = 16 - LOW_BITS
MID_MASK = 2 ** MID_BITS - 1
LOG2E = math.log2(math.e)
VMEM_LIMIT = 56 * 1024 * 1024
SSM_CHUNK = 64

def _params(*sem):
    return pltpu.CompilerParams(dimension_semantics=sem, vmem_limit_bytes=VMEM_LIMIT)


def _resident(shape):
    return pl.BlockSpec(shape, lambda *_: (0,) * len(shape), pipeline_mode=pl.Buffered(1))


def _dot(a, b):
    return jnp.dot(a, b, preferred_element_type=F32)


def _dot_nt(a, b):
    return lax.dot_general(a, b, (((1,), (1,)), ((), ())), preferred_element_type=F32)


def _layer_norm(y, g, b):
    mu = jnp.mean(y, axis=-1, keepdims=True)
    d = y - mu
    var = jnp.mean(d * d, axis=-1, keepdims=True)
    return d * lax.rsqrt(var + LN_EPS) * g + b


def _fold_rows(x, rows):
    while x.shape[0] > rows:
        half = x.shape[0] // 2
        x = x[:half] + x[half:]
    return x


FF_CHUNK = 256


def _swiglu(xb, wg, wu, wd):
    d_ff = wg.shape[1]
    acc = None
    for c0 in range(0, d_ff, FF_CHUNK):
        c1 = min(c0 + FF_CHUNK, d_ff)
        gate = _dot(xb, wg[:, c0:c1])
        up = _dot(xb, wu[:, c0:c1])
        hid = (gate * jax.nn.sigmoid(gate) * up).astype(BF16)
        part = _dot(hid, wd[c0:c1, :])
        acc = part if acc is None else acc + part
    return acc


def _ffn_inproj_kernel(x_ref, wg, wu, wd, g_ref, b_ref, *refs, transposed):
    n = len(transposed)
    w_refs, x1_ref, o_refs = refs[:n], refs[n], refs[n + 1:]
    x = x_ref[...]
    x1 = _layer_norm(ALPHA * x + 0.5 * _swiglu(x.astype(BF16), wg, wu, wd), g_ref[...], b_ref[...])
    x1_ref[...] = x1
    xb = x1.astype(BF16)
    for w_ref, o_ref, tr in zip(w_refs, o_refs, transposed):
        out = _dot_nt(w_ref[...], xb) if tr else _dot(xb, w_ref[...])
        o_ref[...] = out.astype(o_ref.dtype)


def _ffn_inproj(x, wg, wu, wd, g, b, weights, dtypes, transposed, *, tm):
    t_rows, d = x.shape
    row = lambda w: pl.BlockSpec((tm, w), lambda i: (i, 0))
    out_specs, out_shape = [row(d)], [jax.ShapeDtypeStruct((t_rows, d), F32)]
    for w, dt, tr in zip(weights, dtypes, transposed):
        if tr:
            out_specs.append(pl.BlockSpec((w.shape[0], tm), lambda i: (0, i)))
            out_shape.append(jax.ShapeDtypeStruct((w.shape[0], t_rows), dt))
        else:
            out_specs.append(row(w.shape[1]))
            out_shape.append(jax.ShapeDtypeStruct((t_rows, w.shape[1]), dt))
    consts = [wg, wu, wd, g, b] + list(weights)
    outs = pl.pallas_call(
        functools.partial(_ffn_inproj_kernel, transposed=tuple(transposed)),
        grid=(t_rows // tm,),
        in_specs=[row(d)] + [_resident(c.shape) for c in consts],
        out_specs=out_specs,
        out_shape=out_shape,
        compiler_params=_params("parallel"),
        name="ffn_ln_in_proj",
    )(x, *consts)
    return outs[0], outs[1:]


def _mix_ffn_kernel(x_ref, da_ref, ssm_ref, sa_ref, p_ref, w1, w2, w3, g2_ref, b2_ref,
                    wg, wu, wd, wpp, wpg, g3_ref, b3_ref, o_ref):
    mix = _dot(da_ref[...], w1[...]) + _dot(ssm_ref[...], w2[...]) + _dot(sa_ref[...], w3[...])
    x = _layer_norm(ALPHA * x_ref[...] + mix, g2_ref[...], b2_ref[...])
    xb = x.astype(BF16)
    y = ALPHA * x + 0.5 * _swiglu(xb, wg, wu, wd)
    y = y + _dot(p_ref[...].astype(BF16), wpp[...]) * jax.nn.sigmoid(_dot(xb, wpg[...]))
    o_ref[...] = _layer_norm(y, g3_ref[...], b3_ref[...])


def _mix_ffn(x, o_da, o_ssm, o_sa, p, consts, *, tm):
    t_rows, d = x.shape
    row = lambda a: pl.BlockSpec((tm, a.shape[1]), lambda i: (i, 0))
    acts = [x, o_da, o_ssm, o_sa, p]
    return pl.pallas_call(
        _mix_ffn_kernel,
        grid=(t_rows // tm,),
        in_specs=[row(a) for a in acts] + [_resident(c.shape) for c in consts],
        out_specs=pl.BlockSpec((tm, d), lambda i: (i, 0)),
        out_shape=jax.ShapeDtypeStruct((t_rows, d), F32),
        compiler_params=_params("parallel"),
        name="out_proj_ln_ffn_ple_ln",
    )(*acts, *consts)


def _softmax_probs(s, m_ref, shift):
    m_prev = m_ref[...]
    m_cur = jnp.max(s, axis=0, keepdims=True)
    if shift is not None:
        m_cur = m_cur + shift
    m_next = jnp.maximum(m_prev, m_cur)
    m_ref[...] = m_next
    p = jnp.exp2(s - (m_next if shift is None else m_next - shift))
    return p.astype(BF16), jnp.exp2(m_prev - m_next)


def _accumulate(acc_ref, rescale, v_t, p):
    v_aug = jnp.concatenate([v_t, jnp.ones((BF16_ROWS, v_t.shape[1]), v_t.dtype)], axis=0)
    acc_ref[...] = rescale * acc_ref[...] + _dot(v_aug, p)


def _softmax_step(s, v_t, m_ref, acc_ref, shift):
    p, rescale = _softmax_probs(s, m_ref, shift)
    _accumulate(acc_ref, rescale, v_t, p)


def _far_chunks(n, scores, values, m_ref, acc_ref, shift):
    def body(i, carry):
        _softmax_step(scores(i), values(i), m_ref, acc_ref, shift)
        return carry

    lax.fori_loop(0, n, body, 0)


def _da_kernel(far_ref, lam_ref, q_ref, k_ref, v_ref, bn_ref, g_ref, o_ref,
               qd_ref, m_ref, acc_ref, *, t, post_scale):
    h = pl.program_id(1)
    qi = pl.program_id(2)
    q = q_ref[...].astype(F32)
    rowq = lax.broadcasted_iota(I32, q.shape, 0)
    qd_ref[:, 0:t] = jnp.where(rowq < DA_QK_DIM, q, 0.0).astype(BF16)
    qd_ref[:, t:2 * t] = jnp.where(rowq >= DA_QK_DIM, q, 0.0).astype(BF16)
    m_ref[...] = jnp.full(m_ref.shape, NEG, F32)
    acc_ref[...] = jnp.zeros(acc_ref.shape, F32)
    far = far_ref[h]

    def scores(kb):
        return _dot(k_ref[pl.ds(pl.multiple_of(kb * t, t), t), :], qd_ref[...])

    def values(kb):
        return v_ref[:, pl.ds(pl.multiple_of(kb * t, t), t)]

    def near_block(kb, near):
        bias = bn_ref[near]
        _softmax_step(scores(kb) + jnp.concatenate([bias, bias], axis=1), values(kb), m_ref, acc_ref, None)

    _far_chunks(jnp.maximum(qi - 1, 0), scores, values, m_ref, acc_ref, far)

    @pl.when(qi >= 1)
    def _():
        near_block(qi - 1, 1)

    near_block(qi, 0)

    acc = acc_ref[...]
    o = acc[0:DA_V_DIM] / acc[DA_V_DIM:DA_V_DIM + 1]
    o = o[:, 0:t] - lam_ref[0] * o[:, t:2 * t]
    ms = jnp.mean(o * o, axis=0, keepdims=True)
    o = o * lax.rsqrt(ms + LN_EPS) * g_ref[...] * post_scale
    o_ref[...] = o.T.astype(o_ref.dtype)


def _diff_attention(q_t, k, v_t, bias_near, bias_far, lam, subln_g, *, bsz, seq, t, post_scale):
    heads = q_t.shape[0] // DA_V_DIM
    nq = seq // t
    smem = pl.BlockSpec(memory_space=pltpu.SMEM)
    va = DA_V_DIM + BF16_ROWS
    return pl.pallas_call(
        functools.partial(_da_kernel, t=t, post_scale=post_scale),
        grid=(bsz, heads, nq),
        in_specs=[
            smem, smem,
            pl.BlockSpec((DA_V_DIM, t), lambda b, h, i: (h, b * nq + i)),
            pl.BlockSpec((None, seq, DA_V_DIM), lambda b, h, i: (b, 0, h)),
            pl.BlockSpec((DA_V_DIM, seq), lambda b, h, i: (h, b)),
            pl.BlockSpec((None, 2, t, t), lambda b, h, i: (h, 0, 0, 0)),
            pl.BlockSpec((DA_V_DIM, t), lambda b, h, i: (0, 0)),
        ],
        out_specs=pl.BlockSpec((None, t, DA_V_DIM), lambda b, h, i: (b, i, h)),
        out_shape=jax.ShapeDtypeStruct((bsz, seq, heads * DA_V_DIM), BF16),
        scratch_shapes=[
            pltpu.VMEM((DA_V_DIM, 2 * t), BF16),
            pltpu.VMEM((1, 2 * t), F32),
            pltpu.VMEM((va, 2 * t), F32),
        ],
        compiler_params=_params("parallel", "parallel", "arbitrary"),
        name="diff_attention",
    )(bias_far, lam, q_t, k, v_t, bias_near, subln_g)


def _dsa_kernel(far_ref, ixq_ref, ixw_ref, ixk_ref, saq_ref, sak_ref, sav_ref, bn_ref, o_ref,
                qi8_ref, keys_ref, hi_ref, qs4_ref, m_ref, acc_ref, *, t, topk):
    qi = pl.program_id(1)
    kf = float(topk)

    qi8_ref[...] = jnp.zeros(qi8_ref.shape, BF16)
    for h in range(IDX_HEADS):
        qi8_ref[0:IDX_DIM, h * t:(h + 1) * t] = ixq_ref[h * IDX_DIM:(h + 1) * IDX_DIM, :]

    def rows(kb, n=1):
        return pl.ds(pl.multiple_of(kb * t, t), n * t)

    def chunk(kb):
        return keys_ref[rows(kb), :]

    def index_keys(kb, n=1):
        kblk = ixk_ref[rows(kb, n), :]
        sc = None
        for h in range(IDX_HEADS):
            r = _dot(kblk, qi8_ref[:, h * t:(h + 1) * t])
            term = ixw_ref[h:h + 1, :] * jnp.maximum(r, 0.0)
            sc = term if sc is None else sc + term
        return jnp.where(jnp.abs(sc) < TINY, 0.0, sc)

    def store_keys(kb, sc, n=1):
        bits = lax.bitcast_convert_type(sc, I32)
        keys_ref[rows(kb, n), :] = bits ^ ((bits >> 31) & 0x7FFFFFFF)
        hi_ref[rows(kb, n), :] = lax.bitcast_convert_type(bits & HI_MASK, F32).astype(BF16)

    def in_pairs(count, step):
        def pair(j, carry):
            step(2 * j, 2)
            return carry
        lax.fori_loop(0, count // 2, pair, 0)

        @pl.when(count % 2 == 1)
        def _():
            step(count - 1, 1)

    in_pairs(qi, lambda kb, n: store_keys(kb, index_keys(kb, n), n))
    krow = lax.broadcasted_iota(I32, (t, t), 0)
    qcol = lax.broadcasted_iota(I32, (t, t), 1)
    store_keys(qi, jnp.where(krow <= qcol, index_keys(qi), -jnp.inf))

    def count(fn):
        def body(kb, cnt):
            return cnt + _fold_rows(fn(kb, chunk(kb)), SUBLANES)
        cnt = lax.fori_loop(0, qi + 1, body, jnp.zeros((SUBLANES, t), I32))
        return jnp.sum(cnt.astype(F32), axis=0, keepdims=True)

    def count_hi(thr):
        def body(kb, cnt):
            hit = jnp.where(hi_ref[rows(kb), :] >= thr, jnp.ones((), BF16), jnp.zeros((), BF16))
            return cnt + _fold_rows(hit, BF16_ROWS).astype(F32)
        cnt = lax.fori_loop(0, qi + 1, body, jnp.zeros((BF16_ROWS, t), F32))
        return jnp.sum(cnt, axis=0, keepdims=True)

    def bf16_of(pattern):
        return lax.bitcast_convert_type(jnp.left_shift(pattern, 16), F32).astype(BF16)

    def code_value(code):
        pattern = code ^ ((code >> 31) & 0x7FFF)
        return bf16_of(jnp.where(code > 0, jnp.maximum(pattern, BF16_MIN_NORMAL), pattern))

    def hi_bit(i, st):
        code, open_, above = st
        cand = code + jnp.left_shift(jnp.int32(1), 15 - i)
        c = count_hi(code_value(cand))
        code = jnp.where(open_ > 0.0, jnp.where(c >= kf, cand, code), code)
        above = jnp.where(open_ > 0.0, jnp.where(c >= kf, above, c), above)
        return code, jnp.where(c == kf, 0.0, open_), above

    code, open_, above = lax.fori_loop(
        0, 16, hi_bit, (jnp.full((1, t), -2 ** 15, I32), jnp.ones((1, t), F32), jnp.zeros((1, t), F32)))
    zero_tie = jnp.where(code == 0, open_, 0.0)
    open_ = jnp.where(code == 0, 0.0, open_)

    value = code_value(code)
    higher = above
    need = kf - higher

    def recode(kb, carry):
        mid = ((chunk(kb) >> LOW_BITS) & MID_MASK) + BF16_MIN_NORMAL
        hi_ref[rows(kb), :] = jnp.where(hi_ref[rows(kb), :] == value, bf16_of(mid), jnp.zeros((), BF16))
        return carry

    lax.fori_loop(0, qi + 1, recode, 0)

    def mid_cond(st):
        i, _, _, _, n_open = st
        return jnp.logical_and(i < MID_BITS, n_open > 0.0)

    def mid_body(st):
        i, mid, open_, above, _ = st
        cand = mid + jnp.left_shift(jnp.int32(1), MID_BITS - 1 - i)
        c = count_hi(bf16_of(cand + BF16_MIN_NORMAL))
        mid = jnp.where(open_ > 0.0, jnp.where(c >= need, cand, mid), mid)
        above = jnp.where(open_ > 0.0, jnp.where(c >= need, above, higher + c), above)
        open_ = jnp.where(c == need, 0.0, open_)
        return i + 1, mid, open_, above, jnp.max(open_)

    _, mid, open_, above, _ = lax.while_loop(
        mid_cond, mid_body, (jnp.int32(0), jnp.zeros((1, t), I32), open_, above, jnp.max(open_)))

    def search_cond(st):
        i, _, _, _, n_open = st
        return jnp.logical_and(i < 32, n_open > 0.0)

    def search_body(st):
        i, tau, open_, above, _ = st
        cand = tau + jnp.left_shift(jnp.int32(1), 31 - i)
        c = count(lambda kb, blk: jnp.where(blk >= cand, 1, 0))
        tau = jnp.where(open_ > 0.0, jnp.where(c >= kf, cand, tau), tau)
        above = jnp.where(open_ > 0.0, jnp.where(c >= kf, above, c), above)
        open_ = jnp.where(c == kf, 0.0, open_)
        return i + 1, tau, open_, above, jnp.max(open_)

    _, tau, open_, above, _ = lax.while_loop(
        search_cond, search_body,
        (jnp.int32(32 - LOW_BITS), jnp.left_shift(code, 16) + jnp.left_shift(mid, LOW_BITS), open_, above,
         jnp.max(open_)))

    @pl.when(jnp.max(jnp.maximum(open_, zero_tie)) > 0.0)
    def _():
        need = kf - above
        below = jnp.where(qcol <= krow, 1.0, 0.0).astype(BF16)

        def retire(kb, seen):
            blk = chunk(kb)
            tied = jnp.where(blk == tau, 1.0, 0.0).astype(BF16)
            rank = _dot(below, tied) + seen
            keys_ref[rows(kb), :] = jnp.where(blk == tau, jnp.where(rank > need, INT_MIN, blk), blk)
            return rank[t - 1:t, :]

        lax.fori_loop(0, qi + 1, retire, jnp.zeros((1, t), F32))

    qs4_ref[...] = jnp.zeros(qs4_ref.shape, BF16)
    for h in range(SA_HEADS):
        qs4_ref[0:SA_HEAD_DIM, h * t:(h + 1) * t] = saq_ref[h * SA_HEAD_DIM:(h + 1) * SA_HEAD_DIM, :]
    m_ref[...] = jnp.full(m_ref.shape, NEG, F32)
    acc_ref[...] = jnp.zeros(acc_ref.shape, F32)

    def scores(kb, near=None, n=1):
        s = _dot(sak_ref[rows(kb, n), :], qs4_ref[...])
        sel = keys_ref[rows(kb, n), :] >= tau
        parts = []
        for h in range(SA_HEADS):
            sh = s[:, h * t:(h + 1) * t]
            if near is not None:
                sh = sh + bn_ref[h, near]
            parts.append(jnp.where(sel, sh, NEG))
        return jnp.concatenate(parts, axis=1)

    far = jnp.concatenate([jnp.full((1, t), far_ref[h], F32) for h in range(SA_HEADS)], axis=1)

    def values(kb, n=1):
        return sav_ref[:, rows(kb, n)]

    in_pairs(jnp.maximum(qi - 1, 0),
             lambda kb, n: _softmax_step(scores(kb, None, n), values(kb, n), m_ref, acc_ref, far))

    @pl.when(qi >= 1)
    def _():
        _softmax_step(scores(qi - 1, 1), values(qi - 1), m_ref, acc_ref, None)

    _softmax_step(scores(qi, 0), values(qi), m_ref, acc_ref, None)

    acc = acc_ref[...]
    o = acc[0:SA_HEAD_DIM] / acc[SA_HEAD_DIM:SA_HEAD_DIM + 1]
    o = jnp.concatenate([o[:, h * t:(h + 1) * t] for h in range(SA_HEADS)], axis=0)
    o_ref[...] = o.T.astype(o_ref.dtype)


def _sparse_attention(ixq_t, ixw_t, ixk, saq_t, sak, sav_t, bias_near, bias_far, *, bsz, seq, t, topk):
    nq = seq // t
    smem = pl.BlockSpec(memory_space=pltpu.SMEM)
    qcols = lambda r: pl.BlockSpec((r, t), lambda b, i: (0, b * nq + i))
    full = lambda w: pl.BlockSpec((None, seq, w), lambda b, i: (b, 0, 0), pipeline_mode=pl.Buffered(1))
    va = SA_HEAD_DIM + BF16_ROWS
    return pl.pallas_call(
        functools.partial(_dsa_kernel, t=t, topk=topk),
        grid=(bsz, nq),
        in_specs=[
            smem,
            qcols(IDX_HEADS * IDX_DIM), qcols(IDX_HEADS), full(LANES),
            qcols(SA_WIDTH), full(LANES),
            pl.BlockSpec((SA_HEAD_DIM, seq), lambda b, i: (0, b), pipeline_mode=pl.Buffered(1)),
            _resident((SA_HEADS, 2, t, t)),
        ],
        out_specs=pl.BlockSpec((None, t, SA_WIDTH), lambda b, i: (b, i, 0)),
        out_shape=jax.ShapeDtypeStruct((bsz, seq, SA_WIDTH), BF16),
        scratch_shapes=[
            pltpu.VMEM((LANES, IDX_HEADS * t), BF16),
            pltpu.VMEM((seq, t), I32),
            pltpu.VMEM((seq, t), BF16),
            pltpu.VMEM((LANES, SA_HEADS * t), BF16),
            pltpu.VMEM((1, SA_HEADS * t), F32),
            pltpu.VMEM((va, SA_HEADS * t), F32),
        ],
        compiler_params=_params("parallel", "arbitrary"),
        name="sparse_attention",
    )(bias_far, ixq_t, ixw_t, ixk, saq_t, sak, sav_t, bias_near)


def _ssm_state_kernel(u_ref, w_ref, o_ref):
    o_ref[...] = _dot(u_ref[...], w_ref[...])


def _ssm_scan_kernel(loc_ref, a1_ref, a2_ref, o_ref):
    a1 = a1_ref[...]
    a2 = a2_ref[...]

    def body(n, s):
        o_ref[n] = s
        return a1 * s + a2 * pltpu.roll(s, SSM_STATE, axis=1) + loc_ref[n]

    lax.fori_loop(0, loc_ref.shape[0], body, jnp.zeros(a1.shape, F32))


def _ssm_out_kernel(u_ref, s_ref, toep_ref, wout_ref, o_ref):
    o_ref[...] = (_dot(u_ref[...], toep_ref[...])
                  + _dot(s_ref[...].astype(BF16), wout_ref[...]))


def _ssm_gate_kernel(y_ref, u_ref, d_ref, w_ref, o_ref):
    y = jax.nn.gelu(y_ref[...] + d_ref[...] * u_ref[...])
    o_ref[...] = (y * jax.nn.sigmoid(_dot(y.astype(BF16), w_ref[...]))).astype(o_ref.dtype)


def _ssm_tables(lam_re, lam_im, log_dt, b_re, b_im, c_re, c_im, tc):
    hp = lax.Precision.HIGHEST
    dt = jnp.exp(log_dt)[:, None]
    n = jnp.arange(tc + 1, dtype=F32)[:, None, None]
    mag = jnp.exp(lam_re * dt * n)
    pw_re = mag * jnp.cos(lam_im * dt * n)
    pw_im = mag * jnp.sin(lam_im * dt * n)
    den = lam_re * lam_re + lam_im * lam_im
    nr, ni = pw_re[1] - 1.0, pw_im[1]
    f_re = (nr * lam_re + ni * lam_im) / den
    f_im = (ni * lam_re - nr * lam_im) / den
    bb_re = f_re[..., None] * b_re - f_im[..., None] * b_im
    bb_im = f_re[..., None] * b_im + f_im[..., None] * b_re
    ca_re = c_re[None] * pw_re[:, :, None, :] - c_im[None] * pw_im[:, :, None, :]
    ca_im = c_re[None] * pw_im[:, :, None, :] + c_im[None] * pw_re[:, :, None, :]
    groups = lam_re.shape[0]
    inv_mag = jnp.exp(-lam_re * dt * n[:tc])
    iw_re = inv_mag * jnp.cos(lam_im * dt * n[:tc])
    iw_im = -inv_mag * jnp.sin(lam_im * dt * n[:tc])
    l_re = iw_re[..., None] * bb_re[None] - iw_im[..., None] * bb_im[None]
    l_im = iw_re[..., None] * bb_im[None] + iw_im[..., None] * bb_re[None]
    left = jnp.concatenate([l_re, l_im], axis=2).transpose(1, 0, 3, 2)
    right = jnp.concatenate([ca_re[:tc], -ca_im[:tc]], axis=3).transpose(1, 0, 2, 3)
    cw = tc * SSM_GROUP
    toep = jnp.einsum('gik,gjk->gij', left.reshape(groups, cw, 2 * SSM_STATE),
                      right.reshape(groups, cw, 2 * SSM_STATE), precision=hp)
    step = jnp.arange(cw) // SSM_GROUP
    toep = jnp.where(step[:, None] <= step[None, :], toep, 0.0)
    rev_re, rev_im = pw_re[tc - 1::-1][:tc], pw_im[tc - 1::-1][:tc]
    ws_re = rev_re[..., None] * bb_re[None] - rev_im[..., None] * bb_im[None]
    ws_im = rev_re[..., None] * bb_im[None] + rev_im[..., None] * bb_re[None]
    wstate = jnp.concatenate([ws_re, ws_im], axis=2)
    wstate = wstate.transpose(1, 0, 3, 2).reshape(groups, tc * SSM_GROUP, 2 * SSM_STATE)
    wout = jnp.concatenate([ca_re[1:], -ca_im[1:]], axis=3)
    wout = wout.transpose(1, 3, 0, 2).reshape(groups, 2 * SSM_STATE, tc * SSM_GROUP)
    dec_re, dec_im = pw_re[tc], pw_im[tc]
    a1 = jnp.concatenate([dec_re, dec_re], axis=1)
    a2 = jnp.concatenate([-dec_im, dec_im], axis=1)
    return toep.astype(BF16), wstate.astype(BF16), wout.astype(BF16), a1, a2


def _ssm(u, tables, d_skip, w_glu, *, bsz, seq, tm):
    toep, wstate, wout, a1, a2 = tables
    groups = toep.shape[0]
    tc = SSM_CHUNK
    nc = seq // tc
    rows = bsz * nc
    cw = tc * SSM_GROUP
    ug = u.astype(BF16).reshape(bsz, nc, tc, groups, SSM_GROUP).transpose(3, 0, 1, 2, 4).reshape(groups, rows, cw)
    gspec = lambda r, c: pl.BlockSpec((None, r, c), lambda g: (g, 0, 0))
    loc = pl.pallas_call(
        _ssm_state_kernel,
        grid=(groups,),
        in_specs=[gspec(rows, cw), gspec(cw, 2 * SSM_STATE)],
        out_specs=gspec(rows, 2 * SSM_STATE),
        out_shape=jax.ShapeDtypeStruct((groups, rows, 2 * SSM_STATE), F32),
        compiler_params=_params("parallel"),
        name="ssm_chunk_state",
    )(ug, wstate)
    gb = groups * bsz
    loc_t = loc.reshape(groups, bsz, nc, 2 * SSM_STATE).transpose(2, 0, 1, 3).reshape(nc, gb, 2 * SSM_STATE)
    a1r = jnp.repeat(a1, bsz, axis=0)
    a2r = jnp.repeat(a2, bsz, axis=0)
    rb = gb
    prev = pl.pallas_call(
        _ssm_scan_kernel,
        grid=(gb // rb,),
        in_specs=[pl.BlockSpec((nc, rb, 2 * SSM_STATE), lambda i: (0, i, 0)),
                  pl.BlockSpec((rb, 2 * SSM_STATE), lambda i: (i, 0)),
                  pl.BlockSpec((rb, 2 * SSM_STATE), lambda i: (i, 0))],
        out_specs=pl.BlockSpec((nc, rb, 2 * SSM_STATE), lambda i: (0, i, 0)),
        out_shape=jax.ShapeDtypeStruct((nc, gb, 2 * SSM_STATE), F32),
        compiler_params=_params("parallel"),
        name="ssm_chunk_scan",
    )(loc_t, a1r, a2r)
    prev_g = prev.reshape(nc, groups, bsz, 2 * SSM_STATE).transpose(1, 2, 0, 3).reshape(groups, rows, 2 * SSM_STATE)
    y = pl.pallas_call(
        _ssm_out_kernel,
        grid=(groups,),
        in_specs=[gspec(rows, cw), gspec(rows, 2 * SSM_STATE), gspec(cw, cw), gspec(2 * SSM_STATE, cw)],
        out_specs=gspec(rows, cw),
        out_shape=jax.ShapeDtypeStruct((groups, rows, cw), F32),
        compiler_params=_params("parallel"),
        name="ssm_chunk_out",
    )(ug, prev_g, toep, wout)
    y = y.reshape(groups, bsz, nc, tc, SSM_GROUP).transpose(1, 2, 3, 0, 4).reshape(bsz * seq, groups * SSM_GROUP)
    width = groups * SSM_GROUP
    return pl.pallas_call(
        _ssm_gate_kernel,
        grid=(bsz * seq // tm,),
        in_specs=[pl.BlockSpec((tm, width), lambda i: (i, 0)),
                  pl.BlockSpec((tm, width), lambda i: (i, 0)),
                  _resident((1, width)), _resident((width, width))],
        out_specs=pl.BlockSpec((tm, width), lambda i: (i, 0)),
        out_shape=jax.ShapeDtypeStruct((bsz * seq, width), BF16),
        compiler_params=_params("parallel"),
        name="ssm_gate",
    )(y, u, d_skip, w_glu)


def _t5_bucket(n):
    max_exact = REL_BUCKETS // 2
    nf = jnp.maximum(n, 1).astype(F32)
    large = max_exact + (jnp.log(nf / max_exact) / math.log(REL_MAX_DIST / max_exact)
                         * (REL_BUCKETS - max_exact)).astype(I32)
    large = jnp.minimum(large, REL_BUCKETS - 1)
    return jnp.where(n < max_exact, n, large)


def _bias_tiles(table, t):
    assert t >= REL_MAX_DIST
    table = table.astype(F32) * LOG2E
    key = jnp.arange(t)[:, None]
    query = jnp.arange(t)[None, :]
    dist = jnp.stack([query - key, t + query - key])
    bucket = _t5_bucket(jnp.maximum(dist, 0))
    tiles = jnp.zeros((table.shape[1],) + dist.shape, F32)
    for b in range(REL_BUCKETS):
        tiles = jnp.where(bucket[None] == b, table[b][:, None, None, None], tiles)
    return jnp.where(dist[None] >= 0, tiles, NEG), table[REL_BUCKETS - 1]


def _split_w_in(w_in):
    offs = [0]
    for s in IN_SIZES:
        offs.append(offs[-1] + s)
    da_q, da_k, da_v, ssm_u, sa_q, sa_k, sa_v, ix_q, ix_k, ix_w = (
        w_in[:, offs[j]:offs[j + 1]] for j in range(len(IN_SIZES)))
    w_scale = IDX_HEADS ** -0.5 * IDX_DIM ** -0.5
    pad = lambda w: jnp.pad(w, ((0, 0), (0, LANES - w.shape[1])))
    weights = [(da_q * (DA_QK_DIM ** -0.5 * LOG2E)).T, da_k, da_v.T, ssm_u,
               (sa_q * (SA_HEAD_DIM ** -0.5 * LOG2E)).T, pad(sa_k), sa_v.T,
               ix_q.T, pad(ix_k), (ix_w * w_scale).T]
    dtypes = [BF16, BF16, BF16, F32, BF16, BF16, BF16, BF16, BF16, F32]
    transposed = [True, False, True, False, True, False, True, True, False, True]
    return [w.astype(BF16) for w in weights], dtypes, transposed


def _plan(bsz, seq):
    rows = bsz * seq
    tm = 512 if rows % 512 == 0 else rows
    t_da = 1024 if seq % 1024 == 0 and seq >= 4096 else 256 if seq % 256 == 0 else 128
    t_sa = 512 if seq % 512 == 0 and seq >= 2048 else 256 if seq % 256 == 0 else 128
    tm_gate = 2048 if rows % 2048 == 0 else tm
    return dict(tm=tm, tm_gate=tm_gate, t_da=t_da, t_sa=t_sa)


def kernel(x, p, rel_bias, ffn1_w_gate, ffn1_w_up, ffn1_w_down, ln1_g, ln1_b, w_in, w_o, da_lam_q1, da_lam_k1, da_lam_q2, da_lam_k2, da_subln_g, ssm_lam_re, ssm_lam_im, ssm_log_dt, ssm_b_re, ssm_b_im, ssm_c_re, ssm_c_im, ssm_d, ssm_w_glu, ln2_g, ln2_b, ffn2_w_gate, ffn2_w_up, ffn2_w_down, ple_w_proj, ple_w_gate, ln3_g, ln3_b):
    bsz, seq, d = x.shape
    rows = bsz * seq
    plan = _plan(bsz, seq)
    tm, t_da, t_sa = plan["tm"], plan["t_da"], plan["t_sa"]
    topk = min(TOPK_MAX, seq // 4)
    da_near, da_far = _bias_tiles(rel_bias[:, :DA_HEADS], t_da)
    sa_near, sa_far = _bias_tiles(rel_bias[:, DA_HEADS:], t_sa)
    vec = lambda a: a.reshape(1, -1).astype(F32)
    da_w = DA_HEADS * DA_V_DIM

    h = x.reshape(rows, d)
    for i in range(DEPTH):
        lam_init = 0.8 - 0.6 * math.exp(-0.3 * i)
        weights, dtypes, transposed = _split_w_in(w_in[i])
        h, (da_q, da_k, da_v, ssm_u, sa_q, sa_k, sa_v, ix_q, ix_k, ix_w) = _ffn_inproj(
            h, ffn1_w_gate[i].astype(BF16), ffn1_w_up[i].astype(BF16), ffn1_w_down[i].astype(BF16),
            vec(ln1_g[i]), vec(ln1_b[i]), weights, dtypes, transposed, tm=tm)
        b3 = lambda a: a.reshape(bsz, seq, a.shape[-1])

        lam = (jnp.exp(jnp.sum(da_lam_q1[i].astype(F32) * da_lam_k1[i]))
               - jnp.exp(jnp.sum(da_lam_q2[i].astype(F32) * da_lam_k2[i])) + lam_init)
        subln = jnp.broadcast_to(da_subln_g[i].astype(F32)[:, None], (DA_V_DIM, t_da))
        o_da = _diff_attention(da_q, b3(da_k), da_v, da_near, da_far,
                               lam.reshape(1).astype(F32), subln, bsz=bsz, seq=seq, t=t_da,
                               post_scale=1.0 - lam_init)

        tables = _ssm_tables(ssm_lam_re[i].astype(F32), ssm_lam_im[i].astype(F32), ssm_log_dt[i].astype(F32),
                             ssm_b_re[i].astype(F32), ssm_b_im[i].astype(F32),
                             ssm_c_re[i].astype(F32), ssm_c_im[i].astype(F32), SSM_CHUNK)
        o_ssm = _ssm(ssm_u, tables, vec(ssm_d[i]), ssm_w_glu[i].astype(BF16), bsz=bsz, seq=seq,
                     tm=plan["tm_gate"])

        o_sa = _sparse_attention(ix_q, ix_w, b3(ix_k), sa_q, b3(sa_k), sa_v, sa_near, sa_far,
                                 bsz=bsz, seq=seq, t=t_sa, topk=topk)

        wo = w_o[i].astype(BF16)
        h = _mix_ffn(h, o_da.reshape(rows, da_w), o_ssm, o_sa.reshape(rows, SA_WIDTH), p[i].reshape(rows, -1),
                     [wo[:da_w], wo[da_w:da_w + SSM_WIDTH], wo[da_w + SSM_WIDTH:], vec(ln2_g[i]), vec(ln2_b[i]),
                      ffn2_w_gate[i].astype(BF16), ffn2_w_up[i].astype(BF16), ffn2_w_down[i].astype(BF16),
                      ple_w_proj[i].astype(BF16), ple_w_gate[i].astype(BF16), vec(ln3_g[i]), vec(ln3_b[i])],
                     tm=tm)
    return h.reshape(bsz, seq, d)
```

```python
import functools
import math

import jax
import jax.numpy as jnp
from jax import lax
from jax.experimental import pallas as pl
from jax.experimental.pallas import tpu as pltpu

F32 = jnp.float32
BF16 = jnp.bfloat16
I32 = jnp.int32

DEPTH = 2
DA_QK_DIM = 64
DA_V_DIM = 2 * DA_QK_DIM
DA_HEADS = 4
SSM_GROUP = 16
SSM_GROUPS = 16
SSM_STATE = 64
SSM_WIDTH = SSM_GROUP * SSM_GROUPS
SA_HEAD_DIM = 64
SA_HEADS = 4
SA_WIDTH = SA_HEADS * SA_HEAD_DIM
IDX_HEADS = 8
IDX_DIM = 32
TOPK_MAX = 256
REL_BUCKETS = 32
REL_MAX_DIST = 128
ALPHA = (2 * DEPTH) ** 0.25
LN_EPS = 1e-5
IN_SIZES = (DA_HEADS * 2 * DA_QK_DIM, DA_HEADS * 2 * DA_QK_DIM, DA_HEADS * DA_V_DIM,
            SSM_WIDTH, SA_WIDTH, SA_HEAD_DIM, SA_HEAD_DIM,
            IDX_HEADS * IDX_DIM, IDX_DIM, IDX_HEADS)

LANES = 128
SUBLANES = 8
BF16_ROWS = 16
NEG = -1e30
INT_MIN = -2 ** 31
HI_MASK = -2 ** 16
TINY = 2.0 ** -126
BF16_MIN_NORMAL = 0x0080
LOW_BITS = 2
MID_BITS = 16 - LOW_BITS
MID_MASK = 2 ** MID_BITS - 1
ZERO_KEY_TOP = 2 ** 16 - 1
LOG2E = math.log2(math.e)
VMEM_LIMIT = 56 * 1024 * 1024
SSM_CHUNK = 64

def _params(*sem):
    return pltpu.CompilerParams(dimension_semantics=sem, vmem_limit_bytes=VMEM_LIMIT)


def _resident(shape):
    return pl.BlockSpec(shape, lambda *_: (0,) * len(shape), pipeline_mode=pl.Buffered(1))


def _dot(a, b):
    return jnp.dot(a, b, preferred_element_type=F32)


def _dot_nt(a, b):
    return lax.dot_general(a, b, (((1,), (1,)), ((), ())), preferred_element_type=F32)


def _layer_norm(y, g, b):
    mu = jnp.mean(y, axis=-1, keepdims=True)
    d = y - mu
    var = jnp.mean(d * d, axis=-1, keepdims=True)
    return d * lax.rsqrt(var + LN_EPS) * g + b


def _fold_rows(x, rows):
    while x.shape[0] > rows:
        half = x.shape[0] // 2
        x = x[:half] + x[half:]
    return x


FF_CHUNK = 256


def _swiglu(xb, wg, wu, wd):
    d_ff = wg.shape[1]
    acc = None
    for c0 in range(0, d_ff, FF_CHUNK):
        c1 = min(c0 + FF_CHUNK, d_ff)
        gate = _dot(xb, wg[:, c0:c1])
        up = _dot(xb, wu[:, c0:c1])
        hid = (gate * jax.nn.sigmoid(gate) * up).astype(BF16)
        part = _dot(hid, wd[c0:c1, :])
        acc = part if acc is None else acc + part
    return acc


def _ffn_inproj_kernel(x_ref, wg, wu, wd, g_ref, b_ref, *refs, transposed):
    n = len(transposed)
    w_refs, x1_ref, o_refs = refs[:n], refs[n], refs[n + 1:]
    x = x_ref[...]
    x1 = _layer_norm(ALPHA * x + 0.5 * _swiglu(x.astype(BF16), wg, wu, wd), g_ref[...], b_ref[...])
    x1_ref[...] = x1
    xb = x1.astype(BF16)
    for w_ref, o_ref, tr in zip(w_refs, o_refs, transposed):
        out = _dot_nt(w_ref[...], xb) if tr else _dot(xb, w_ref[...])
        o_ref[...] = out.astype(o_ref.dtype)


def _ffn_inproj(x, wg, wu, wd, g, b, weights, dtypes, transposed, *, tm):
    t_rows, d = x.shape
    row = lambda w: pl.BlockSpec((tm, w), lambda i: (i, 0))
    out_specs, out_shape = [row(d)], [jax.ShapeDtypeStruct((t_rows, d), F32)]
    for w, dt, tr in zip(weights, dtypes, transposed):
        if tr:
            out_specs.append(pl.BlockSpec((w.shape[0], tm), lambda i: (0, i)))
            out_shape.append(jax.ShapeDtypeStruct((w.shape[0], t_rows), dt))
        else:
            out_specs.append(row(w.shape[1]))
            out_shape.append(jax.ShapeDtypeStruct((t_rows, w.shape[1]), dt))
    consts = [wg, wu, wd, g, b] + list(weights)
    outs = pl.pallas_call(
        functools.partial(_ffn_inproj_kernel, transposed=tuple(transposed)),
        grid=(t_rows // tm,),
        in_specs=[row(d)] + [_resident(c.shape) for c in consts],
        out_specs=out_specs,
        out_shape=out_shape,
        compiler_params=_params("parallel"),
        name="ffn_ln_in_proj",
    )(x, *consts)
    return outs[0], outs[1:]


def _mix_ffn_kernel(x_ref, da_ref, ssm_ref, sa_ref, p_ref, w1, w2, w3, g2_ref, b2_ref,
                    wg, wu, wd, wpp, wpg, g3_ref, b3_ref, o_ref):
    mix = _dot(da_ref[...], w1[...]) + _dot(ssm_ref[...], w2[...]) + _dot(sa_ref[...], w3[...])
    x = _layer_norm(ALPHA * x_ref[...] + mix, g2_ref[...], b2_ref[...])
    xb = x.astype(BF16)
    y = ALPHA * x + 0.5 * _swiglu(xb, wg, wu, wd)
    y = y + _dot(p_ref[...].astype(BF16), wpp[...]) * jax.nn.sigmoid(_dot(xb, wpg[...]))
    o_ref[...] = _layer_norm(y, g3_ref[...], b3_ref[...])


def _mix_ffn(x, o_da, o_ssm, o_sa, p, consts, *, tm):
    t_rows, d = x.shape
    row = lambda a: pl.BlockSpec((tm, a.shape[1]), lambda i: (i, 0))
    acts = [x, o_da, o_ssm, o_sa, p]
    return pl.pallas_call(
        _mix_ffn_kernel,
        grid=(t_rows // tm,),
        in_specs=[row(a) for a in acts] + [_resident(c.shape) for c in consts],
        out_specs=pl.BlockSpec((tm, d), lambda i: (i, 0)),
        out_shape=jax.ShapeDtypeStruct((t_rows, d), F32),
        compiler_params=_params("parallel"),
        name="out_proj_ln_ffn_ple_ln",
    )(*acts, *consts)


def _softmax_probs(s, m_ref, shift):
    m_prev = m_ref[...]
    m_cur = jnp.max(s, axis=0, keepdims=True)
    if shift is not None:
        m_cur = m_cur + shift
    m_next = jnp.maximum(m_prev, m_cur)
    m_ref[...] = m_next
    p = jnp.exp2(s - (m_next if shift is None else m_next - shift))
    return p.astype(BF16), jnp.exp2(m_prev - m_next)


def _accumulate(acc_ref, rescale, v_t, p):
    v_aug = jnp.concatenate([v_t, jnp.ones((BF16_ROWS, v_t.shape[1]), v_t.dtype)], axis=0)
    acc_ref[...] = rescale * acc_ref[...] + _dot(v_aug, p)


def _softmax_step(s, v_t, m_ref, acc_ref, shift):
    p, rescale = _softmax_probs(s, m_ref, shift)
    _accumulate(acc_ref, rescale, v_t, p)


def _far_chunks(n, scores, values, m_ref, acc_ref, shift):
    def body(i, carry):
        _softmax_step(scores(i), values(i), m_ref, acc_ref, shift)
        return carry

    lax.fori_loop(0, n, body, 0)


def _da_kernel(far_ref, lam_ref, q_ref, k_ref, v_ref, bn_ref, g_ref, o_ref,
               qd_ref, m_ref, acc_ref, *, t, post_scale):
    h = pl.program_id(1)
    qi = pl.program_id(2)
    q = q_ref[...].astype(F32)
    rowq = lax.broadcasted_iota(I32, q.shape, 0)
    qd_ref[:, 0:t] = jnp.where(rowq < DA_QK_DIM, q, 0.0).astype(BF16)
    qd_ref[:, t:2 * t] = jnp.where(rowq >= DA_QK_DIM, q, 0.0).astype(BF16)
    m_ref[...] = jnp.full(m_ref.shape, NEG, F32)
    acc_ref[...] = jnp.zeros(acc_ref.shape, F32)
    far = far_ref[h]

    def scores(kb):
        return _dot(k_ref[pl.ds(pl.multiple_of(kb * t, t), t), :], qd_ref[...])

    def values(kb):
        return v_ref[:, pl.ds(pl.multiple_of(kb * t, t), t)]

    def near_block(kb, near):
        bias = bn_ref[near]
        _softmax_step(scores(kb) + jnp.concatenate([bias, bias], axis=1), values(kb), m_ref, acc_ref, None)

    _far_chunks(jnp.maximum(qi - 1, 0), scores, values, m_ref, acc_ref, far)

    @pl.when(qi >= 1)
    def _():
        near_block(qi - 1, 1)

    near_block(qi, 0)

    acc = acc_ref[...]
    o = acc[0:DA_V_DIM] / acc[DA_V_DIM:DA_V_DIM + 1]
    o = o[:, 0:t] - lam_ref[0] * o[:, t:2 * t]
    ms = jnp.mean(o * o, axis=0, keepdims=True)
    o = o * lax.rsqrt(ms + LN_EPS) * g_ref[...] * post_scale
    o_ref[...] = o.T.astype(o_ref.dtype)


def _diff_attention(q_t, k, v_t, bias_near, bias_far, lam, subln_g, *, bsz, seq, t, post_scale):
    heads = q_t.shape[0] // DA_V_DIM
    nq = seq // t
    smem = pl.BlockSpec(memory_space=pltpu.SMEM)
    va = DA_V_DIM + BF16_ROWS
    return pl.pallas_call(
        functools.partial(_da_kernel, t=t, post_scale=post_scale),
        grid=(bsz, heads, nq),
        in_specs=[
            smem, smem,
            pl.BlockSpec((DA_V_DIM, t), lambda b, h, i: (h, b * nq + i)),
            pl.BlockSpec((None, seq, DA_V_DIM), lambda b, h, i: (b, 0, h)),
            pl.BlockSpec((DA_V_DIM, seq), lambda b, h, i: (h, b)),
            pl.BlockSpec((None, 2, t, t), lambda b, h, i: (h, 0, 0, 0)),
            pl.BlockSpec((DA_V_DIM, t), lambda b, h, i: (0, 0)),
        ],
        out_specs=pl.BlockSpec((None, t, DA_V_DIM), lambda b, h, i: (b, i, h)),
        out_shape=jax.ShapeDtypeStruct((bsz, seq, heads * DA_V_DIM), BF16),
        scratch_shapes=[
            pltpu.VMEM((DA_V_DIM, 2 * t), BF16),
            pltpu.VMEM((1, 2 * t), F32),
            pltpu.VMEM((va, 2 * t), F32),
        ],
        compiler_params=_params("parallel", "parallel", "arbitrary"),
        name="diff_attention",
    )(bias_far, lam, q_t, k, v_t, bias_near, subln_g)


def _dsa_kernel(far_ref, ixq_ref, ixw_ref, ixk_ref, saq_ref, sak_ref, sav_ref, bn_ref, o_ref,
                qi8_ref, keys_ref, hi_ref, qs4_ref, m_ref, acc_ref, *, t, topk):
    qi = pl.program_id(1)
    kf = float(topk)

    qi8_ref[...] = jnp.zeros(qi8_ref.shape, BF16)
    for h in range(IDX_HEADS):
        qi8_ref[0:IDX_DIM, h * t:(h + 1) * t] = ixq_ref[h * IDX_DIM:(h + 1) * IDX_DIM, :]

    def rows(kb, n=1):
        return pl.ds(pl.multiple_of(kb * t, t), n * t)

    def chunk(kb):
        return keys_ref[rows(kb), :]

    def index_keys(kb, n=1):
        kblk = ixk_ref[rows(kb, n), :]
        sc = None
        for h in range(IDX_HEADS):
            r = _dot(kblk, qi8_ref[:, h * t:(h + 1) * t])
            term = ixw_ref[h:h + 1, :] * jnp.maximum(r, 0.0)
            sc = term if sc is None else sc + term
        return jnp.where(jnp.abs(sc) < TINY, 0.0, sc)

    def store_keys(kb, sc, n=1):
        bits = lax.bitcast_convert_type(sc, I32)
        key = bits ^ ((bits >> 31) & 0x7FFFFFFF)
        zero_key = (ZERO_KEY_TOP - kb * t) - lax.broadcasted_iota(I32, (n * t, t), 0)
        keys_ref[rows(kb, n), :] = jnp.where(sc == 0.0, zero_key, key)
        hi_ref[rows(kb, n), :] = lax.bitcast_convert_type(bits & HI_MASK, F32).astype(BF16)

    def in_pairs(count, step):
        def pair(j, carry):
            step(2 * j, 2)
            return carry
        lax.fori_loop(0, count // 2, pair, 0)

        @pl.when(count % 2 == 1)
        def _():
            step(count - 1, 1)

    in_pairs(qi, lambda kb, n: store_keys(kb, index_keys(kb, n), n))
    krow = lax.broadcasted_iota(I32, (t, t), 0)
    qcol = lax.broadcasted_iota(I32, (t, t), 1)
    store_keys(qi, jnp.where(krow <= qcol, index_keys(qi), -jnp.inf))

    def count(fn):
        def body(kb, cnt):
            return cnt + _fold_rows(fn(kb, chunk(kb)), SUBLANES)
        cnt = lax.fori_loop(0, qi + 1, body, jnp.zeros((SUBLANES, t), I32))
        return jnp.sum(cnt.astype(F32), axis=0, keepdims=True)

    def count_hi(thr):
        def body(kb, cnt):
            hit = jnp.where(hi_ref[rows(kb), :] >= thr, jnp.ones((), BF16), jnp.zeros((), BF16))
            return cnt + _fold_rows(hit, BF16_ROWS).astype(F32)
        cnt = lax.fori_loop(0, qi + 1, body, jnp.zeros((BF16_ROWS, t), F32))
        return jnp.sum(cnt, axis=0, keepdims=True)

    def bf16_of(pattern):
        return lax.bitcast_convert_type(jnp.left_shift(pattern, 16), F32).astype(BF16)

    def code_value(code):
        pattern = code ^ ((code >> 31) & 0x7FFF)
        return bf16_of(jnp.where(code > 0, jnp.maximum(pattern, BF16_MIN_NORMAL), pattern))

    def hi_bit(i, st):
        code, open_, above = st
        cand = code + jnp.left_shift(jnp.int32(1), 15 - i)
        c = count_hi(code_value(cand))
        code = jnp.where(open_ > 0.0, jnp.where(c >= kf, cand, code), code)
        above = jnp.where(open_ > 0.0, jnp.where(c >= kf, above, c), above)
        return code, jnp.where(c == kf, 0.0, open_), above

    code, open_, above = lax.fori_loop(
        0, 16, hi_bit, (jnp.full((1, t), -2 ** 15, I32), jnp.ones((1, t), F32), jnp.zeros((1, t), F32)))

    value = code_value(code)
    higher = above
    need = kf - higher

    def recode(kb, carry):
        mid = ((chunk(kb) >> LOW_BITS) & MID_MASK) + BF16_MIN_NORMAL
        hi_ref[rows(kb), :] = jnp.where(hi_ref[rows(kb), :] == value, bf16_of(mid), jnp.zeros((), BF16))
        return carry

    lax.fori_loop(0, qi + 1, recode, 0)

    def mid_cond(st):
        i, _, _, _, n_open = st
        return jnp.logical_and(i < MID_BITS, n_open > 0.0)

    def mid_body(st):
        i, mid, open_, above, _ = st
        cand = mid + jnp.left_shift(jnp.int32(1), MID_BITS - 1 - i)
        c = count_hi(bf16_of(cand + BF16_MIN_NORMAL))
        mid = jnp.where(open_ > 0.0, jnp.where(c >= need, cand, mid), mid)
        above = jnp.where(open_ > 0.0, jnp.where(c >= need, above, higher + c), above)
        open_ = jnp.where(c == need, 0.0, open_)
        return i + 1, mid, open_, above, jnp.max(open_)

    _, mid, open_, above, _ = lax.while_loop(
        mid_cond, mid_body, (jnp.int32(0), jnp.zeros((1, t), I32), open_, above, jnp.max(open_)))

    def search_cond(st):
        i, _, _, _, n_open = st
        return jnp.logical_and(i < 32, n_open > 0.0)

    def search_body(st):
        i, tau, open_, above, _ = st
        cand = tau + jnp.left_shift(jnp.int32(1), 31 - i)
        c = count(lambda kb, blk: jnp.where(blk >= cand, 1, 0))
        tau = jnp.where(open_ > 0.0, jnp.where(c >= kf, cand, tau), tau)
        above = jnp.where(open_ > 0.0, jnp.where(c >= kf, above, c), above)
        open_ = jnp.where(c == kf, 0.0, open_)
        return i + 1, tau, open_, above, jnp.max(open_)

    _, tau, open_, above, _ = lax.while_loop(
        search_cond, search_body,
        (jnp.int32(32 - LOW_BITS), jnp.left_shift(code, 16) + jnp.left_shift(mid, LOW_BITS), open_, above,
         jnp.max(open_)))

    @pl.when(jnp.max(open_) > 0.0)
    def _():
        need = kf - above
        below = jnp.where(qcol <= krow, 1.0, 0.0).astype(BF16)

        def retire(kb, seen):
            blk = chunk(kb)
            tied = jnp.where(blk == tau, 1.0, 0.0).astype(BF16)
            rank = _dot(below, tied) + seen
            keys_ref[rows(kb), :] = jnp.where(blk == tau, jnp.where(rank > need, INT_MIN, blk), blk)
            return rank[t - 1:t, :]

        lax.fori_loop(0, qi + 1, retire, jnp.zeros((1, t), F32))

    qs4_ref[...] = jnp.zeros(qs4_ref.shape, BF16)
    for h in range(SA_HEADS):
        qs4_ref[0:SA_HEAD_DIM, h * t:(h + 1) * t] = saq_ref[h * SA_HEAD_DIM:(h + 1) * SA_HEAD_DIM, :]
    m_ref[...] = jnp.full(m_ref.shape, NEG, F32)
    acc_ref[...] = jnp.zeros(acc_ref.shape, F32)

    def scores(kb, near=None, n=1):
        s = _dot(sak_ref[rows(kb, n), :], qs4_ref[...])
        sel = keys_ref[rows(kb, n), :] >= tau
        parts = []
        for h in range(SA_HEADS):
            sh = s[:, h * t:(h + 1) * t]
            if near is not None:
                sh = sh + bn_ref[h, near]
            parts.append(jnp.where(sel, sh, NEG))
        return jnp.concatenate(parts, axis=1)

    far = jnp.concatenate([jnp.full((1, t), far_ref[h], F32) for h in range(SA_HEADS)], axis=1)

    def values(kb, n=1):
        return sav_ref[:, rows(kb, n)]

    in_pairs(jnp.maximum(qi - 1, 0),
             lambda kb, n: _softmax_step(scores(kb, None, n), values(kb, n), m_ref, acc_ref, far))

    @pl.when(qi >= 1)
    def _():
        _softmax_step(scores(qi - 1, 1), values(qi - 1), m_ref, acc_ref, None)

    _softmax_step(scores(qi, 0), values(qi), m_ref, acc_ref, None)

    acc = acc_ref[...]
    o = acc[0:SA_HEAD_DIM] / acc[SA_HEAD_DIM:SA_HEAD_DIM + 1]
    o = jnp.concatenate([o[:, h * t:(h + 1) * t] for h in range(SA_HEADS)], axis=0)
    o_ref[...] = o.T.astype(o_ref.dtype)


def _sparse_attention(ixq_t, ixw_t, ixk, saq_t, sak, sav_t, bias_near, bias_far, *, bsz, seq, t, topk):
    nq = seq // t
    assert seq <= ZERO_KEY_TOP
    smem = pl.BlockSpec(memory_space=pltpu.SMEM)
    qcols = lambda r: pl.BlockSpec((r, t), lambda b, i: (0, b * nq + i))
    full = lambda w: pl.BlockSpec((None, seq, w), lambda b, i: (b, 0, 0), pipeline_mode=pl.Buffered(1))
    va = SA_HEAD_DIM + BF16_ROWS
    return pl.pallas_call(
        functools.partial(_dsa_kernel, t=t, topk=topk),
        grid=(bsz, nq),
        in_specs=[
            smem,
            qcols(IDX_HEADS * IDX_DIM), qcols(IDX_HEADS), full(LANES),
            qcols(SA_WIDTH), full(LANES),
            pl.BlockSpec((SA_HEAD_DIM, seq), lambda b, i: (0, b), pipeline_mode=pl.Buffered(1)),
            _resident((SA_HEADS, 2, t, t)),
        ],
        out_specs=pl.BlockSpec((None, t, SA_WIDTH), lambda b, i: (b, i, 0)),
        out_shape=jax.ShapeDtypeStruct((bsz, seq, SA_WIDTH), BF16),
        scratch_shapes=[
            pltpu.VMEM((LANES, IDX_HEADS * t), BF16),
            pltpu.VMEM((seq, t), I32),
            pltpu.VMEM((seq, t), BF16),
            pltpu.VMEM((LANES, SA_HEADS * t), BF16),
            pltpu.VMEM((1, SA_HEADS * t), F32),
            pltpu.VMEM((va, SA_HEADS * t), F32),
        ],
        compiler_params=_params("parallel", "arbitrary"),
        name="sparse_attention",
    )(bias_far, ixq_t, ixw_t, ixk, saq_t, sak, sav_t, bias_near)


def _ssm_state_kernel(u_ref, w_ref, o_ref):
    o_ref[...] = _dot(u_ref[...], w_ref[...])


def _ssm_scan_kernel(loc_ref, a1_ref, a2_ref, o_ref):
    a1 = a1_ref[...]
    a2 = a2_ref[...]

    def body(n, s):
        o_ref[n] = s
        return a1 * s + a2 * pltpu.roll(s, SSM_STATE, axis=1) + loc_ref[n]

    lax.fori_loop(0, loc_ref.shape[0], body, jnp.zeros(a1.shape, F32))


def _ssm_out_kernel(u_ref, s_ref, toep_ref, wout_ref, o_ref):
    o_ref[...] = (_dot(u_ref[...], toep_ref[...])
                  + _dot(s_ref[...].astype(BF16), wout_ref[...]))


def _ssm_gate_kernel(y_ref, u_ref, d_ref, w_ref, o_ref):
    y = jax.nn.gelu(y_ref[...] + d_ref[...] * u_ref[...])
    o_ref[...] = (y * jax.nn.sigmoid(_dot(y.astype(BF16), w_ref[...]))).astype(o_ref.dtype)


def _ssm_tables(lam_re, lam_im, log_dt, b_re, b_im, c_re, c_im, tc):
    hp = lax.Precision.HIGHEST
    dt = jnp.exp(log_dt)[:, None]
    n = jnp.arange(tc + 1, dtype=F32)[:, None, None]
    mag = jnp.exp(lam_re * dt * n)
    pw_re = mag * jnp.cos(lam_im * dt * n)
    pw_im = mag * jnp.sin(lam_im * dt * n)
    den = lam_re * lam_re + lam_im * lam_im
    nr, ni = pw_re[1] - 1.0, pw_im[1]
    f_re = (nr * lam_re + ni * lam_im) / den
    f_im = (ni * lam_re - nr * lam_im) / den
    bb_re = f_re[..., None] * b_re - f_im[..., None] * b_im
    bb_im = f_re[..., None] * b_im + f_im[..., None] * b_re
    ca_re = c_re[None] * pw_re[:, :, None, :] - c_im[None] * pw_im[:, :, None, :]
    ca_im = c_re[None] * pw_im[:, :, None, :] + c_im[None] * pw_re[:, :, None, :]
    groups = lam_re.shape[0]
    inv_mag = jnp.exp(-lam_re * dt * n[:tc])
    iw_re = inv_mag * jnp.cos(lam_im * dt * n[:tc])
    iw_im = -inv_mag * jnp.sin(lam_im * dt * n[:tc])
    l_re = iw_re[..., None] * bb_re[None] - iw_im[..., None] * bb_im[None]
    l_im = iw_re[..., None] * bb_im[None] + iw_im[..., None] * bb_re[None]
    left = jnp.concatenate([l_re, l_im], axis=2).transpose(1, 0, 3, 2)
    right = jnp.concatenate([ca_re[:tc], -ca_im[:tc]], axis=3).transpose(1, 0, 2, 3)
    cw = tc * SSM_GROUP
    toep = jnp.einsum('gik,gjk->gij', left.reshape(groups, cw, 2 * SSM_STATE),
                      right.reshape(groups, cw, 2 * SSM_STATE), precision=hp)
    step = jnp.arange(cw) // SSM_GROUP
    toep = jnp.where(step[:, None] <= step[None, :], toep, 0.0)
    rev_re, rev_im = pw_re[tc - 1::-1][:tc], pw_im[tc - 1::-1][:tc]
    ws_re = rev_re[..., None] * bb_re[None] - rev_im[..., None] * bb_im[None]
    ws_im = rev_re[..., None] * bb_im[None] + rev_im[..., None] * bb_re[None]
    wstate = jnp.concatenate([ws_re, ws_im], axis=2)
    wstate = wstate.transpose(1, 0, 3, 2).reshape(groups, tc * SSM_GROUP, 2 * SSM_STATE)
    wout = jnp.concatenate([ca_re[1:], -ca_im[1:]], axis=3)
    wout = wout.transpose(1, 3, 0, 2).reshape(groups, 2 * SSM_STATE, tc * SSM_GROUP)
    dec_re, dec_im = pw_re[tc], pw_im[tc]
    a1 = jnp.concatenate([dec_re, dec_re], axis=1)
    a2 = jnp.concatenate([-dec_im, dec_im], axis=1)
    return toep.astype(BF16), wstate.astype(BF16), wout.astype(BF16), a1, a2


def _ssm(u, tables, d_skip, w_glu, *, bsz, seq, tm):
    toep, wstate, wout, a1, a2 = tables
    groups = toep.shape[0]
    tc = SSM_CHUNK
    nc = seq // tc
    rows = bsz * nc
    cw = tc * SSM_GROUP
    ug = u.astype(BF16).reshape(bsz, nc, tc, groups, SSM_GROUP).transpose(3, 0, 1, 2, 4).reshape(groups, rows, cw)
    gspec = lambda r, c: pl.BlockSpec((None, r, c), lambda g: (g, 0, 0))
    loc = pl.pallas_call(
        _ssm_state_kernel,
        grid=(groups,),
        in_specs=[gspec(rows, cw), gspec(cw, 2 * SSM_STATE)],
        out_specs=gspec(rows, 2 * SSM_STATE),
        out_shape=jax.ShapeDtypeStruct((groups, rows, 2 * SSM_STATE), F32),
        compiler_params=_params("parallel"),
        name="ssm_chunk_state",
    )(ug, wstate)
    gb = groups * bsz
    loc_t = loc.reshape(groups, bsz, nc, 2 * SSM_STATE).transpose(2, 0, 1, 3).reshape(nc, gb, 2 * SSM_STATE)
    a1r = jnp.repeat(a1, bsz, axis=0)
    a2r = jnp.repeat(a2, bsz, axis=0)
    rb = gb
    prev = pl.pallas_call(
        _ssm_scan_kernel,
        grid=(gb // rb,),
        in_specs=[pl.BlockSpec((nc, rb, 2 * SSM_STATE), lambda i: (0, i, 0)),
                  pl.BlockSpec((rb, 2 * SSM_STATE), lambda i: (i, 0)),
                  pl.BlockSpec((rb, 2 * SSM_STATE), lambda i: (i, 0))],
        out_specs=pl.BlockSpec((nc, rb, 2 * SSM_STATE), lambda i: (0, i, 0)),
        out_shape=jax.ShapeDtypeStruct((nc, gb, 2 * SSM_STATE), F32),
        compiler_params=_params("parallel"),
        name="ssm_chunk_scan",
    )(loc_t, a1r, a2r)
    prev_g = prev.reshape(nc, groups, bsz, 2 * SSM_STATE).transpose(1, 2, 0, 3).reshape(groups, rows, 2 * SSM_STATE)
    y = pl.pallas_call(
        _ssm_out_kernel,
        grid=(groups,),
        in_specs=[gspec(rows, cw), gspec(rows, 2 * SSM_STATE), gspec(cw, cw), gspec(2 * SSM_STATE, cw)],
        out_specs=gspec(rows, cw),
        out_shape=jax.ShapeDtypeStruct((groups, rows, cw), F32),
        compiler_params=_params("parallel"),
        name="ssm_chunk_out",
    )(ug, prev_g, toep, wout)
    y = y.reshape(groups, bsz, nc, tc, SSM_GROUP).transpose(1, 2, 3, 0, 4).reshape(bsz * seq, groups * SSM_GROUP)
    width = groups * SSM_GROUP
    return pl.pallas_call(
        _ssm_gate_kernel,
        grid=(bsz * seq // tm,),
        in_specs=[pl.BlockSpec((tm, width), lambda i: (i, 0)),
                  pl.BlockSpec((tm, width), lambda i: (i, 0)),
                  _resident((1, width)), _resident((width, width))],
        out_specs=pl.BlockSpec((tm, width), lambda i: (i, 0)),
        out_shape=jax.ShapeDtypeStruct((bsz * seq, width), BF16),
        compiler_params=_params("parallel"),
        name="ssm_gate",
    )(y, u, d_skip, w_glu)


def _t5_bucket(n):
    max_exact = REL_BUCKETS // 2
    nf = jnp.maximum(n, 1).astype(F32)
    large = max_exact + (jnp.log(nf / max_exact) / math.log(REL_MAX_DIST / max_exact)
                         * (REL_BUCKETS - max_exact)).astype(I32)
    large = jnp.minimum(large, REL_BUCKETS - 1)
    return jnp.where(n < max_exact, n, large)


def _bias_tiles(table, t):
    assert t >= REL_MAX_DIST
    table = table.astype(F32) * LOG2E
    key = jnp.arange(t)[:, None]
    query = jnp.arange(t)[None, :]
    dist = jnp.stack([query - key, t + query - key])
    bucket = _t5_bucket(jnp.maximum(dist, 0))
    tiles = jnp.zeros((table.shape[1],) + dist.shape, F32)
    for b in range(REL_BUCKETS):
        tiles = jnp.where(bucket[None] == b, table[b][:, None, None, None], tiles)
    return jnp.where(dist[None] >= 0, tiles, NEG), table[REL_BUCKETS - 1]


def _split_w_in(w_in):
    offs = [0]
    for s in IN_SIZES:
        offs.append(offs[-1] + s)
    da_q, da_k, da_v, ssm_u, sa_q, sa_k, sa_v, ix_q, ix_k, ix_w = (
        w_in[:, offs[j]:offs[j + 1]] for j in range(len(IN_SIZES)))
    w_scale = IDX_HEADS ** -0.5 * IDX_DIM ** -0.5
    pad = lambda w: jnp.pad(w, ((0, 0), (0, LANES - w.shape[1])))
    weights = [(da_q * (DA_QK_DIM ** -0.5 * LOG2E)).T, da_k, da_v.T, ssm_u,
               (sa_q * (SA_HEAD_DIM ** -0.5 * LOG2E)).T, pad(sa_k), sa_v.T,
               ix_q.T, pad(ix_k), (ix_w * w_scale).T]
    dtypes = [BF16, BF16, BF16, F32, BF16, BF16, BF16, BF16, BF16, F32]
    transposed = [True, False, True, False, True, False, True, True, False, True]
    return [w.astype(BF16) for w in weights], dtypes, transposed


def _plan(bsz, seq):
    rows = bsz * seq
    tm = 512 if rows % 512 == 0 else rows
    t_da = 1024 if seq % 1024 == 0 and seq >= 4096 else 256 if seq % 256 == 0 else 128
    t_sa = 512 if seq % 512 == 0 and seq >= 2048 else 256 if seq % 256 == 0 else 128
    tm_gate = 2048 if rows % 2048 == 0 else tm
    return dict(tm=tm, tm_gate=tm_gate, t_da=t_da, t_sa=t_sa)


def kernel(x, p, rel_bias, ffn1_w_gate, ffn1_w_up, ffn1_w_down, ln1_g, ln1_b, w_in, w_o, da_lam_q1, da_lam_k1, da_lam_q2, da_lam_k2, da_subln_g, ssm_lam_re, ssm_lam_im, ssm_log_dt, ssm_b_re, ssm_b_im, ssm_c_re, ssm_c_im, ssm_d, ssm_w_glu, ln2_g, ln2_b, ffn2_w_gate, ffn2_w_up, ffn2_w_down, ple_w_proj, ple_w_gate, ln3_g, ln3_b):
    bsz, seq, d = x.shape
    rows = bsz * seq
    plan = _plan(bsz, seq)
    tm, t_da, t_sa = plan["tm"], plan["t_da"], plan["t_sa"]
    topk = min(TOPK_MAX, seq // 4)
    da_near, da_far = _bias_tiles(rel_bias[:, :DA_HEADS], t_da)
    sa_near, sa_far = _bias_tiles(rel_bias[:, DA_HEADS:], t_sa)
    vec = lambda a: a.reshape(1, -1).astype(F32)
    da_w = DA_HEADS * DA_V_DIM

    h = x.reshape(rows, d)
    for i in range(DEPTH):
        lam_init = 0.8 - 0.6 * math.exp(-0.3 * i)
        weights, dtypes, transposed = _split_w_in(w_in[i])
        h, (da_q, da_k, da_v, ssm_u, sa_q, sa_k, sa_v, ix_q, ix_k, ix_w) = _ffn_inproj(
            h, ffn1_w_gate[i].astype(BF16), ffn1_w_up[i].astype(BF16), ffn1_w_down[i].astype(BF16),
            vec(ln1_g[i]), vec(ln1_b[i]), weights, dtypes, transposed, tm=tm)
        b3 = lambda a: a.reshape(bsz, seq, a.shape[-1])

        lam = (jnp.exp(jnp.sum(da_lam_q1[i].astype(F32) * da_lam_k1[i]))
               - jnp.exp(jnp.sum(da_lam_q2[i].astype(F32) * da_lam_k2[i])) + lam_init)
        subln = jnp.broadcast_to(da_subln_g[i].astype(F32)[:, None], (DA_V_DIM, t_da))
        o_da = _diff_attention(da_q, b3(da_k), da_v, da_near, da_far,
                               lam.reshape(1).astype(F32), subln, bsz=bsz, seq=seq, t=t_da,
                               post_scale=1.0 - lam_init)

        tables = _ssm_tables(ssm_lam_re[i].astype(F32), ssm_lam_im[i].astype(F32), ssm_log_dt[i].astype(F32),
                             ssm_b_re[i].astype(F32), ssm_b_im[i].astype(F32),
                             ssm_c_re[i].astype(F32), ssm_c_im[i].astype(F32), SSM_CHUNK)
        o_ssm = _ssm(ssm_u, tables, vec(ssm_d[i]), ssm_w_glu[i].astype(BF16), bsz=bsz, seq=seq,
                     tm=plan["tm_gate"])

        o_sa = _sparse_attention(ix_q, ix_w, b3(ix_k), sa_q, b3(sa_k), sa_v, sa_near, sa_far,
                                 bsz=bsz, seq=seq, t=t_sa, topk=topk)

        wo = w_o[i].astype(BF16)
        h = _mix_ffn(h, o_da.reshape(rows, da_w), o_ssm, o_sa.reshape(rows, SA_WIDTH), p[i].reshape(rows, -1),
                     [wo[:da_w], wo[da_w:da_w + SSM_WIDTH], wo[da_w + SSM_WIDTH:], vec(ln2_g[i]), vec(ln2_b[i]),
                      ffn2_w_gate[i].astype(BF16), ffn2_w_up[i].astype(BF16), ffn2_w_down[i].astype(BF16),
                      ple_w_proj[i].astype(BF16), ple_w_gate[i].astype(BF16), vec(ln3_g[i]), vec(ln3_b[i])],
                     tm=tm)
    return h.reshape(bsz, seq, d)
```

```python
import functools
import math

import jax
import jax.numpy as jnp
from jax import lax
from jax.experimental import pallas as pl
from jax.experimental.pallas import tpu as pltpu

F32 = jnp.float32
BF16 = jnp.bfloat16
I32 = jnp.int32

DEPTH = 2
DA_QK_DIM = 64
DA_V_DIM = 2 * DA_QK_DIM
DA_HEADS = 4
SSM_GROUP = 16
SSM_GROUPS = 16
SSM_STATE = 64
SSM_WIDTH = SSM_GROUP * SSM_GROUPS
SA_HEAD_DIM = 64
SA_HEADS = 4
SA_WIDTH = SA_HEADS * SA_HEAD_DIM
IDX_HEADS = 8
IDX_DIM = 32
TOPK_MAX = 256
REL_BUCKETS = 32
REL_MAX_DIST = 128
ALPHA = (2 * DEPTH) ** 0.25
LN_EPS = 1e-5
IN_SIZES = (DA_HEADS * 2 * DA_QK_DIM, DA_HEADS * 2 * DA_QK_DIM, DA_HEADS * DA_V_DIM,
            SSM_WIDTH, SA_WIDTH, SA_HEAD_DIM, SA_HEAD_DIM,
            IDX_HEADS * IDX_DIM, IDX_DIM, IDX_HEADS)

LANES = 128
SUBLANES = 8
BF16_ROWS = 16
NEG = -1e30
INT_MIN = -2 ** 31
HI_MASK = -2 ** 16
TINY = 2.0 ** -126
BF16_MIN_NORMAL = 0x0080
LOW_BITS = 2
MID_BITS = 16 - LOW_BITS
MID_MASK = 2 ** MID_BITS - 1
ZERO_KEY_TOP = 2 ** 16 - 1
LOG2E = math.log2(math.e)
VMEM_LIMIT = 56 * 1024 * 1024
SSM_CHUNK = 64

def _params(*sem):
    return pltpu.CompilerParams(dimension_semantics=sem, vmem_limit_bytes=VMEM_LIMIT)


def _resident(shape):
    return pl.BlockSpec(shape, lambda *_: (0,) * len(shape), pipeline_mode=pl.Buffered(1))


def _dot(a, b):
    return jnp.dot(a, b, preferred_element_type=F32)


def _dot_nt(a, b):
    return lax.dot_general(a, b, (((1,), (1,)), ((), ())), preferred_element_type=F32)


def _layer_norm(y, g, b):
    mu = jnp.mean(y, axis=-1, keepdims=True)
    d = y - mu
    var = jnp.mean(d * d, axis=-1, keepdims=True)
    return d * lax.rsqrt(var + LN_EPS) * g + b


def _fold_rows(x, rows):
    while x.shape[0] > rows:
        half = x.shape[0] // 2
        x = x[:half] + x[half:]
    return x


FF_CHUNK = 256


def _swiglu(xb, wg, wu, wd):
    d_ff = wg.shape[1]
    acc = None
    for c0 in range(0, d_ff, FF_CHUNK):
        c1 = min(c0 + FF_CHUNK, d_ff)
        gate = _dot(xb, wg[:, c0:c1])
        up = _dot(xb, wu[:, c0:c1])
        hid = (gate * jax.nn.sigmoid(gate) * up).astype(BF16)
        part = _dot(hid, wd[c0:c1, :])
        acc = part if acc is None else acc + part
    return acc


def _ffn_inproj_kernel(x_ref, wg, wu, wd, g_ref, b_ref, *refs, transposed):
    n = len(transposed)
    w_refs, x1_ref, o_refs = refs[:n], refs[n], refs[n + 1:]
    x = x_ref[...]
    x1 = _layer_norm(ALPHA * x + 0.5 * _swiglu(x.astype(BF16), wg, wu, wd), g_ref[...], b_ref[...])
    x1_ref[...] = x1
    xb = x1.astype(BF16)
    for w_ref, o_ref, tr in zip(w_refs, o_refs, transposed):
        out = _dot_nt(w_ref[...], xb) if tr else _dot(xb, w_ref[...])
        o_ref[...] = out.astype(o_ref.dtype)


def _ffn_inproj(x, wg, wu, wd, g, b, weights, dtypes, transposed, *, tm):
    t_rows, d = x.shape
    row = lambda w: pl.BlockSpec((tm, w), lambda i: (i, 0))
    out_specs, out_shape = [row(d)], [jax.ShapeDtypeStruct((t_rows, d), F32)]
    for w, dt, tr in zip(weights, dtypes, transposed):
        if tr:
            out_specs.append(pl.BlockSpec((w.shape[0], tm), lambda i: (0, i)))
            out_shape.append(jax.ShapeDtypeStruct((w.shape[0], t_rows), dt))
        else:
            out_specs.append(row(w.shape[1]))
            out_shape.append(jax.ShapeDtypeStruct((t_rows, w.shape[1]), dt))
    consts = [wg, wu, wd, g, b] + list(weights)
    outs = pl.pallas_call(
        functools.partial(_ffn_inproj_kernel, transposed=tuple(transposed)),
        grid=(t_rows // tm,),
        in_specs=[row(d)] + [_resident(c.shape) for c in consts],
        out_specs=out_specs,
        out_shape=out_shape,
        compiler_params=_params("parallel"),
        name="ffn_ln_in_proj",
    )(x, *consts)
    return outs[0], outs[1:]


def _mix_ffn_kernel(x_ref, da_ref, ssm_ref, sa_ref, p_ref, w1, w2, w3, g2_ref, b2_ref,
                    wg, wu, wd, wpp, wpg, g3_ref, b3_ref, o_ref):
    mix = _dot(da_ref[...], w1[...]) + _dot(ssm_ref[...], w2[...]) + _dot(sa_ref[...], w3[...])
    x = _layer_norm(ALPHA * x_ref[...] + mix, g2_ref[...], b2_ref[...])
    xb = x.astype(BF16)
    y = ALPHA * x + 0.5 * _swiglu(xb, wg, wu, wd)
    y = y + _dot(p_ref[...].astype(BF16), wpp[...]) * jax.nn.sigmoid(_dot(xb, wpg[...]))
    o_ref[...] = _layer_norm(y, g3_ref[...], b3_ref[...])


def _mix_ffn(x, o_da, o_ssm, o_sa, p, consts, *, tm):
    t_rows, d = x.shape
    row = lambda a: pl.BlockSpec((tm, a.shape[1]), lambda i: (i, 0))
    acts = [x, o_da, o_ssm, o_sa, p]
    return pl.pallas_call(
        _mix_ffn_kernel,
        grid=(t_rows // tm,),
        in_specs=[row(a) for a in acts] + [_resident(c.shape) for c in consts],
        out_specs=pl.BlockSpec((tm, d), lambda i: (i, 0)),
        out_shape=jax.ShapeDtypeStruct((t_rows, d), F32),
        compiler_params=_params("parallel"),
        name="out_proj_ln_ffn_ple_ln",
    )(*acts, *consts)


def _softmax_probs(s, m_ref, shift):
    m_prev = m_ref[...]
    m_cur = jnp.max(s, axis=0, keepdims=True)
    if shift is not None:
        m_cur = m_cur + shift
    m_next = jnp.maximum(m_prev, m_cur)
    m_ref[...] = m_next
    p = jnp.exp2(s - (m_next if shift is None else m_next - shift))
    return p.astype(BF16), jnp.exp2(m_prev - m_next)


def _accumulate(acc_ref, rescale, v_t, p):
    v_aug = jnp.concatenate([v_t, jnp.ones((BF16_ROWS, v_t.shape[1]), v_t.dtype)], axis=0)
    acc_ref[...] = rescale * acc_ref[...] + _dot(v_aug, p)


def _softmax_step(s, v_t, m_ref, acc_ref, shift):
    p, rescale = _softmax_probs(s, m_ref, shift)
    _accumulate(acc_ref, rescale, v_t, p)


def _far_chunks(n, scores, values, m_ref, acc_ref, shift):
    def pair(j, carry):
        s0, s1 = scores(2 * j), scores(2 * j + 1)
        _softmax_step(s0, values(2 * j), m_ref, acc_ref, shift)
        _softmax_step(s1, values(2 * j + 1), m_ref, acc_ref, shift)
        return carry

    lax.fori_loop(0, n // 2, pair, 0)

    @pl.when(n % 2 == 1)
    def _():
        _softmax_step(scores(n - 1), values(n - 1), m_ref, acc_ref, shift)


def _da_kernel(far_ref, lam_ref, q_ref, k_ref, v_ref, bn_ref, g_ref, o_ref,
               qd_ref, m_ref, acc_ref, *, t, post_scale):
    h = pl.program_id(1)
    qi = pl.program_id(2)
    q = q_ref[...].astype(F32)
    rowq = lax.broadcasted_iota(I32, q.shape, 0)
    qd_ref[:, 0:t] = jnp.where(rowq < DA_QK_DIM, q, 0.0).astype(BF16)
    qd_ref[:, t:2 * t] = jnp.where(rowq >= DA_QK_DIM, q, 0.0).astype(BF16)
    m_ref[...] = jnp.full(m_ref.shape, NEG, F32)
    acc_ref[...] = jnp.zeros(acc_ref.shape, F32)
    far = far_ref[h]

    def scores(kb):
        return _dot(k_ref[pl.ds(pl.multiple_of(kb * t, t), t), :], qd_ref[...])

    def values(kb):
        return v_ref[:, pl.ds(pl.multiple_of(kb * t, t), t)]

    def near_block(kb, near):
        bias = bn_ref[near]
        _softmax_step(scores(kb) + jnp.concatenate([bias, bias], axis=1), values(kb), m_ref, acc_ref, None)

    _far_chunks(jnp.maximum(qi - 1, 0), scores, values, m_ref, acc_ref, far)

    @pl.when(qi >= 1)
    def _():
        near_block(qi - 1, 1)

    near_block(qi, 0)

    acc = acc_ref[...]
    o = acc[0:DA_V_DIM] / acc[DA_V_DIM:DA_V_DIM + 1]
    o = o[:, 0:t] - lam_ref[0] * o[:, t:2 * t]
    ms = jnp.mean(o * o, axis=0, keepdims=True)
    o = o * lax.rsqrt(ms + LN_EPS) * g_ref[...] * post_scale
    o_ref[...] = o.T.astype(o_ref.dtype)


def _diff_attention(q_t, k, v_t, bias_near, bias_far, lam, subln_g, *, bsz, seq, t, post_scale):
    heads = q_t.shape[0] // DA_V_DIM
    nq = seq // t
    smem = pl.BlockSpec(memory_space=pltpu.SMEM)
    va = DA_V_DIM + BF16_ROWS
    return pl.pallas_call(
        functools.partial(_da_kernel, t=t, post_scale=post_scale),
        grid=(bsz, heads, nq),
        in_specs=[
            smem, smem,
            pl.BlockSpec((DA_V_DIM, t), lambda b, h, i: (h, b * nq + i)),
            pl.BlockSpec((None, seq, DA_V_DIM), lambda b, h, i: (b, 0, h)),
            pl.BlockSpec((DA_V_DIM, seq), lambda b, h, i: (h, b)),
            pl.BlockSpec((None, 2, t, t), lambda b, h, i: (h, 0, 0, 0)),
            pl.BlockSpec((DA_V_DIM, t), lambda b, h, i: (0, 0)),
        ],
        out_specs=pl.BlockSpec((None, t, DA_V_DIM), lambda b, h, i: (b, i, h)),
        out_shape=jax.ShapeDtypeStruct((bsz, seq, heads * DA_V_DIM), BF16),
        scratch_shapes=[
            pltpu.VMEM((DA_V_DIM, 2 * t), BF16),
            pltpu.VMEM((1, 2 * t), F32),
            pltpu.VMEM((va, 2 * t), F32),
        ],
        compiler_params=_params("parallel", "parallel", "arbitrary"),
        name="diff_attention",
    )(bias_far, lam, q_t, k, v_t, bias_near, subln_g)


def _dsa_kernel(far_ref, ixq_ref, ixw_ref, ixk_ref, saq_ref, sak_ref, sav_ref, bn_ref, o_ref,
                qi8_ref, keys_ref, hi_ref, qs4_ref, m_ref, acc_ref, *, t, topk):
    qi = pl.program_id(1)
    kf = float(topk)

    qi8_ref[...] = jnp.zeros(qi8_ref.shape, BF16)
    for h in range(IDX_HEADS):
        qi8_ref[0:IDX_DIM, h * t:(h + 1) * t] = ixq_ref[h * IDX_DIM:(h + 1) * IDX_DIM, :]

    def rows(kb, n=1):
        return pl.ds(pl.multiple_of(kb * t, t), n * t)

    def chunk(kb):
        return keys_ref[rows(kb), :]

    def index_keys(kb, n=1):
        kblk = ixk_ref[rows(kb, n), :]
        sc = None
        for h in range(IDX_HEADS):
            r = _dot(kblk, qi8_ref[:, h * t:(h + 1) * t])
            term = ixw_ref[h:h + 1, :] * jnp.maximum(r, 0.0)
            sc = term if sc is None else sc + term
        return jnp.where(jnp.abs(sc) < TINY, 0.0, sc)

    def store_keys(kb, sc, n=1):
        bits = lax.bitcast_convert_type(sc, I32)
        key = bits ^ ((bits >> 31) & 0x7FFFFFFF)
        zero_key = (ZERO_KEY_TOP - kb * t) - lax.broadcasted_iota(I32, (n * t, t), 0)
        keys_ref[rows(kb, n), :] = jnp.where(sc == 0.0, zero_key, key)
        hi_ref[rows(kb, n), :] = lax.bitcast_convert_type(bits & HI_MASK, F32).astype(BF16)

    def in_pairs(count, step):
        def pair(j, carry):
            step(2 * j, 2)
            return carry
        lax.fori_loop(0, count // 2, pair, 0)

        @pl.when(count % 2 == 1)
        def _():
            step(count - 1, 1)

    in_pairs(qi, lambda kb, n: store_keys(kb, index_keys(kb, n), n))
    krow = lax.broadcasted_iota(I32, (t, t), 0)
    qcol = lax.broadcasted_iota(I32, (t, t), 1)
    store_keys(qi, jnp.where(krow <= qcol, index_keys(qi), -jnp.inf))

    def count(fn):
        def body(kb, cnt):
            return cnt + _fold_rows(fn(kb, chunk(kb)), SUBLANES)
        cnt = lax.fori_loop(0, qi + 1, body, jnp.zeros((SUBLANES, t), I32))
        return jnp.sum(cnt.astype(F32), axis=0, keepdims=True)

    def count_hi(thr):
        def body(kb, cnt):
            hit = jnp.where(hi_ref[rows(kb), :] >= thr, jnp.ones((), BF16), jnp.zeros((), BF16))
            return cnt + _fold_rows(hit, BF16_ROWS).astype(F32)
        cnt = lax.fori_loop(0, qi + 1, body, jnp.zeros((BF16_ROWS, t), F32))
        return jnp.sum(cnt, axis=0, keepdims=True)

    def bf16_of(pattern):
        return lax.bitcast_convert_type(jnp.left_shift(pattern, 16), F32).astype(BF16)

    def code_value(code):
        pattern = code ^ ((code >> 31) & 0x7FFF)
        return bf16_of(jnp.where(code > 0, jnp.maximum(pattern, BF16_MIN_NORMAL), pattern))

    def hi_bit(i, st):
        code, open_, above = st
        cand = code + jnp.left_shift(jnp.int32(1), 15 - i)
        c = count_hi(code_value(cand))
        code = jnp.where(open_ > 0.0, jnp.where(c >= kf, cand, code), code)
        above = jnp.where(open_ > 0.0, jnp.where(c >= kf, above, c), above)
        return code, jnp.where(c == kf, 0.0, open_), above

    code, open_, above = lax.fori_loop(
        0, 16, hi_bit, (jnp.full((1, t), -2 ** 15, I32), jnp.ones((1, t), F32), jnp.zeros((1, t), F32)))

    value = code_value(code)
    higher = above
    need = kf - higher

    def recode(kb, carry):
        mid = ((chunk(kb) >> LOW_BITS) & MID_MASK) + BF16_MIN_NORMAL
        hi_ref[rows(kb), :] = jnp.where(hi_ref[rows(kb), :] == value, bf16_of(mid), jnp.zeros((), BF16))
        return carry

    lax.fori_loop(0, qi + 1, recode, 0)

    def mid_cond(st):
        i, _, _, _, n_open = st
        return jnp.logical_and(i < MID_BITS, n_open > 0.0)

    def mid_body(st):
        i, mid, open_, above, _ = st
        cand = mid + jnp.left_shift(jnp.int32(1), MID_BITS - 1 - i)
        c = count_hi(bf16_of(cand + BF16_MIN_NORMAL))
        mid = jnp.where(open_ > 0.0, jnp.where(c >= need, cand, mid), mid)
        above = jnp.where(open_ > 0.0, jnp.where(c >= need, above, higher + c), above)
        open_ = jnp.where(c == need, 0.0, open_)
        return i + 1, mid, open_, above, jnp.max(open_)

    _, mid, open_, above, _ = lax.while_loop(
        mid_cond, mid_body, (jnp.int32(0), jnp.zeros((1, t), I32), open_, above, jnp.max(open_)))

    def search_cond(st):
        i, _, _, _, n_open = st
        return jnp.logical_and(i < 32, n_open > 0.0)

    def search_body(st):
        i, tau, open_, above, _ = st
        cand = tau + jnp.left_shift(jnp.int32(1), 31 - i)
        c = count(lambda kb, blk: jnp.where(blk >= cand, 1, 0))
        tau = jnp.where(open_ > 0.0, jnp.where(c >= kf, cand, tau), tau)
        above = jnp.where(open_ > 0.0, jnp.where(c >= kf, above, c), above)
        open_ = jnp.where(c == kf, 0.0, open_)
        return i + 1, tau, open_, above, jnp.max(open_)

    _, tau, open_, above, _ = lax.while_loop(
        search_cond, search_body,
        (jnp.int32(32 - LOW_BITS), jnp.left_shift(code, 16) + jnp.left_shift(mid, LOW_BITS), open_, above,
         jnp.max(open_)))

    @pl.when(jnp.max(open_) > 0.0)
    def _():
        need = kf - above
        below = jnp.where(qcol <= krow, 1.0, 0.0).astype(BF16)

        def retire(kb, seen):
            blk = chunk(kb)
            tied = jnp.where(blk == tau, 1.0, 0.0).astype(BF16)
            rank = _dot(below, tied) + seen
            keys_ref[rows(kb), :] = jnp.where(blk == tau, jnp.where(rank > need, INT_MIN, blk), blk)
            return rank[t - 1:t, :]

        lax.fori_loop(0, qi + 1, retire, jnp.zeros((1, t), F32))

    qs4_ref[...] = jnp.zeros(qs4_ref.shape, BF16)
    for h in range(SA_HEADS):
        qs4_ref[0:SA_HEAD_DIM, h * t:(h + 1) * t] = saq_ref[h * SA_HEAD_DIM:(h + 1) * SA_HEAD_DIM, :]
    m_ref[...] = jnp.full(m_ref.shape, NEG, F32)
    acc_ref[...] = jnp.zeros(acc_ref.shape, F32)

    def scores(kb, near=None, n=1):
        s = _dot(sak_ref[rows(kb, n), :], qs4_ref[...])
        sel = keys_ref[rows(kb, n), :] >= tau
        parts = []
        for h in range(SA_HEADS):
            sh = s[:, h * t:(h + 1) * t]
            if near is not None:
                sh = sh + bn_ref[h, near]
            parts.append(jnp.where(sel, sh, NEG))
        return jnp.concatenate(parts, axis=1)

    far = jnp.concatenate([jnp.full((1, t), far_ref[h], F32) for h in range(SA_HEADS)], axis=1)

    def values(kb, n=1):
        return sav_ref[:, rows(kb, n)]

    in_pairs(jnp.maximum(qi - 1, 0),
             lambda kb, n: _softmax_step(scores(kb, None, n), values(kb, n), m_ref, acc_ref, far))

    @pl.when(qi >= 1)
    def _():
        _softmax_step(scores(qi - 1, 1), values(qi - 1), m_ref, acc_ref, None)

    _softmax_step(scores(qi, 0), values(qi), m_ref, acc_ref, None)

    acc = acc_ref[...]
    o = acc[0:SA_HEAD_DIM] / acc[SA_HEAD_DIM:SA_HEAD_DIM + 1]
    o = jnp.concatenate([o[:, h * t:(h + 1) * t] for h in range(SA_HEADS)], axis=0)
    o_ref[...] = o.T.astype(o_ref.dtype)


def _sparse_attention(ixq_t, ixw_t, ixk, saq_t, sak, sav_t, bias_near, bias_far, *, bsz, seq, t, topk):
    nq = seq // t
    assert seq <= ZERO_KEY_TOP
    smem = pl.BlockSpec(memory_space=pltpu.SMEM)
    qcols = lambda r: pl.BlockSpec((r, t), lambda b, i: (0, b * nq + i))
    full = lambda w: pl.BlockSpec((None, seq, w), lambda b, i: (b, 0, 0), pipeline_mode=pl.Buffered(1))
    va = SA_HEAD_DIM + BF16_ROWS
    return pl.pallas_call(
        functools.partial(_dsa_kernel, t=t, topk=topk),
        grid=(bsz, nq),
        in_specs=[
            smem,
            qcols(IDX_HEADS * IDX_DIM), qcols(IDX_HEADS), full(LANES),
            qcols(SA_WIDTH), full(LANES),
            pl.BlockSpec((SA_HEAD_DIM, seq), lambda b, i: (0, b), pipeline_mode=pl.Buffered(1)),
            _resident((SA_HEADS, 2, t, t)),
        ],
        out_specs=pl.BlockSpec((None, t, SA_WIDTH), lambda b, i: (b, i, 0)),
        out_shape=jax.ShapeDtypeStruct((bsz, seq, SA_WIDTH), BF16),
        scratch_shapes=[
            pltpu.VMEM((LANES, IDX_HEADS * t), BF16),
            pltpu.VMEM((seq, t), I32),
            pltpu.VMEM((seq, t), BF16),
            pltpu.VMEM((LANES, SA_HEADS * t), BF16),
            pltpu.VMEM((1, SA_HEADS * t), F32),
            pltpu.VMEM((va, SA_HEADS * t), F32),
        ],
        compiler_params=_params("parallel", "arbitrary"),
        name="sparse_attention",
    )(bias_far, ixq_t, ixw_t, ixk, saq_t, sak, sav_t, bias_near)


def _ssm_state_kernel(u_ref, w_ref, o_ref):
    o_ref[...] = _dot(u_ref[...], w_ref[...])


def _ssm_scan_kernel(loc_ref, a1_ref, a2_ref, o_ref):
    a1 = a1_ref[...]
    a2 = a2_ref[...]

    def body(n, s):
        o_ref[n] = s
        return a1 * s + a2 * pltpu.roll(s, SSM_STATE, axis=1) + loc_ref[n]

    lax.fori_loop(0, loc_ref.shape[0], body, jnp.zeros(a1.shape, F32))


def _ssm_out_kernel(u_ref, s_ref, toep_ref, wout_ref, o_ref):
    o_ref[...] = (_dot(u_ref[...], toep_ref[...])
                  + _dot(s_ref[...].astype(BF16), wout_ref[...]))


def _ssm_gate_kernel(y_ref, u_ref, d_ref, w_ref, o_ref):
    y = jax.nn.gelu(y_ref[...] + d_ref[...] * u_ref[...])
    o_ref[...] = (y * jax.nn.sigmoid(_dot(y.astype(BF16), w_ref[...]))).astype(o_ref.dtype)


def _ssm_tables(lam_re, lam_im, log_dt, b_re, b_im, c_re, c_im, tc):
    hp = lax.Precision.HIGHEST
    dt = jnp.exp(log_dt)[:, None]
    n = jnp.arange(tc + 1, dtype=F32)[:, None, None]
    mag = jnp.exp(lam_re * dt * n)
    pw_re = mag * jnp.cos(lam_im * dt * n)
    pw_im = mag * jnp.sin(lam_im * dt * n)
    den = lam_re * lam_re + lam_im * lam_im
    nr, ni = pw_re[1] - 1.0, pw_im[1]
    f_re = (nr * lam_re + ni * lam_im) / den
    f_im = (ni * lam_re - nr * lam_im) / den
    bb_re = f_re[..., None] * b_re - f_im[..., None] * b_im
    bb_im = f_re[..., None] * b_im + f_im[..., None] * b_re
    ca_re = c_re[None] * pw_re[:, :, None, :] - c_im[None] * pw_im[:, :, None, :]
    ca_im = c_re[None] * pw_im[:, :, None, :] + c_im[None] * pw_re[:, :, None, :]
    groups = lam_re.shape[0]
    inv_mag = jnp.exp(-lam_re * dt * n[:tc])
    iw_re = inv_mag * jnp.cos(lam_im * dt * n[:tc])
    iw_im = -inv_mag * jnp.sin(lam_im * dt * n[:tc])
    l_re = iw_re[..., None] * bb_re[None] - iw_im[..., None] * bb_im[None]
    l_im = iw_re[..., None] * bb_im[None] + iw_im[..., None] * bb_re[None]
    left = jnp.concatenate([l_re, l_im], axis=2).transpose(1, 0, 3, 2)
    right = jnp.concatenate([ca_re[:tc], -ca_im[:tc]], axis=3).transpose(1, 0, 2, 3)
    cw = tc * SSM_GROUP
    toep = jnp.einsum('gik,gjk->gij', left.reshape(groups, cw, 2 * SSM_STATE),
                      right.reshape(groups, cw, 2 * SSM_STATE), precision=hp)
    step = jnp.arange(cw) // SSM_GROUP
    toep = jnp.where(step[:, None] <= step[None, :], toep, 0.0)
    rev_re, rev_im = pw_re[tc - 1::-1][:tc], pw_im[tc - 1::-1][:tc]
    ws_re = rev_re[..., None] * bb_re[None] - rev_im[..., None] * bb_im[None]
    ws_im = rev_re[..., None] * bb_im[None] + rev_im[..., None] * bb_re[None]
    wstate = jnp.concatenate([ws_re, ws_im], axis=2)
    wstate = wstate.transpose(1, 0, 3, 2).reshape(groups, tc * SSM_GROUP, 2 * SSM_STATE)
    wout = jnp.concatenate([ca_re[1:], -ca_im[1:]], axis=3)
    wout = wout.transpose(1, 3, 0, 2).reshape(groups, 2 * SSM_STATE, tc * SSM_GROUP)
    dec_re, dec_im = pw_re[tc], pw_im[tc]
    a1 = jnp.concatenate([dec_re, dec_re], axis=1)
    a2 = jnp.concatenate([-dec_im, dec_im], axis=1)
    return toep.astype(BF16), wstate.astype(BF16), wout.astype(BF16), a1, a2


def _ssm(u, tables, d_skip, w_glu, *, bsz, seq, tm):
    toep, wstate, wout, a1, a2 = tables
    groups = toep.shape[0]
    tc = SSM_CHUNK
    nc = seq // tc
    rows = bsz * nc
    cw = tc * SSM_GROUP
    ug = u.astype(BF16).reshape(bsz, nc, tc, groups, SSM_GROUP).transpose(3, 0, 1, 2, 4).reshape(groups, rows, cw)
    gspec = lambda r, c: pl.BlockSpec((None, r, c), lambda g: (g, 0, 0))
    loc = pl.pallas_call(
        _ssm_state_kernel,
        grid=(groups,),
        in_specs=[gspec(rows, cw), gspec(cw, 2 * SSM_STATE)],
        out_specs=gspec(rows, 2 * SSM_STATE),
        out_shape=jax.ShapeDtypeStruct((groups, rows, 2 * SSM_STATE), F32),
        compiler_params=_params("parallel"),
        name="ssm_chunk_state",
    )(ug, wstate)
    gb = groups * bsz
    loc_t = loc.reshape(groups, bsz, nc, 2 * SSM_STATE).transpose(2, 0, 1, 3).reshape(nc, gb, 2 * SSM_STATE)
    a1r = jnp.repeat(a1, bsz, axis=0)
    a2r = jnp.repeat(a2, bsz, axis=0)
    rb = gb
    prev = pl.pallas_call(
        _ssm_scan_kernel,
        grid=(gb // rb,),
        in_specs=[pl.BlockSpec((nc, rb, 2 * SSM_STATE), lambda i: (0, i, 0)),
                  pl.BlockSpec((rb, 2 * SSM_STATE), lambda i: (i, 0)),
                  pl.BlockSpec((rb, 2 * SSM_STATE), lambda i: (i, 0))],
        out_specs=pl.BlockSpec((nc, rb, 2 * SSM_STATE), lambda i: (0, i, 0)),
        out_shape=jax.ShapeDtypeStruct((nc, gb, 2 * SSM_STATE), F32),
        compiler_params=_params("parallel"),
        name="ssm_chunk_scan",
    )(loc_t, a1r, a2r)
    prev_g = prev.reshape(nc, groups, bsz, 2 * SSM_STATE).transpose(1, 2, 0, 3).reshape(groups, rows, 2 * SSM_STATE)
    y = pl.pallas_call(
        _ssm_out_kernel,
        grid=(groups,),
        in_specs=[gspec(rows, cw), gspec(rows, 2 * SSM_STATE), gspec(cw, cw), gspec(2 * SSM_STATE, cw)],
        out_specs=gspec(rows, cw),
        out_shape=jax.ShapeDtypeStruct((groups, rows, cw), F32),
        compiler_params=_params("parallel"),
        name="ssm_chunk_out",
    )(ug, prev_g, toep, wout)
    y = y.reshape(groups, bsz, nc, tc, SSM_GROUP).transpose(1, 2, 3, 0, 4).reshape(bsz * seq, groups * SSM_GROUP)
    width = groups * SSM_GROUP
    return pl.pallas_call(
        _ssm_gate_kernel,
        grid=(bsz * seq // tm,),
        in_specs=[pl.BlockSpec((tm, width), lambda i: (i, 0)),
                  pl.BlockSpec((tm, width), lambda i: (i, 0)),
                  _resident((1, width)), _resident((width, width))],
        out_specs=pl.BlockSpec((tm, width), lambda i: (i, 0)),
        out_shape=jax.ShapeDtypeStruct((bsz * seq, width), BF16),
        compiler_params=_params("parallel"),
        name="ssm_gate",
    )(y, u, d_skip, w_glu)


def _t5_bucket(n):
    max_exact = REL_BUCKETS // 2
    nf = jnp.maximum(n, 1).astype(F32)
    large = max_exact + (jnp.log(nf / max_exact) / math.log(REL_MAX_DIST / max_exact)
                         * (REL_BUCKETS - max_exact)).astype(I32)
    large = jnp.minimum(large, REL_BUCKETS - 1)
    return jnp.where(n < max_exact, n, large)


def _bias_tiles(table, t):
    assert t >= REL_MAX_DIST
    table = table.astype(F32) * LOG2E
    key = jnp.arange(t)[:, None]
    query = jnp.arange(t)[None, :]
    dist = jnp.stack([query - key, t + query - key])
    bucket = _t5_bucket(jnp.maximum(dist, 0))
    tiles = jnp.zeros((table.shape[1],) + dist.shape, F32)
    for b in range(REL_BUCKETS):
        tiles = jnp.where(bucket[None] == b, table[b][:, None, None, None], tiles)
    return jnp.where(dist[None] >= 0, tiles, NEG), table[REL_BUCKETS - 1]


def _split_w_in(w_in):
    offs = [0]
    for s in IN_SIZES:
        offs.append(offs[-1] + s)
    da_q, da_k, da_v, ssm_u, sa_q, sa_k, sa_v, ix_q, ix_k, ix_w = (
        w_in[:, offs[j]:offs[j + 1]] for j in range(len(IN_SIZES)))
    w_scale = IDX_HEADS ** -0.5 * IDX_DIM ** -0.5
    pad = lambda w: jnp.pad(w, ((0, 0), (0, LANES - w.shape[1])))
    weights = [(da_q * (DA_QK_DIM ** -0.5 * LOG2E)).T, da_k, da_v.T, ssm_u,
               (sa_q * (SA_HEAD_DIM ** -0.5 * LOG2E)).T, pad(sa_k), sa_v.T,
               ix_q.T, pad(ix_k), (ix_w * w_scale).T]
    dtypes = [BF16, BF16, BF16, F32, BF16, BF16, BF16, BF16, BF16, F32]
    transposed = [True, False, True, False, True, False, True, True, False, True]
    return [w.astype(BF16) for w in weights], dtypes, transposed


def _plan(bsz, seq):
    rows = bsz * seq
    tm = 512 if rows % 512 == 0 else rows
    t_da = 1024 if seq % 1024 == 0 and seq >= 4096 else 256 if seq % 256 == 0 else 128
    t_sa = 512 if seq % 512 == 0 and seq >= 2048 else 256 if seq % 256 == 0 else 128
    tm_gate = 2048 if rows % 2048 == 0 else tm
    return dict(tm=tm, tm_gate=tm_gate, t_da=t_da, t_sa=t_sa)


def kernel(x, p, rel_bias, ffn1_w_gate, ffn1_w_up, ffn1_w_down, ln1_g, ln1_b, w_in, w_o, da_lam_q1, da_lam_k1, da_lam_q2, da_lam_k2, da_subln_g, ssm_lam_re, ssm_lam_im, ssm_log_dt, ssm_b_re, ssm_b_im, ssm_c_re, ssm_c_im, ssm_d, ssm_w_glu, ln2_g, ln2_b, ffn2_w_gate, ffn2_w_up, ffn2_w_down, ple_w_proj, ple_w_gate, ln3_g, ln3_b):
    bsz, seq, d = x.shape
    rows = bsz * seq
    plan = _plan(bsz, seq)
    tm, t_da, t_sa = plan["tm"], plan["t_da"], plan["t_sa"]
    topk = min(TOPK_MAX, seq // 4)
    da_near, da_far = _bias_tiles(rel_bias[:, :DA_HEADS], t_da)
    sa_near, sa_far = _bias_tiles(rel_bias[:, DA_HEADS:], t_sa)
    vec = lambda a: a.reshape(1, -1).astype(F32)
    da_w = DA_HEADS * DA_V_DIM

    h = x.reshape(rows, d)
    for i in range(DEPTH):
        lam_init = 0.8 - 0.6 * math.exp(-0.3 * i)
        weights, dtypes, transposed = _split_w_in(w_in[i])
        h, (da_q, da_k, da_v, ssm_u, sa_q, sa_k, sa_v, ix_q, ix_k, ix_w) = _ffn_inproj(
            h, ffn1_w_gate[i].astype(BF16), ffn1_w_up[i].astype(BF16), ffn1_w_down[i].astype(BF16),
            vec(ln1_g[i]), vec(ln1_b[i]), weights, dtypes, transposed, tm=tm)
        b3 = lambda a: a.reshape(bsz, seq, a.shape[-1])

        lam = (jnp.exp(jnp.sum(da_lam_q1[i].astype(F32) * da_lam_k1[i]))
               - jnp.exp(jnp.sum(da_lam_q2[i].astype(F32) * da_lam_k2[i])) + lam_init)
        subln = jnp.broadcast_to(da_subln_g[i].astype(F32)[:, None], (DA_V_DIM, t_da))
        o_da = _diff_attention(da_q, b3(da_k), da_v, da_near, da_far,
                               lam.reshape(1).astype(F32), subln, bsz=bsz, seq=seq, t=t_da,
                               post_scale=1.0 - lam_init)

        tables = _ssm_tables(ssm_lam_re[i].astype(F32), ssm_lam_im[i].astype(F32), ssm_log_dt[i].astype(F32),
                             ssm_b_re[i].astype(F32), ssm_b_im[i].astype(F32),
                             ssm_c_re[i].astype(F32), ssm_c_im[i].astype(F32), SSM_CHUNK)
        o_ssm = _ssm(ssm_u, tables, vec(ssm_d[i]), ssm_w_glu[i].astype(BF16), bsz=bsz, seq=seq,
                     tm=plan["tm_gate"])

        o_sa = _sparse_attention(ix_q, ix_w, b3(ix_k), sa_q, b3(sa_k), sa_v, sa_near, sa_far,
                                 bsz=bsz, seq=seq, t=t_sa, topk=topk)

        wo = w_o[i].astype(BF16)
        h = _mix_ffn(h, o_da.reshape(rows, da_w), o_ssm, o_sa.reshape(rows, SA_WIDTH), p[i].reshape(rows, -1),
                     [wo[:da_w], wo[da_w:da_w + SSM_WIDTH], wo[da_w + SSM_WIDTH:], vec(ln2_g[i]), vec(ln2_b[i]),
                      ffn2_w_gate[i].astype(BF16), ffn2_w_up[i].astype(BF16), ffn2_w_down[i].astype(BF16),
                      ple_w_proj[i].astype(BF16), ple_w_gate[i].astype(BF16), vec(ln3_g[i]), vec(ln3_b[i])],
                     tm=tm)
    return h.reshape(bsz, seq, d)
```

```python
import functools
import math

import jax
import jax.numpy as jnp
from jax import lax
from jax.experimental import pallas as pl
from jax.experimental.pallas import tpu as pltpu

F32 = jnp.float32
BF16 = jnp.bfloat16
I32 = jnp.int32

DEPTH = 2
DA_QK_DIM = 64
DA_V_DIM = 2 * DA_QK_DIM
DA_HEADS = 4
SSM_GROUP = 16
SSM_GROUPS = 16
SSM_STATE = 64
SSM_WIDTH = SSM_GROUP * SSM_GROUPS
SA_HEAD_DIM = 64
SA_HEADS = 4
SA_WIDTH = SA_HEADS * SA_HEAD_DIM
IDX_HEADS = 8
IDX_DIM = 32
TOPK_MAX = 256
REL_BUCKETS = 32
REL_MAX_DIST = 128
ALPHA = (2 * DEPTH) ** 0.25
LN_EPS = 1e-5
IN_SIZES = (DA_HEADS * 2 * DA_QK_DIM, DA_HEADS * 2 * DA_QK_DIM, DA_HEADS * DA_V_DIM,
            SSM_WIDTH, SA_WIDTH, SA_HEAD_DIM, SA_HEAD_DIM,
            IDX_HEADS * IDX_DIM, IDX_DIM, IDX_HEADS)

LANES = 128
SUBLANES = 8
BF16_ROWS = 16
NEG = -1e30
INT_MIN = -2 ** 31
HI_MASK = -2 ** 16
TINY = 2.0 ** -126
BF16_MIN_NORMAL = 0x0080
LOW_BITS = 2
MID_BITS = 16 - LOW_BITS
MID_MASK = 2 ** MID_BITS - 1
ZERO_KEY_TOP = 2 ** 16 - 1
LOG2E = math.log2(math.e)
VMEM_LIMIT = 56 * 1024 * 1024
SSM_CHUNK = 64

def _params(*sem):
    return pltpu.CompilerParams(dimension_semantics=sem, vmem_limit_bytes=VMEM_LIMIT)


def _resident(shape):
    return pl.BlockSpec(shape, lambda *_: (0,) * len(shape), pipeline_mode=pl.Buffered(1))


def _dot(a, b):
    return jnp.dot(a, b, preferred_element_type=F32)


def _dot_nt(a, b):
    return lax.dot_general(a, b, (((1,), (1,)), ((), ())), preferred_element_type=F32)


def _layer_norm(y, g, b):
    mu = jnp.mean(y, axis=-1, keepdims=True)
    d = y - mu
    var = jnp.mean(d * d, axis=-1, keepdims=True)
    return d * lax.rsqrt(var + LN_EPS) * g + b


def _fold_rows(x, rows):
    while x.shape[0] > rows:
        half = x.shape[0] // 2
        x = x[:half] + x[half:]
    return x


FF_CHUNK = 256


def _swiglu(xb, wg, wu, wd):
    d_ff = wg.shape[1]
    acc = None
    for c0 in range(0, d_ff, FF_CHUNK):
        c1 = min(c0 + FF_CHUNK, d_ff)
        gate = _dot(xb, wg[:, c0:c1])
        up = _dot(xb, wu[:, c0:c1])
        hid = (gate * jax.nn.sigmoid(gate) * up).astype(BF16)
        part = _dot(hid, wd[c0:c1, :])
        acc = part if acc is None else acc + part
    return acc


def _ffn_inproj_kernel(x_ref, wg, wu, wd, g_ref, b_ref, *refs, transposed):
    n = len(transposed)
    w_refs, x1_ref, o_refs = refs[:n], refs[n], refs[n + 1:]
    x = x_ref[...]
    x1 = _layer_norm(ALPHA * x + 0.5 * _swiglu(x.astype(BF16), wg, wu, wd), g_ref[...], b_ref[...])
    x1_ref[...] = x1
    xb = x1.astype(BF16)
    for w_ref, o_ref, tr in zip(w_refs, o_refs, transposed):
        out = _dot_nt(w_ref[...], xb) if tr else _dot(xb, w_ref[...])
        o_ref[...] = out.astype(o_ref.dtype)


def _ffn_inproj(x, wg, wu, wd, g, b, weights, dtypes, transposed, *, tm):
    t_rows, d = x.shape
    row = lambda w: pl.BlockSpec((tm, w), lambda i: (i, 0))
    out_specs, out_shape = [row(d)], [jax.ShapeDtypeStruct((t_rows, d), F32)]
    for w, dt, tr in zip(weights, dtypes, transposed):
        if tr:
            out_specs.append(pl.BlockSpec((w.shape[0], tm), lambda i: (0, i)))
            out_shape.append(jax.ShapeDtypeStruct((w.shape[0], t_rows), dt))
        else:
            out_specs.append(row(w.shape[1]))
            out_shape.append(jax.ShapeDtypeStruct((t_rows, w.shape[1]), dt))
    consts = [wg, wu, wd, g, b] + list(weights)
    outs = pl.pallas_call(
        functools.partial(_ffn_inproj_kernel, transposed=tuple(transposed)),
        grid=(t_rows // tm,),
        in_specs=[row(d)] + [_resident(c.shape) for c in consts],
        out_specs=out_specs,
        out_shape=out_shape,
        compiler_params=_params("parallel"),
        name="ffn_ln_in_proj",
    )(x, *consts)
    return outs[0], outs[1:]


def _mix_ffn_kernel(x_ref, da_ref, ssm_ref, sa_ref, p_ref, w1, w2, w3, g2_ref, b2_ref,
                    wg, wu, wd, wpp, wpg, g3_ref, b3_ref, o_ref):
    mix = _dot(da_ref[...], w1[...]) + _dot(ssm_ref[...], w2[...]) + _dot(sa_ref[...], w3[...])
    x = _layer_norm(ALPHA * x_ref[...] + mix, g2_ref[...], b2_ref[...])
    xb = x.astype(BF16)
    y = ALPHA * x + 0.5 * _swiglu(xb, wg, wu, wd)
    y = y + _dot(p_ref[...].astype(BF16), wpp[...]) * jax.nn.sigmoid(_dot(xb, wpg[...]))
    o_ref[...] = _layer_norm(y, g3_ref[...], b3_ref[...])


def _mix_ffn(x, o_da, o_ssm, o_sa, p, consts, *, tm):
    t_rows, d = x.shape
    row = lambda a: pl.BlockSpec((tm, a.shape[1]), lambda i: (i, 0))
    acts = [x, o_da, o_ssm, o_sa, p]
    return pl.pallas_call(
        _mix_ffn_kernel,
        grid=(t_rows // tm,),
        in_specs=[row(a) for a in acts] + [_resident(c.shape) for c in consts],
        out_specs=pl.BlockSpec((tm, d), lambda i: (i, 0)),
        out_shape=jax.ShapeDtypeStruct((t_rows, d), F32),
        compiler_params=_params("parallel"),
        name="out_proj_ln_ffn_ple_ln",
    )(*acts, *consts)


def _softmax_probs(s, m_ref, shift):
    m_prev = m_ref[...]
    m_cur = jnp.max(s, axis=0, keepdims=True)
    if shift is not None:
        m_cur = m_cur + shift
    m_next = jnp.maximum(m_prev, m_cur)
    m_ref[...] = m_next
    p = jnp.exp2(s - (m_next if shift is None else m_next - shift))
    return p.astype(BF16), jnp.exp2(m_prev - m_next)


def _accumulate(acc_ref, rescale, v_t, p):
    v_aug = jnp.concatenate([v_t, jnp.ones((BF16_ROWS, v_t.shape[1]), v_t.dtype)], axis=0)
    acc_ref[...] = rescale * acc_ref[...] + _dot(v_aug, p)


def _softmax_step(s, v_t, m_ref, acc_ref, shift):
    p, rescale = _softmax_probs(s, m_ref, shift)
    _accumulate(acc_ref, rescale, v_t, p)


def _far_chunks(n, scores, values, m_ref, acc_ref, shift):
    def pair(j, carry):
        s0, s1 = scores(2 * j), scores(2 * j + 1)
        _softmax_step(s0, values(2 * j), m_ref, acc_ref, shift)
        _softmax_step(s1, values(2 * j + 1), m_ref, acc_ref, shift)
        return carry

    lax.fori_loop(0, n // 2, pair, 0)

    @pl.when(n % 2 == 1)
    def _():
        _softmax_step(scores(n - 1), values(n - 1), m_ref, acc_ref, shift)


def _da_kernel(far_ref, lam_ref, q_ref, k_ref, v_ref, bn_ref, g_ref, o_ref,
               qd_ref, m_ref, acc_ref, *, t, post_scale):
    h = pl.program_id(1)
    qi = pl.program_id(2)
    q = q_ref[...].astype(F32)
    rowq = lax.broadcasted_iota(I32, q.shape, 0)
    qd_ref[:, 0:t] = jnp.where(rowq < DA_QK_DIM, q, 0.0).astype(BF16)
    qd_ref[:, t:2 * t] = jnp.where(rowq >= DA_QK_DIM, q, 0.0).astype(BF16)
    m_ref[...] = jnp.full(m_ref.shape, NEG, F32)
    acc_ref[...] = jnp.zeros(acc_ref.shape, F32)
    far = far_ref[h]

    def scores(kb):
        return _dot(k_ref[pl.ds(pl.multiple_of(kb * t, t), t), :], qd_ref[...])

    def values(kb):
        return v_ref[:, pl.ds(pl.multiple_of(kb * t, t), t)]

    def near_scores(kb, near):
        bias = bn_ref[near]
        return scores(kb) + jnp.concatenate([bias, bias], axis=1)

    _far_chunks(jnp.maximum(qi - 1, 0), scores, values, m_ref, acc_ref, far)

    @pl.when(qi >= 1)
    def _():
        s1, s0 = near_scores(qi - 1, 1), near_scores(qi, 0)
        _softmax_step(s1, values(qi - 1), m_ref, acc_ref, None)
        _softmax_step(s0, values(qi), m_ref, acc_ref, None)

    @pl.when(qi == 0)
    def _():
        _softmax_step(near_scores(0, 0), values(0), m_ref, acc_ref, None)

    acc = acc_ref[...]
    o = acc[0:DA_V_DIM] / acc[DA_V_DIM:DA_V_DIM + 1]
    o = o[:, 0:t] - lam_ref[0] * o[:, t:2 * t]
    ms = jnp.mean(o * o, axis=0, keepdims=True)
    o = o * lax.rsqrt(ms + LN_EPS) * g_ref[...] * post_scale
    o_ref[...] = o.T.astype(o_ref.dtype)


def _diff_attention(q_t, k, v_t, bias_near, bias_far, lam, subln_g, *, bsz, seq, t, post_scale):
    heads = q_t.shape[0] // DA_V_DIM
    nq = seq // t
    smem = pl.BlockSpec(memory_space=pltpu.SMEM)
    va = DA_V_DIM + BF16_ROWS
    return pl.pallas_call(
        functools.partial(_da_kernel, t=t, post_scale=post_scale),
        grid=(bsz, heads, nq),
        in_specs=[
            smem, smem,
            pl.BlockSpec((DA_V_DIM, t), lambda b, h, i: (h, b * nq + i)),
            pl.BlockSpec((None, seq, DA_V_DIM), lambda b, h, i: (b, 0, h)),
            pl.BlockSpec((DA_V_DIM, seq), lambda b, h, i: (h, b)),
            pl.BlockSpec((None, 2, t, t), lambda b, h, i: (h, 0, 0, 0)),
            pl.BlockSpec((DA_V_DIM, t), lambda b, h, i: (0, 0)),
        ],
        out_specs=pl.BlockSpec((None, t, DA_V_DIM), lambda b, h, i: (b, i, h)),
        out_shape=jax.ShapeDtypeStruct((bsz, seq, heads * DA_V_DIM), BF16),
        scratch_shapes=[
            pltpu.VMEM((DA_V_DIM, 2 * t), BF16),
            pltpu.VMEM((1, 2 * t), F32),
            pltpu.VMEM((va, 2 * t), F32),
        ],
        compiler_params=_params("parallel", "parallel", "arbitrary"),
        name="diff_attention",
    )(bias_far, lam, q_t, k, v_t, bias_near, subln_g)


def _dsa_kernel(far_ref, ixq_ref, ixw_ref, ixk_ref, saq_ref, sak_ref, sav_ref, bn_ref, o_ref,
                qi8_ref, keys_ref, hi_ref, qs4_ref, m_ref, acc_ref, *, t, topk):
    qi = pl.program_id(1)
    kf = float(topk)

    qi8_ref[...] = jnp.zeros(qi8_ref.shape, BF16)
    for h in range(IDX_HEADS):
        qi8_ref[0:IDX_DIM, h * t:(h + 1) * t] = ixq_ref[h * IDX_DIM:(h + 1) * IDX_DIM, :]

    def rows(kb, n=1):
        return pl.ds(pl.multiple_of(kb * t, t), n * t)

    def chunk(kb):
        return keys_ref[rows(kb), :]

    def index_keys(kb, n=1):
        kblk = ixk_ref[rows(kb, n), :]
        sc = None
        for h in range(IDX_HEADS):
            r = _dot(kblk, qi8_ref[:, h * t:(h + 1) * t])
            term = ixw_ref[h:h + 1, :] * jnp.maximum(r, 0.0)
            sc = term if sc is None else sc + term
        return jnp.where(jnp.abs(sc) < TINY, 0.0, sc)

    def store_keys(kb, sc, n=1):
        bits = lax.bitcast_convert_type(sc, I32)
        key = bits ^ ((bits >> 31) & 0x7FFFFFFF)
        zero_key = (ZERO_KEY_TOP - kb * t) - lax.broadcasted_iota(I32, (n * t, t), 0)
        keys_ref[rows(kb, n), :] = jnp.where(sc == 0.0, zero_key, key)
        hi_ref[rows(kb, n), :] = lax.bitcast_convert_type(bits & HI_MASK, F32).astype(BF16)

    def in_pairs(count, step):
        def pair(j, carry):
            step(2 * j, 2)
            return carry
        lax.fori_loop(0, count // 2, pair, 0)

        @pl.when(count % 2 == 1)
        def _():
            step(count - 1, 1)

    in_pairs(qi, lambda kb, n: store_keys(kb, index_keys(kb, n), n))
    krow = lax.broadcasted_iota(I32, (t, t), 0)
    qcol = lax.broadcasted_iota(I32, (t, t), 1)
    store_keys(qi, jnp.where(krow <= qcol, index_keys(qi), -jnp.inf))

    def count(fn):
        def body(kb, cnt):
            return cnt + _fold_rows(fn(kb, chunk(kb)), SUBLANES)
        cnt = lax.fori_loop(0, qi + 1, body, jnp.zeros((SUBLANES, t), I32))
        return jnp.sum(cnt.astype(F32), axis=0, keepdims=True)

    def count_hi(thr):
        def body(kb, cnt):
            hit = jnp.where(hi_ref[rows(kb), :] >= thr, jnp.ones((), BF16), jnp.zeros((), BF16))
            return cnt + _fold_rows(hit, BF16_ROWS).astype(F32)
        cnt = lax.fori_loop(0, qi + 1, body, jnp.zeros((BF16_ROWS, t), F32))
        return jnp.sum(cnt, axis=0, keepdims=True)

    def bf16_of(pattern):
        return lax.bitcast_convert_type(jnp.left_shift(pattern, 16), F32).astype(BF16)

    def code_value(code):
        pattern = code ^ ((code >> 31) & 0x7FFF)
        return bf16_of(jnp.where(code > 0, jnp.maximum(pattern, BF16_MIN_NORMAL), pattern))

    def hi_bit(i, st):
        code, open_, above = st
        cand = code + jnp.left_shift(jnp.int32(1), 15 - i)
        c = count_hi(code_value(cand))
        code = jnp.where(open_ > 0.0, jnp.where(c >= kf, cand, code), code)
        above = jnp.where(open_ > 0.0, jnp.where(c >= kf, above, c), above)
        return code, jnp.where(c == kf, 0.0, open_), above

    code, open_, above = lax.fori_loop(
        0, 16, hi_bit, (jnp.full((1, t), -2 ** 15, I32), jnp.ones((1, t), F32), jnp.zeros((1, t), F32)))

    value = code_value(code)
    higher = above
    need = kf - higher

    def recode(kb, carry):
        mid = ((chunk(kb) >> LOW_BITS) & MID_MASK) + BF16_MIN_NORMAL
        hi_ref[rows(kb), :] = jnp.where(hi_ref[rows(kb), :] == value, bf16_of(mid), jnp.zeros((), BF16))
        return carry

    lax.fori_loop(0, qi + 1, recode, 0)

    def mid_cond(st):
        i, _, _, _, n_open = st
        return jnp.logical_and(i < MID_BITS, n_open > 0.0)

    def mid_body(st):
        i, mid, open_, above, _ = st
        cand = mid + jnp.left_shift(jnp.int32(1), MID_BITS - 1 - i)
        c = count_hi(bf16_of(cand + BF16_MIN_NORMAL))
        mid = jnp.where(open_ > 0.0, jnp.where(c >= need, cand, mid), mid)
        above = jnp.where(open_ > 0.0, jnp.where(c >= need, above, higher + c), above)
        open_ = jnp.where(c == need, 0.0, open_)
        return i + 1, mid, open_, above, jnp.max(open_)

    _, mid, open_, above, _ = lax.while_loop(
        mid_cond, mid_body, (jnp.int32(0), jnp.zeros((1, t), I32), open_, above, jnp.max(open_)))

    def search_cond(st):
        i, _, _, _, n_open = st
        return jnp.logical_and(i < 32, n_open > 0.0)

    def search_body(st):
        i, tau, open_, above, _ = st
        cand = tau + jnp.left_shift(jnp.int32(1), 31 - i)
        c = count(lambda kb, blk: jnp.where(blk >= cand, 1, 0))
        tau = jnp.where(open_ > 0.0, jnp.where(c >= kf, cand, tau), tau)
        above = jnp.where(open_ > 0.0, jnp.where(c >= kf, above, c), above)
        open_ = jnp.where(c == kf, 0.0, open_)
        return i + 1, tau, open_, above, jnp.max(open_)

    _, tau, open_, above, _ = lax.while_loop(
        search_cond, search_body,
        (jnp.int32(32 - LOW_BITS), jnp.left_shift(code, 16) + jnp.left_shift(mid, LOW_BITS), open_, above,
         jnp.max(open_)))

    @pl.when(jnp.max(open_) > 0.0)
    def _():
        need = kf - above
        below = jnp.where(qcol <= krow, 1.0, 0.0).astype(BF16)

        def retire(kb, seen):
            blk = chunk(kb)
            tied = jnp.where(blk == tau, 1.0, 0.0).astype(BF16)
            rank = _dot(below, tied) + seen
            keys_ref[rows(kb), :] = jnp.where(blk == tau, jnp.where(rank > need, INT_MIN, blk), blk)
            return rank[t - 1:t, :]

        lax.fori_loop(0, qi + 1, retire, jnp.zeros((1, t), F32))

    qs4_ref[...] = jnp.zeros(qs4_ref.shape, BF16)
    for h in range(SA_HEADS):
        qs4_ref[0:SA_HEAD_DIM, h * t:(h + 1) * t] = saq_ref[h * SA_HEAD_DIM:(h + 1) * SA_HEAD_DIM, :]
    m_ref[...] = jnp.full(m_ref.shape, NEG, F32)
    acc_ref[...] = jnp.zeros(acc_ref.shape, F32)

    def scores(kb, near=None, n=1):
        s = _dot(sak_ref[rows(kb, n), :], qs4_ref[...])
        sel = keys_ref[rows(kb, n), :] >= tau
        parts = []
        for h in range(SA_HEADS):
            sh = s[:, h * t:(h + 1) * t]
            if near is not None:
                sh = sh + bn_ref[h, near]
            parts.append(jnp.where(sel, sh, NEG))
        return jnp.concatenate(parts, axis=1)

    far = jnp.concatenate([jnp.full((1, t), far_ref[h], F32) for h in range(SA_HEADS)], axis=1)

    def values(kb, n=1):
        return sav_ref[:, rows(kb, n)]

    in_pairs(jnp.maximum(qi - 1, 0),
             lambda kb, n: _softmax_step(scores(kb, None, n), values(kb, n), m_ref, acc_ref, far))

    @pl.when(qi >= 1)
    def _():
        _softmax_step(scores(qi - 1, 1), values(qi - 1), m_ref, acc_ref, None)

    _softmax_step(scores(qi, 0), values(qi), m_ref, acc_ref, None)

    acc = acc_ref[...]
    o = acc[0:SA_HEAD_DIM] / acc[SA_HEAD_DIM:SA_HEAD_DIM + 1]
    o = jnp.concatenate([o[:, h * t:(h + 1) * t] for h in range(SA_HEADS)], axis=0)
    o_ref[...] = o.T.astype(o_ref.dtype)


def _sparse_attention(ixq_t, ixw_t, ixk, saq_t, sak, sav_t, bias_near, bias_far, *, bsz, seq, t, topk):
    nq = seq // t
    assert seq <= ZERO_KEY_TOP
    smem = pl.BlockSpec(memory_space=pltpu.SMEM)
    qcols = lambda r: pl.BlockSpec((r, t), lambda b, i: (0, b * nq + i))
    full = lambda w: pl.BlockSpec((None, seq, w), lambda b, i: (b, 0, 0), pipeline_mode=pl.Buffered(1))
    va = SA_HEAD_DIM + BF16_ROWS
    return pl.pallas_call(
        functools.partial(_dsa_kernel, t=t, topk=topk),
        grid=(bsz, nq),
        in_specs=[
            smem,
            qcols(IDX_HEADS * IDX_DIM), qcols(IDX_HEADS), full(LANES),
            qcols(SA_WIDTH), full(LANES),
            pl.BlockSpec((SA_HEAD_DIM, seq), lambda b, i: (0, b), pipeline_mode=pl.Buffered(1)),
            _resident((SA_HEADS, 2, t, t)),
        ],
        out_specs=pl.BlockSpec((None, t, SA_WIDTH), lambda b, i: (b, i, 0)),
        out_shape=jax.ShapeDtypeStruct((bsz, seq, SA_WIDTH), BF16),
        scratch_shapes=[
            pltpu.VMEM((LANES, IDX_HEADS * t), BF16),
            pltpu.VMEM((seq, t), I32),
            pltpu.VMEM((seq, t), BF16),
            pltpu.VMEM((LANES, SA_HEADS * t), BF16),
            pltpu.VMEM((1, SA_HEADS * t), F32),
            pltpu.VMEM((va, SA_HEADS * t), F32),
        ],
        compiler_params=_params("parallel", "arbitrary"),
        name="sparse_attention",
    )(bias_far, ixq_t, ixw_t, ixk, saq_t, sak, sav_t, bias_near)


def _ssm_state_kernel(u_ref, w_ref, o_ref):
    o_ref[...] = _dot(u_ref[...], w_ref[...])


def _ssm_scan_kernel(loc_ref, a1_ref, a2_ref, o_ref):
    a1 = a1_ref[...]
    a2 = a2_ref[...]

    def body(n, s):
        o_ref[n] = s
        return a1 * s + a2 * pltpu.roll(s, SSM_STATE, axis=1) + loc_ref[n]

    lax.fori_loop(0, loc_ref.shape[0], body, jnp.zeros(a1.shape, F32))


def _ssm_out_kernel(u_ref, s_ref, toep_ref, wout_ref, o_ref):
    o_ref[...] = (_dot(u_ref[...], toep_ref[...])
                  + _dot(s_ref[...].astype(BF16), wout_ref[...]))


def _ssm_gate_kernel(y_ref, u_ref, d_ref, w_ref, o_ref):
    y = jax.nn.gelu(y_ref[...] + d_ref[...] * u_ref[...])
    o_ref[...] = (y * jax.nn.sigmoid(_dot(y.astype(BF16), w_ref[...]))).astype(o_ref.dtype)


def _ssm_tables(lam_re, lam_im, log_dt, b_re, b_im, c_re, c_im, tc):
    hp = lax.Precision.HIGHEST
    dt = jnp.exp(log_dt)[:, None]
    n = jnp.arange(tc + 1, dtype=F32)[:, None, None]
    mag = jnp.exp(lam_re * dt * n)
    pw_re = mag * jnp.cos(lam_im * dt * n)
    pw_im = mag * jnp.sin(lam_im * dt * n)
    den = lam_re * lam_re + lam_im * lam_im
    nr, ni = pw_re[1] - 1.0, pw_im[1]
    f_re = (nr * lam_re + ni * lam_im) / den
    f_im = (ni * lam_re - nr * lam_im) / den
    bb_re = f_re[..., None] * b_re - f_im[..., None] * b_im
    bb_im = f_re[..., None] * b_im + f_im[..., None] * b_re
    ca_re = c_re[None] * pw_re[:, :, None, :] - c_im[None] * pw_im[:, :, None, :]
    ca_im = c_re[None] * pw_im[:, :, None, :] + c_im[None] * pw_re[:, :, None, :]
    groups = lam_re.shape[0]
    inv_mag = jnp.exp(-lam_re * dt * n[:tc])
    iw_re = inv_mag * jnp.cos(lam_im * dt * n[:tc])
    iw_im = -inv_mag * jnp.sin(lam_im * dt * n[:tc])
    l_re = iw_re[..., None] * bb_re[None] - iw_im[..., None] * bb_im[None]
    l_im = iw_re[..., None] * bb_im[None] + iw_im[..., None] * bb_re[None]
    left = jnp.concatenate([l_re, l_im], axis=2).transpose(1, 0, 3, 2)
    right = jnp.concatenate([ca_re[:tc], -ca_im[:tc]], axis=3).transpose(1, 0, 2, 3)
    cw = tc * SSM_GROUP
    toep = jnp.einsum('gik,gjk->gij', left.reshape(groups, cw, 2 * SSM_STATE),
                      right.reshape(groups, cw, 2 * SSM_STATE), precision=hp)
    step = jnp.arange(cw) // SSM_GROUP
    toep = jnp.where(step[:, None] <= step[None, :], toep, 0.0)
    rev_re, rev_im = pw_re[tc - 1::-1][:tc], pw_im[tc - 1::-1][:tc]
    ws_re = rev_re[..., None] * bb_re[None] - rev_im[..., None] * bb_im[None]
    ws_im = rev_re[..., None] * bb_im[None] + rev_im[..., None] * bb_re[None]
    wstate = jnp.concatenate([ws_re, ws_im], axis=2)
    wstate = wstate.transpose(1, 0, 3, 2).reshape(groups, tc * SSM_GROUP, 2 * SSM_STATE)
    wout = jnp.concatenate([ca_re[1:], -ca_im[1:]], axis=3)
    wout = wout.transpose(1, 3, 0, 2).reshape(groups, 2 * SSM_STATE, tc * SSM_GROUP)
    dec_re, dec_im = pw_re[tc], pw_im[tc]
    a1 = jnp.concatenate([dec_re, dec_re], axis=1)
    a2 = jnp.concatenate([-dec_im, dec_im], axis=1)
    return toep.astype(BF16), wstate.astype(BF16), wout.astype(BF16), a1, a2


def _ssm(u, tables, d_skip, w_glu, *, bsz, seq, tm):
    toep, wstate, wout, a1, a2 = tables
    groups = toep.shape[0]
    tc = SSM_CHUNK
    nc = seq // tc
    rows = bsz * nc
    cw = tc * SSM_GROUP
    ug = u.astype(BF16).reshape(bsz, nc, tc, groups, SSM_GROUP).transpose(3, 0, 1, 2, 4).reshape(groups, rows, cw)
    gspec = lambda r, c: pl.BlockSpec((None, r, c), lambda g: (g, 0, 0))
    loc = pl.pallas_call(
        _ssm_state_kernel,
        grid=(groups,),
        in_specs=[gspec(rows, cw), gspec(cw, 2 * SSM_STATE)],
        out_specs=gspec(rows, 2 * SSM_STATE),
        out_shape=jax.ShapeDtypeStruct((groups, rows, 2 * SSM_STATE), F32),
        compiler_params=_params("parallel"),
        name="ssm_chunk_state",
    )(ug, wstate)
    gb = groups * bsz
    loc_t = loc.reshape(groups, bsz, nc, 2 * SSM_STATE).transpose(2, 0, 1, 3).reshape(nc, gb, 2 * SSM_STATE)
    a1r = jnp.repeat(a1, bsz, axis=0)
    a2r = jnp.repeat(a2, bsz, axis=0)
    rb = gb
    prev = pl.pallas_call(
        _ssm_scan_kernel,
        grid=(gb // rb,),
        in_specs=[pl.BlockSpec((nc, rb, 2 * SSM_STATE), lambda i: (0, i, 0)),
                  pl.BlockSpec((rb, 2 * SSM_STATE), lambda i: (i, 0)),
                  pl.BlockSpec((rb, 2 * SSM_STATE), lambda i: (i, 0))],
        out_specs=pl.BlockSpec((nc, rb, 2 * SSM_STATE), lambda i: (0, i, 0)),
        out_shape=jax.ShapeDtypeStruct((nc, gb, 2 * SSM_STATE), F32),
        compiler_params=_params("parallel"),
        name="ssm_chunk_scan",
    )(loc_t, a1r, a2r)
    prev_g = prev.reshape(nc, groups, bsz, 2 * SSM_STATE).transpose(1, 2, 0, 3).reshape(groups, rows, 2 * SSM_STATE)
    y = pl.pallas_call(
        _ssm_out_kernel,
        grid=(groups,),
        in_specs=[gspec(rows, cw), gspec(rows, 2 * SSM_STATE), gspec(cw, cw), gspec(2 * SSM_STATE, cw)],
        out_specs=gspec(rows, cw),
        out_shape=jax.ShapeDtypeStruct((groups, rows, cw), F32),
        compiler_params=_params("parallel"),
        name="ssm_chunk_out",
    )(ug, prev_g, toep, wout)
    y = y.reshape(groups, bsz, nc, tc, SSM_GROUP).transpose(1, 2, 3, 0, 4).reshape(bsz * seq, groups * SSM_GROUP)
    width = groups * SSM_GROUP
    return pl.pallas_call(
        _ssm_gate_kernel,
        grid=(bsz * seq // tm,),
        in_specs=[pl.BlockSpec((tm, width), lambda i: (i, 0)),
                  pl.BlockSpec((tm, width), lambda i: (i, 0)),
                  _resident((1, width)), _resident((width, width))],
        out_specs=pl.BlockSpec((tm, width), lambda i: (i, 0)),
        out_shape=jax.ShapeDtypeStruct((bsz * seq, width), BF16),
        compiler_params=_params("parallel"),
        name="ssm_gate",
    )(y, u, d_skip, w_glu)


def _t5_bucket(n):
    max_exact = REL_BUCKETS // 2
    nf = jnp.maximum(n, 1).astype(F32)
    large = max_exact + (jnp.log(nf / max_exact) / math.log(REL_MAX_DIST / max_exact)
                         * (REL_BUCKETS - max_exact)).astype(I32)
    large = jnp.minimum(large, REL_BUCKETS - 1)
    return jnp.where(n < max_exact, n, large)


def _bias_tiles(table, t):
    assert t >= REL_MAX_DIST
    table = table.astype(F32) * LOG2E
    key = jnp.arange(t)[:, None]
    query = jnp.arange(t)[None, :]
    dist = jnp.stack([query - key, t + query - key])
    bucket = _t5_bucket(jnp.maximum(dist, 0))
    tiles = jnp.zeros((table.shape[1],) + dist.shape, F32)
    for b in range(REL_BUCKETS):
        tiles = jnp.where(bucket[None] == b, table[b][:, None, None, None], tiles)
    return jnp.where(dist[None] >= 0, tiles, NEG), table[REL_BUCKETS - 1]


def _split_w_in(w_in):
    offs = [0]
    for s in IN_SIZES:
        offs.append(offs[-1] + s)
    da_q, da_k, da_v, ssm_u, sa_q, sa_k, sa_v, ix_q, ix_k, ix_w = (
        w_in[:, offs[j]:offs[j + 1]] for j in range(len(IN_SIZES)))
    w_scale = IDX_HEADS ** -0.5 * IDX_DIM ** -0.5
    pad = lambda w: jnp.pad(w, ((0, 0), (0, LANES - w.shape[1])))
    weights = [(da_q * (DA_QK_DIM ** -0.5 * LOG2E)).T, da_k, da_v.T, ssm_u,
               (sa_q * (SA_HEAD_DIM ** -0.5 * LOG2E)).T, pad(sa_k), sa_v.T,
               ix_q.T, pad(ix_k), (ix_w * w_scale).T]
    dtypes = [BF16, BF16, BF16, F32, BF16, BF16, BF16, BF16, BF16, F32]
    transposed = [True, False, True, False, True, False, True, True, False, True]
    return [w.astype(BF16) for w in weights], dtypes, transposed


def _plan(bsz, seq):
    rows = bsz * seq
    tm = 512 if rows % 512 == 0 else rows
    t_da = 1024 if seq % 1024 == 0 and seq >= 4096 else 256 if seq % 256 == 0 else 128
    t_sa = 512 if seq % 512 == 0 and seq >= 2048 else 256 if seq % 256 == 0 else 128
    tm_gate = 2048 if rows % 2048 == 0 else tm
    return dict(tm=tm, tm_gate=tm_gate, t_da=t_da, t_sa=t_sa)


def kernel(x, p, rel_bias, ffn1_w_gate, ffn1_w_up, ffn1_w_down, ln1_g, ln1_b, w_in, w_o, da_lam_q1, da_lam_k1, da_lam_q2, da_lam_k2, da_subln_g, ssm_lam_re, ssm_lam_im, ssm_log_dt, ssm_b_re, ssm_b_im, ssm_c_re, ssm_c_im, ssm_d, ssm_w_glu, ln2_g, ln2_b, ffn2_w_gate, ffn2_w_up, ffn2_w_down, ple_w_proj, ple_w_gate, ln3_g, ln3_b):
    bsz, seq, d = x.shape
    rows = bsz * seq
    plan = _plan(bsz, seq)
    tm, t_da, t_sa = plan["tm"], plan["t_da"], plan["t_sa"]
    topk = min(TOPK_MAX, seq // 4)
    da_near, da_far = _bias_tiles(rel_bias[:, :DA_HEADS], t_da)
    sa_near, sa_far = _bias_tiles(rel_bias[:, DA_HEADS:], t_sa)
    vec = lambda a: a.reshape(1, -1).astype(F32)
    da_w = DA_HEADS * DA_V_DIM

    h = x.reshape(rows, d)
    for i in range(DEPTH):
        lam_init = 0.8 - 0.6 * math.exp(-0.3 * i)
        weights, dtypes, transposed = _split_w_in(w_in[i])
        h, (da_q, da_k, da_v, ssm_u, sa_q, sa_k, sa_v, ix_q, ix_k, ix_w) = _ffn_inproj(
            h, ffn1_w_gate[i].astype(BF16), ffn1_w_up[i].astype(BF16), ffn1_w_down[i].astype(BF16),
            vec(ln1_g[i]), vec(ln1_b[i]), weights, dtypes, transposed, tm=tm)
        b3 = lambda a: a.reshape(bsz, seq, a.shape[-1])

        lam = (jnp.exp(jnp.sum(da_lam_q1[i].astype(F32) * da_lam_k1[i]))
               - jnp.exp(jnp.sum(da_lam_q2[i].astype(F32) * da_lam_k2[i])) + lam_init)
        subln = jnp.broadcast_to(da_subln_g[i].astype(F32)[:, None], (DA_V_DIM, t_da))
        o_da = _diff_attention(da_q, b3(da_k), da_v, da_near, da_far,
                               lam.reshape(1).astype(F32), subln, bsz=bsz, seq=seq, t=t_da,
                               post_scale=1.0 - lam_init)

        tables = _ssm_tables(ssm_lam_re[i].astype(F32), ssm_lam_im[i].astype(F32), ssm_log_dt[i].astype(F32),
                             ssm_b_re[i].astype(F32), ssm_b_im[i].astype(F32),
                             ssm_c_re[i].astype(F32), ssm_c_im[i].astype(F32), SSM_CHUNK)
        o_ssm = _ssm(ssm_u, tables, vec(ssm_d[i]), ssm_w_glu[i].astype(BF16), bsz=bsz, seq=seq,
                     tm=plan["tm_gate"])

        o_sa = _sparse_attention(ix_q, ix_w, b3(ix_k), sa_q, b3(sa_k), sa_v, sa_near, sa_far,
                                 bsz=bsz, seq=seq, t=t_sa, topk=topk)

        wo = w_o[i].astype(BF16)
        h = _mix_ffn(h, o_da.reshape(rows, da_w), o_ssm, o_sa.reshape(rows, SA_WIDTH), p[i].reshape(rows, -1),
                     [wo[:da_w], wo[da_w:da_w + SSM_WIDTH], wo[da_w + SSM_WIDTH:], vec(ln2_g[i]), vec(ln2_b[i]),
                      ffn2_w_gate[i].astype(BF16), ffn2_w_up[i].astype(BF16), ffn2_w_down[i].astype(BF16),
                      ple_w_proj[i].astype(BF16), ple_w_gate[i].astype(BF16), vec(ln3_g[i]), vec(ln3_b[i])],
                     tm=tm)
    return h.reshape(bsz, seq, d)
```

```python
import functools
import math

import jax
import jax.numpy as jnp
from jax import lax
from jax.experimental import pallas as pl
from jax.experimental.pallas import tpu as pltpu

F32 = jnp.float32
BF16 = jnp.bfloat16
I32 = jnp.int32

DEPTH = 2
DA_QK_DIM = 64
DA_V_DIM = 2 * DA_QK_DIM
DA_HEADS = 4
SSM_GROUP = 16
SSM_GROUPS = 16
SSM_STATE = 64
SSM_WIDTH = SSM_GROUP * SSM_GROUPS
SA_HEAD_DIM = 64
SA_HEADS = 4
SA_WIDTH = SA_HEADS * SA_HEAD_DIM
IDX_HEADS = 8
IDX_DIM = 32
TOPK_MAX = 256
REL_BUCKETS = 32
REL_MAX_DIST = 128
ALPHA = (2 * DEPTH) ** 0.25
LN_EPS = 1e-5
IN_SIZES = (DA_HEADS * 2 * DA_QK_DIM, DA_HEADS * 2 * DA_QK_DIM, DA_HEADS * DA_V_DIM,
            SSM_WIDTH, SA_WIDTH, SA_HEAD_DIM, SA_HEAD_DIM,
            IDX_HEADS * IDX_DIM, IDX_DIM, IDX_HEADS)

LANES = 128
SUBLANES = 8
BF16_ROWS = 16
NEG = -1e30
INT_MIN = -2 ** 31
HI_MASK = -2 ** 16
TINY = 2.0 ** -126
BF16_MIN_NORMAL = 0x0080
LOW_BITS = 2
MID_BITS = 16 - LOW_BITS
MID_MASK = 2 ** MID_BITS - 1
ZERO_KEY_TOP = 2 ** 16 - 1
LOG2E = math.log2(math.e)
VMEM_LIMIT = 56 * 1024 * 1024
SSM_CHUNK = 64

def _params(*sem):
    return pltpu.CompilerParams(dimension_semantics=sem, vmem_limit_bytes=VMEM_LIMIT)


def _resident(shape):
    return pl.BlockSpec(shape, lambda *_: (0,) * len(shape), pipeline_mode=pl.Buffered(1))


def _dot(a, b):
    return jnp.dot(a, b, preferred_element_type=F32)


def _dot_nt(a, b):
    return lax.dot_general(a, b, (((1,), (1,)), ((), ())), preferred_element_type=F32)


def _layer_norm(y, g, b):
    mu = jnp.mean(y, axis=-1, keepdims=True)
    d = y - mu
    var = jnp.mean(d * d, axis=-1, keepdims=True)
    return d * lax.rsqrt(var + LN_EPS) * g + b


def _fold_rows(x, rows):
    while x.shape[0] > rows:
        half = x.shape[0] // 2
        x = x[:half] + x[half:]
    return x


FF_CHUNK = 256


def _swiglu(xb, wg, wu, wd):
    d_ff = wg.shape[1]
    acc = None
    for c0 in range(0, d_ff, FF_CHUNK):
        c1 = min(c0 + FF_CHUNK, d_ff)
        gate = _dot(xb, wg[:, c0:c1])
        up = _dot(xb, wu[:, c0:c1])
        hid = (gate * jax.nn.sigmoid(gate) * up).astype(BF16)
        part = _dot(hid, wd[c0:c1, :])
        acc = part if acc is None else acc + part
    return acc


def _ffn_inproj_kernel(x_ref, wg, wu, wd, g_ref, b_ref, *refs, transposed):
    n = len(transposed)
    w_refs, x1_ref, o_refs = refs[:n], refs[n], refs[n + 1:]
    x = x_ref[...]
    x1 = _layer_norm(ALPHA * x + 0.5 * _swiglu(x.astype(BF16), wg, wu, wd), g_ref[...], b_ref[...])
    x1_ref[...] = x1
    xb = x1.astype(BF16)
    for w_ref, o_ref, tr in zip(w_refs, o_refs, transposed):
        out = _dot_nt(w_ref[...], xb) if tr else _dot(xb, w_ref[...])
        o_ref[...] = out.astype(o_ref.dtype)


def _ffn_inproj(x, wg, wu, wd, g, b, weights, dtypes, transposed, *, tm):
    t_rows, d = x.shape
    row = lambda w: pl.BlockSpec((tm, w), lambda i: (i, 0))
    out_specs, out_shape = [row(d)], [jax.ShapeDtypeStruct((t_rows, d), F32)]
    for w, dt, tr in zip(weights, dtypes, transposed):
        if tr:
            out_specs.append(pl.BlockSpec((w.shape[0], tm), lambda i: (0, i)))
            out_shape.append(jax.ShapeDtypeStruct((w.shape[0], t_rows), dt))
        else:
            out_specs.append(row(w.shape[1]))
            out_shape.append(jax.ShapeDtypeStruct((t_rows, w.shape[1]), dt))
    consts = [wg, wu, wd, g, b] + list(weights)
    outs = pl.pallas_call(
        functools.partial(_ffn_inproj_kernel, transposed=tuple(transposed)),
        grid=(t_rows // tm,),
        in_specs=[row(d)] + [_resident(c.shape) for c in consts],
        out_specs=out_specs,
        out_shape=out_shape,
        compiler_params=_params("parallel"),
        name="ffn_ln_in_proj",
    )(x, *consts)
    return outs[0], outs[1:]


def _mix_ffn_kernel(x_ref, da_ref, ssm_ref, sa_ref, p_ref, w1, w2, w3, g2_ref, b2_ref,
                    wg, wu, wd, wpp, wpg, g3_ref, b3_ref, o_ref):
    mix = _dot(da_ref[...], w1[...]) + _dot(ssm_ref[...], w2[...]) + _dot(sa_ref[...], w3[...])
    x = _layer_norm(ALPHA * x_ref[...] + mix, g2_ref[...], b2_ref[...])
    xb = x.astype(BF16)
    y = ALPHA * x + 0.5 * _swiglu(xb, wg, wu, wd)
    y = y + _dot(p_ref[...].astype(BF16), wpp[...]) * jax.nn.sigmoid(_dot(xb, wpg[...]))
    o_ref[...] = _layer_norm(y, g3_ref[...], b3_ref[...])


def _mix_ffn(x, o_da, o_ssm, o_sa, p, consts, *, tm):
    t_rows, d = x.shape
    row = lambda a: pl.BlockSpec((tm, a.shape[1]), lambda i: (i, 0))
    acts = [x, o_da, o_ssm, o_sa, p]
    return pl.pallas_call(
        _mix_ffn_kernel,
        grid=(t_rows // tm,),
        in_specs=[row(a) for a in acts] + [_resident(c.shape) for c in consts],
        out_specs=pl.BlockSpec((tm, d), lambda i: (i, 0)),
        out_shape=jax.ShapeDtypeStruct((t_rows, d), F32),
        compiler_params=_params("parallel"),
        name="out_proj_ln_ffn_ple_ln",
    )(*acts, *consts)


def _softmax_probs(s, m_ref, shift):
    m_prev = m_ref[...]
    m_cur = jnp.max(s, axis=0, keepdims=True)
    if shift is not None:
        m_cur = m_cur + shift
    m_next = jnp.maximum(m_prev, m_cur)
    m_ref[...] = m_next
    p = jnp.exp2(s - (m_next if shift is None else m_next - shift))
    return p.astype(BF16), jnp.exp2(m_prev - m_next)


def _accumulate(acc_ref, rescale, v_t, p):
    v_aug = jnp.concatenate([v_t, jnp.ones((BF16_ROWS, v_t.shape[1]), v_t.dtype)], axis=0)
    acc_ref[...] = rescale * acc_ref[...] + _dot(v_aug, p)


def _softmax_step(s, v_t, m_ref, acc_ref, shift):
    p, rescale = _softmax_probs(s, m_ref, shift)
    _accumulate(acc_ref, rescale, v_t, p)


def _far_chunks(n, scores, values, m_ref, acc_ref, shift):
    def pair(j, carry):
        s0, s1 = scores(2 * j), scores(2 * j + 1)
        _softmax_step(s0, values(2 * j), m_ref, acc_ref, shift)
        _softmax_step(s1, values(2 * j + 1), m_ref, acc_ref, shift)
        return carry

    lax.fori_loop(0, n // 2, pair, 0)

    @pl.when(n % 2 == 1)
    def _():
        _softmax_step(scores(n - 1), values(n - 1), m_ref, acc_ref, shift)


def _da_kernel(far_ref, lam_ref, q_ref, k_ref, v_ref, bn_ref, g_ref, o_ref,
               qd_ref, m_ref, acc_ref, *, t, post_scale):
    h = pl.program_id(1)
    qi = pl.program_id(2)
    q = q_ref[...].astype(F32)
    rowq = lax.broadcasted_iota(I32, q.shape, 0)
    qd_ref[:, 0:t] = jnp.where(rowq < DA_QK_DIM, q, 0.0).astype(BF16)
    qd_ref[:, t:2 * t] = jnp.where(rowq >= DA_QK_DIM, q, 0.0).astype(BF16)
    m_ref[...] = jnp.full(m_ref.shape, NEG, F32)
    acc_ref[...] = jnp.zeros(acc_ref.shape, F32)
    far = far_ref[h]

    def scores(kb):
        return _dot(k_ref[pl.ds(pl.multiple_of(kb * t, t), t), :], qd_ref[...])

    def values(kb):
        return v_ref[:, pl.ds(pl.multiple_of(kb * t, t), t)]

    def near_block(kb, near):
        bias = bn_ref[near]
        _softmax_step(scores(kb) + jnp.concatenate([bias, bias], axis=1), values(kb), m_ref, acc_ref, None)

    _far_chunks(jnp.maximum(qi - 1, 0), scores, values, m_ref, acc_ref, far)

    @pl.when(qi >= 1)
    def _():
        near_block(qi - 1, 1)

    near_block(qi, 0)

    acc = acc_ref[...]
    o = acc[0:DA_V_DIM] / acc[DA_V_DIM:DA_V_DIM + 1]
    o = o[:, 0:t] - lam_ref[0] * o[:, t:2 * t]
    ms = jnp.mean(o * o, axis=0, keepdims=True)
    o = o * lax.rsqrt(ms + LN_EPS) * g_ref[...] * post_scale
    o_ref[...] = o.T.astype(o_ref.dtype)


def _diff_attention(q_t, k, v_t, bias_near, bias_far, lam, subln_g, *, bsz, seq, t, post_scale):
    heads = q_t.shape[0] // DA_V_DIM
    nq = seq // t
    smem = pl.BlockSpec(memory_space=pltpu.SMEM)
    va = DA_V_DIM + BF16_ROWS
    return pl.pallas_call(
        functools.partial(_da_kernel, t=t, post_scale=post_scale),
        grid=(bsz, heads, nq),
        in_specs=[
            smem, smem,
            pl.BlockSpec((DA_V_DIM, t), lambda b, h, i: (h, b * nq + i)),
            pl.BlockSpec((None, seq, DA_V_DIM), lambda b, h, i: (b, 0, h)),
            pl.BlockSpec((DA_V_DIM, seq), lambda b, h, i: (h, b)),
            pl.BlockSpec((None, 2, t, t), lambda b, h, i: (h, 0, 0, 0)),
            pl.BlockSpec((DA_V_DIM, t), lambda b, h, i: (0, 0)),
        ],
        out_specs=pl.BlockSpec((None, t, DA_V_DIM), lambda b, h, i: (b, i, h)),
        out_shape=jax.ShapeDtypeStruct((bsz, seq, heads * DA_V_DIM), BF16),
        scratch_shapes=[
            pltpu.VMEM((DA_V_DIM, 2 * t), BF16),
            pltpu.VMEM((1, 2 * t), F32),
            pltpu.VMEM((va, 2 * t), F32),
        ],
        compiler_params=_params("parallel", "parallel", "arbitrary"),
        name="diff_attention",
    )(bias_far, lam, q_t, k, v_t, bias_near, subln_g)


def _dsa_kernel(far_ref, ixq_ref, ixw_ref, ixk_ref, saq_ref, sak_ref, sav_ref, bn_ref, o_ref,
                qi8_ref, keys_ref, hi_ref, qs4_ref, m_ref, acc_ref, *, t, topk):
    qi = pl.program_id(1)
    kf = float(topk)

    qi8_ref[...] = jnp.zeros(qi8_ref.shape, BF16)
    for h in range(IDX_HEADS):
        qi8_ref[0:IDX_DIM, h * t:(h + 1) * t] = ixq_ref[h * IDX_DIM:(h + 1) * IDX_DIM, :]

    def rows(kb, n=1):
        return pl.ds(pl.multiple_of(kb * t, t), n * t)

    def chunk(kb):
        return keys_ref[rows(kb), :]

    def index_keys(kb, n=1):
        kblk = ixk_ref[rows(kb, n), :]
        sc = None
        for h in range(IDX_HEADS):
            r = _dot(kblk, qi8_ref[:, h * t:(h + 1) * t])
            term = ixw_ref[h:h + 1, :] * jnp.maximum(r, 0.0)
            sc = term if sc is None else sc + term
        return jnp.where(jnp.abs(sc) < TINY, 0.0, sc)

    def store_keys(kb, sc, n=1):
        bits = lax.bitcast_convert_type(sc, I32)
        key = bits ^ ((bits >> 31) & 0x7FFFFFFF)
        zero_key = (ZERO_KEY_TOP - kb * t) - lax.broadcasted_iota(I32, (n * t, t), 0)
        keys_ref[rows(kb, n), :] = jnp.where(sc == 0.0, zero_key, key)
        hi_ref[rows(kb, n), :] = lax.bitcast_convert_type(bits & HI_MASK, F32).astype(BF16)

    def in_pairs(count, step):
        def pair(j, carry):
            step(2 * j, 2)
            return carry
        lax.fori_loop(0, count // 2, pair, 0)

        @pl.when(count % 2 == 1)
        def _():
            step(count - 1, 1)

    in_pairs(qi, lambda kb, n: store_keys(kb, index_keys(kb, n), n))
    krow = lax.broadcasted_iota(I32, (t, t), 0)
    qcol = lax.broadcasted_iota(I32, (t, t), 1)
    store_keys(qi, jnp.where(krow <= qcol, index_keys(qi), -jnp.inf))

    def count(fn):
        def body(kb, cnt):
            return cnt + _fold_rows(fn(kb, chunk(kb)), SUBLANES)
        cnt = lax.fori_loop(0, qi + 1, body, jnp.zeros((SUBLANES, t), I32))
        return jnp.sum(cnt.astype(F32), axis=0, keepdims=True)

    def count_hi(thr):
        def body(kb, cnt):
            hit = jnp.where(hi_ref[rows(kb), :] >= thr, jnp.ones((), BF16), jnp.zeros((), BF16))
            return cnt + _fold_rows(hit, BF16_ROWS).astype(F32)
        cnt = lax.fori_loop(0, qi + 1, body, jnp.zeros((BF16_ROWS, t), F32))
        return jnp.sum(cnt, axis=0, keepdims=True)

    def bf16_of(pattern):
        return lax.bitcast_convert_type(jnp.left_shift(pattern, 16), F32).astype(BF16)

    def code_value(code):
        pattern = code ^ ((code >> 31) & 0x7FFF)
        return bf16_of(jnp.where(code > 0, jnp.maximum(pattern, BF16_MIN_NORMAL), pattern))

    def hi_bit(i, st):
        code, open_, above = st
        cand = code + jnp.left_shift(jnp.int32(1), 15 - i)
        c = count_hi(code_value(cand))
        code = jnp.where(open_ > 0.0, jnp.where(c >= kf, cand, code), code)
        above = jnp.where(open_ > 0.0, jnp.where(c >= kf, above, c), above)
        return code, jnp.where(c == kf, 0.0, open_), above

    code, open_, above = lax.fori_loop(
        0, 16, hi_bit, (jnp.full((1, t), -2 ** 15, I32), jnp.ones((1, t), F32), jnp.zeros((1, t), F32)))

    value = code_value(code)
    higher = above
    need = kf - higher

    def recode(kb, carry):
        mid = ((chunk(kb) >> LOW_BITS) & MID_MASK) + BF16_MIN_NORMAL
        hi_ref[rows(kb), :] = jnp.where(hi_ref[rows(kb), :] == value, bf16_of(mid), jnp.zeros((), BF16))
        return carry

    lax.fori_loop(0, qi + 1, recode, 0)

    def mid_cond(st):
        i, _, _, _, n_open = st
        return jnp.logical_and(i < MID_BITS, n_open > 0.0)

    def mid_body(st):
        i, mid, open_, above, _ = st
        cand = mid + jnp.left_shift(jnp.int32(1), MID_BITS - 1 - i)
        c = count_hi(bf16_of(cand + BF16_MIN_NORMAL))
        mid = jnp.where(open_ > 0.0, jnp.where(c >= need, cand, mid), mid)
        above = jnp.where(open_ > 0.0, jnp.where(c >= need, above, higher + c), above)
        open_ = jnp.where(c == need, 0.0, open_)
        return i + 1, mid, open_, above, jnp.max(open_)

    _, mid, open_, above, _ = lax.while_loop(
        mid_cond, mid_body, (jnp.int32(0), jnp.zeros((1, t), I32), open_, above, jnp.max(open_)))

    def search_cond(st):
        i, _, _, _, n_open = st
        return jnp.logical_and(i < 32, n_open > 0.0)

    def search_body(st):
        i, tau, open_, above, _ = st
        cand = tau + jnp.left_shift(jnp.int32(1), 31 - i)
        c = count(lambda kb, blk: jnp.where(blk >= cand, 1, 0))
        tau = jnp.where(open_ > 0.0, jnp.where(c >= kf, cand, tau), tau)
        above = jnp.where(open_ > 0.0, jnp.where(c >= kf, above, c), above)
        open_ = jnp.where(c == kf, 0.0, open_)
        return i + 1, tau, open_, above, jnp.max(open_)

    _, tau, open_, above, _ = lax.while_loop(
        search_cond, search_body,
        (jnp.int32(32 - LOW_BITS), jnp.left_shift(code, 16) + jnp.left_shift(mid, LOW_BITS), open_, above,
         jnp.max(open_)))

    @pl.when(jnp.max(open_) > 0.0)
    def _():
        need = kf - above
        below = jnp.where(qcol <= krow, 1.0, 0.0).astype(BF16)

        def retire(kb, seen):
            blk = chunk(kb)
            tied = jnp.where(blk == tau, 1.0, 0.0).astype(BF16)
            rank = _dot(below, tied) + seen
            keys_ref[rows(kb), :] = jnp.where(blk == tau, jnp.where(rank > need, INT_MIN, blk), blk)
            return rank[t - 1:t, :]

        lax.fori_loop(0, qi + 1, retire, jnp.zeros((1, t), F32))

    qs4_ref[...] = jnp.zeros(qs4_ref.shape, BF16)
    for h in range(SA_HEADS):
        qs4_ref[0:SA_HEAD_DIM, h * t:(h + 1) * t] = saq_ref[h * SA_HEAD_DIM:(h + 1) * SA_HEAD_DIM, :]
    m_ref[...] = jnp.full(m_ref.shape, NEG, F32)
    acc_ref[...] = jnp.zeros(acc_ref.shape, F32)

    def scores(kb, near=None, n=1):
        s = _dot(sak_ref[rows(kb, n), :], qs4_ref[...])
        sel = keys_ref[rows(kb, n), :] >= tau
        parts = []
        for h in range(SA_HEADS):
            sh = s[:, h * t:(h + 1) * t]
            if near is not None:
                sh = sh + bn_ref[h, near]
            parts.append(jnp.where(sel, sh, NEG))
        return jnp.concatenate(parts, axis=1)

    far = jnp.concatenate([jnp.full((1, t), far_ref[h], F32) for h in range(SA_HEADS)], axis=1)

    def values(kb, n=1):
        return sav_ref[:, rows(kb, n)]

    def far_step(kb, n):
        tiles = [scores(kb + r) for r in range(n)]
        for r in range(n):
            _softmax_step(tiles[r], values(kb + r), m_ref, acc_ref, far)

    in_pairs(jnp.maximum(qi - 1, 0), far_step)

    @pl.when(qi >= 1)
    def _():
        _softmax_step(scores(qi - 1, 1), values(qi - 1), m_ref, acc_ref, None)

    _softmax_step(scores(qi, 0), values(qi), m_ref, acc_ref, None)

    acc = acc_ref[...]
    o = acc[0:SA_HEAD_DIM] / acc[SA_HEAD_DIM:SA_HEAD_DIM + 1]
    o = jnp.concatenate([o[:, h * t:(h + 1) * t] for h in range(SA_HEADS)], axis=0)
    o_ref[...] = o.T.astype(o_ref.dtype)


def _sparse_attention(ixq_t, ixw_t, ixk, saq_t, sak, sav_t, bias_near, bias_far, *, bsz, seq, t, topk):
    nq = seq // t
    assert seq <= ZERO_KEY_TOP
    smem = pl.BlockSpec(memory_space=pltpu.SMEM)
    qcols = lambda r: pl.BlockSpec((r, t), lambda b, i: (0, b * nq + i))
    full = lambda w: pl.BlockSpec((None, seq, w), lambda b, i: (b, 0, 0), pipeline_mode=pl.Buffered(1))
    va = SA_HEAD_DIM + BF16_ROWS
    return pl.pallas_call(
        functools.partial(_dsa_kernel, t=t, topk=topk),
        grid=(bsz, nq),
        in_specs=[
            smem,
            qcols(IDX_HEADS * IDX_DIM), qcols(IDX_HEADS), full(LANES),
            qcols(SA_WIDTH), full(LANES),
            pl.BlockSpec((SA_HEAD_DIM, seq), lambda b, i: (0, b), pipeline_mode=pl.Buffered(1)),
            _resident((SA_HEADS, 2, t, t)),
        ],
        out_specs=pl.BlockSpec((None, t, SA_WIDTH), lambda b, i: (b, i, 0)),
        out_shape=jax.ShapeDtypeStruct((bsz, seq, SA_WIDTH), BF16),
        scratch_shapes=[
            pltpu.VMEM((LANES, IDX_HEADS * t), BF16),
            pltpu.VMEM((seq, t), I32),
            pltpu.VMEM((seq, t), BF16),
            pltpu.VMEM((LANES, SA_HEADS * t), BF16),
            pltpu.VMEM((1, SA_HEADS * t), F32),
            pltpu.VMEM((va, SA_HEADS * t), F32),
        ],
        compiler_params=_params("parallel", "arbitrary"),
        name="sparse_attention",
    )(bias_far, ixq_t, ixw_t, ixk, saq_t, sak, sav_t, bias_near)


def _ssm_state_kernel(u_ref, w_ref, o_ref):
    o_ref[...] = _dot(u_ref[...], w_ref[...])


def _ssm_scan_kernel(loc_ref, a1_ref, a2_ref, o_ref):
    a1 = a1_ref[...]
    a2 = a2_ref[...]

    def body(n, s):
        o_ref[n] = s
        return a1 * s + a2 * pltpu.roll(s, SSM_STATE, axis=1) + loc_ref[n]

    lax.fori_loop(0, loc_ref.shape[0], body, jnp.zeros(a1.shape, F32))


def _ssm_out_kernel(u_ref, s_ref, toep_ref, wout_ref, o_ref):
    o_ref[...] = (_dot(u_ref[...], toep_ref[...])
                  + _dot(s_ref[...].astype(BF16), wout_ref[...]))


def _ssm_gate_kernel(y_ref, u_ref, d_ref, w_ref, o_ref):
    y = jax.nn.gelu(y_ref[...] + d_ref[...] * u_ref[...])
    o_ref[...] = (y * jax.nn.sigmoid(_dot(y.astype(BF16), w_ref[...]))).astype(o_ref.dtype)


def _ssm_tables(lam_re, lam_im, log_dt, b_re, b_im, c_re, c_im, tc):
    hp = lax.Precision.HIGHEST
    dt = jnp.exp(log_dt)[:, None]
    n = jnp.arange(tc + 1, dtype=F32)[:, None, None]
    mag = jnp.exp(lam_re * dt * n)
    pw_re = mag * jnp.cos(lam_im * dt * n)
    pw_im = mag * jnp.sin(lam_im * dt * n)
    den = lam_re * lam_re + lam_im * lam_im
    nr, ni = pw_re[1] - 1.0, pw_im[1]
    f_re = (nr * lam_re + ni * lam_im) / den
    f_im = (ni * lam_re - nr * lam_im) / den
    bb_re = f_re[..., None] * b_re - f_im[..., None] * b_im
    bb_im = f_re[..., None] * b_im + f_im[..., None] * b_re
    ca_re = c_re[None] * pw_re[:, :, None, :] - c_im[None] * pw_im[:, :, None, :]
    ca_im = c_re[None] * pw_im[:, :, None, :] + c_im[None] * pw_re[:, :, None, :]
    groups = lam_re.shape[0]
    inv_mag = jnp.exp(-lam_re * dt * n[:tc])
    iw_re = inv_mag * jnp.cos(lam_im * dt * n[:tc])
    iw_im = -inv_mag * jnp.sin(lam_im * dt * n[:tc])
    l_re = iw_re[..., None] * bb_re[None] - iw_im[..., None] * bb_im[None]
    l_im = iw_re[..., None] * bb_im[None] + iw_im[..., None] * bb_re[None]
    left = jnp.concatenate([l_re, l_im], axis=2).transpose(1, 0, 3, 2)
    right = jnp.concatenate([ca_re[:tc], -ca_im[:tc]], axis=3).transpose(1, 0, 2, 3)
    cw = tc * SSM_GROUP
    toep = jnp.einsum('gik,gjk->gij', left.reshape(groups, cw, 2 * SSM_STATE),
                      right.reshape(groups, cw, 2 * SSM_STATE), precision=hp)
    step = jnp.arange(cw) // SSM_GROUP
    toep = jnp.where(step[:, None] <= step[None, :], toep, 0.0)
    rev_re, rev_im = pw_re[tc - 1::-1][:tc], pw_im[tc - 1::-1][:tc]
    ws_re = rev_re[..., None] * bb_re[None] - rev_im[..., None] * bb_im[None]
    ws_im = rev_re[..., None] * bb_im[None] + rev_im[..., None] * bb_re[None]
    wstate = jnp.concatenate([ws_re, ws_im], axis=2)
    wstate = wstate.transpose(1, 0, 3, 2).reshape(groups, tc * SSM_GROUP, 2 * SSM_STATE)
    wout = jnp.concatenate([ca_re[1:], -ca_im[1:]], axis=3)
    wout = wout.transpose(1, 3, 0, 2).reshape(groups, 2 * SSM_STATE, tc * SSM_GROUP)
    dec_re, dec_im = pw_re[tc], pw_im[tc]
    a1 = jnp.concatenate([dec_re, dec_re], axis=1)
    a2 = jnp.concatenate([-dec_im, dec_im], axis=1)
    return toep.astype(BF16), wstate.astype(BF16), wout.astype(BF16), a1, a2


def _ssm(u, tables, d_skip, w_glu, *, bsz, seq, tm):
    toep, wstate, wout, a1, a2 = tables
    groups = toep.shape[0]
    tc = SSM_CHUNK
    nc = seq // tc
    rows = bsz * nc
    cw = tc * SSM_GROUP
    ug = u.astype(BF16).reshape(bsz, nc, tc, groups, SSM_GROUP).transpose(3, 0, 1, 2, 4).reshape(groups, rows, cw)
    gspec = lambda r, c: pl.BlockSpec((None, r, c), lambda g: (g, 0, 0))
    loc = pl.pallas_call(
        _ssm_state_kernel,
        grid=(groups,),
        in_specs=[gspec(rows, cw), gspec(cw, 2 * SSM_STATE)],
        out_specs=gspec(rows, 2 * SSM_STATE),
        out_shape=jax.ShapeDtypeStruct((groups, rows, 2 * SSM_STATE), F32),
        compiler_params=_params("parallel"),
        name="ssm_chunk_state",
    )(ug, wstate)
    gb = groups * bsz
    loc_t = loc.reshape(groups, bsz, nc, 2 * SSM_STATE).transpose(2, 0, 1, 3).reshape(nc, gb, 2 * SSM_STATE)
    a1r = jnp.repeat(a1, bsz, axis=0)
    a2r = jnp.repeat(a2, bsz, axis=0)
    rb = gb
    prev = pl.pallas_call(
        _ssm_scan_kernel,
        grid=(gb // rb,),
        in_specs=[pl.BlockSpec((nc, rb, 2 * SSM_STATE), lambda i: (0, i, 0)),
                  pl.BlockSpec((rb, 2 * SSM_STATE), lambda i: (i, 0)),
                  pl.BlockSpec((rb, 2 * SSM_STATE), lambda i: (i, 0))],
        out_specs=pl.BlockSpec((nc, rb, 2 * SSM_STATE), lambda i: (0, i, 0)),
        out_shape=jax.ShapeDtypeStruct((nc, gb, 2 * SSM_STATE), F32),
        compiler_params=_params("parallel"),
        name="ssm_chunk_scan",
    )(loc_t, a1r, a2r)
    prev_g = prev.reshape(nc, groups, bsz, 2 * SSM_STATE).transpose(1, 2, 0, 3).reshape(groups, rows, 2 * SSM_STATE)
    y = pl.pallas_call(
        _ssm_out_kernel,
        grid=(groups,),
        in_specs=[gspec(rows, cw), gspec(rows, 2 * SSM_STATE), gspec(cw, cw), gspec(2 * SSM_STATE, cw)],
        out_specs=gspec(rows, cw),
        out_shape=jax.ShapeDtypeStruct((groups, rows, cw), F32),
        compiler_params=_params("parallel"),
        name="ssm_chunk_out",
    )(ug, prev_g, toep, wout)
    y = y.reshape(groups, bsz, nc, tc, SSM_GROUP).transpose(1, 2, 3, 0, 4).reshape(bsz * seq, groups * SSM_GROUP)
    width = groups * SSM_GROUP
    return pl.pallas_call(
        _ssm_gate_kernel,
        grid=(bsz * seq // tm,),
        in_specs=[pl.BlockSpec((tm, width), lambda i: (i, 0)),
                  pl.BlockSpec((tm, width), lambda i: (i, 0)),
                  _resident((1, width)), _resident((width, width))],
        out_specs=pl.BlockSpec((tm, width), lambda i: (i, 0)),
        out_shape=jax.ShapeDtypeStruct((bsz * seq, width), BF16),
        compiler_params=_params("parallel"),
        name="ssm_gate",
    )(y, u, d_skip, w_glu)


def _t5_bucket(n):
    max_exact = REL_BUCKETS // 2
    nf = jnp.maximum(n, 1).astype(F32)
    large = max_exact + (jnp.log(nf / max_exact) / math.log(REL_MAX_DIST / max_exact)
                         * (REL_BUCKETS - max_exact)).astype(I32)
    large = jnp.minimum(large, REL_BUCKETS - 1)
    return jnp.where(n < max_exact, n, large)


def _bias_tiles(table, t):
    assert t >= REL_MAX_DIST
    table = table.astype(F32) * LOG2E
    key = jnp.arange(t)[:, None]
    query = jnp.arange(t)[None, :]
    dist = jnp.stack([query - key, t + query - key])
    bucket = _t5_bucket(jnp.maximum(dist, 0))
    tiles = jnp.zeros((table.shape[1],) + dist.shape, F32)
    for b in range(REL_BUCKETS):
        tiles = jnp.where(bucket[None] == b, table[b][:, None, None, None], tiles)
    return jnp.where(dist[None] >= 0, tiles, NEG), table[REL_BUCKETS - 1]


def _split_w_in(w_in):
    offs = [0]
    for s in IN_SIZES:
        offs.append(offs[-1] + s)
    da_q, da_k, da_v, ssm_u, sa_q, sa_k, sa_v, ix_q, ix_k, ix_w = (
        w_in[:, offs[j]:offs[j + 1]] for j in range(len(IN_SIZES)))
    w_scale = IDX_HEADS ** -0.5 * IDX_DIM ** -0.5
    pad = lambda w: jnp.pad(w, ((0, 0), (0, LANES - w.shape[1])))
    weights = [(da_q * (DA_QK_DIM ** -0.5 * LOG2E)).T, da_k, da_v.T, ssm_u,
               (sa_q * (SA_HEAD_DIM ** -0.5 * LOG2E)).T, pad(sa_k), sa_v.T,
               ix_q.T, pad(ix_k), (ix_w * w_scale).T]
    dtypes = [BF16, BF16, BF16, F32, BF16, BF16, BF16, BF16, BF16, F32]
    transposed = [True, False, True, False, True, False, True, True, False, True]
    return [w.astype(BF16) for w in weights], dtypes, transposed


def _plan(bsz, seq):
    rows = bsz * seq
    tm = 512 if rows % 512 == 0 else rows
    t_da = 1024 if seq % 1024 == 0 and seq >= 4096 else 256 if seq % 256 == 0 else 128
    t_sa = 512 if seq % 512 == 0 and seq >= 2048 else 256 if seq % 256 == 0 else 128
    tm_gate = 2048 if rows % 2048 == 0 else tm
    return dict(tm=tm, tm_gate=tm_gate, t_da=t_da, t_sa=t_sa)


def kernel(x, p, rel_bias, ffn1_w_gate, ffn1_w_up, ffn1_w_down, ln1_g, ln1_b, w_in, w_o, da_lam_q1, da_lam_k1, da_lam_q2, da_lam_k2, da_subln_g, ssm_lam_re, ssm_lam_im, ssm_log_dt, ssm_b_re, ssm_b_im, ssm_c_re, ssm_c_im, ssm_d, ssm_w_glu, ln2_g, ln2_b, ffn2_w_gate, ffn2_w_up, ffn2_w_down, ple_w_proj, ple_w_gate, ln3_g, ln3_b):
    bsz, seq, d = x.shape
    rows = bsz * seq
    plan = _plan(bsz, seq)
    tm, t_da, t_sa = plan["tm"], plan["t_da"], plan["t_sa"]
    topk = min(TOPK_MAX, seq // 4)
    da_near, da_far = _bias_tiles(rel_bias[:, :DA_HEADS], t_da)
    sa_near, sa_far = _bias_tiles(rel_bias[:, DA_HEADS:], t_sa)
    vec = lambda a: a.reshape(1, -1).astype(F32)
    da_w = DA_HEADS * DA_V_DIM

    h = x.reshape(rows, d)
    for i in range(DEPTH):
        lam_init = 0.8 - 0.6 * math.exp(-0.3 * i)
        weights, dtypes, transposed = _split_w_in(w_in[i])
        h, (da_q, da_k, da_v, ssm_u, sa_q, sa_k, sa_v, ix_q, ix_k, ix_w) = _ffn_inproj(
            h, ffn1_w_gate[i].astype(BF16), ffn1_w_up[i].astype(BF16), ffn1_w_down[i].astype(BF16),
            vec(ln1_g[i]), vec(ln1_b[i]), weights, dtypes, transposed, tm=tm)
        b3 = lambda a: a.reshape(bsz, seq, a.shape[-1])

        lam = (jnp.exp(jnp.sum(da_lam_q1[i].astype(F32) * da_lam_k1[i]))
               - jnp.exp(jnp.sum(da_lam_q2[i].astype(F32) * da_lam_k2[i])) + lam_init)
        subln = jnp.broadcast_to(da_subln_g[i].astype(F32)[:, None], (DA_V_DIM, t_da))
        o_da = _diff_attention(da_q, b3(da_k), da_v, da_near, da_far,
                               lam.reshape(1).astype(F32), subln, bsz=bsz, seq=seq, t=t_da,
                               post_scale=1.0 - lam_init)

        tables = _ssm_tables(ssm_lam_re[i].astype(F32), ssm_lam_im[i].astype(F32), ssm_log_dt[i].astype(F32),
                             ssm_b_re[i].astype(F32), ssm_b_im[i].astype(F32),
                             ssm_c_re[i].astype(F32), ssm_c_im[i].astype(F32), SSM_CHUNK)
        o_ssm = _ssm(ssm_u, tables, vec(ssm_d[i]), ssm_w_glu[i].astype(BF16), bsz=bsz, seq=seq,
                     tm=plan["tm_gate"])

        o_sa = _sparse_attention(ix_q, ix_w, b3(ix_k), sa_q, b3(sa_k), sa_v, sa_near, sa_far,
                                 bsz=bsz, seq=seq, t=t_sa, topk=topk)

        wo = w_o[i].astype(BF16)
        h = _mix_ffn(h, o_da.reshape(rows, da_w), o_ssm, o_sa.reshape(rows, SA_WIDTH), p[i].reshape(rows, -1),
                     [wo[:da_w], wo[da_w:da_w + SSM_WIDTH], wo[da_w + SSM_WIDTH:], vec(ln2_g[i]), vec(ln2_b[i]),
                      ffn2_w_gate[i].astype(BF16), ffn2_w_up[i].astype(BF16), ffn2_w_down[i].astype(BF16),
                      ple_w_proj[i].astype(BF16), ple_w_gate[i].astype(BF16), vec(ln3_g[i]), vec(ln3_b[i])],
                     tm=tm)
    return h.reshape(bsz, seq, d)
```
